```python
import jax
import jax.numpy as jnp
from jax import lax
import numpy as np

D_MODEL = 1024
BATCH = 1
SEQ = 16384
DEPTH = 2

GRID_W = 64
CTX_LEN = 256
HEAD_DIM = 64
N_Q_HEADS = 8
N_KV_HEADS = 2
Q_PER_KV = N_Q_HEADS // N_KV_HEADS
ATTN_W = N_Q_HEADS * HEAD_DIM
KV_W = N_KV_HEADS * HEAD_DIM
POOL_W = D_MODEL // 4
POOL_WINDOWS = (2, 4, 8, 16)
POOL_GROUPS = len(POOL_WINDOWS)
POOL_GW = POOL_W // POOL_GROUPS
CONV_W = D_MODEL // 4
CONV_K = 31
MIX_W = ATTN_W + POOL_W + CONV_W
IN_W = ATTN_W + 2 * KV_W + POOL_W + 2 * CONV_W
Q_BLOCK = 128
ROPE_THETA = 10000.0
N_GROUPS = 4
EXPERTS_PER_GROUP = 8
N_EXPERTS = N_GROUPS * EXPERTS_PER_GROUP
TOP_K_IN_GROUP = 2
D_EXPERT = D_MODEL // 2
MOE_BLOCK = 128
N_MOD = 6
ALPHA = (2 * DEPTH) ** 0.25
BETA = (8 * DEPTH) ** -0.25
EPS = 1e-6

kernel_name = 'hybrid_headgroup_prefix_hmoe_block'


def ln_plain(x):
    xf = x.astype(jnp.float32)
    mu = xf.mean(-1, keepdims=True)
    var = jnp.square(xf - mu).mean(-1, keepdims=True)
    return ((xf - mu) * lax.rsqrt(var + EPS)).astype(x.dtype)


def ln_affine(x, g, b):
    xf = x.astype(jnp.float32)
    mu = xf.mean(-1, keepdims=True)
    var = jnp.square(xf - mu).mean(-1, keepdims=True)
    return ((xf - mu) * lax.rsqrt(var + EPS) * g + b).astype(x.dtype)


def rms_heads(t, g):
    tf = t.astype(jnp.float32)
    return (tf * lax.rsqrt(jnp.square(tf).mean(-1, keepdims=True) + EPS) * g).astype(t.dtype)


def modulate(h, shift, scale):
    return h * (1 + scale) + shift


def rope_tables(n_tok):
    rows = n_tok // GRID_W
    row = jnp.broadcast_to(jnp.arange(rows)[:, None], (rows, GRID_W)).reshape(-1)
    col = jnp.broadcast_to(jnp.arange(GRID_W)[None, :], (rows, GRID_W)).reshape(-1)
    n_freq = HEAD_DIM // 4
    inv = ROPE_THETA ** (-jnp.arange(n_freq, dtype=jnp.float32) / n_freq)
    pos = jnp.stack([row, col], axis=-1).astype(jnp.float32)
    ang = pos[:, :, None] * inv
    return jnp.cos(ang), jnp.sin(ang)


def apply_rope(t, cos, sin):
    b, n, h, _ = t.shape
    tr = t.reshape(b, n, h, 2, 2, HEAD_DIM // 4)
    t1, t2 = tr[..., 0, :], tr[..., 1, :]
    cs = cos[None, :, None].astype(t.dtype)
    sn = sin[None, :, None].astype(t.dtype)
    out = jnp.stack([t1 * cs - t2 * sn, t2 * cs + t1 * sn], axis=-2)
    return out.reshape(t.shape)


def gqa(q, k, v):
    s = jnp.einsum('bqkgd,bskd->bkgqs', q, k).astype(jnp.float32) * (HEAD_DIM ** -0.5)
    p = jax.nn.softmax(s, axis=-1).astype(v.dtype)
    return jnp.einsum('bkgqs,bskd->bqkgd', p, v)


def latent_attention(q, k_all, v_all):
    b, n = q.shape[:2]
    nb = n // Q_BLOCK
    qb = q.reshape(b, nb, Q_BLOCK, N_KV_HEADS, Q_PER_KV, HEAD_DIM).swapaxes(0, 1)
    o = lax.map(lambda qblk: gqa(qblk, k_all, v_all), qb)
    return o.swapaxes(0, 1).reshape(b, n, ATTN_W)


def pool_mixer(u, w_pool, pool_scale):
    b, n, _ = u.shape
    ug = u.reshape(b, n, POOL_GROUPS, POOL_GW).astype(jnp.float32)
    cs = jnp.concatenate([jnp.zeros((b, 1, POOL_GROUPS, POOL_GW), jnp.float32), jnp.cumsum(ug, axis=1)], axis=1)
    t = jnp.arange(n)
    outs = []
    for gi, w in enumerate(POOL_WINDOWS):
        lo = jnp.clip(t - w // 2, 0, n)
        hi = jnp.clip(t + (w - w // 2), 0, n)
        cg = cs[:, :, gi]
        mean = (cg[:, hi] - cg[:, lo]) / (hi - lo).astype(jnp.float32)[None, :, None]
        outs.append(mean - ug[:, :, gi])
    pooled = jnp.stack(outs, axis=2).astype(u.dtype)
    y = jnp.einsum('blgc,gcd->blgd', pooled, w_pool)
    return y.reshape(b, n, POOL_W) * pool_scale


def conv_module(a, g, w_dw, b_dw, cv_g, cv_b, w_pw):
    u = a * jax.nn.sigmoid(g)
    y = lax.conv_general_dilated(u, w_dw[:, None, :], window_strides=(1,),
                                 padding=[(CONV_K // 2, CONV_K // 2)],
                                 dimension_numbers=('NWC', 'WIO', 'NWC'),
                                 feature_group_count=CONV_W) + b_dw
    return jax.nn.silu(ln_affine(y, cv_g, cv_b)) @ w_pw


def split_in(p):
    o = np.cumsum([0, ATTN_W, KV_W, KV_W, POOL_W, CONV_W, CONV_W])
    return tuple(p[..., int(o[i]):int(o[i + 1])] for i in range(6))


def heads(t, n):
    return t.reshape(*t.shape[:-1], n, HEAD_DIM)


def hier_moe(h, w_rg, b_rg, w_re, b_re, wg, wu, wd):
    T, D = h.shape
    A = T * TOP_K_IN_GROUP
    nblk = -(-A // MOE_BLOCK) + N_EXPERTS
    lg = (h @ w_rg).astype(jnp.float32) + b_rg
    pg = jax.nn.softmax(lg, axis=-1)
    gsel = jnp.argmax(lg, axis=-1)
    p_group = jnp.take_along_axis(pg, gsel[:, None], axis=1)
    le = ((h @ w_re).astype(jnp.float32) + b_re).reshape(T, N_GROUPS, EXPERTS_PER_GROUP)
    le_sel = jnp.take_along_axis(le, gsel[:, None, None], axis=1)[:, 0]
    top_v, top_i = lax.top_k(le_sel, TOP_K_IN_GROUP)
    gate = (p_group * jax.nn.softmax(top_v, axis=-1)).astype(h.dtype)
    eid = (gsel[:, None] * EXPERTS_PER_GROUP + top_i).reshape(-1)
    tok = jnp.repeat(jnp.arange(T), TOP_K_IN_GROUP)
    gw = gate.reshape(-1)
    order = jnp.argsort(eid)
    e_s, tok_s, w_s = eid[order], tok[order], gw[order]
    counts = jnp.bincount(eid, length=N_EXPERTS)
    starts = jnp.cumsum(counts) - counts
    padded = (counts + MOE_BLOCK - 1) // MOE_BLOCK * MOE_BLOCK
    pends = jnp.cumsum(padded)
    pstarts = pends - padded
    dest = pstarts[e_s] + jnp.arange(A) - starts[e_s]
    P = nblk * MOE_BLOCK
    tok_buf = jnp.zeros((P,), jnp.int32).at[dest].set(tok_s.astype(jnp.int32))
    w_buf = jnp.zeros((P,), h.dtype).at[dest].set(w_s)
    blk_e = jnp.minimum(jnp.searchsorted(pends, jnp.arange(nblk) * MOE_BLOCK, side='right'), N_EXPERTS - 1)

    def expert_block(args):
        tb, e = args
        xb = h[tb]
        return (jax.nn.silu(xb @ wg[e]) * (xb @ wu[e])) @ wd[e]

    ys = lax.map(expert_block, (tok_buf.reshape(nblk, MOE_BLOCK), blk_e))
    return jnp.zeros_like(h).at[tok_buf].add(ys.reshape(P, D) * w_buf[:, None])


def setup_inputs(seed: int = 0) -> dict:
    key = jax.random.key(seed)
    ks = jax.random.split(key, 28)

    def nrm(k, shape, s):
        return jax.random.normal(k, shape, jnp.float32) * s

    L = DEPTH
    w_in = nrm(ks[6], (L, D_MODEL, IN_W), D_MODEL ** -0.5)
    w_in = w_in.at[:, :, ATTN_W + KV_W:ATTN_W + 2 * KV_W].multiply(BETA)
    return {
        'x': nrm(ks[0], (BATCH, SEQ, D_MODEL), 1.0),
        'c': nrm(ks[1], (BATCH, D_MODEL), 1.0),
        'ctx': nrm(ks[2], (BATCH, CTX_LEN, D_MODEL), 1.0),
        'c_ctx': nrm(ks[3], (D_MODEL,), 1.0),
        'w_mod': nrm(ks[4], (L, D_MODEL, N_MOD * D_MODEL), D_MODEL ** -0.5),
        'b_mod': nrm(ks[5], (L, N_MOD * D_MODEL), 0.02),
        'w_in': w_in,
        'q_gain': 1.0 + nrm(ks[7], (L, HEAD_DIM), 0.02),
        'k_gain': 1.0 + nrm(ks[8], (L, HEAD_DIM), 0.02),
        'w_pool': nrm(ks[9], (L, POOL_GROUPS, POOL_GW, POOL_GW), POOL_GW ** -0.5),
        'pool_scale': 1.0 + nrm(ks[10], (L, POOL_W), 0.02),
        'w_dw': nrm(ks[11], (L, CONV_K, CONV_W), CONV_K ** -0.5),
        'b_dw': nrm(ks[12], (L, CONV_W), 0.02),
        'cv_ln_g': 1.0 + nrm(ks[13], (L, CONV_W), 0.02),
        'cv_ln_b': nrm(ks[14], (L, CONV_W), 0.02),
        'w_cv_pw': nrm(ks[15], (L, CONV_W, CONV_W), CONV_W ** -0.5),
        'w_out': nrm(ks[16], (L, MIX_W, D_MODEL), BETA * MIX_W ** -0.5),
        'ln1_g': 1.0 + nrm(ks[17], (L, D_MODEL), 0.02),
        'ln1_b': nrm(ks[18], (L, D_MODEL), 0.02),
        'ln2_g': 1.0 + nrm(ks[19], (L, D_MODEL), 0.02),
        'ln2_b': nrm(ks[20], (L, D_MODEL), 0.02),
        'w_rg': nrm(ks[21], (L, D_MODEL, N_GROUPS), D_MODEL ** -0.5),
        'b_rg': nrm(ks[22], (L, N_GROUPS), 0.01),
        'w_re': nrm(ks[23], (L, D_MODEL, N_EXPERTS), D_MODEL ** -0.5),
        'b_re': nrm(ks[24], (L, N_EXPERTS), 0.01),
        'w_e_gate': nrm(ks[25], (L, N_EXPERTS, D_MODEL, D_EXPERT), D_MODEL ** -0.5),
        'w_e_up': nrm(ks[26], (L, N_EXPERTS, D_MODEL, D_EXPERT), D_MODEL ** -0.5),
        'w_e_down': nrm(ks[27], (L, N_EXPERTS, D_EXPERT, D_MODEL), BETA * D_EXPERT ** -0.5),
    }


def reference(x, c, ctx, c_ctx, w_mod, b_mod, w_in, q_gain, k_gain, w_pool, pool_scale, w_dw, b_dw,
              cv_ln_g, cv_ln_b, w_cv_pw, w_out, ln1_g, ln1_b, ln2_g, ln2_b, w_rg, b_rg, w_re, b_re,
              w_e_gate, w_e_up, w_e_down):
    B, L, D = x.shape
    n_ctx = ctx.shape[1]
    cos, sin = rope_tables(L)
    xl, xc = x, ctx
    for l in range(DEPTH):
        last = l == DEPTH - 1
        mod_l = (jax.nn.silu(c) @ w_mod[l] + b_mod[l])[:, None, :]
        mod_c = jax.nn.silu(c_ctx) @ w_mod[l] + b_mod[l]
        sh1, sc1, g1, sh2, sc2, g2 = jnp.split(mod_l, N_MOD, axis=-1)
        csh1, csc1, cg1, csh2, csc2, cg2 = jnp.split(mod_c, N_MOD, axis=-1)

        hl = modulate(ln_plain(xl), sh1, sc1)
        hc = modulate(ln_plain(xc), csh1, csc1)
        ql, kl, vl, ul, al, gl = split_in(hl @ w_in[l])
        if last:
            kc, vc = jnp.split(hc @ w_in[l][:, ATTN_W:ATTN_W + 2 * KV_W], 2, axis=-1)
        else:
            qc, kc, vc, uc, ac, gc = split_in(hc @ w_in[l])
        kc = rms_heads(heads(kc, N_KV_HEADS), k_gain[l])
        vc = heads(vc, N_KV_HEADS)
        ql = apply_rope(rms_heads(heads(ql, N_Q_HEADS), q_gain[l]), cos, sin)
        kl = apply_rope(rms_heads(heads(kl, N_KV_HEADS), k_gain[l]), cos, sin)
        vl = heads(vl, N_KV_HEADS)
        k_all = jnp.concatenate([kc, kl], axis=1)
        v_all = jnp.concatenate([vc, vl], axis=1)
        attn_l = latent_attention(ql, k_all, v_all)
        yl = jnp.concatenate([attn_l,
                              pool_mixer(ul, w_pool[l], pool_scale[l]),
                              conv_module(al, gl, w_dw[l], b_dw[l], cv_ln_g[l], cv_ln_b[l], w_cv_pw[l])],
                             axis=-1) @ w_out[l]
        if not last:
            qc = rms_heads(heads(qc, N_Q_HEADS), q_gain[l]).reshape(B, n_ctx, N_KV_HEADS, Q_PER_KV, HEAD_DIM)
            attn_c = gqa(qc, kc, vc).reshape(B, n_ctx, ATTN_W)
            yc = jnp.concatenate([attn_c,
                                  pool_mixer(uc, w_pool[l], pool_scale[l]),
                                  conv_module(ac, gc, w_dw[l], b_dw[l], cv_ln_g[l], cv_ln_b[l], w_cv_pw[l])],
                                 axis=-1) @ w_out[l]
            xc = ln_affine(ALPHA * xc + cg1 * yc, ln1_g[l], ln1_b[l])
        xl = ln_affine(ALPHA * xl + g1 * yl, ln1_g[l], ln1_b[l])

        hl = modulate(ln_plain(xl), sh2, sc2).reshape(B * L, D)
        if last:
            ol = hier_moe(hl, w_rg[l], b_rg[l], w_re[l], b_re[l], w_e_gate[l], w_e_up[l], w_e_down[l])
        else:
            hc = modulate(ln_plain(xc), csh2, csc2).reshape(B * n_ctx, D)
            o = hier_moe(jnp.concatenate([hc, hl], axis=0), w_rg[l], b_rg[l], w_re[l], b_re[l],
                         w_e_gate[l], w_e_up[l], w_e_down[l])
            oc, ol = o[:B * n_ctx], o[B * n_ctx:]
            xc = ln_affine(ALPHA * xc + cg2 * oc.reshape(xc.shape), ln2_g[l], ln2_b[l])
        xl = ln_affine(ALPHA * xl + g2 * ol.reshape(xl.shape), ln2_g[l], ln2_b[l])
    return xl
```

```python
import functools

import jax
import jax.numpy as jnp
from jax import lax
from jax.experimental import pallas as pl
from jax.experimental.pallas import tpu as pltpu

D = 1024
GRID_W = 64
HD = 64
NQ = 8
NKV = 2
QPK = NQ // NKV
ATTN_W = NQ * HD
KV_W = NKV * HD
POOL_W = 256
POOL_WINDOWS = (2, 4, 8, 16)
POOL_GW = POOL_W // len(POOL_WINDOWS)
CONV_W = 256
CONV_K = 31
N_GROUPS = 4
EPG = 8
N_EXPERTS = N_GROUPS * EPG
D_EXPERT = 512
N_MOD = 6
ROPE_THETA = 10000.0
EPS = 1e-6

TM = 256
HALO = 16
EBLK = 256
LANES = 128
VMEM_LIMIT = 56 * 1024 * 1024
NEG_BIG = -1e30

_HI = lax.Precision.HIGHEST


def _cparams(sem):
    return pltpu.CompilerParams(dimension_semantics=sem, vmem_limit_bytes=VMEM_LIMIT)


def _full(shape):
    n = len(shape)
    return pl.BlockSpec(shape, lambda *a, n=n: (0,) * n)


def _mod_kernel(cc_ref, w_ref, b_ref, o_ref):
    cc = cc_ref[...]
    a = cc * jax.nn.sigmoid(cc)
    o_ref[0] = jnp.dot(a, w_ref[0], precision=_HI, preferred_element_type=jnp.float32) + b_ref[0]


def _modulation(cc, w_mod, b_mod):
    depth = w_mod.shape[0]
    nw = w_mod.shape[2]
    bn = 1024
    return pl.pallas_call(
        _mod_kernel,
        grid=(depth, nw // bn),
        in_specs=[
            pl.BlockSpec((8, D), lambda l, j: (0, 0)),
            pl.BlockSpec((1, D, bn), lambda l, j: (l, 0, j)),
            pl.BlockSpec((1, 1, bn), lambda l, j: (l, 0, j)),
        ],
        out_specs=pl.BlockSpec((1, 8, bn), lambda l, j: (l, 0, j)),
        out_shape=jax.ShapeDtypeStruct((depth, 8, nw), jnp.float32),
        compiler_params=_cparams(("arbitrary", "arbitrary")),
        name="modulation",
    )(cc, w_mod, b_mod.reshape(depth, 1, nw))


def _ln_plain(x):
    mu = jnp.mean(x, axis=-1, keepdims=True)
    xc = x - mu
    var = jnp.mean(xc * xc, axis=-1, keepdims=True)
    return xc * lax.rsqrt(var + EPS)


def _mod_rows(mod_ref, idx):
    row = jnp.where(pl.program_id(0) == 0, 1, 0)
    return mod_ref[pl.ds(row, 1), idx * D:(idx + 1) * D]


def _rms_rope_t(t, gain, cos, sin):
    ms = jnp.mean(t * t, axis=0, keepdims=True)
    t = t * lax.rsqrt(ms + EPS) * gain
    outs = []
    for a in range(2):
        t1 = t[32 * a:32 * a + 16]
        t2 = t[32 * a + 16:32 * a + 32]
        c = cos[16 * a:16 * a + 16]
        s = sin[16 * a:16 * a + 16]
        outs.append(t1 * c - t2 * s)
        outs.append(t2 * c + t1 * s)
    return jnp.concatenate(outs, axis=0)


def _inproj_kernel(x_ref, mod_ref, wt_ref, wn_ref, qg_ref, kg_ref, cos_ref, sin_ref,
                   q_ref, k_ref, v_ref, u_ref, ag_ref):
    x = x_ref[...]
    h = _ln_plain(x) * (1.0 + _mod_rows(mod_ref, 1)) + _mod_rows(mod_ref, 0)
    hb = h.astype(jnp.bfloat16)
    qkv_t = lax.dot_general(wt_ref[...], hb, (((1,), (1,)), ((), ())),
                            preferred_element_type=jnp.float32)
    cos = cos_ref[...]
    sin = sin_ref[...]
    qg = qg_ref[...]
    kg = kg_ref[...]
    zeros = jnp.zeros((HD, TM), jnp.bfloat16)
    for hq in range(NQ):
        t = _rms_rope_t(qkv_t[hq * HD:(hq + 1) * HD], qg, cos, sin)
        tb = (t * (HD ** -0.5)).astype(jnp.bfloat16)
        g, hh = hq // QPK, hq % QPK
        q_ref[g, hh] = jnp.concatenate([tb, zeros] if g == 0 else [zeros, tb], axis=0)
    kts = [_rms_rope_t(qkv_t[ATTN_W + g * HD:ATTN_W + (g + 1) * HD], kg, cos, sin)
           for g in range(NKV)]
    k_ref[...] = jnp.concatenate(kts, axis=0).T.astype(jnp.bfloat16)
    v_ref[...] = qkv_t[ATTN_W + KV_W:ATTN_W + 2 * KV_W].astype(jnp.bfloat16)
    uag = jnp.dot(hb, wn_ref[...], preferred_element_type=jnp.float32)
    u_ref[...] = uag[:, :POOL_W]
    a = uag[:, POOL_W:POOL_W + CONV_W]
    gt = uag[:, POOL_W + CONV_W:]
    ag_ref[...] = a * jax.nn.sigmoid(gt)


def _inproj(xa, mod_l, w_t, w_n, qg, kg, cos_t, sin_t):
    ttot = xa.shape[0]
    nt = ttot // TM
    return pl.pallas_call(
        _inproj_kernel,
        grid=(nt,),
        in_specs=[
            pl.BlockSpec((TM, D), lambda i: (i, 0)),
            _full((8, N_MOD * D)),
            _full((ATTN_W + 2 * KV_W, D)),
            _full((D, POOL_W + 2 * CONV_W)),
            _full((HD, TM)),
            _full((HD, TM)),
            pl.BlockSpec((32, TM), lambda i: (0, i)),
            pl.BlockSpec((32, TM), lambda i: (0, i)),
        ],
        out_specs=[
            pl.BlockSpec((NKV, QPK, 2 * HD, TM), lambda i: (0, 0, 0, i)),
            pl.BlockSpec((TM, KV_W), lambda i: (i, 0)),
            pl.BlockSpec((KV_W, TM), lambda i: (0, i)),
            pl.BlockSpec((TM, POOL_W), lambda i: (i, 0)),
            pl.BlockSpec((TM, CONV_W), lambda i: (i, 0)),
        ],
        out_shape=[
            jax.ShapeDtypeStruct((NKV, QPK, 2 * HD, ttot), jnp.bfloat16),
            jax.ShapeDtypeStruct((ttot, KV_W), jnp.bfloat16),
            jax.ShapeDtypeStruct((KV_W, ttot), jnp.bfloat16),
            jax.ShapeDtypeStruct((ttot, POOL_W), jnp.float32),
            jax.ShapeDtypeStruct((ttot, CONV_W), jnp.float32),
        ],
        compiler_params=_cparams(("arbitrary",)),
        name="inproj",
    )(xa, mod_l, w_t, w_n, qg, kg, cos_t, sin_t)


TK = 256


def _attn_kernel(q_ref, k_ref, v_ref, o_ref, m_ref, l_ref, acc_ref, *, n_lat_chunks):
    i = pl.program_id(1)
    m_ref[...] = jnp.full(m_ref.shape, NEG_BIG, jnp.float32)
    l_ref[...] = jnp.zeros(l_ref.shape, jnp.float32)
    acc_ref[...] = jnp.zeros(acc_ref.shape, jnp.float32)

    def chunk(start):
        kc = k_ref[pl.ds(start, TK), :]
        vc = v_ref[0, :, pl.ds(start, TK)]
        for hh in range(QPK):
            s = jnp.dot(kc, q_ref[0, hh], preferred_element_type=jnp.float32)
            m_old = m_ref[hh]
            m_new = jnp.maximum(m_old, jnp.max(s, axis=0, keepdims=True))
            alpha = jnp.exp(m_old - m_new)
            p = jnp.exp(s - m_new)
            l_ref[hh] = alpha * l_ref[hh] + jnp.sum(p, axis=0, keepdims=True)
            acc_ref[hh] = alpha * acc_ref[hh] + jnp.dot(
                vc, p.astype(jnp.bfloat16), preferred_element_type=jnp.float32)
            m_ref[hh] = m_new

    chunk(0)

    @pl.when(i > 0)
    def _():
        def body(c, carry):
            chunk(pl.multiple_of(TM + c * TK, TK))
            return carry
        lax.fori_loop(0, n_lat_chunks, body, 0)

    outs = [acc_ref[hh] / l_ref[hh] for hh in range(QPK)]
    o_ref[...] = jnp.concatenate(outs, axis=0).T


def _attention(q_t, k2, v_t):
    ttot = k2.shape[0]
    nt = ttot // TM
    v3 = v_t.reshape(NKV, HD, ttot)
    kern = functools.partial(_attn_kernel, n_lat_chunks=(ttot - TM) // TK)
    return pl.pallas_call(
        kern,
        grid=(NKV, nt),
        in_specs=[
            pl.BlockSpec((1, QPK, 2 * HD, TM), lambda g, i: (g, 0, 0, i)),
            pl.BlockSpec((ttot, KV_W), lambda g, i: (0, 0)),
            pl.BlockSpec((1, HD, ttot), lambda g, i: (g, 0, 0)),
        ],
        out_specs=pl.BlockSpec((TM, QPK * HD), lambda g, i: (i, g)),
        out_shape=jax.ShapeDtypeStruct((ttot, ATTN_W), jnp.float32),
        scratch_shapes=[
            pltpu.VMEM((QPK, 1, TM), jnp.float32),
            pltpu.VMEM((QPK, 1, TM), jnp.float32),
            pltpu.VMEM((QPK, HD, TM), jnp.float32),
        ],
        compiler_params=_cparams(("arbitrary", "arbitrary")),
        name="attention",
    )(q_t, k2, v3)


def _fill_ext(ext_ref, cur_ref, prev_ref, next_ref, left_ok, right_ok):
    zero = jnp.zeros((HALO, cur_ref.shape[1]), jnp.float32)
    ext_ref[0:HALO, :] = jnp.where(left_ok, prev_ref[...], zero)
    ext_ref[HALO:HALO + TM, :] = cur_ref[...]
    ext_ref[HALO + TM:HALO + TM + HALO, :] = jnp.where(right_ok, next_ref[...], zero)


def _mix_kernel(x_ref, attn_ref, u_ref, up_ref, un_ref, ag_ref, agp_ref, agn_ref, mod_ref,
                wpool_ref, pscale_ref, wdw_ref, bdw_ref, cvg_ref, cvb_ref, wpw_ref, wout_ref,
                ln1g_ref, ln1b_ref, wr_ref, br_ref,
                x1_ref, h2_ref, route_ref, cnt_ref,
                uext_ref, agext_ref, base_ref, *, alpha, n_lat):
    i = pl.program_id(0)
    nt = pl.num_programs(0)
    is_ctx = i == 0
    left_ok = i >= 2
    right_ok = jnp.logical_and(i >= 1, i < nt - 1)

    @pl.when(i == 0)
    def _():
        base_ref[...] = jnp.zeros(base_ref.shape, jnp.float32)

    _fill_ext(uext_ref, u_ref, up_ref, un_ref, left_ok, right_ok)
    _fill_ext(agext_ref, ag_ref, agp_ref, agn_ref, left_ok, right_ok)

    def ush(off, rows=TM):
        return uext_ref[HALO + off:HALO + off + rows, :]

    a2 = ush(-8, TM + 15) + ush(-7, TM + 15)
    a4 = a2[0:TM + 13] + a2[2:TM + 15]
    a8 = a4[0:TM + 9] + a4[4:TM + 13]
    a16 = a8[0:TM] + a8[8:TM + 8]
    sums = (a2[7:7 + TM], a4[6:6 + TM], a8[4:4 + TM], a16)
    pos = lax.broadcasted_iota(jnp.int32, (TM, 1), 0) + jnp.where(is_ctx, 0, (i - 1) * TM)
    seq_n = jnp.where(is_ctx, TM, n_lat)
    u_cur = u_ref[...]
    lane = lax.broadcasted_iota(jnp.int32, (TM, POOL_W), 1)
    pooled = jnp.zeros((TM, POOL_W), jnp.float32)
    for gi, w in enumerate(POOL_WINDOWS):
        lo = jnp.maximum(pos - w // 2, 0)
        hi = jnp.minimum(pos + (w - w // 2), seq_n)
        inv = 1.0 / (hi - lo).astype(jnp.float32)
        in_group = jnp.logical_and(lane >= gi * POOL_GW, lane < (gi + 1) * POOL_GW)
        pooled = jnp.where(in_group, sums[gi] * inv - u_cur, pooled)
    y_pool = jnp.dot(pooled.astype(jnp.bfloat16), wpool_ref[...],
                     preferred_element_type=jnp.float32) * pscale_ref[...]

    conv = jnp.zeros((TM, CONV_W), jnp.float32) + bdw_ref[...]
    for k in range(CONV_K):
        off = HALO + k - CONV_K // 2
        conv = conv + agext_ref[off:off + TM, :] * wdw_ref[k:k + 1, :]
    cn = _ln_plain(conv) * cvg_ref[...] + cvb_ref[...]
    cn = cn * jax.nn.sigmoid(cn)
    y_conv = jnp.dot(cn.astype(jnp.bfloat16), wpw_ref[...], preferred_element_type=jnp.float32)

    y = jnp.dot(attn_ref[...].astype(jnp.bfloat16), wout_ref[0:ATTN_W, :],
                preferred_element_type=jnp.float32)
    y = y + jnp.dot(y_pool.astype(jnp.bfloat16), wout_ref[ATTN_W:ATTN_W + POOL_W, :],
                    preferred_element_type=jnp.float32)
    y = y + jnp.dot(y_conv.astype(jnp.bfloat16), wout_ref[ATTN_W + POOL_W:, :],
                    preferred_element_type=jnp.float32)
    x1 = _ln_plain(alpha * x_ref[...] + _mod_rows(mod_ref, 2) * y) * ln1g_ref[...] + ln1b_ref[...]
    x1_ref[...] = x1
    h2 = _ln_plain(x1) * (1.0 + _mod_rows(mod_ref, 4)) + _mod_rows(mod_ref, 3)
    h2_ref[...] = h2

    logits = jnp.dot(h2, wr_ref[...], precision=_HI, preferred_element_type=jnp.float32) + br_ref[...]
    ln = lax.broadcasted_iota(jnp.int32, (TM, LANES), 1)
    neg = jnp.float32(-jnp.inf)
    is_g = ln < N_GROUPS
    lg = jnp.where(is_g, logits, neg)
    mg = jnp.max(lg, axis=-1, keepdims=True)
    gsel = jnp.min(jnp.where(lg == mg, ln, LANES), axis=-1, keepdims=True)
    p_group = 1.0 / jnp.sum(jnp.exp(lg - mg), axis=-1, keepdims=True)
    e_lane = ln - N_GROUPS
    in_sel = jnp.logical_and(e_lane >= gsel * EPG, e_lane < (gsel + 1) * EPG)
    le = jnp.where(in_sel, logits, neg)
    v1 = jnp.max(le, axis=-1, keepdims=True)
    e1 = jnp.min(jnp.where(le == v1, e_lane, LANES), axis=-1, keepdims=True)
    le2 = jnp.where(e_lane == e1, neg, le)
    v2 = jnp.max(le2, axis=-1, keepdims=True)
    e2 = jnp.min(jnp.where(le2 == v2, e_lane, LANES), axis=-1, keepdims=True)
    ex = jnp.exp(v2 - v1)
    gate1 = p_group / (1.0 + ex)
    gate2 = p_group * ex / (1.0 + ex)

    oh = jnp.logical_or(ln == e1, ln == e2)
    ohb = jnp.where(oh, 1.0, 0.0).astype(jnp.bfloat16)
    r_i = lax.broadcasted_iota(jnp.int32, (TM, TM), 0)
    c_i = lax.broadcasted_iota(jnp.int32, (TM, TM), 1)
    tri = jnp.where(c_i < r_i, 1.0, 0.0).astype(jnp.bfloat16)
    before = jnp.dot(tri, ohb, preferred_element_type=jnp.float32) + base_ref[...]
    rank1 = jnp.sum(jnp.where(ln == e1, before, 0.0), axis=-1, keepdims=True)
    rank2 = jnp.sum(jnp.where(ln == e2, before, 0.0), axis=-1, keepdims=True)
    base_ref[...] = base_ref[...] + jnp.sum(ohb.astype(jnp.float32), axis=0, keepdims=True)
    cnt_ref[...] = jnp.broadcast_to(base_ref[...], cnt_ref.shape)

    route = jnp.zeros((TM, LANES), jnp.float32)
    for j, val in enumerate((e1.astype(jnp.float32), e2.astype(jnp.float32), rank1, rank2,
                             gate1, gate2)):
        route = jnp.where(ln == j, val, route)
    route_ref[...] = route


def _mixer(xa, attn, u, ag, mod_l, wpool, pscale, wdw, bdw, cvg, cvb, wpw, wout, ln1g, ln1b,
           wr, br, alpha):
    ttot = xa.shape[0]
    nt = ttot // TM
    hb = TM // HALO
    nhb = ttot // HALO

    def tile(w):
        return pl.BlockSpec((TM, w), lambda i: (i, 0))

    def prev(w):
        return pl.BlockSpec((HALO, w), lambda i: (jnp.maximum(i * hb - 1, 0), 0))

    def nxt(w):
        return pl.BlockSpec((HALO, w), lambda i: (jnp.minimum((i + 1) * hb, nhb - 1), 0))

    kern = functools.partial(_mix_kernel, alpha=alpha, n_lat=ttot - TM)
    return pl.pallas_call(
        kern,
        grid=(nt,),
        in_specs=[
            tile(D), tile(ATTN_W),
            tile(POOL_W), prev(POOL_W), nxt(POOL_W),
            tile(CONV_W), prev(CONV_W), nxt(CONV_W),
            _full((8, N_MOD * D)),
            _full((POOL_W, POOL_W)), _full((1, POOL_W)),
            _full((32, CONV_W)), _full((1, CONV_W)), _full((1, CONV_W)), _full((1, CONV_W)),
            _full((CONV_W, CONV_W)), _full((D, D)),
            _full((1, D)), _full((1, D)),
            _full((D, LANES)), _full((1, LANES)),
        ],
        out_specs=[tile(D), tile(D), tile(LANES), _full((8, LANES))],
        out_shape=[
            jax.ShapeDtypeStruct((ttot, D), jnp.float32),
            jax.ShapeDtypeStruct((ttot, D), jnp.float32),
            jax.ShapeDtypeStruct((ttot, LANES), jnp.float32),
            jax.ShapeDtypeStruct((8, LANES), jnp.float32),
        ],
        scratch_shapes=[
            pltpu.VMEM((TM + 2 * HALO, POOL_W), jnp.float32),
            pltpu.VMEM((TM + 2 * HALO, CONV_W), jnp.float32),
            pltpu.VMEM((1, LANES), jnp.float32),
        ],
        compiler_params=_cparams(("arbitrary",)),
        name="mixer",
    )(xa, attn, u, u, u, ag, ag, ag, mod_l, wpool, pscale, wdw, bdw, cvg, cvb, wpw, wout,
      ln1g, ln1b, wr, br)


def _row_copy_scatter(h_ref, xs_ref, sem, r, d):
    return pltpu.make_async_copy(h_ref.at[pl.ds(r, 1), :], xs_ref.at[pl.ds(d, 1), :], sem)


def _dispatch_kernel(d1_ref, d2_ref, h_ref, xs_in_ref, xs_ref, sem):
    del xs_in_ref

    def start(r, carry):
        _row_copy_scatter(h_ref, xs_ref, sem, r, d1_ref[0, 0, r]).start()
        _row_copy_scatter(h_ref, xs_ref, sem, r, d2_ref[0, 0, r]).start()
        return carry

    lax.fori_loop(0, TM, start, 0)

    def wait(r, carry):
        _row_copy_scatter(h_ref, xs_ref, sem, r, d1_ref[0, 0, r]).wait()
        _row_copy_scatter(h_ref, xs_ref, sem, r, d2_ref[0, 0, r]).wait()
        return carry

    lax.fori_loop(0, TM, wait, 0)


def _dispatch(h2, d1, d2, n_slots):
    ttot = h2.shape[0]
    nt = ttot // TM
    smem = pl.BlockSpec((1, 1, TM), lambda i: (i, 0, 0), memory_space=pltpu.SMEM)
    xs0 = jnp.zeros((n_slots, D), jnp.float32)
    return pl.pallas_call(
        _dispatch_kernel,
        grid=(nt,),
        in_specs=[smem, smem, pl.BlockSpec((TM, D), lambda i: (i, 0)),
                  pl.BlockSpec(memory_space=pl.ANY)],
        out_specs=pl.BlockSpec(memory_space=pl.ANY),
        out_shape=jax.ShapeDtypeStruct((n_slots, D), jnp.float32),
        scratch_shapes=[pltpu.SemaphoreType.DMA(())],
        input_output_aliases={3: 0},
        compiler_params=_cparams(("arbitrary",)),
        name="dispatch",
    )(d1, d2, h2, xs0)


def _expert_kernel(be_ref, nu_ref, x_ref, wg_ref, wu_ref, wd_ref, y_ref):
    b = pl.program_id(0)

    @pl.when(b < nu_ref[0])
    def _():
        x = x_ref[...].astype(jnp.bfloat16)
        gt = jnp.dot(x, wg_ref[0].astype(jnp.bfloat16), preferred_element_type=jnp.float32)
        up = jnp.dot(x, wu_ref[0].astype(jnp.bfloat16), preferred_element_type=jnp.float32)
        hm = (gt * jax.nn.sigmoid(gt) * up).astype(jnp.bfloat16)
        y_ref[...] = jnp.dot(hm, wd_ref[0].astype(jnp.bfloat16), preferred_element_type=jnp.float32)

    @pl.when(b >= nu_ref[0])
    def _():
        y_ref[...] = jnp.zeros(y_ref.shape, jnp.float32)


def _experts(xs, blk_e, n_used, wg, wu, wd):
    n_slots = xs.shape[0]
    nblk = n_slots // EBLK

    def row_map(b, be, nu):
        return (jnp.minimum(b, nu[0] - 1), 0)

    def w_map(b, be, nu):
        return (be[b], 0, 0)

    grid_spec = pltpu.PrefetchScalarGridSpec(
        num_scalar_prefetch=2,
        grid=(nblk,),
        in_specs=[
            pl.BlockSpec((EBLK, D), row_map),
            pl.BlockSpec((1, D, D_EXPERT), w_map),
            pl.BlockSpec((1, D, D_EXPERT), w_map),
            pl.BlockSpec((1, D_EXPERT, D), w_map),
        ],
        out_specs=pl.BlockSpec((EBLK, D), lambda b, be, nu: (b, 0)),
    )
    return pl.pallas_call(
        _expert_kernel,
        grid_spec=grid_spec,
        out_shape=jax.ShapeDtypeStruct((n_slots, D), jnp.float32),
        compiler_params=_cparams(("arbitrary",)),
        name="experts",
    )(blk_e, n_used, xs, wg, wu, wd)


def _row_copy_gather(ys_ref, buf_ref, sem, r, d):
    return pltpu.make_async_copy(ys_ref.at[pl.ds(d, 1), :], buf_ref.at[pl.ds(r, 1), :], sem)


def _combine_kernel(d1_ref, d2_ref, x1_ref, route_ref, mod_ref, ln2g_ref, ln2b_ref, ys_ref,
                    o_ref, y1_buf, y2_buf, sem, *, alpha):
    def start(r, carry):
        _row_copy_gather(ys_ref, y1_buf, sem, r, d1_ref[0, 0, r]).start()
        _row_copy_gather(ys_ref, y2_buf, sem, r, d2_ref[0, 0, r]).start()
        return carry

    lax.fori_loop(0, TM, start, 0)

    def wait(r, carry):
        _row_copy_gather(ys_ref, y1_buf, sem, r, d1_ref[0, 0, r]).wait()
        _row_copy_gather(ys_ref, y2_buf, sem, r, d2_ref[0, 0, r]).wait()
        return carry

    lax.fori_loop(0, TM, wait, 0)

    route = route_ref[...]
    g1 = route[:, 4:5]
    g2 = route[:, 5:6]
    o = g1 * y1_buf[...] + g2 * y2_buf[...]
    z = alpha * x1_ref[...] + _mod_rows(mod_ref, 5) * o
    o_ref[...] = _ln_plain(z) * ln2g_ref[...] + ln2b_ref[...]


def _combine(x1, route, mod_l, ln2g, ln2b, ys, d1, d2, alpha):
    ttot = x1.shape[0]
    nt = ttot // TM
    smem = pl.BlockSpec((1, 1, TM), lambda i: (i, 0, 0), memory_space=pltpu.SMEM)
    kern = functools.partial(_combine_kernel, alpha=alpha)
    return pl.pallas_call(
        kern,
        grid=(nt,),
        in_specs=[smem, smem,
                  pl.BlockSpec((TM, D), lambda i: (i, 0)),
                  pl.BlockSpec((TM, LANES), lambda i: (i, 0)),
                  _full((8, N_MOD * D)), _full((1, D)), _full((1, D)),
                  pl.BlockSpec(memory_space=pl.ANY)],
        out_specs=pl.BlockSpec((TM, D), lambda i: (i, 0)),
        out_shape=jax.ShapeDtypeStruct((ttot, D), jnp.float32),
        scratch_shapes=[pltpu.VMEM((TM, D), jnp.float32), pltpu.VMEM((TM, D), jnp.float32),
                        pltpu.SemaphoreType.DMA(())],
        compiler_params=_cparams(("arbitrary",)),
        name="combine",
    )(d1, d2, x1, route, mod_l, ln2g, ln2b, ys)


def _rope_tables_t(n_lat, n_ctx):
    rows = n_lat // GRID_W
    row = jnp.broadcast_to(jnp.arange(rows)[:, None], (rows, GRID_W)).reshape(-1)
    col = jnp.broadcast_to(jnp.arange(GRID_W)[None, :], (rows, GRID_W)).reshape(-1)
    n_freq = HD // 4
    inv = ROPE_THETA ** (-jnp.arange(n_freq, dtype=jnp.float32) / n_freq)
    pos = jnp.stack([row, col], axis=-1).astype(jnp.float32)
    ang = (pos[:, :, None] * inv).reshape(n_lat, 2 * n_freq)
    cos = jnp.concatenate([jnp.ones((n_ctx, 2 * n_freq), jnp.float32), jnp.cos(ang)], axis=0)
    sin = jnp.concatenate([jnp.zeros((n_ctx, 2 * n_freq), jnp.float32), jnp.sin(ang)], axis=0)
    return cos.T, sin.T


def _row(v):
    return v.reshape(1, -1)


def kernel(x, c, ctx, c_ctx, w_mod, b_mod, w_in, q_gain, k_gain, w_pool, pool_scale, w_dw, b_dw,
           cv_ln_g, cv_ln_b, w_cv_pw, w_out, ln1_g, ln1_b, ln2_g, ln2_b, w_rg, b_rg, w_re, b_re,
           w_e_gate, w_e_up, w_e_down):
    depth = w_mod.shape[0]
    n_lat = x.shape[1]
    n_ctx = ctx.shape[1]
    assert x.shape[0] == 1 and n_ctx == TM and n_lat % TM == 0 and n_lat % GRID_W == 0
    ttot = n_ctx + n_lat
    alpha = float((2 * depth) ** 0.25)
    n_blk = -(-(2 * ttot) // EBLK) + N_EXPERTS
    n_slots = n_blk * EBLK

    xa = jnp.concatenate([ctx[0], x[0]], axis=0)
    cc = jnp.zeros((8, D), jnp.float32).at[0].set(c[0]).at[1].set(c_ctx)
    mod = _modulation(cc, w_mod, b_mod)
    cos_t, sin_t = _rope_tables_t(n_lat, n_ctx)

    o = [0, ATTN_W, ATTN_W + KV_W, ATTN_W + 2 * KV_W, ATTN_W + 2 * KV_W + POOL_W,
         ATTN_W + 2 * KV_W + POOL_W + CONV_W, ATTN_W + 2 * KV_W + POOL_W + 2 * CONV_W]
    for l in range(depth):
        w_t = w_in[l][:, :o[3]].T.astype(jnp.bfloat16)
        w_n = w_in[l][:, o[3]:].astype(jnp.bfloat16)
        qg = jnp.broadcast_to(q_gain[l][:, None], (HD, TM))
        kg = jnp.broadcast_to(k_gain[l][:, None], (HD, TM))
        q_t, k2, v_t, u, ag = _inproj(xa, mod[l], w_t, w_n, qg, kg, cos_t, sin_t)
        attn = _attention(q_t, k2, v_t)

        wpool = jnp.zeros((POOL_W, POOL_W), jnp.float32)
        for gi in range(len(POOL_WINDOWS)):
            wpool = wpool.at[gi * POOL_GW:(gi + 1) * POOL_GW,
                             gi * POOL_GW:(gi + 1) * POOL_GW].set(w_pool[l, gi])
        wdw = jnp.zeros((32, CONV_W), jnp.float32).at[:CONV_K].set(w_dw[l])
        wr = jnp.zeros((D, LANES), jnp.float32)
        wr = wr.at[:, :N_GROUPS].set(w_rg[l]).at[:, N_GROUPS:N_GROUPS + N_EXPERTS].set(w_re[l])
        br = jnp.zeros((1, LANES), jnp.float32)
        br = br.at[0, :N_GROUPS].set(b_rg[l]).at[0, N_GROUPS:N_GROUPS + N_EXPERTS].set(b_re[l])
        x1, h2, route, cnt = _mixer(
            xa, attn, u, ag, mod[l], wpool.astype(jnp.bfloat16), _row(pool_scale[l]), wdw,
            _row(b_dw[l]), _row(cv_ln_g[l]), _row(cv_ln_b[l]), w_cv_pw[l].astype(jnp.bfloat16),
            w_out[l].astype(jnp.bfloat16), _row(ln1_g[l]), _row(ln1_b[l]), wr, br, alpha)

        counts = cnt[0, :N_EXPERTS].astype(jnp.int32)
        padded = (counts + EBLK - 1) // EBLK * EBLK
        pends = jnp.cumsum(padded)
        pstarts = pends - padded
        e1 = route[:, 0].astype(jnp.int32)
        e2 = route[:, 1].astype(jnp.int32)
        d1 = (pstarts[e1] + route[:, 2].astype(jnp.int32)).reshape(ttot // TM, 1, TM)
        d2 = (pstarts[e2] + route[:, 3].astype(jnp.int32)).reshape(ttot // TM, 1, TM)
        n_used = (pends[-1] // EBLK).astype(jnp.int32).reshape(1)
        blk_start = jnp.arange(n_blk, dtype=jnp.int32) * EBLK
        blk_e = jnp.searchsorted(pends, jnp.minimum(blk_start, pends[-1] - 1), side='right')
        blk_e = jnp.minimum(blk_e, N_EXPERTS - 1).astype(jnp.int32)

        xs = _dispatch(h2, d1, d2, n_slots)
        ys = _experts(xs, blk_e, n_used, w_e_gate[l], w_e_up[l], w_e_down[l])
        xa = _combine(x1, route, mod[l], _row(ln2_g[l]), _row(ln2_b[l]), ys, d1, d2, alpha)
    return xa[n_ctx:][None]
```

```python
import functools

import jax
import jax.numpy as jnp
from jax import lax
from jax.experimental import pallas as pl
from jax.experimental.pallas import tpu as pltpu

D = 1024
GRID_W = 64
HD = 64
NQ = 8
NKV = 2
QPK = NQ // NKV
ATTN_W = NQ * HD
KV_W = NKV * HD
POOL_W = 256
POOL_WINDOWS = (2, 4, 8, 16)
POOL_GW = POOL_W // len(POOL_WINDOWS)
CONV_W = 256
CONV_K = 31
N_GROUPS = 4
EPG = 8
N_EXPERTS = N_GROUPS * EPG
D_EXPERT = 512
N_MOD = 6
ROPE_THETA = 10000.0
EPS = 1e-6

TM = 256
HALO = 16
EBLK = 256
LANES = 128
VROWS = HD + 16
LOG2E = 1.4426950408889634
BOUND_MARGIN = 1.0 + 2.0 ** -5
VMEM_LIMIT = 56 * 1024 * 1024
NEG_BIG = -1e30

_HI = lax.Precision.HIGHEST


def _cparams(sem):
    return pltpu.CompilerParams(dimension_semantics=sem, vmem_limit_bytes=VMEM_LIMIT)


def _full(shape):
    n = len(shape)
    return pl.BlockSpec(shape, lambda *a, n=n: (0,) * n)


def _mod_kernel(cc_ref, w_ref, b_ref, o_ref):
    cc = cc_ref[...]
    a = cc * jax.nn.sigmoid(cc)
    o_ref[0] = jnp.dot(a, w_ref[0], precision=_HI, preferred_element_type=jnp.float32) + b_ref[0]


def _modulation(cc, w_mod, b_mod):
    depth = w_mod.shape[0]
    nw = w_mod.shape[2]
    bn = 1024
    return pl.pallas_call(
        _mod_kernel,
        grid=(depth, nw // bn),
        in_specs=[
            pl.BlockSpec((8, D), lambda l, j: (0, 0)),
            pl.BlockSpec((1, D, bn), lambda l, j: (l, 0, j)),
            pl.BlockSpec((1, 1, bn), lambda l, j: (l, 0, j)),
        ],
        out_specs=pl.BlockSpec((1, 8, bn), lambda l, j: (l, 0, j)),
        out_shape=jax.ShapeDtypeStruct((depth, 8, nw), jnp.float32),
        compiler_params=_cparams(("arbitrary", "arbitrary")),
        name="modulation",
    )(cc, w_mod, b_mod.reshape(depth, 1, nw))


def _ln_plain(x):
    mu = jnp.mean(x, axis=-1, keepdims=True)
    xc = x - mu
    var = jnp.mean(xc * xc, axis=-1, keepdims=True)
    return xc * lax.rsqrt(var + EPS)


def _mod_rows(mod_ref, idx):
    row = jnp.where(pl.program_id(0) == 0, 1, 0)
    return mod_ref[pl.ds(row, 1), idx * D:(idx + 1) * D]


def _rms_rope_t(t, gain, cos, sin):
    ms = jnp.mean(t * t, axis=0, keepdims=True)
    t = t * lax.rsqrt(ms + EPS) * gain
    outs = []
    for a in range(2):
        t1 = t[32 * a:32 * a + 16]
        t2 = t[32 * a + 16:32 * a + 32]
        c = cos[16 * a:16 * a + 16]
        s = sin[16 * a:16 * a + 16]
        outs.append(t1 * c - t2 * s)
        outs.append(t2 * c + t1 * s)
    return jnp.concatenate(outs, axis=0)


def _inproj_kernel(x_ref, mod_ref, wt_ref, wn_ref, qg_ref, kg_ref, cos_ref, sin_ref,
                   q_ref, k_ref, v_ref, kmax_ref, u_ref, ag_ref):
    x = x_ref[...]
    h = _ln_plain(x) * (1.0 + _mod_rows(mod_ref, 1)) + _mod_rows(mod_ref, 0)
    hb = h.astype(jnp.bfloat16)
    qkv_t = lax.dot_general(wt_ref[...], hb, (((1,), (1,)), ((), ())),
                            preferred_element_type=jnp.float32)
    cos = cos_ref[...]
    sin = sin_ref[...]
    qg = qg_ref[...]
    kg = kg_ref[...]
    ones_row = jnp.where(lax.broadcasted_iota(jnp.int32, (HD, TM), 0) == 0, 1.0, 0.0)
    ones_pad = jnp.where(lax.broadcasted_iota(jnp.int32, (VROWS - HD, TM), 0) == 0, 1.0, 0.0)
    for hq in range(NQ):
        t = _rms_rope_t(qkv_t[hq * HD:(hq + 1) * HD], qg, cos, sin) * (LOG2E * HD ** -0.5)
        nq = jnp.sqrt(jnp.sum(t * t, axis=0, keepdims=True))
        q_ref[hq // QPK, hq % QPK] = jnp.concatenate(
            [t, ones_row * (-BOUND_MARGIN * nq)], axis=0).astype(jnp.bfloat16)
    for g in range(NKV):
        kt = _rms_rope_t(qkv_t[ATTN_W + g * HD:ATTN_W + (g + 1) * HD], kg, cos, sin)
        kn = jnp.max(jnp.sum(kt * kt, axis=0, keepdims=True), axis=1, keepdims=True)
        kmax_ref[0, g] = jnp.broadcast_to(kn, (8, LANES))
        k_ref[g] = jnp.concatenate([kt, ones_row], axis=0).T.astype(jnp.bfloat16)
        vt = qkv_t[ATTN_W + KV_W + g * HD:ATTN_W + KV_W + (g + 1) * HD]
        v_ref[g] = jnp.concatenate([vt, ones_pad], axis=0).astype(jnp.bfloat16)
    uag = jnp.dot(hb, wn_ref[...], preferred_element_type=jnp.float32)
    u_ref[...] = uag[:, :POOL_W]
    a = uag[:, POOL_W:POOL_W + CONV_W]
    gt = uag[:, POOL_W + CONV_W:]
    ag_ref[...] = a * jax.nn.sigmoid(gt)


def _inproj(xa, mod_l, w_t, w_n, qg, kg, cos_t, sin_t):
    ttot = xa.shape[0]
    nt = ttot // TM
    return pl.pallas_call(
        _inproj_kernel,
        grid=(nt,),
        in_specs=[
            pl.BlockSpec((TM, D), lambda i: (i, 0)),
            _full((8, N_MOD * D)),
            _full((ATTN_W + 2 * KV_W, D)),
            _full((D, POOL_W + 2 * CONV_W)),
            _full((HD, TM)),
            _full((HD, TM)),
            pl.BlockSpec((32, TM), lambda i: (0, i)),
            pl.BlockSpec((32, TM), lambda i: (0, i)),
        ],
        out_specs=[
            pl.BlockSpec((NKV, QPK, 2 * HD, TM), lambda i: (0, 0, 0, i)),
            pl.BlockSpec((NKV, TM, 2 * HD), lambda i: (0, i, 0)),
            pl.BlockSpec((NKV, VROWS, TM), lambda i: (0, 0, i)),
            pl.BlockSpec((1, NKV, 8, LANES), lambda i: (i, 0, 0, 0)),
            pl.BlockSpec((TM, POOL_W), lambda i: (i, 0)),
            pl.BlockSpec((TM, CONV_W), lambda i: (i, 0)),
        ],
        out_shape=[
            jax.ShapeDtypeStruct((NKV, QPK, 2 * HD, ttot), jnp.bfloat16),
            jax.ShapeDtypeStruct((NKV, ttot, 2 * HD), jnp.bfloat16),
            jax.ShapeDtypeStruct((NKV, VROWS, ttot), jnp.bfloat16),
            jax.ShapeDtypeStruct((nt, NKV, 8, LANES), jnp.float32),
            jax.ShapeDtypeStruct((ttot, POOL_W), jnp.float32),
            jax.ShapeDtypeStruct((ttot, CONV_W), jnp.float32),
        ],
        compiler_params=_cparams(("arbitrary",)),
        name="inproj",
    )(xa, mod_l, w_t, w_n, qg, kg, cos_t, sin_t)


TK = 256
GK = 1280
L_MIN = 2.0 ** -60


def _attn_kernel(kmax_ref, q_ref, k_ref, v_ref, o_ref, qs_ref, s_ref, acc_ref, m_ref, *,
                 n_chunks, n_groups):
    g = pl.program_id(0)
    i = pl.program_id(1)
    nch = jnp.where(i == 0, 1, n_chunks)
    row = lax.broadcasted_iota(jnp.int32, (2 * HD, TM), 0)

    kmax = kmax_ref[g]
    for hh in range(QPK):
        qf = q_ref[0, hh].astype(jnp.float32)
        qs_ref[hh] = jnp.where(row == HD, qf * kmax, qf).astype(jnp.bfloat16)
    acc_ref[...] = jnp.zeros(acc_ref.shape, jnp.float32)

    def scores(grp, slot):
        kc = k_ref[0, pl.ds(pl.multiple_of(grp * GK, GK), GK), :]
        for hh in range(QPK):
            s_ref[slot, hh] = jnp.dot(kc, qs_ref[hh], preferred_element_type=jnp.float32)

    def consume(grp, slot):
        vc = v_ref[0, :, pl.ds(pl.multiple_of(grp * GK, GK), GK)]
        for hh in range(QPK):
            p = jnp.exp2(s_ref[slot, hh]).astype(jnp.bfloat16)
            acc_ref[hh] += jnp.dot(vc, p, preferred_element_type=jnp.float32)

    @pl.when(i == 0)
    def _():
        kc = k_ref[0, 0:TM, :]
        vc = v_ref[0, :, 0:TM]
        for hh in range(QPK):
            p = jnp.exp2(jnp.dot(kc, qs_ref[hh], preferred_element_type=jnp.float32))
            acc_ref[hh] = jnp.dot(vc, p.astype(jnp.bfloat16), preferred_element_type=jnp.float32)

    @pl.when(i > 0)
    def _():
        scores(0, 0)
        n_pairs = (n_groups - 1) // 2

        def pair(j, carry):
            scores(2 * j + 1, 1)
            consume(2 * j, 0)
            scores(2 * j + 2, 0)
            consume(2 * j + 1, 1)
            return carry

        lax.fori_loop(0, n_pairs, pair, 0)
        done = 2 * n_pairs
        if n_groups - 1 - done == 1:
            scores(done + 1, 1)
            consume(done, 0)
            consume(done + 1, 1)
        else:
            consume(done, 0)

    l_min = jnp.min(acc_ref[:, HD:HD + 1, :])

    @pl.when(l_min < L_MIN)
    def _():
        for hh in range(QPK):
            qs_ref[hh] = jnp.where(row == HD, 0.0, q_ref[0, hh].astype(jnp.float32)
                                   ).astype(jnp.bfloat16)
        acc_ref[...] = jnp.zeros(acc_ref.shape, jnp.float32)
        m_ref[...] = jnp.full(m_ref.shape, NEG_BIG, jnp.float32)

        def body2(c, carry):
            start = pl.multiple_of(c * TK, TK)
            kc = k_ref[0, pl.ds(start, TK), :]
            vc = v_ref[0, :, pl.ds(start, TK)]
            for hh in range(QPK):
                s = jnp.dot(kc, qs_ref[hh], preferred_element_type=jnp.float32)
                m_old = m_ref[hh]
                m_new = jnp.maximum(m_old, jnp.max(s, axis=0, keepdims=True))
                p = jnp.exp2(s - m_new).astype(jnp.bfloat16)
                acc_ref[hh] = jnp.exp2(m_old - m_new) * acc_ref[hh] + jnp.dot(
                    vc, p, preferred_element_type=jnp.float32)
                m_ref[hh] = m_new
            return carry

        lax.fori_loop(0, nch, body2, 0)

    outs = [acc_ref[hh, 0:HD, :] / acc_ref[hh, HD:HD + 1, :] for hh in range(QPK)]
    o_ref[...] = jnp.concatenate(outs, axis=0).T


def _attention(q_t, k2, v_t, kmax):
    ttot = k2.shape[1]
    nt = ttot // TM
    assert ttot % GK == 0
    kern = functools.partial(_attn_kernel, n_chunks=ttot // TK, n_groups=ttot // GK)
    grid_spec = pltpu.PrefetchScalarGridSpec(
        num_scalar_prefetch=1,
        grid=(NKV, nt),
        in_specs=[
            pl.BlockSpec((1, QPK, 2 * HD, TM), lambda g, i, km: (g, 0, 0, i)),
            pl.BlockSpec((1, ttot, 2 * HD), lambda g, i, km: (g, 0, 0)),
            pl.BlockSpec((1, VROWS, ttot), lambda g, i, km: (g, 0, 0)),
        ],
        out_specs=pl.BlockSpec((TM, QPK * HD), lambda g, i, km: (i, g)),
        scratch_shapes=[
            pltpu.VMEM((QPK, 2 * HD, TM), jnp.bfloat16),
            pltpu.VMEM((2, QPK, GK, TM), jnp.float32),
            pltpu.VMEM((QPK, VROWS, TM), jnp.float32),
            pltpu.VMEM((QPK, 1, TM), jnp.float32),
        ],
    )
    return pl.pallas_call(
        kern,
        grid_spec=grid_spec,
        out_shape=jax.ShapeDtypeStruct((ttot, ATTN_W), jnp.float32),
        compiler_params=_cparams(("arbitrary", "arbitrary")),
        name="attention",
    )(kmax, q_t, k2, v_t)


def _fill_ext(ext_ref, cur_ref, prev_ref, next_ref, left_ok, right_ok):
    zero = jnp.zeros((HALO, cur_ref.shape[1]), jnp.float32)
    ext_ref[0:HALO, :] = jnp.where(left_ok, prev_ref[...], zero)
    ext_ref[HALO:HALO + TM, :] = cur_ref[...]
    ext_ref[HALO + TM:HALO + TM + HALO, :] = jnp.where(right_ok, next_ref[...], zero)


def _mix_kernel(x_ref, attn_ref, u_ref, up_ref, un_ref, ag_ref, agp_ref, agn_ref, mod_ref,
                wpool_ref, pscale_ref, wdw_ref, bdw_ref, cvg_ref, cvb_ref, wpw_ref, wout_ref,
                ln1g_ref, ln1b_ref, wr_ref, br_ref,
                x1_ref, h2_ref, route_ref, cnt_ref,
                uext_ref, agext_ref, base_ref, *, alpha, n_lat):
    i = pl.program_id(0)
    nt = pl.num_programs(0)
    is_ctx = i == 0
    left_ok = i >= 2
    right_ok = jnp.logical_and(i >= 1, i < nt - 1)

    @pl.when(i == 0)
    def _():
        base_ref[...] = jnp.zeros(base_ref.shape, jnp.float32)

    _fill_ext(uext_ref, u_ref, up_ref, un_ref, left_ok, right_ok)
    _fill_ext(agext_ref, ag_ref, agp_ref, agn_ref, left_ok, right_ok)

    def ush(off, rows=TM):
        return uext_ref[HALO + off:HALO + off + rows, :]

    a2 = ush(-8, TM + 15) + ush(-7, TM + 15)
    a4 = a2[0:TM + 13] + a2[2:TM + 15]
    a8 = a4[0:TM + 9] + a4[4:TM + 13]
    a16 = a8[0:TM] + a8[8:TM + 8]
    sums = (a2[7:7 + TM], a4[6:6 + TM], a8[4:4 + TM], a16)
    pos = lax.broadcasted_iota(jnp.int32, (TM, 1), 0) + jnp.where(is_ctx, 0, (i - 1) * TM)
    seq_n = jnp.where(is_ctx, TM, n_lat)
    u_cur = u_ref[...]
    lane = lax.broadcasted_iota(jnp.int32, (TM, POOL_W), 1)
    pooled = jnp.zeros((TM, POOL_W), jnp.float32)
    for gi, w in enumerate(POOL_WINDOWS):
        lo = jnp.maximum(pos - w // 2, 0)
        hi = jnp.minimum(pos + (w - w // 2), seq_n)
        inv = 1.0 / (hi - lo).astype(jnp.float32)
        in_group = jnp.logical_and(lane >= gi * POOL_GW, lane < (gi + 1) * POOL_GW)
        pooled = jnp.where(in_group, sums[gi] * inv - u_cur, pooled)
    y_pool = jnp.dot(pooled.astype(jnp.bfloat16), wpool_ref[...],
                     preferred_element_type=jnp.float32) * pscale_ref[...]

    conv = jnp.zeros((TM, CONV_W), jnp.float32) + bdw_ref[...]
    for k in range(CONV_K):
        off = HALO + k - CONV_K // 2
        conv = conv + agext_ref[off:off + TM, :] * wdw_ref[k:k + 1, :]
    cn = _ln_plain(conv) * cvg_ref[...] + cvb_ref[...]
    cn = cn * jax.nn.sigmoid(cn)
    y_conv = jnp.dot(cn.astype(jnp.bfloat16), wpw_ref[...], preferred_element_type=jnp.float32)

    y = jnp.dot(attn_ref[...].astype(jnp.bfloat16), wout_ref[0:ATTN_W, :],
                preferred_element_type=jnp.float32)
    y = y + jnp.dot(y_pool.astype(jnp.bfloat16), wout_ref[ATTN_W:ATTN_W + POOL_W, :],
                    preferred_element_type=jnp.float32)
    y = y + jnp.dot(y_conv.astype(jnp.bfloat16), wout_ref[ATTN_W + POOL_W:, :],
                    preferred_element_type=jnp.float32)
    x1 = _ln_plain(alpha * x_ref[...] + _mod_rows(mod_ref, 2) * y) * ln1g_ref[...] + ln1b_ref[...]
    x1_ref[...] = x1
    h2 = _ln_plain(x1) * (1.0 + _mod_rows(mod_ref, 4)) + _mod_rows(mod_ref, 3)
    h2_ref[...] = h2

    logits = jnp.dot(h2, wr_ref[...], precision=_HI, preferred_element_type=jnp.float32) + br_ref[...]
    ln = lax.broadcasted_iota(jnp.int32, (TM, LANES), 1)
    neg = jnp.float32(-jnp.inf)
    is_g = ln < N_GROUPS
    lg = jnp.where(is_g, logits, neg)
    mg = jnp.max(lg, axis=-1, keepdims=True)
    gsel = jnp.min(jnp.where(lg == mg, ln, LANES), axis=-1, keepdims=True)
    p_group = 1.0 / jnp.sum(jnp.exp(lg - mg), axis=-1, keepdims=True)
    e_lane = ln - N_GROUPS
    in_sel = jnp.logical_and(e_lane >= gsel * EPG, e_lane < (gsel + 1) * EPG)
    le = jnp.where(in_sel, logits, neg)
    v1 = jnp.max(le, axis=-1, keepdims=True)
    e1 = jnp.min(jnp.where(le == v1, e_lane, LANES), axis=-1, keepdims=True)
    le2 = jnp.where(e_lane == e1, neg, le)
    v2 = jnp.max(le2, axis=-1, keepdims=True)
    e2 = jnp.min(jnp.where(le2 == v2, e_lane, LANES), axis=-1, keepdims=True)
    ex = jnp.exp(v2 - v1)
    gate1 = p_group / (1.0 + ex)
    gate2 = p_group * ex / (1.0 + ex)

    oh = jnp.logical_or(ln == e1, ln == e2)
    ohb = jnp.where(oh, 1.0, 0.0).astype(jnp.bfloat16)
    r_i = lax.broadcasted_iota(jnp.int32, (TM, TM), 0)
    c_i = lax.broadcasted_iota(jnp.int32, (TM, TM), 1)
    tri = jnp.where(c_i < r_i, 1.0, 0.0).astype(jnp.bfloat16)
    before = jnp.dot(tri, ohb, preferred_element_type=jnp.float32) + base_ref[...]
    rank1 = jnp.sum(jnp.where(ln == e1, before, 0.0), axis=-1, keepdims=True)
    rank2 = jnp.sum(jnp.where(ln == e2, before, 0.0), axis=-1, keepdims=True)
    base_ref[...] = base_ref[...] + jnp.sum(ohb.astype(jnp.float32), axis=0, keepdims=True)
    cnt_ref[...] = jnp.broadcast_to(base_ref[...], cnt_ref.shape)

    route = jnp.zeros((TM, LANES), jnp.float32)
    for j, val in enumerate((e1.astype(jnp.float32), e2.astype(jnp.float32), rank1, rank2,
                             gate1, gate2)):
        route = jnp.where(ln == j, val, route)
    route_ref[...] = route


def _mixer(xa, attn, u, ag, mod_l, wpool, pscale, wdw, bdw, cvg, cvb, wpw, wout, ln1g, ln1b,
           wr, br, alpha):
    ttot = xa.shape[0]
    nt = ttot // TM
    hb = TM // HALO
    nhb = ttot // HALO

    def tile(w):
        return pl.BlockSpec((TM, w), lambda i: (i, 0))

    def prev(w):
        return pl.BlockSpec((HALO, w), lambda i: (jnp.maximum(i * hb - 1, 0), 0))

    def nxt(w):
        return pl.BlockSpec((HALO, w), lambda i: (jnp.minimum((i + 1) * hb, nhb - 1), 0))

    kern = functools.partial(_mix_kernel, alpha=alpha, n_lat=ttot - TM)
    return pl.pallas_call(
        kern,
        grid=(nt,),
        in_specs=[
            tile(D), tile(ATTN_W),
            tile(POOL_W), prev(POOL_W), nxt(POOL_W),
            tile(CONV_W), prev(CONV_W), nxt(CONV_W),
            _full((8, N_MOD * D)),
            _full((POOL_W, POOL_W)), _full((1, POOL_W)),
            _full((32, CONV_W)), _full((1, CONV_W)), _full((1, CONV_W)), _full((1, CONV_W)),
            _full((CONV_W, CONV_W)), _full((D, D)),
            _full((1, D)), _full((1, D)),
            _full((D, LANES)), _full((1, LANES)),
        ],
        out_specs=[tile(D), tile(D), tile(LANES), _full((8, LANES))],
        out_shape=[
            jax.ShapeDtypeStruct((ttot, D), jnp.float32),
            jax.ShapeDtypeStruct((ttot, D), jnp.float32),
            jax.ShapeDtypeStruct((ttot, LANES), jnp.float32),
            jax.ShapeDtypeStruct((8, LANES), jnp.float32),
        ],
        scratch_shapes=[
            pltpu.VMEM((TM + 2 * HALO, POOL_W), jnp.float32),
            pltpu.VMEM((TM + 2 * HALO, CONV_W), jnp.float32),
            pltpu.VMEM((1, LANES), jnp.float32),
        ],
        compiler_params=_cparams(("arbitrary",)),
        name="mixer",
    )(xa, attn, u, u, u, ag, ag, ag, mod_l, wpool, pscale, wdw, bdw, cvg, cvb, wpw, wout,
      ln1g, ln1b, wr, br)


def _row_copy_scatter(h_ref, xs_ref, sem, r, d):
    return pltpu.make_async_copy(h_ref.at[pl.ds(r, 1), :], xs_ref.at[pl.ds(d, 1), :], sem)


def _dispatch_kernel(d1_ref, d2_ref, h_ref, xs_in_ref, xs_ref, sem):
    del xs_in_ref

    def start(r, carry):
        _row_copy_scatter(h_ref, xs_ref, sem, r, d1_ref[0, 0, r]).start()
        _row_copy_scatter(h_ref, xs_ref, sem, r, d2_ref[0, 0, r]).start()
        return carry

    lax.fori_loop(0, TM, start, 0)

    def wait(r, carry):
        _row_copy_scatter(h_ref, xs_ref, sem, r, d1_ref[0, 0, r]).wait()
        _row_copy_scatter(h_ref, xs_ref, sem, r, d2_ref[0, 0, r]).wait()
        return carry

    lax.fori_loop(0, TM, wait, 0)


def _dispatch(h2, d1, d2, n_slots):
    ttot = h2.shape[0]
    nt = ttot // TM
    smem = pl.BlockSpec((1, 1, TM), lambda i: (i, 0, 0), memory_space=pltpu.SMEM)
    xs0 = jnp.zeros((n_slots, D), jnp.float32)
    return pl.pallas_call(
        _dispatch_kernel,
        grid=(nt,),
        in_specs=[smem, smem, pl.BlockSpec((TM, D), lambda i: (i, 0)),
                  pl.BlockSpec(memory_space=pl.ANY)],
        out_specs=pl.BlockSpec(memory_space=pl.ANY),
        out_shape=jax.ShapeDtypeStruct((n_slots, D), jnp.float32),
        scratch_shapes=[pltpu.SemaphoreType.DMA(())],
        input_output_aliases={3: 0},
        compiler_params=_cparams(("arbitrary",)),
        name="dispatch",
    )(d1, d2, h2, xs0)


def _expert_kernel(be_ref, nu_ref, x_ref, wg_ref, wu_ref, wd_ref, y_ref):
    b = pl.program_id(0)

    @pl.when(b < nu_ref[0])
    def _():
        x = x_ref[...].astype(jnp.bfloat16)
        gt = jnp.dot(x, wg_ref[0].astype(jnp.bfloat16), preferred_element_type=jnp.float32)
        up = jnp.dot(x, wu_ref[0].astype(jnp.bfloat16), preferred_element_type=jnp.float32)
        hm = (gt * jax.nn.sigmoid(gt) * up).astype(jnp.bfloat16)
        y_ref[...] = jnp.dot(hm, wd_ref[0].astype(jnp.bfloat16), preferred_element_type=jnp.float32)

    @pl.when(b >= nu_ref[0])
    def _():
        y_ref[...] = jnp.zeros(y_ref.shape, jnp.float32)


def _experts(xs, blk_e, n_used, wg, wu, wd):
    n_slots = xs.shape[0]
    nblk = n_slots // EBLK

    def row_map(b, be, nu):
        return (jnp.minimum(b, nu[0] - 1), 0)

    def w_map(b, be, nu):
        return (be[b], 0, 0)

    grid_spec = pltpu.PrefetchScalarGridSpec(
        num_scalar_prefetch=2,
        grid=(nblk,),
        in_specs=[
            pl.BlockSpec((EBLK, D), row_map),
            pl.BlockSpec((1, D, D_EXPERT), w_map),
            pl.BlockSpec((1, D, D_EXPERT), w_map),
            pl.BlockSpec((1, D_EXPERT, D), w_map),
        ],
        out_specs=pl.BlockSpec((EBLK, D), lambda b, be, nu: (b, 0)),
    )
    return pl.pallas_call(
        _expert_kernel,
        grid_spec=grid_spec,
        out_shape=jax.ShapeDtypeStruct((n_slots, D), jnp.float32),
        compiler_params=_cparams(("arbitrary",)),
        name="experts",
    )(blk_e, n_used, xs, wg, wu, wd)


def _row_copy_gather(ys_ref, buf_ref, sem, r, d):
    return pltpu.make_async_copy(ys_ref.at[pl.ds(d, 1), :], buf_ref.at[pl.ds(r, 1), :], sem)


def _combine_kernel(d1_ref, d2_ref, x1_ref, route_ref, mod_ref, ln2g_ref, ln2b_ref, ys_ref,
                    o_ref, y1_buf, y2_buf, sem, *, alpha):
    def start(r, carry):
        _row_copy_gather(ys_ref, y1_buf, sem, r, d1_ref[0, 0, r]).start()
        _row_copy_gather(ys_ref, y2_buf, sem, r, d2_ref[0, 0, r]).start()
        return carry

    lax.fori_loop(0, TM, start, 0)

    def wait(r, carry):
        _row_copy_gather(ys_ref, y1_buf, sem, r, d1_ref[0, 0, r]).wait()
        _row_copy_gather(ys_ref, y2_buf, sem, r, d2_ref[0, 0, r]).wait()
        return carry

    lax.fori_loop(0, TM, wait, 0)

    route = route_ref[...]
    g1 = route[:, 4:5]
    g2 = route[:, 5:6]
    o = g1 * y1_buf[...] + g2 * y2_buf[...]
    z = alpha * x1_ref[...] + _mod_rows(mod_ref, 5) * o
    o_ref[...] = _ln_plain(z) * ln2g_ref[...] + ln2b_ref[...]


def _combine(x1, route, mod_l, ln2g, ln2b, ys, d1, d2, alpha):
    ttot = x1.shape[0]
    nt = ttot // TM
    smem = pl.BlockSpec((1, 1, TM), lambda i: (i, 0, 0), memory_space=pltpu.SMEM)
    kern = functools.partial(_combine_kernel, alpha=alpha)
    return pl.pallas_call(
        kern,
        grid=(nt,),
        in_specs=[smem, smem,
                  pl.BlockSpec((TM, D), lambda i: (i, 0)),
                  pl.BlockSpec((TM, LANES), lambda i: (i, 0)),
                  _full((8, N_MOD * D)), _full((1, D)), _full((1, D)),
                  pl.BlockSpec(memory_space=pl.ANY)],
        out_specs=pl.BlockSpec((TM, D), lambda i: (i, 0)),
        out_shape=jax.ShapeDtypeStruct((ttot, D), jnp.float32),
        scratch_shapes=[pltpu.VMEM((TM, D), jnp.float32), pltpu.VMEM((TM, D), jnp.float32),
                        pltpu.SemaphoreType.DMA(())],
        compiler_params=_cparams(("arbitrary",)),
        name="combine",
    )(d1, d2, x1, route, mod_l, ln2g, ln2b, ys)


def _rope_tables_t(n_lat, n_ctx):
    rows = n_lat // GRID_W
    row = jnp.broadcast_to(jnp.arange(rows)[:, None], (rows, GRID_W)).reshape(-1)
    col = jnp.broadcast_to(jnp.arange(GRID_W)[None, :], (rows, GRID_W)).reshape(-1)
    n_freq = HD // 4
    inv = ROPE_THETA ** (-jnp.arange(n_freq, dtype=jnp.float32) / n_freq)
    pos = jnp.stack([row, col], axis=-1).astype(jnp.float32)
    ang = (pos[:, :, None] * inv).reshape(n_lat, 2 * n_freq)
    cos = jnp.concatenate([jnp.ones((n_ctx, 2 * n_freq), jnp.float32), jnp.cos(ang)], axis=0)
    sin = jnp.concatenate([jnp.zeros((n_ctx, 2 * n_freq), jnp.float32), jnp.sin(ang)], axis=0)
    return cos.T, sin.T


def _row(v):
    return v.reshape(1, -1)


def kernel(x, c, ctx, c_ctx, w_mod, b_mod, w_in, q_gain, k_gain, w_pool, pool_scale, w_dw, b_dw,
           cv_ln_g, cv_ln_b, w_cv_pw, w_out, ln1_g, ln1_b, ln2_g, ln2_b, w_rg, b_rg, w_re, b_re,
           w_e_gate, w_e_up, w_e_down):
    depth = w_mod.shape[0]
    n_lat = x.shape[1]
    n_ctx = ctx.shape[1]
    assert x.shape[0] == 1 and n_ctx == TM and n_lat % TM == 0 and n_lat % GRID_W == 0
    ttot = n_ctx + n_lat
    alpha = float((2 * depth) ** 0.25)
    n_blk = -(-(2 * ttot) // EBLK) + N_EXPERTS
    n_slots = n_blk * EBLK

    xa = jnp.concatenate([ctx[0], x[0]], axis=0)
    cc = jnp.zeros((8, D), jnp.float32).at[0].set(c[0]).at[1].set(c_ctx)
    mod = _modulation(cc, w_mod, b_mod)
    cos_t, sin_t = _rope_tables_t(n_lat, n_ctx)

    o = [0, ATTN_W, ATTN_W + KV_W, ATTN_W + 2 * KV_W, ATTN_W + 2 * KV_W + POOL_W,
         ATTN_W + 2 * KV_W + POOL_W + CONV_W, ATTN_W + 2 * KV_W + POOL_W + 2 * CONV_W]
    for l in range(depth):
        w_t = w_in[l][:, :o[3]].T.astype(jnp.bfloat16)
        w_n = w_in[l][:, o[3]:].astype(jnp.bfloat16)
        qg = jnp.broadcast_to(q_gain[l][:, None], (HD, TM))
        kg = jnp.broadcast_to(k_gain[l][:, None], (HD, TM))
        q_t, k2, v_t, kn2, u, ag = _inproj(xa, mod[l], w_t, w_n, qg, kg, cos_t, sin_t)
        kmax = jnp.sqrt(jnp.max(kn2[:, :, 0, 0], axis=0))
        attn = _attention(q_t, k2, v_t, kmax)

        wpool = jnp.zeros((POOL_W, POOL_W), jnp.float32)
        for gi in range(len(POOL_WINDOWS)):
            wpool = wpool.at[gi * POOL_GW:(gi + 1) * POOL_GW,
                             gi * POOL_GW:(gi + 1) * POOL_GW].set(w_pool[l, gi])
        wdw = jnp.zeros((32, CONV_W), jnp.float32).at[:CONV_K].set(w_dw[l])
        wr = jnp.zeros((D, LANES), jnp.float32)
        wr = wr.at[:, :N_GROUPS].set(w_rg[l]).at[:, N_GROUPS:N_GROUPS + N_EXPERTS].set(w_re[l])
        br = jnp.zeros((1, LANES), jnp.float32)
        br = br.at[0, :N_GROUPS].set(b_rg[l]).at[0, N_GROUPS:N_GROUPS + N_EXPERTS].set(b_re[l])
        x1, h2, route, cnt = _mixer(
            xa, attn, u, ag, mod[l], wpool.astype(jnp.bfloat16), _row(pool_scale[l]), wdw,
            _row(b_dw[l]), _row(cv_ln_g[l]), _row(cv_ln_b[l]), w_cv_pw[l].astype(jnp.bfloat16),
            w_out[l].astype(jnp.bfloat16), _row(ln1_g[l]), _row(ln1_b[l]), wr, br, alpha)

        counts = cnt[0, :N_EXPERTS].astype(jnp.int32)
        padded = (counts + EBLK - 1) // EBLK * EBLK
        pends = jnp.cumsum(padded)
        pstarts = pends - padded
        e1 = route[:, 0].astype(jnp.int32)
        e2 = route[:, 1].astype(jnp.int32)
        d1 = (pstarts[e1] + route[:, 2].astype(jnp.int32)).reshape(ttot // TM, 1, TM)
        d2 = (pstarts[e2] + route[:, 3].astype(jnp.int32)).reshape(ttot // TM, 1, TM)
        n_used = (pends[-1] // EBLK).astype(jnp.int32).reshape(1)
        blk_start = jnp.arange(n_blk, dtype=jnp.int32) * EBLK
        blk_e = jnp.searchsorted(pends, jnp.minimum(blk_start, pends[-1] - 1), side='right')
        blk_e = jnp.minimum(blk_e, N_EXPERTS - 1).astype(jnp.int32)

        xs = _dispatch(h2, d1, d2, n_slots)
        ys = _experts(xs, blk_e, n_used, w_e_gate[l], w_e_up[l], w_e_down[l])
        xa = _combine(x1, route, mod[l], _row(ln2_g[l]), _row(ln2_b[l]), ys, d1, d2, alpha)
    return xa[n_ctx:][None]
```

```python
import functools

import jax
import jax.numpy as jnp
from jax import lax
from jax.experimental import pallas as pl
from jax.experimental.pallas import tpu as pltpu

D = 1024
GRID_W = 64
HD = 64
NQ = 8
NKV = 2
QPK = NQ // NKV
ATTN_W = NQ * HD
KV_W = NKV * HD
POOL_W = 256
POOL_WINDOWS = (2, 4, 8, 16)
POOL_GW = POOL_W // len(POOL_WINDOWS)
CONV_W = 256
CONV_K = 31
N_GROUPS = 4
EPG = 8
N_EXPERTS = N_GROUPS * EPG
D_EXPERT = 512
N_MOD = 6
ROPE_THETA = 10000.0
EPS = 1e-6

TM = 256
HALO = 16
EBLK = 256
LANES = 128
VROWS = HD + 16
LOG2E = 1.4426950408889634
BOUND_MARGIN = 1.0 + 2.0 ** -5
VMEM_LIMIT = 56 * 1024 * 1024
NEG_BIG = -1e30

_HI = lax.Precision.HIGHEST


def _cparams(sem):
    return pltpu.CompilerParams(dimension_semantics=sem, vmem_limit_bytes=VMEM_LIMIT)


def _full(shape):
    n = len(shape)
    return pl.BlockSpec(shape, lambda *a, n=n: (0,) * n)


def _layer(shape, l):
    n = len(shape)
    return pl.BlockSpec((None,) + tuple(shape), lambda *a, n=n: (l,) + (0,) * n)


def _mod_kernel(cc_ref, w_ref, b_ref, o_ref):
    cc = cc_ref[...]
    a = cc * jax.nn.sigmoid(cc)
    o_ref[0] = jnp.dot(a, w_ref[0], precision=_HI, preferred_element_type=jnp.float32) + b_ref[0]


def _modulation(cc, w_mod, b_mod):
    depth = w_mod.shape[0]
    nw = w_mod.shape[2]
    bn = 1024
    return pl.pallas_call(
        _mod_kernel,
        grid=(depth, nw // bn),
        in_specs=[
            pl.BlockSpec((8, D), lambda l, j: (0, 0)),
            pl.BlockSpec((1, D, bn), lambda l, j: (l, 0, j)),
            pl.BlockSpec((1, 1, bn), lambda l, j: (l, 0, j)),
        ],
        out_specs=pl.BlockSpec((1, 8, bn), lambda l, j: (l, 0, j)),
        out_shape=jax.ShapeDtypeStruct((depth, 8, nw), jnp.float32),
        compiler_params=_cparams(("arbitrary", "arbitrary")),
        name="modulation",
    )(cc, w_mod, b_mod.reshape(depth, 1, nw))


def _ln_plain(x):
    mu = jnp.mean(x, axis=-1, keepdims=True)
    xc = x - mu
    var = jnp.mean(xc * xc, axis=-1, keepdims=True)
    return xc * lax.rsqrt(var + EPS)


def _mod_rows(mod_ref, idx):
    row = jnp.where(pl.program_id(0) == 0, 1, 0)
    return mod_ref[pl.ds(row, 1), idx * D:(idx + 1) * D]


def _rms_rope_t(t, gain, cos, sin):
    ms = jnp.mean(t * t, axis=0, keepdims=True)
    t = t * lax.rsqrt(ms + EPS) * gain
    outs = []
    for a in range(2):
        t1 = t[32 * a:32 * a + 16]
        t2 = t[32 * a + 16:32 * a + 32]
        c = cos[16 * a:16 * a + 16]
        s = sin[16 * a:16 * a + 16]
        outs.append(t1 * c - t2 * s)
        outs.append(t2 * c + t1 * s)
    return jnp.concatenate(outs, axis=0)


def _inproj_kernel(x_ref, mod_ref, wt_ref, wn_ref, qg_ref, kg_ref, cos_ref, sin_ref,
                   q_ref, k_ref, v_ref, kmax_ref, u_ref, ag_ref):
    x = x_ref[...]
    h = _ln_plain(x) * (1.0 + _mod_rows(mod_ref, 1)) + _mod_rows(mod_ref, 0)
    hb = h.astype(jnp.bfloat16)
    qkv_t = lax.dot_general(wt_ref[...], hb, (((1,), (1,)), ((), ())),
                            preferred_element_type=jnp.float32)
    cos = cos_ref[...]
    sin = sin_ref[...]
    qg = qg_ref[...]
    kg = kg_ref[...]
    ones_row = jnp.where(lax.broadcasted_iota(jnp.int32, (HD, TM), 0) == 0, 1.0, 0.0)
    ones_pad = jnp.where(lax.broadcasted_iota(jnp.int32, (VROWS - HD, TM), 0) == 0, 1.0, 0.0)
    for hq in range(NQ):
        t = _rms_rope_t(qkv_t[hq * HD:(hq + 1) * HD], qg, cos, sin) * (LOG2E * HD ** -0.5)
        nq = jnp.sqrt(jnp.sum(t * t, axis=0, keepdims=True))
        q_ref[hq // QPK, hq % QPK] = jnp.concatenate(
            [t, ones_row * (-BOUND_MARGIN * nq)], axis=0).astype(jnp.bfloat16)
    for g in range(NKV):
        kt = _rms_rope_t(qkv_t[ATTN_W + g * HD:ATTN_W + (g + 1) * HD], kg, cos, sin)
        kn = jnp.max(jnp.sum(kt * kt, axis=0, keepdims=True), axis=1, keepdims=True)
        kmax_ref[0, g] = jnp.broadcast_to(kn, (8, LANES))
        k_ref[g] = jnp.concatenate([kt, ones_row], axis=0).T.astype(jnp.bfloat16)
        vt = qkv_t[ATTN_W + KV_W + g * HD:ATTN_W + KV_W + (g + 1) * HD]
        v_ref[g] = jnp.concatenate([vt, ones_pad], axis=0).astype(jnp.bfloat16)
    uag = jnp.dot(hb, wn_ref[...], preferred_element_type=jnp.float32)
    u_ref[...] = uag[:, :POOL_W]
    a = uag[:, POOL_W:POOL_W + CONV_W]
    gt = uag[:, POOL_W + CONV_W:]
    ag_ref[...] = a * jax.nn.sigmoid(gt)


def _inproj(xa, mod, w_t, w_n, qg, kg, cos_t, sin_t, l):
    ttot = xa.shape[0]
    nt = ttot // TM
    return pl.pallas_call(
        _inproj_kernel,
        grid=(nt,),
        in_specs=[
            pl.BlockSpec((TM, D), lambda i: (i, 0)),
            _layer((8, N_MOD * D), l),
            _layer((ATTN_W + 2 * KV_W, D), l),
            _layer((D, POOL_W + 2 * CONV_W), l),
            _layer((HD, TM), l),
            _layer((HD, TM), l),
            pl.BlockSpec((32, TM), lambda i: (0, i)),
            pl.BlockSpec((32, TM), lambda i: (0, i)),
        ],
        out_specs=[
            pl.BlockSpec((NKV, QPK, 2 * HD, TM), lambda i: (0, 0, 0, i)),
            pl.BlockSpec((NKV, TM, 2 * HD), lambda i: (0, i, 0)),
            pl.BlockSpec((NKV, VROWS, TM), lambda i: (0, 0, i)),
            pl.BlockSpec((1, NKV, 8, LANES), lambda i: (i, 0, 0, 0)),
            pl.BlockSpec((TM, POOL_W), lambda i: (i, 0)),
            pl.BlockSpec((TM, CONV_W), lambda i: (i, 0)),
        ],
        out_shape=[
            jax.ShapeDtypeStruct((NKV, QPK, 2 * HD, ttot), jnp.bfloat16),
            jax.ShapeDtypeStruct((NKV, ttot, 2 * HD), jnp.bfloat16),
            jax.ShapeDtypeStruct((NKV, VROWS, ttot), jnp.bfloat16),
            jax.ShapeDtypeStruct((nt, NKV, 8, LANES), jnp.float32),
            jax.ShapeDtypeStruct((ttot, POOL_W), jnp.float32),
            jax.ShapeDtypeStruct((ttot, CONV_W), jnp.float32),
        ],
        compiler_params=_cparams(("arbitrary",)),
        name="inproj",
    )(xa, mod, w_t, w_n, qg, kg, cos_t, sin_t)


TK = 256
GK = 1280
L_MIN = 2.0 ** -60


def _attn_kernel(kmax_ref, q_ref, k_ref, v_ref, o_ref, qs_ref, s_ref, acc_ref, m_ref, *,
                 n_chunks, n_groups):
    g = pl.program_id(0)
    i = pl.program_id(1)
    nch = jnp.where(i == 0, 1, n_chunks)
    row = lax.broadcasted_iota(jnp.int32, (2 * HD, TM), 0)
    kmax = kmax_ref[g]

    def prep(src_ref):
        for hh in range(QPK):
            qf = src_ref[0, hh].astype(jnp.float32)
            qs_ref[hh] = jnp.where(row == HD, qf * kmax, qf).astype(jnp.bfloat16)

    prep(q_ref)
    acc_ref[...] = jnp.zeros(acc_ref.shape, jnp.float32)

    def step(nxt, cur):
        for kk in range(GK // TK):
            rows = slice(kk * TK, (kk + 1) * TK)
            if nxt is not None:
                kc = k_ref[0, pl.ds(pl.multiple_of(nxt[0] * GK + kk * TK, TK), TK), :]
            if cur is not None:
                vc = v_ref[0, :, pl.ds(pl.multiple_of(cur[0] * GK + kk * TK, TK), TK)]
            for hh in range(QPK):
                if nxt is not None:
                    s_ref[nxt[1], hh, rows, :] = jnp.dot(
                        kc, qs_ref[hh], preferred_element_type=jnp.float32)
                if cur is not None:
                    p = jnp.exp2(s_ref[cur[1], hh, rows, :]).astype(jnp.bfloat16)
                    acc_ref[hh] += jnp.dot(vc, p, preferred_element_type=jnp.float32)

    @pl.when(i == 0)
    def _():
        kc = k_ref[0, 0:TM, :]
        vc = v_ref[0, :, 0:TM]
        for hh in range(QPK):
            p = jnp.exp2(jnp.dot(kc, qs_ref[hh], preferred_element_type=jnp.float32))
            acc_ref[hh] = jnp.dot(vc, p.astype(jnp.bfloat16), preferred_element_type=jnp.float32)

    @pl.when(i > 0)
    def _():
        step((0, 0), None)
        n_pairs = (n_groups - 1) // 2

        def pair(j, carry):
            step((2 * j + 1, 1), (2 * j, 0))
            step((2 * j + 2, 0), (2 * j + 1, 1))
            return carry

        lax.fori_loop(0, n_pairs, pair, 0)
        for n in range(2 * n_pairs, n_groups - 1):
            step((n + 1, (n + 1) % 2), (n, n % 2))
        step(None, (n_groups - 1, (n_groups - 1) % 2))

    l_min = jnp.min(acc_ref[:, HD:HD + 1, :])

    @pl.when(l_min < L_MIN)
    def _():
        for hh in range(QPK):
            qs_ref[hh] = jnp.where(row == HD, 0.0, q_ref[0, hh].astype(jnp.float32)
                                   ).astype(jnp.bfloat16)
        acc_ref[...] = jnp.zeros(acc_ref.shape, jnp.float32)
        m_ref[...] = jnp.full(m_ref.shape, NEG_BIG, jnp.float32)

        def body2(c, carry):
            start = pl.multiple_of(c * TK, TK)
            kc = k_ref[0, pl.ds(start, TK), :]
            vc = v_ref[0, :, pl.ds(start, TK)]
            for hh in range(QPK):
                s = jnp.dot(kc, qs_ref[hh], preferred_element_type=jnp.float32)
                m_old = m_ref[hh]
                m_new = jnp.maximum(m_old, jnp.max(s, axis=0, keepdims=True))
                p = jnp.exp2(s - m_new).astype(jnp.bfloat16)
                acc_ref[hh] = jnp.exp2(m_old - m_new) * acc_ref[hh] + jnp.dot(
                    vc, p, preferred_element_type=jnp.float32)
                m_ref[hh] = m_new
            return carry

        lax.fori_loop(0, nch, body2, 0)

    outs = [acc_ref[hh, 0:HD, :] / acc_ref[hh, HD:HD + 1, :] for hh in range(QPK)]
    o_ref[...] = jnp.concatenate(outs, axis=0).T


def _attention(q_t, k2, v_t, kmax):
    ttot = k2.shape[1]
    nt = ttot // TM
    assert ttot % GK == 0 and GK % TK == 0
    kern = functools.partial(_attn_kernel, n_chunks=ttot // TK, n_groups=ttot // GK)
    grid_spec = pltpu.PrefetchScalarGridSpec(
        num_scalar_prefetch=1,
        grid=(NKV, nt),
        in_specs=[
            pl.BlockSpec((1, QPK, 2 * HD, TM), lambda g, i, km: (g, 0, 0, i)),
            pl.BlockSpec((1, ttot, 2 * HD), lambda g, i, km: (g, 0, 0)),
            pl.BlockSpec((1, VROWS, ttot), lambda g, i, km: (g, 0, 0)),
        ],
        out_specs=pl.BlockSpec((TM, QPK * HD), lambda g, i, km: (i, g)),
        scratch_shapes=[
            pltpu.VMEM((QPK, 2 * HD, TM), jnp.bfloat16),
            pltpu.VMEM((2, QPK, GK, TM), jnp.float32),
            pltpu.VMEM((QPK, VROWS, TM), jnp.float32),
            pltpu.VMEM((QPK, 1, TM), jnp.float32),
        ],
    )
    return pl.pallas_call(
        kern,
        grid_spec=grid_spec,
        out_shape=jax.ShapeDtypeStruct((ttot, ATTN_W), jnp.float32),
        compiler_params=_cparams(("arbitrary", "arbitrary")),
        name="attention",
    )(kmax, q_t, k2, v_t)


def _fill_ext(ext_ref, cur_ref, prev_ref, next_ref, left_ok, right_ok):
    zero = jnp.zeros((HALO, cur_ref.shape[1]), jnp.float32)
    ext_ref[0:HALO, :] = jnp.where(left_ok, prev_ref[...], zero)
    ext_ref[HALO:HALO + TM, :] = cur_ref[...]
    ext_ref[HALO + TM:HALO + TM + HALO, :] = jnp.where(right_ok, next_ref[...], zero)


def _mix_kernel(x_ref, attn_ref, u_ref, up_ref, un_ref, ag_ref, agp_ref, agn_ref, mod_ref,
                wpool_ref, pscale_ref, wdw_ref, bdw_ref, cvg_ref, cvb_ref, wpw_ref, wout_ref,
                ln1g_ref, ln1b_ref, wr_ref, br_ref,
                x1_ref, h2_ref, route_ref, cnt_ref,
                uext_ref, agext_ref, base_ref, *, alpha, n_lat):
    i = pl.program_id(0)
    nt = pl.num_programs(0)
    is_ctx = i == 0
    left_ok = i >= 2
    right_ok = jnp.logical_and(i >= 1, i < nt - 1)

    @pl.when(i == 0)
    def _():
        base_ref[...] = jnp.zeros(base_ref.shape, jnp.float32)

    _fill_ext(uext_ref, u_ref, up_ref, un_ref, left_ok, right_ok)
    _fill_ext(agext_ref, ag_ref, agp_ref, agn_ref, left_ok, right_ok)

    def ush(off, rows=TM):
        return uext_ref[HALO + off:HALO + off + rows, :]

    a2 = ush(-8, TM + 15) + ush(-7, TM + 15)
    a4 = a2[0:TM + 13] + a2[2:TM + 15]
    a8 = a4[0:TM + 9] + a4[4:TM + 13]
    a16 = a8[0:TM] + a8[8:TM + 8]
    sums = (a2[7:7 + TM], a4[6:6 + TM], a8[4:4 + TM], a16)
    pos = lax.broadcasted_iota(jnp.int32, (TM, 1), 0) + jnp.where(is_ctx, 0, (i - 1) * TM)
    seq_n = jnp.where(is_ctx, TM, n_lat)
    u_cur = u_ref[...]
    lane = lax.broadcasted_iota(jnp.int32, (TM, POOL_W), 1)
    pooled = jnp.zeros((TM, POOL_W), jnp.float32)
    for gi, w in enumerate(POOL_WINDOWS):
        lo = jnp.maximum(pos - w // 2, 0)
        hi = jnp.minimum(pos + (w - w // 2), seq_n)
        inv = 1.0 / (hi - lo).astype(jnp.float32)
        in_group = jnp.logical_and(lane >= gi * POOL_GW, lane < (gi + 1) * POOL_GW)
        pooled = jnp.where(in_group, sums[gi] * inv - u_cur, pooled)
    y_pool = jnp.dot(pooled.astype(jnp.bfloat16), wpool_ref[...],
                     preferred_element_type=jnp.float32) * pscale_ref[...]

    conv = jnp.zeros((TM, CONV_W), jnp.float32) + bdw_ref[...]
    for k in range(CONV_K):
        off = HALO + k - CONV_K // 2
        conv = conv + agext_ref[off:off + TM, :] * wdw_ref[k:k + 1, :]
    cn = _ln_plain(conv) * cvg_ref[...] + cvb_ref[...]
    cn = cn * jax.nn.sigmoid(cn)
    y_conv = jnp.dot(cn.astype(jnp.bfloat16), wpw_ref[...], preferred_element_type=jnp.float32)

    y = jnp.dot(attn_ref[...].astype(jnp.bfloat16), wout_ref[0:ATTN_W, :],
                preferred_element_type=jnp.float32)
    y = y + jnp.dot(y_pool.astype(jnp.bfloat16), wout_ref[ATTN_W:ATTN_W + POOL_W, :],
                    preferred_element_type=jnp.float32)
    y = y + jnp.dot(y_conv.astype(jnp.bfloat16), wout_ref[ATTN_W + POOL_W:, :],
                    preferred_element_type=jnp.float32)
    x1 = _ln_plain(alpha * x_ref[...] + _mod_rows(mod_ref, 2) * y) * ln1g_ref[...] + ln1b_ref[...]
    x1_ref[...] = x1
    h2 = _ln_plain(x1) * (1.0 + _mod_rows(mod_ref, 4)) + _mod_rows(mod_ref, 3)
    h2_ref[...] = h2

    logits = jnp.dot(h2, wr_ref[...], precision=_HI, preferred_element_type=jnp.float32) + br_ref[...]
    ln = lax.broadcasted_iota(jnp.int32, (TM, LANES), 1)
    neg = jnp.float32(-jnp.inf)
    is_g = ln < N_GROUPS
    lg = jnp.where(is_g, logits, neg)
    mg = jnp.max(lg, axis=-1, keepdims=True)
    gsel = jnp.min(jnp.where(lg == mg, ln, LANES), axis=-1, keepdims=True)
    p_group = 1.0 / jnp.sum(jnp.exp(lg - mg), axis=-1, keepdims=True)
    e_lane = ln - N_GROUPS
    in_sel = jnp.logical_and(e_lane >= gsel * EPG, e_lane < (gsel + 1) * EPG)
    le = jnp.where(in_sel, logits, neg)
    v1 = jnp.max(le, axis=-1, keepdims=True)
    e1 = jnp.min(jnp.where(le == v1, e_lane, LANES), axis=-1, keepdims=True)
    le2 = jnp.where(e_lane == e1, neg, le)
    v2 = jnp.max(le2, axis=-1, keepdims=True)
    e2 = jnp.min(jnp.where(le2 == v2, e_lane, LANES), axis=-1, keepdims=True)
    ex = jnp.exp(v2 - v1)
    gate1 = p_group / (1.0 + ex)
    gate2 = p_group * ex / (1.0 + ex)

    oh = jnp.logical_or(ln == e1, ln == e2)
    ohb = jnp.where(oh, 1.0, 0.0).astype(jnp.bfloat16)
    r_i = lax.broadcasted_iota(jnp.int32, (TM, TM), 0)
    c_i = lax.broadcasted_iota(jnp.int32, (TM, TM), 1)
    tri = jnp.where(c_i < r_i, 1.0, 0.0).astype(jnp.bfloat16)
    before = jnp.dot(tri, ohb, preferred_element_type=jnp.float32) + base_ref[...]
    rank1 = jnp.sum(jnp.where(ln == e1, before, 0.0), axis=-1, keepdims=True)
    rank2 = jnp.sum(jnp.where(ln == e2, before, 0.0), axis=-1, keepdims=True)
    base_ref[...] = base_ref[...] + jnp.sum(ohb.astype(jnp.float32), axis=0, keepdims=True)
    cnt_ref[...] = jnp.broadcast_to(base_ref[...], cnt_ref.shape)

    route = jnp.zeros((TM, LANES), jnp.float32)
    for j, val in enumerate((e1.astype(jnp.float32), e2.astype(jnp.float32), rank1, rank2,
                             gate1, gate2)):
        route = jnp.where(ln == j, val, route)
    route_ref[...] = route


def _mixer(xa, attn, u, ag, mod, wpool, pscale, wdw, bdw, cvg, cvb, wpw, wout, ln1g, ln1b,
           wr, br, alpha, l):
    ttot = xa.shape[0]
    nt = ttot // TM
    hb = TM // HALO
    nhb = ttot // HALO

    def tile(w):
        return pl.BlockSpec((TM, w), lambda i: (i, 0))

    def prev(w):
        return pl.BlockSpec((HALO, w), lambda i: (jnp.maximum(i * hb - 1, 0), 0))

    def nxt(w):
        return pl.BlockSpec((HALO, w), lambda i: (jnp.minimum((i + 1) * hb, nhb - 1), 0))

    kern = functools.partial(_mix_kernel, alpha=alpha, n_lat=ttot - TM)
    return pl.pallas_call(
        kern,
        grid=(nt,),
        in_specs=[
            tile(D), tile(ATTN_W),
            tile(POOL_W), prev(POOL_W), nxt(POOL_W),
            tile(CONV_W), prev(CONV_W), nxt(CONV_W),
            _layer((8, N_MOD * D), l),
            _layer((POOL_W, POOL_W), l), _layer((1, POOL_W), l),
            _layer((32, CONV_W), l), _layer((1, CONV_W), l), _layer((1, CONV_W), l),
            _layer((1, CONV_W), l),
            _layer((CONV_W, CONV_W), l), _layer((D, D), l),
            _layer((1, D), l), _layer((1, D), l),
            _layer((D, LANES), l), _layer((1, LANES), l),
        ],
        out_specs=[tile(D), tile(D), tile(LANES), _full((8, LANES))],
        out_shape=[
            jax.ShapeDtypeStruct((ttot, D), jnp.float32),
            jax.ShapeDtypeStruct((ttot, D), jnp.float32),
            jax.ShapeDtypeStruct((ttot, LANES), jnp.float32),
            jax.ShapeDtypeStruct((8, LANES), jnp.float32),
        ],
        scratch_shapes=[
            pltpu.VMEM((TM + 2 * HALO, POOL_W), jnp.float32),
            pltpu.VMEM((TM + 2 * HALO, CONV_W), jnp.float32),
            pltpu.VMEM((1, LANES), jnp.float32),
        ],
        compiler_params=_cparams(("arbitrary",)),
        name="mixer",
    )(xa, attn, u, u, u, ag, ag, ag, mod, wpool, pscale, wdw, bdw, cvg, cvb, wpw, wout,
      ln1g, ln1b, wr, br)


def _row_copy_scatter(h_ref, xs_ref, sem, r, d):
    return pltpu.make_async_copy(h_ref.at[pl.ds(r, 1), :], xs_ref.at[pl.ds(d, 1), :], sem)


def _dispatch_kernel(d1_ref, d2_ref, h_ref, xs_in_ref, xs_ref, sem):
    del xs_in_ref

    def start(r, carry):
        _row_copy_scatter(h_ref, xs_ref, sem, r, d1_ref[0, 0, r]).start()
        _row_copy_scatter(h_ref, xs_ref, sem, r, d2_ref[0, 0, r]).start()
        return carry

    lax.fori_loop(0, TM, start, 0)

    def wait(r, carry):
        _row_copy_scatter(h_ref, xs_ref, sem, r, d1_ref[0, 0, r]).wait()
        _row_copy_scatter(h_ref, xs_ref, sem, r, d2_ref[0, 0, r]).wait()
        return carry

    lax.fori_loop(0, TM, wait, 0)


def _dispatch(h2, d1, d2, n_slots):
    ttot = h2.shape[0]
    nt = ttot // TM
    smem = pl.BlockSpec((1, 1, TM), lambda i: (i, 0, 0), memory_space=pltpu.SMEM)
    xs0 = jnp.zeros((n_slots, D), jnp.float32)
    return pl.pallas_call(
        _dispatch_kernel,
        grid=(nt,),
        in_specs=[smem, smem, pl.BlockSpec((TM, D), lambda i: (i, 0)),
                  pl.BlockSpec(memory_space=pl.ANY)],
        out_specs=pl.BlockSpec(memory_space=pl.ANY),
        out_shape=jax.ShapeDtypeStruct((n_slots, D), jnp.float32),
        scratch_shapes=[pltpu.SemaphoreType.DMA(())],
        input_output_aliases={3: 0},
        compiler_params=_cparams(("arbitrary",)),
        name="dispatch",
    )(d1, d2, h2, xs0)


def _expert_kernel(be_ref, nu_ref, x_ref, wg_ref, wu_ref, wd_ref, y_ref):
    b = pl.program_id(0)

    @pl.when(b < nu_ref[0])
    def _():
        x = x_ref[...].astype(jnp.bfloat16)
        gt = jnp.dot(x, wg_ref[0].astype(jnp.bfloat16), preferred_element_type=jnp.float32)
        up = jnp.dot(x, wu_ref[0].astype(jnp.bfloat16), preferred_element_type=jnp.float32)
        hm = (gt * jax.nn.sigmoid(gt) * up).astype(jnp.bfloat16)
        y_ref[...] = jnp.dot(hm, wd_ref[0].astype(jnp.bfloat16), preferred_element_type=jnp.float32)

    @pl.when(b >= nu_ref[0])
    def _():
        y_ref[...] = jnp.zeros(y_ref.shape, jnp.float32)


def _experts(xs, blk_e, n_used, wg, wu, wd, l):
    n_slots = xs.shape[0]
    nblk = n_slots // EBLK

    def row_map(b, be, nu):
        return (jnp.minimum(b, nu[0] - 1), 0)

    def w_map(b, be, nu):
        return (l, be[b], 0, 0)

    grid_spec = pltpu.PrefetchScalarGridSpec(
        num_scalar_prefetch=2,
        grid=(nblk,),
        in_specs=[
            pl.BlockSpec((EBLK, D), row_map),
            pl.BlockSpec((None, 1, D, D_EXPERT), w_map),
            pl.BlockSpec((None, 1, D, D_EXPERT), w_map),
            pl.BlockSpec((None, 1, D_EXPERT, D), w_map),
        ],
        out_specs=pl.BlockSpec((EBLK, D), lambda b, be, nu: (b, 0)),
    )
    return pl.pallas_call(
        _expert_kernel,
        grid_spec=grid_spec,
        out_shape=jax.ShapeDtypeStruct((n_slots, D), jnp.float32),
        compiler_params=_cparams(("arbitrary",)),
        name="experts",
    )(blk_e, n_used, xs, wg, wu, wd)


def _row_copy_gather(ys_ref, buf_ref, sem, r, d):
    return pltpu.make_async_copy(ys_ref.at[pl.ds(d, 1), :], buf_ref.at[pl.ds(r, 1), :], sem)


def _combine_kernel(d1_ref, d2_ref, x1_ref, route_ref, mod_ref, ln2g_ref, ln2b_ref, ys_ref,
                    o_ref, y1_buf, y2_buf, sem, *, alpha):
    def start(r, carry):
        _row_copy_gather(ys_ref, y1_buf, sem, r, d1_ref[0, 0, r]).start()
        _row_copy_gather(ys_ref, y2_buf, sem, r, d2_ref[0, 0, r]).start()
        return carry

    lax.fori_loop(0, TM, start, 0)

    def wait(r, carry):
        _row_copy_gather(ys_ref, y1_buf, sem, r, d1_ref[0, 0, r]).wait()
        _row_copy_gather(ys_ref, y2_buf, sem, r, d2_ref[0, 0, r]).wait()
        return carry

    lax.fori_loop(0, TM, wait, 0)

    route = route_ref[...]
    g1 = route[:, 4:5]
    g2 = route[:, 5:6]
    o = g1 * y1_buf[...] + g2 * y2_buf[...]
    z = alpha * x1_ref[...] + _mod_rows(mod_ref, 5) * o
    o_ref[...] = _ln_plain(z) * ln2g_ref[...] + ln2b_ref[...]


def _combine(x1, route, mod, ln2g, ln2b, ys, d1, d2, alpha, l, latents_only):
    ttot = x1.shape[0]
    nt = ttot // TM
    smem = pl.BlockSpec((1, 1, TM), lambda i: (i, 0, 0), memory_space=pltpu.SMEM)
    kern = functools.partial(_combine_kernel, alpha=alpha)
    if latents_only:
        out_spec = pl.BlockSpec((TM, D), lambda i: (jnp.maximum(i - 1, 0), 0))
        out_rows = ttot - TM
    else:
        out_spec = pl.BlockSpec((TM, D), lambda i: (i, 0))
        out_rows = ttot
    return pl.pallas_call(
        kern,
        grid=(nt,),
        in_specs=[smem, smem,
                  pl.BlockSpec((TM, D), lambda i: (i, 0)),
                  pl.BlockSpec((TM, LANES), lambda i: (i, 0)),
                  _layer((8, N_MOD * D), l), _layer((1, D), l), _layer((1, D), l),
                  pl.BlockSpec(memory_space=pl.ANY)],
        out_specs=out_spec,
        out_shape=jax.ShapeDtypeStruct((out_rows, D), jnp.float32),
        scratch_shapes=[pltpu.VMEM((TM, D), jnp.float32), pltpu.VMEM((TM, D), jnp.float32),
                        pltpu.SemaphoreType.DMA(())],
        compiler_params=_cparams(("arbitrary",)),
        name="combine",
    )(d1, d2, x1, route, mod, ln2g, ln2b, ys)


def _rope_tables_t(n_lat, n_ctx):
    rows = n_lat // GRID_W
    row = jnp.broadcast_to(jnp.arange(rows)[:, None], (rows, GRID_W)).reshape(-1)
    col = jnp.broadcast_to(jnp.arange(GRID_W)[None, :], (rows, GRID_W)).reshape(-1)
    n_freq = HD // 4
    inv = ROPE_THETA ** (-jnp.arange(n_freq, dtype=jnp.float32) / n_freq)
    pos = jnp.stack([row, col], axis=-1).astype(jnp.float32)
    ang = (pos[:, :, None] * inv).reshape(n_lat, 2 * n_freq)
    cos = jnp.concatenate([jnp.ones((n_ctx, 2 * n_freq), jnp.float32), jnp.cos(ang)], axis=0)
    sin = jnp.concatenate([jnp.zeros((n_ctx, 2 * n_freq), jnp.float32), jnp.sin(ang)], axis=0)
    return cos.T, sin.T


def _row(v):
    return v.reshape(1, -1)


def kernel(x, c, ctx, c_ctx, w_mod, b_mod, w_in, q_gain, k_gain, w_pool, pool_scale, w_dw, b_dw,
           cv_ln_g, cv_ln_b, w_cv_pw, w_out, ln1_g, ln1_b, ln2_g, ln2_b, w_rg, b_rg, w_re, b_re,
           w_e_gate, w_e_up, w_e_down):
    depth = w_mod.shape[0]
    n_lat = x.shape[1]
    n_ctx = ctx.shape[1]
    assert x.shape[0] == 1 and n_ctx == TM and n_lat % TM == 0 and n_lat % GRID_W == 0
    ttot = n_ctx + n_lat
    alpha = float((2 * depth) ** 0.25)
    n_blk = -(-(2 * ttot) // EBLK) + N_EXPERTS
    n_slots = n_blk * EBLK

    xa = jnp.concatenate([ctx[0], x[0]], axis=0)
    cc = jnp.concatenate([c, c_ctx[None], jnp.zeros((6, D), jnp.float32)], axis=0)
    mod = _modulation(cc, w_mod, b_mod)
    cos_t, sin_t = _rope_tables_t(n_lat, n_ctx)

    n_qkv = ATTN_W + 2 * KV_W
    bf = jnp.bfloat16
    w_t = jnp.swapaxes(w_in[:, :, :n_qkv], 1, 2).astype(bf)
    w_n = w_in[:, :, n_qkv:].astype(bf)
    qg = jnp.broadcast_to(q_gain[:, :, None], (depth, HD, TM))
    kg = jnp.broadcast_to(k_gain[:, :, None], (depth, HD, TM))
    n_pg = len(POOL_WINDOWS)
    wpool = (jnp.eye(n_pg, dtype=jnp.float32)[None, :, None, :, None] * w_pool[:, :, :, None, :]
             ).reshape(depth, POOL_W, POOL_W).astype(bf)
    wdw = jnp.pad(w_dw, ((0, 0), (0, 32 - CONV_K), (0, 0)))
    n_r = N_GROUPS + N_EXPERTS
    wr = jnp.concatenate([w_rg, w_re, jnp.zeros((depth, D, LANES - n_r), jnp.float32)], axis=2)
    br = jnp.concatenate([b_rg, b_re, jnp.zeros((depth, LANES - n_r), jnp.float32)], axis=1)
    wpw = w_cv_pw.astype(bf)
    wout = w_out.astype(bf)

    def rows(v):
        return v[:, None, :]

    e_ids = jnp.arange(N_EXPERTS, dtype=jnp.int32)
    blk_start = jnp.arange(n_blk, dtype=jnp.int32) * EBLK
    for l in range(depth):
        q_t, k2, v_t, kn2, u, ag = _inproj(xa, mod, w_t, w_n, qg, kg, cos_t, sin_t, l)
        kmax = jnp.sqrt(jnp.max(kn2[:, :, 0, 0], axis=0))
        attn = _attention(q_t, k2, v_t, kmax)
        x1, h2, route, cnt = _mixer(
            xa, attn, u, ag, mod, wpool, rows(pool_scale), wdw, rows(b_dw), rows(cv_ln_g),
            rows(cv_ln_b), wpw, wout, rows(ln1_g), rows(ln1_b), wr, rows(br), alpha, l)

        counts = cnt[0, :N_EXPERTS].astype(jnp.int32)
        padded = (counts + EBLK - 1) // EBLK * EBLK
        pends = jnp.cumsum(padded)
        pstarts = pends - padded
        ri = route[:, :4].astype(jnp.int32)

        def dest(e, r):
            start = jnp.sum(jnp.where(e[:, None] == e_ids[None, :], pstarts[None, :], 0), axis=1)
            return (start + r).reshape(ttot // TM, 1, TM)

        d1 = dest(ri[:, 0], ri[:, 2])
        d2 = dest(ri[:, 1], ri[:, 3])
        n_used = (pends[-1] // EBLK).astype(jnp.int32).reshape(1)
        first_row = jnp.minimum(blk_start, pends[-1] - 1)
        blk_e = jnp.sum((pends[None, :] <= first_row[:, None]).astype(jnp.int32), axis=1)
        blk_e = jnp.minimum(blk_e, N_EXPERTS - 1)

        xs = _dispatch(h2, d1, d2, n_slots)
        ys = _experts(xs, blk_e, n_used, w_e_gate, w_e_up, w_e_down, l)
        xa = _combine(x1, route, mod, rows(ln2_g), rows(ln2_b), ys, d1, d2, alpha, l,
                      latents_only=(l == depth - 1))
    return xa[None]
```

```python
import functools

import jax
import jax.numpy as jnp
from jax import lax
from jax.experimental import pallas as pl
from jax.experimental.pallas import tpu as pltpu

D = 1024
GRID_W = 64
HD = 64
NQ = 8
NKV = 2
QPK = NQ // NKV
ATTN_W = NQ * HD
KV_W = NKV * HD
POOL_W = 256
POOL_WINDOWS = (2, 4, 8, 16)
POOL_GW = POOL_W // len(POOL_WINDOWS)
CONV_W = 256
CONV_K = 31
N_GROUPS = 4
EPG = 8
N_EXPERTS = N_GROUPS * EPG
D_EXPERT = 512
N_MOD = 6
ROPE_THETA = 10000.0
EPS = 1e-6

TM = 256
HALO = 16
EBLK = 256
LANES = 128
VROWS = HD + 16
LOG2E = 1.4426950408889634
BOUND_MARGIN = 1.0 + 2.0 ** -5
VMEM_LIMIT = 56 * 1024 * 1024
NEG_BIG = -1e30

_HI = lax.Precision.HIGHEST


def _cparams(sem):
    return pltpu.CompilerParams(dimension_semantics=sem, vmem_limit_bytes=VMEM_LIMIT)


def _full(shape):
    n = len(shape)
    return pl.BlockSpec(shape, lambda *a, n=n: (0,) * n)


def _layer(shape, l):
    n = len(shape)
    return pl.BlockSpec((None,) + tuple(shape), lambda *a, n=n: (l,) + (0,) * n)


def _mod_kernel(cc_ref, w_ref, b_ref, o_ref):
    cc = cc_ref[...]
    a = cc * jax.nn.sigmoid(cc)
    o_ref[0] = jnp.dot(a, w_ref[0], precision=_HI, preferred_element_type=jnp.float32) + b_ref[0]


def _modulation(cc, w_mod, b_mod):
    depth = w_mod.shape[0]
    nw = w_mod.shape[2]
    bn = 1024
    return pl.pallas_call(
        _mod_kernel,
        grid=(depth, nw // bn),
        in_specs=[
            pl.BlockSpec((8, D), lambda l, j: (0, 0)),
            pl.BlockSpec((1, D, bn), lambda l, j: (l, 0, j)),
            pl.BlockSpec((1, 1, bn), lambda l, j: (l, 0, j)),
        ],
        out_specs=pl.BlockSpec((1, 8, bn), lambda l, j: (l, 0, j)),
        out_shape=jax.ShapeDtypeStruct((depth, 8, nw), jnp.float32),
        compiler_params=_cparams(("arbitrary", "arbitrary")),
        name="modulation",
    )(cc, w_mod, b_mod.reshape(depth, 1, nw))


def _ln_plain(x):
    mu = jnp.mean(x, axis=-1, keepdims=True)
    xc = x - mu
    var = jnp.mean(xc * xc, axis=-1, keepdims=True)
    return xc * lax.rsqrt(var + EPS)


def _mod_rows(mod_ref, idx):
    row = jnp.where(pl.program_id(0) == 0, 1, 0)
    return mod_ref[pl.ds(row, 1), idx * D:(idx + 1) * D]


def _rms_rope_t(t, gain, cos, sin):
    ms = jnp.mean(t * t, axis=0, keepdims=True)
    t = t * lax.rsqrt(ms + EPS) * gain
    outs = []
    for a in range(2):
        t1 = t[32 * a:32 * a + 16]
        t2 = t[32 * a + 16:32 * a + 32]
        c = cos[16 * a:16 * a + 16]
        s = sin[16 * a:16 * a + 16]
        outs.append(t1 * c - t2 * s)
        outs.append(t2 * c + t1 * s)
    return jnp.concatenate(outs, axis=0)


def _inproj_kernel(x_ref, mod_ref, wt_ref, wn_ref, qg_ref, kg_ref, cos_ref, sin_ref,
                   q_ref, k_ref, v_ref, kmax_ref, u_ref, ag_ref):
    x = x_ref[...]
    h = _ln_plain(x) * (1.0 + _mod_rows(mod_ref, 1)) + _mod_rows(mod_ref, 0)
    hb = h.astype(jnp.bfloat16)
    qkv_t = lax.dot_general(wt_ref[...], hb, (((1,), (1,)), ((), ())),
                            preferred_element_type=jnp.float32)
    cos = cos_ref[...]
    sin = sin_ref[...]
    qg = qg_ref[...]
    kg = kg_ref[...]
    ones_row = jnp.where(lax.broadcasted_iota(jnp.int32, (HD, TM), 0) == 0, 1.0, 0.0)
    ones_pad = jnp.where(lax.broadcasted_iota(jnp.int32, (VROWS - HD, TM), 0) == 0, 1.0, 0.0)
    for hq in range(NQ):
        t = _rms_rope_t(qkv_t[hq * HD:(hq + 1) * HD], qg, cos, sin) * (LOG2E * HD ** -0.5)
        nq = jnp.sqrt(jnp.sum(t * t, axis=0, keepdims=True))
        q_ref[hq // QPK, hq % QPK] = jnp.concatenate(
            [t, ones_row * (-BOUND_MARGIN * nq)], axis=0).astype(jnp.bfloat16)
    for g in range(NKV):
        kt = _rms_rope_t(qkv_t[ATTN_W + g * HD:ATTN_W + (g + 1) * HD], kg, cos, sin)
        kn = jnp.max(jnp.sum(kt * kt, axis=0, keepdims=True), axis=1, keepdims=True)
        kmax_ref[0, g] = jnp.broadcast_to(kn, (8, LANES))
        k_ref[g] = jnp.concatenate([kt, ones_row], axis=0).T.astype(jnp.bfloat16)
        vt = qkv_t[ATTN_W + KV_W + g * HD:ATTN_W + KV_W + (g + 1) * HD]
        v_ref[g] = jnp.concatenate([vt, ones_pad], axis=0).astype(jnp.bfloat16)
    uag = jnp.dot(hb, wn_ref[...], preferred_element_type=jnp.float32)
    u_ref[...] = uag[:, :POOL_W]
    a = uag[:, POOL_W:POOL_W + CONV_W]
    gt = uag[:, POOL_W + CONV_W:]
    ag_ref[...] = a * jax.nn.sigmoid(gt)


def _inproj(xa, mod, w_t, w_n, qg, kg, cos_t, sin_t, l):
    ttot = xa.shape[0]
    nt = ttot // TM
    return pl.pallas_call(
        _inproj_kernel,
        grid=(nt,),
        in_specs=[
            pl.BlockSpec((TM, D), lambda i: (i, 0)),
            _layer((8, N_MOD * D), l),
            _layer((ATTN_W + 2 * KV_W, D), l),
            _layer((D, POOL_W + 2 * CONV_W), l),
            _layer((HD, TM), l),
            _layer((HD, TM), l),
            pl.BlockSpec((32, TM), lambda i: (0, i)),
            pl.BlockSpec((32, TM), lambda i: (0, i)),
        ],
        out_specs=[
            pl.BlockSpec((NKV, QPK, 2 * HD, TM), lambda i: (0, 0, 0, i)),
            pl.BlockSpec((NKV, TM, 2 * HD), lambda i: (0, i, 0)),
            pl.BlockSpec((NKV, VROWS, TM), lambda i: (0, 0, i)),
            pl.BlockSpec((1, NKV, 8, LANES), lambda i: (i, 0, 0, 0)),
            pl.BlockSpec((TM, POOL_W), lambda i: (i, 0)),
            pl.BlockSpec((TM, CONV_W), lambda i: (i, 0)),
        ],
        out_shape=[
            jax.ShapeDtypeStruct((NKV, QPK, 2 * HD, ttot), jnp.bfloat16),
            jax.ShapeDtypeStruct((NKV, ttot, 2 * HD), jnp.bfloat16),
            jax.ShapeDtypeStruct((NKV, VROWS, ttot), jnp.bfloat16),
            jax.ShapeDtypeStruct((nt, NKV, 8, LANES), jnp.float32),
            jax.ShapeDtypeStruct((ttot, POOL_W), jnp.float32),
            jax.ShapeDtypeStruct((ttot, CONV_W), jnp.float32),
        ],
        compiler_params=_cparams(("arbitrary",)),
        name="inproj",
    )(xa, mod, w_t, w_n, qg, kg, cos_t, sin_t)


TK = 256
GK = 1280
L_MIN = 2.0 ** -60


def _attn_kernel(kmax_ref, q_ref, k_ref, v_ref, o_ref, qs_ref, s_ref, acc_ref, m_ref, *,
                 n_chunks, n_groups):
    g = pl.program_id(0)
    i = pl.program_id(1)
    nch = jnp.where(i == 0, 1, n_chunks)
    row = lax.broadcasted_iota(jnp.int32, (2 * HD, TM), 0)
    kmax = kmax_ref[g]

    def prep(src_ref):
        for hh in range(QPK):
            qf = src_ref[0, hh].astype(jnp.float32)
            qs_ref[hh] = jnp.where(row == HD, qf * kmax, qf).astype(jnp.bfloat16)

    prep(q_ref)
    acc_ref[...] = jnp.zeros(acc_ref.shape, jnp.float32)

    def step(nxt, cur):
        for kk in range(GK // TK):
            rows = slice(kk * TK, (kk + 1) * TK)
            if nxt is not None:
                kc = k_ref[0, pl.ds(pl.multiple_of(nxt[0] * GK + kk * TK, TK), TK), :]
            if cur is not None:
                vc = v_ref[0, :, pl.ds(pl.multiple_of(cur[0] * GK + kk * TK, TK), TK)]
            for hh in range(QPK):
                if nxt is not None:
                    s_ref[nxt[1], hh, rows, :] = jnp.dot(
                        kc, qs_ref[hh], preferred_element_type=jnp.float32)
                if cur is not None:
                    p = jnp.exp2(s_ref[cur[1], hh, rows, :]).astype(jnp.bfloat16)
                    acc_ref[hh] += jnp.dot(vc, p, preferred_element_type=jnp.float32)

    @pl.when(i == 0)
    def _():
        kc = k_ref[0, 0:TM, :]
        vc = v_ref[0, :, 0:TM]
        for hh in range(QPK):
            p = jnp.exp2(jnp.dot(kc, qs_ref[hh], preferred_element_type=jnp.float32))
            acc_ref[hh] = jnp.dot(vc, p.astype(jnp.bfloat16), preferred_element_type=jnp.float32)

    @pl.when(i > 0)
    def _():
        step((0, 0), None)
        n_pairs = (n_groups - 1) // 2

        def pair(j, carry):
            step((2 * j + 1, 1), (2 * j, 0))
            step((2 * j + 2, 0), (2 * j + 1, 1))
            return carry

        lax.fori_loop(0, n_pairs, pair, 0)
        for n in range(2 * n_pairs, n_groups - 1):
            step((n + 1, (n + 1) % 2), (n, n % 2))
        step(None, (n_groups - 1, (n_groups - 1) % 2))

    l_min = jnp.min(acc_ref[:, HD:HD + 1, :])

    @pl.when(l_min < L_MIN)
    def _():
        for hh in range(QPK):
            qs_ref[hh] = jnp.where(row == HD, 0.0, q_ref[0, hh].astype(jnp.float32)
                                   ).astype(jnp.bfloat16)
        acc_ref[...] = jnp.zeros(acc_ref.shape, jnp.float32)
        m_ref[...] = jnp.full(m_ref.shape, NEG_BIG, jnp.float32)

        def body2(c, carry):
            start = pl.multiple_of(c * TK, TK)
            kc = k_ref[0, pl.ds(start, TK), :]
            vc = v_ref[0, :, pl.ds(start, TK)]
            for hh in range(QPK):
                s = jnp.dot(kc, qs_ref[hh], preferred_element_type=jnp.float32)
                m_old = m_ref[hh]
                m_new = jnp.maximum(m_old, jnp.max(s, axis=0, keepdims=True))
                p = jnp.exp2(s - m_new).astype(jnp.bfloat16)
                acc_ref[hh] = jnp.exp2(m_old - m_new) * acc_ref[hh] + jnp.dot(
                    vc, p, preferred_element_type=jnp.float32)
                m_ref[hh] = m_new
            return carry

        lax.fori_loop(0, nch, body2, 0)

    outs = [acc_ref[hh, 0:HD, :] / acc_ref[hh, HD:HD + 1, :] for hh in range(QPK)]
    o_ref[...] = jnp.concatenate(outs, axis=0).T


def _attention(q_t, k2, v_t, kmax):
    ttot = k2.shape[1]
    nt = ttot // TM
    assert ttot % GK == 0 and GK % TK == 0
    kern = functools.partial(_attn_kernel, n_chunks=ttot // TK, n_groups=ttot // GK)
    grid_spec = pltpu.PrefetchScalarGridSpec(
        num_scalar_prefetch=1,
        grid=(NKV, nt),
        in_specs=[
            pl.BlockSpec((1, QPK, 2 * HD, TM), lambda g, i, km: (g, 0, 0, i)),
            pl.BlockSpec((1, ttot, 2 * HD), lambda g, i, km: (g, 0, 0)),
            pl.BlockSpec((1, VROWS, ttot), lambda g, i, km: (g, 0, 0)),
        ],
        out_specs=pl.BlockSpec((TM, QPK * HD), lambda g, i, km: (i, g)),
        scratch_shapes=[
            pltpu.VMEM((QPK, 2 * HD, TM), jnp.bfloat16),
            pltpu.VMEM((2, QPK, GK, TM), jnp.float32),
            pltpu.VMEM((QPK, VROWS, TM), jnp.float32),
            pltpu.VMEM((QPK, 1, TM), jnp.float32),
        ],
    )
    return pl.pallas_call(
        kern,
        grid_spec=grid_spec,
        out_shape=jax.ShapeDtypeStruct((ttot, ATTN_W), jnp.float32),
        compiler_params=_cparams(("arbitrary", "arbitrary")),
        name="attention",
    )(kmax, q_t, k2, v_t)


def _fill_ext(ext_ref, cur_ref, prev_ref, next_ref, left_ok, right_ok):
    zero = jnp.zeros((HALO, cur_ref.shape[1]), jnp.float32)
    ext_ref[0:HALO, :] = jnp.where(left_ok, prev_ref[...], zero)
    ext_ref[HALO:HALO + TM, :] = cur_ref[...]
    ext_ref[HALO + TM:HALO + TM + HALO, :] = jnp.where(right_ok, next_ref[...], zero)


def _mix_kernel(x_ref, attn_ref, u_ref, up_ref, un_ref, ag_ref, agp_ref, agn_ref, mod_ref,
                wpool_ref, pscale_ref, wdw_ref, bdw_ref, cvg_ref, cvb_ref, wpw_ref, wout_ref,
                ln1g_ref, ln1b_ref, wr_ref, br_ref,
                x1_ref, h2_ref, route_ref, cnt_ref,
                uext_ref, agext_ref, base_ref, *, alpha, n_lat):
    i = pl.program_id(0)
    nt = pl.num_programs(0)
    is_ctx = i == 0
    left_ok = i >= 2
    right_ok = jnp.logical_and(i >= 1, i < nt - 1)

    @pl.when(i == 0)
    def _():
        base_ref[...] = jnp.zeros(base_ref.shape, jnp.float32)

    _fill_ext(uext_ref, u_ref, up_ref, un_ref, left_ok, right_ok)
    _fill_ext(agext_ref, ag_ref, agp_ref, agn_ref, left_ok, right_ok)

    def ush(off, rows=TM):
        return uext_ref[HALO + off:HALO + off + rows, :]

    a2 = ush(-8, TM + 15) + ush(-7, TM + 15)
    a4 = a2[0:TM + 13] + a2[2:TM + 15]
    a8 = a4[0:TM + 9] + a4[4:TM + 13]
    a16 = a8[0:TM] + a8[8:TM + 8]
    sums = (a2[7:7 + TM], a4[6:6 + TM], a8[4:4 + TM], a16)
    pos = lax.broadcasted_iota(jnp.int32, (TM, 1), 0) + jnp.where(is_ctx, 0, (i - 1) * TM)
    seq_n = jnp.where(is_ctx, TM, n_lat)
    u_cur = u_ref[...]
    lane = lax.broadcasted_iota(jnp.int32, (TM, POOL_W), 1)
    pooled = jnp.zeros((TM, POOL_W), jnp.float32)
    for gi, w in enumerate(POOL_WINDOWS):
        lo = jnp.maximum(pos - w // 2, 0)
        hi = jnp.minimum(pos + (w - w // 2), seq_n)
        inv = 1.0 / (hi - lo).astype(jnp.float32)
        in_group = jnp.logical_and(lane >= gi * POOL_GW, lane < (gi + 1) * POOL_GW)
        pooled = jnp.where(in_group, sums[gi] * inv - u_cur, pooled)
    y_pool = jnp.dot(pooled.astype(jnp.bfloat16), wpool_ref[...],
                     preferred_element_type=jnp.float32) * pscale_ref[...]

    conv = jnp.zeros((TM, CONV_W), jnp.float32) + bdw_ref[...]
    for k in range(CONV_K):
        off = HALO + k - CONV_K // 2
        conv = conv + agext_ref[off:off + TM, :] * wdw_ref[k:k + 1, :]
    cn = _ln_plain(conv) * cvg_ref[...] + cvb_ref[...]
    cn = cn * jax.nn.sigmoid(cn)
    y_conv = jnp.dot(cn.astype(jnp.bfloat16), wpw_ref[...], preferred_element_type=jnp.float32)

    y = jnp.dot(attn_ref[...].astype(jnp.bfloat16), wout_ref[0:ATTN_W, :],
                preferred_element_type=jnp.float32)
    y = y + jnp.dot(y_pool.astype(jnp.bfloat16), wout_ref[ATTN_W:ATTN_W + POOL_W, :],
                    preferred_element_type=jnp.float32)
    y = y + jnp.dot(y_conv.astype(jnp.bfloat16), wout_ref[ATTN_W + POOL_W:, :],
                    preferred_element_type=jnp.float32)
    x1 = _ln_plain(alpha * x_ref[...] + _mod_rows(mod_ref, 2) * y) * ln1g_ref[...] + ln1b_ref[...]
    x1_ref[...] = x1
    h2 = _ln_plain(x1) * (1.0 + _mod_rows(mod_ref, 4)) + _mod_rows(mod_ref, 3)
    h2_ref[...] = h2

    h_hi = h2.astype(jnp.bfloat16)
    h_lo = (h2 - h_hi.astype(jnp.float32)).astype(jnp.bfloat16)
    hw = jnp.dot(h_hi, wr_ref[...], preferred_element_type=jnp.float32)
    logits = (hw[:, :LANES] + hw[:, LANES:] + br_ref[...]
              + jnp.dot(h_lo, wr_ref[:, :LANES], preferred_element_type=jnp.float32))
    ln = lax.broadcasted_iota(jnp.int32, (TM, LANES), 1)
    neg = jnp.float32(-jnp.inf)
    is_g = ln < N_GROUPS
    lg = jnp.where(is_g, logits, neg)
    mg = jnp.max(lg, axis=-1, keepdims=True)
    gsel = jnp.min(jnp.where(lg == mg, ln, LANES), axis=-1, keepdims=True)
    p_group = 1.0 / jnp.sum(jnp.exp(lg - mg), axis=-1, keepdims=True)
    e_lane = ln - N_GROUPS
    in_sel = jnp.logical_and(e_lane >= gsel * EPG, e_lane < (gsel + 1) * EPG)
    le = jnp.where(in_sel, logits, neg)
    v1 = jnp.max(le, axis=-1, keepdims=True)
    e1 = jnp.min(jnp.where(le == v1, e_lane, LANES), axis=-1, keepdims=True)
    le2 = jnp.where(e_lane == e1, neg, le)
    v2 = jnp.max(le2, axis=-1, keepdims=True)
    e2 = jnp.min(jnp.where(le2 == v2, e_lane, LANES), axis=-1, keepdims=True)
    ex = jnp.exp(v2 - v1)
    gate1 = p_group / (1.0 + ex)
    gate2 = p_group * ex / (1.0 + ex)

    oh = jnp.logical_or(ln == e1, ln == e2)
    ohb = jnp.where(oh, 1.0, 0.0).astype(jnp.bfloat16)
    r_i = lax.broadcasted_iota(jnp.int32, (TM, TM), 0)
    c_i = lax.broadcasted_iota(jnp.int32, (TM, TM), 1)
    tri = jnp.where(c_i < r_i, 1.0, 0.0).astype(jnp.bfloat16)
    before = jnp.dot(tri, ohb, preferred_element_type=jnp.float32) + base_ref[...]
    rank1 = jnp.sum(jnp.where(ln == e1, before, 0.0), axis=-1, keepdims=True)
    rank2 = jnp.sum(jnp.where(ln == e2, before, 0.0), axis=-1, keepdims=True)
    base_ref[...] = base_ref[...] + jnp.sum(ohb.astype(jnp.float32), axis=0, keepdims=True)
    cnt_ref[...] = jnp.broadcast_to(base_ref[...], cnt_ref.shape)

    route = jnp.zeros((TM, LANES), jnp.float32)
    for j, val in enumerate((e1.astype(jnp.float32), e2.astype(jnp.float32), rank1, rank2,
                             gate1, gate2)):
        route = jnp.where(ln == j, val, route)
    route_ref[...] = route


def _mixer(xa, attn, u, ag, mod, wpool, pscale, wdw, bdw, cvg, cvb, wpw, wout, ln1g, ln1b,
           wr, br, alpha, l):
    ttot = xa.shape[0]
    nt = ttot // TM
    hb = TM // HALO
    nhb = ttot // HALO

    def tile(w):
        return pl.BlockSpec((TM, w), lambda i: (i, 0))

    def prev(w):
        return pl.BlockSpec((HALO, w), lambda i: (jnp.maximum(i * hb - 1, 0), 0))

    def nxt(w):
        return pl.BlockSpec((HALO, w), lambda i: (jnp.minimum((i + 1) * hb, nhb - 1), 0))

    kern = functools.partial(_mix_kernel, alpha=alpha, n_lat=ttot - TM)
    return pl.pallas_call(
        kern,
        grid=(nt,),
        in_specs=[
            tile(D), tile(ATTN_W),
            tile(POOL_W), prev(POOL_W), nxt(POOL_W),
            tile(CONV_W), prev(CONV_W), nxt(CONV_W),
            _layer((8, N_MOD * D), l),
            _layer((POOL_W, POOL_W), l), _layer((1, POOL_W), l),
            _layer((32, CONV_W), l), _layer((1, CONV_W), l), _layer((1, CONV_W), l),
            _layer((1, CONV_W), l),
            _layer((CONV_W, CONV_W), l), _layer((D, D), l),
            _layer((1, D), l), _layer((1, D), l),
            _layer((D, 2 * LANES), l), _layer((1, LANES), l),
        ],
        out_specs=[tile(D), tile(D), tile(LANES), _full((8, LANES))],
        out_shape=[
            jax.ShapeDtypeStruct((ttot, D), jnp.float32),
            jax.ShapeDtypeStruct((ttot, D), jnp.float32),
            jax.ShapeDtypeStruct((ttot, LANES), jnp.float32),
            jax.ShapeDtypeStruct((8, LANES), jnp.float32),
        ],
        scratch_shapes=[
            pltpu.VMEM((TM + 2 * HALO, POOL_W), jnp.float32),
            pltpu.VMEM((TM + 2 * HALO, CONV_W), jnp.float32),
            pltpu.VMEM((1, LANES), jnp.float32),
        ],
        compiler_params=_cparams(("arbitrary",)),
        name="mixer",
    )(xa, attn, u, u, u, ag, ag, ag, mod, wpool, pscale, wdw, bdw, cvg, cvb, wpw, wout,
      ln1g, ln1b, wr, br)


ROW_GROUP = 8


def _dispatch_kernel(d1_ref, d2_ref, h_ref, xs_in_ref, xs_ref, sem):
    del xs_in_ref

    def start(r8, carry):
        base = pl.multiple_of(r8 * ROW_GROUP, ROW_GROUP)
        rows = h_ref.at[pl.ds(base, ROW_GROUP), :]
        for j in range(ROW_GROUP):
            for d_ref in (d1_ref, d2_ref):
                pltpu.make_async_copy(rows.at[pl.ds(j, 1), :],
                                      xs_ref.at[pl.ds(d_ref[0, 0, base + j], 1), :], sem).start()
        return carry

    lax.fori_loop(0, TM // ROW_GROUP, start, 0)
    for _ in range(2):
        pltpu.make_async_copy(h_ref, xs_ref.at[pl.ds(0, TM), :], sem).wait()


def _dispatch(h2, d1, d2, n_slots):
    ttot = h2.shape[0]
    nt = ttot // TM
    smem = pl.BlockSpec((1, 1, TM), lambda i: (i, 0, 0), memory_space=pltpu.SMEM)
    xs0 = jnp.zeros((n_slots, D), jnp.float32)
    return pl.pallas_call(
        _dispatch_kernel,
        grid=(nt,),
        in_specs=[smem, smem, pl.BlockSpec((TM, D), lambda i: (i, 0)),
                  pl.BlockSpec(memory_space=pl.ANY)],
        out_specs=pl.BlockSpec(memory_space=pl.ANY),
        out_shape=jax.ShapeDtypeStruct((n_slots, D), jnp.float32),
        scratch_shapes=[pltpu.SemaphoreType.DMA(())],
        input_output_aliases={3: 0},
        compiler_params=_cparams(("arbitrary",)),
        name="dispatch",
    )(d1, d2, h2, xs0)


def _expert_kernel(be_ref, nu_ref, x_ref, wg_ref, wu_ref, wd_ref, y_ref, wgu_b, wd_b):
    b = pl.program_id(0)
    new_expert = jnp.logical_or(b == 0, be_ref[b] != be_ref[jnp.maximum(b - 1, 0)])

    @pl.when(jnp.logical_and(b < nu_ref[0], new_expert))
    def _():
        wgu_b[:, :D_EXPERT] = wg_ref[0].astype(jnp.bfloat16)
        wgu_b[:, D_EXPERT:] = wu_ref[0].astype(jnp.bfloat16)
        wd_b[...] = wd_ref[0].astype(jnp.bfloat16)

    @pl.when(b < nu_ref[0])
    def _():
        x = x_ref[...].astype(jnp.bfloat16)
        gu = jnp.dot(x, wgu_b[...], preferred_element_type=jnp.float32)
        gt = gu[:, :D_EXPERT]
        hm = (gt * jax.nn.sigmoid(gt) * gu[:, D_EXPERT:]).astype(jnp.bfloat16)
        y_ref[...] = jnp.dot(hm, wd_b[...], preferred_element_type=jnp.float32)

    @pl.when(b >= nu_ref[0])
    def _():
        y_ref[...] = jnp.zeros(y_ref.shape, jnp.float32)


def _experts(xs, blk_e, n_used, wg, wu, wd, l):
    n_slots = xs.shape[0]
    nblk = n_slots // EBLK

    def row_map(b, be, nu):
        return (jnp.minimum(b, nu[0] - 1), 0)

    def w_map(b, be, nu):
        return (l, be[b], 0, 0)

    grid_spec = pltpu.PrefetchScalarGridSpec(
        num_scalar_prefetch=2,
        grid=(nblk,),
        in_specs=[
            pl.BlockSpec((EBLK, D), row_map),
            pl.BlockSpec((None, 1, D, D_EXPERT), w_map),
            pl.BlockSpec((None, 1, D, D_EXPERT), w_map),
            pl.BlockSpec((None, 1, D_EXPERT, D), w_map),
        ],
        out_specs=pl.BlockSpec((EBLK, D), lambda b, be, nu: (b, 0)),
        scratch_shapes=[pltpu.VMEM((D, 2 * D_EXPERT), jnp.bfloat16),
                        pltpu.VMEM((D_EXPERT, D), jnp.bfloat16)],
    )
    return pl.pallas_call(
        _expert_kernel,
        grid_spec=grid_spec,
        out_shape=jax.ShapeDtypeStruct((n_slots, D), jnp.float32),
        compiler_params=_cparams(("arbitrary",)),
        name="experts",
    )(blk_e, n_used, xs, wg, wu, wd)


def _combine_kernel(d1_ref, d2_ref, x1_ref, route_ref, mod_ref, ln2g_ref, ln2b_ref, ys_ref,
                    o_ref, y1_buf, y2_buf, sem, *, alpha):
    def start(r8, carry):
        base = pl.multiple_of(r8 * ROW_GROUP, ROW_GROUP)
        for d_ref, buf in ((d1_ref, y1_buf), (d2_ref, y2_buf)):
            rows = buf.at[pl.ds(base, ROW_GROUP), :]
            for j in range(ROW_GROUP):
                pltpu.make_async_copy(ys_ref.at[pl.ds(d_ref[0, 0, base + j], 1), :],
                                      rows.at[pl.ds(j, 1), :], sem).start()
        return carry

    lax.fori_loop(0, TM // ROW_GROUP, start, 0)
    for buf in (y1_buf, y2_buf):
        pltpu.make_async_copy(ys_ref.at[pl.ds(0, TM), :], buf, sem).wait()

    route = route_ref[...]
    g1 = route[:, 4:5]
    g2 = route[:, 5:6]
    o = g1 * y1_buf[...] + g2 * y2_buf[...]
    z = alpha * x1_ref[...] + _mod_rows(mod_ref, 5) * o
    o_ref[...] = _ln_plain(z) * ln2g_ref[...] + ln2b_ref[...]


def _combine(x1, route, mod, ln2g, ln2b, ys, d1, d2, alpha, l, latents_only):
    ttot = x1.shape[0]
    nt = ttot // TM
    smem = pl.BlockSpec((1, 1, TM), lambda i: (i, 0, 0), memory_space=pltpu.SMEM)
    kern = functools.partial(_combine_kernel, alpha=alpha)
    if latents_only:
        out_spec = pl.BlockSpec((TM, D), lambda i: (jnp.maximum(i - 1, 0), 0))
        out_rows = ttot - TM
    else:
        out_spec = pl.BlockSpec((TM, D), lambda i: (i, 0))
        out_rows = ttot
    return pl.pallas_call(
        kern,
        grid=(nt,),
        in_specs=[smem, smem,
                  pl.BlockSpec((TM, D), lambda i: (i, 0)),
                  pl.BlockSpec((TM, LANES), lambda i: (i, 0)),
                  _layer((8, N_MOD * D), l), _layer((1, D), l), _layer((1, D), l),
                  pl.BlockSpec(memory_space=pl.ANY)],
        out_specs=out_spec,
        out_shape=jax.ShapeDtypeStruct((out_rows, D), jnp.float32),
        scratch_shapes=[pltpu.VMEM((TM, D), jnp.float32), pltpu.VMEM((TM, D), jnp.float32),
                        pltpu.SemaphoreType.DMA(())],
        compiler_params=_cparams(("arbitrary",)),
        name="combine",
    )(d1, d2, x1, route, mod, ln2g, ln2b, ys)


def _rope_tables_t(n_lat, n_ctx):
    rows = n_lat // GRID_W
    row = jnp.broadcast_to(jnp.arange(rows)[:, None], (rows, GRID_W)).reshape(-1)
    col = jnp.broadcast_to(jnp.arange(GRID_W)[None, :], (rows, GRID_W)).reshape(-1)
    n_freq = HD // 4
    inv = ROPE_THETA ** (-jnp.arange(n_freq, dtype=jnp.float32) / n_freq)
    pos = jnp.stack([row, col], axis=-1).astype(jnp.float32)
    ang = (pos[:, :, None] * inv).reshape(n_lat, 2 * n_freq)
    cos = jnp.concatenate([jnp.ones((n_ctx, 2 * n_freq), jnp.float32), jnp.cos(ang)], axis=0)
    sin = jnp.concatenate([jnp.zeros((n_ctx, 2 * n_freq), jnp.float32), jnp.sin(ang)], axis=0)
    return cos.T, sin.T


def _row(v):
    return v.reshape(1, -1)


def kernel(x, c, ctx, c_ctx, w_mod, b_mod, w_in, q_gain, k_gain, w_pool, pool_scale, w_dw, b_dw,
           cv_ln_g, cv_ln_b, w_cv_pw, w_out, ln1_g, ln1_b, ln2_g, ln2_b, w_rg, b_rg, w_re, b_re,
           w_e_gate, w_e_up, w_e_down):
    depth = w_mod.shape[0]
    n_lat = x.shape[1]
    n_ctx = ctx.shape[1]
    assert x.shape[0] == 1 and n_ctx == TM and n_lat % TM == 0 and n_lat % GRID_W == 0
    ttot = n_ctx + n_lat
    alpha = float((2 * depth) ** 0.25)
    n_blk = -(-(2 * ttot) // EBLK) + N_EXPERTS
    n_slots = n_blk * EBLK

    xa = jnp.concatenate([ctx[0], x[0]], axis=0)
    cc = jnp.concatenate([c, c_ctx[None], jnp.zeros((6, D), jnp.float32)], axis=0)
    mod = _modulation(cc, w_mod, b_mod)
    cos_t, sin_t = _rope_tables_t(n_lat, n_ctx)

    n_qkv = ATTN_W + 2 * KV_W
    bf = jnp.bfloat16
    w_t = jnp.swapaxes(w_in[:, :, :n_qkv], 1, 2).astype(bf)
    w_n = w_in[:, :, n_qkv:].astype(bf)
    qg = jnp.broadcast_to(q_gain[:, :, None], (depth, HD, TM))
    kg = jnp.broadcast_to(k_gain[:, :, None], (depth, HD, TM))
    n_pg = len(POOL_WINDOWS)
    wpool = (jnp.eye(n_pg, dtype=jnp.float32)[None, :, None, :, None] * w_pool[:, :, :, None, :]
             ).reshape(depth, POOL_W, POOL_W).astype(bf)
    wdw = jnp.pad(w_dw, ((0, 0), (0, 32 - CONV_K), (0, 0)))
    n_r = N_GROUPS + N_EXPERTS
    wr = jnp.concatenate([w_rg, w_re, jnp.zeros((depth, D, LANES - n_r), jnp.float32)], axis=2)
    br = jnp.concatenate([b_rg, b_re, jnp.zeros((depth, LANES - n_r), jnp.float32)], axis=1)
    wr_hi = wr.astype(bf)
    wr = jnp.concatenate([wr_hi, (wr - wr_hi.astype(jnp.float32)).astype(bf)], axis=2)
    wpw = w_cv_pw.astype(bf)
    wout = w_out.astype(bf)

    def rows(v):
        return v[:, None, :]

    e_ids = jnp.arange(N_EXPERTS, dtype=jnp.int32)
    blk_start = jnp.arange(n_blk, dtype=jnp.int32) * EBLK
    for l in range(depth):
        q_t, k2, v_t, kn2, u, ag = _inproj(xa, mod, w_t, w_n, qg, kg, cos_t, sin_t, l)
        kmax = jnp.sqrt(jnp.max(kn2[:, :, 0, 0], axis=0))
        attn = _attention(q_t, k2, v_t, kmax)
        x1, h2, route, cnt = _mixer(
            xa, attn, u, ag, mod, wpool, rows(pool_scale), wdw, rows(b_dw), rows(cv_ln_g),
            rows(cv_ln_b), wpw, wout, rows(ln1_g), rows(ln1_b), wr, rows(br), alpha, l)

        counts = cnt[0, :N_EXPERTS].astype(jnp.int32)
        padded = (counts + EBLK - 1) // EBLK * EBLK
        pends = jnp.cumsum(padded)
        pstarts = pends - padded
        ri = route[:, :4].astype(jnp.int32)

        def dest(e, r):
            start = jnp.sum(jnp.where(e[:, None] == e_ids[None, :], pstarts[None, :], 0), axis=1)
            return (start + r).reshape(ttot // TM, 1, TM)

        d1 = dest(ri[:, 0], ri[:, 2])
        d2 = dest(ri[:, 1], ri[:, 3])
        n_used = (pends[-1] // EBLK).astype(jnp.int32).reshape(1)
        first_row = jnp.minimum(blk_start, pends[-1] - 1)
        blk_e = jnp.sum((pends[None, :] <= first_row[:, None]).astype(jnp.int32), axis=1)
        blk_e = jnp.minimum(blk_e, N_EXPERTS - 1)

        xs = _dispatch(h2, d1, d2, n_slots)
        ys = _experts(xs, blk_e, n_used, w_e_gate, w_e_up, w_e_down, l)
        xa = _combine(x1, route, mod, rows(ln2_g), rows(ln2_b), ys, d1, d2, alpha, l,
                      latents_only=(l == depth - 1))
    return xa[None]
```

```python
import functools

import jax
import jax.numpy as jnp
from jax import lax
from jax.experimental import pallas as pl
from jax.experimental.pallas import tpu as pltpu

D = 1024
GRID_W = 64
HD = 64
NQ = 8
NKV = 2
QPK = NQ // NKV
ATTN_W = NQ * HD
KV_W = NKV * HD
POOL_W = 256
POOL_WINDOWS = (2, 4, 8, 16)
POOL_GW = POOL_W // len(POOL_WINDOWS)
CONV_W = 256
CONV_K = 31
N_GROUPS = 4
EPG = 8
N_EXPERTS = N_GROUPS * EPG
D_EXPERT = 512
N_MOD = 6
ROPE_THETA = 10000.0
EPS = 1e-6

TM = 256
HALO = 16
EBLK = 256
LANES = 128
VROWS = HD + 16
LOG2E = 1.4426950408889634
BOUND_MARGIN = 1.0 + 2.0 ** -5
VMEM_LIMIT = 56 * 1024 * 1024
NEG_BIG = -1e30

_HI = lax.Precision.HIGHEST


def _cparams(sem):
    return pltpu.CompilerParams(dimension_semantics=sem, vmem_limit_bytes=VMEM_LIMIT)


def _full(shape):
    n = len(shape)
    return pl.BlockSpec(shape, lambda *a, n=n: (0,) * n)


def _layer(shape, l):
    n = len(shape)
    return pl.BlockSpec((None,) + tuple(shape), lambda *a, n=n: (l,) + (0,) * n)


def _mod_kernel(cc_ref, w_ref, b_ref, o_ref):
    cc = cc_ref[...]
    a = cc * jax.nn.sigmoid(cc)
    o_ref[0] = jnp.dot(a, w_ref[0], precision=_HI, preferred_element_type=jnp.float32) + b_ref[0]


def _modulation(cc, w_mod, b_mod):
    depth = w_mod.shape[0]
    nw = w_mod.shape[2]
    bn = 1024
    return pl.pallas_call(
        _mod_kernel,
        grid=(depth, nw // bn),
        in_specs=[
            pl.BlockSpec((8, D), lambda l, j: (0, 0)),
            pl.BlockSpec((1, D, bn), lambda l, j: (l, 0, j)),
            pl.BlockSpec((1, 1, bn), lambda l, j: (l, 0, j)),
        ],
        out_specs=pl.BlockSpec((1, 8, bn), lambda l, j: (l, 0, j)),
        out_shape=jax.ShapeDtypeStruct((depth, 8, nw), jnp.float32),
        compiler_params=_cparams(("arbitrary", "arbitrary")),
        name="modulation",
    )(cc, w_mod, b_mod.reshape(depth, 1, nw))


def _ln_plain(x):
    mu = jnp.mean(x, axis=-1, keepdims=True)
    xc = x - mu
    var = jnp.mean(xc * xc, axis=-1, keepdims=True)
    return xc * lax.rsqrt(var + EPS)


def _mod_rows(mod_ref, idx):
    row = jnp.where(pl.program_id(0) == 0, 1, 0)
    return mod_ref[pl.ds(row, 1), idx * D:(idx + 1) * D]


def _rms_rope_t(t, gain, cos, sin):
    ms = jnp.mean(t * t, axis=0, keepdims=True)
    t = t * lax.rsqrt(ms + EPS) * gain
    outs = []
    for a in range(2):
        t1 = t[32 * a:32 * a + 16]
        t2 = t[32 * a + 16:32 * a + 32]
        c = cos[16 * a:16 * a + 16]
        s = sin[16 * a:16 * a + 16]
        outs.append(t1 * c - t2 * s)
        outs.append(t2 * c + t1 * s)
    return jnp.concatenate(outs, axis=0)


def _inproj_kernel(x_ref, mod_ref, wt_ref, wn_ref, qg_ref, kg_ref, cos_ref, sin_ref,
                   q_ref, k_ref, v_ref, kmax_ref, u_ref, ag_ref):
    x = x_ref[...]
    h = _ln_plain(x) * (1.0 + _mod_rows(mod_ref, 1)) + _mod_rows(mod_ref, 0)
    hb = h.astype(jnp.bfloat16)
    qkv_t = lax.dot_general(wt_ref[...], hb, (((1,), (1,)), ((), ())),
                            preferred_element_type=jnp.float32)
    cos = cos_ref[...]
    sin = sin_ref[...]
    qg = qg_ref[...]
    kg = kg_ref[...]
    ones_row = jnp.where(lax.broadcasted_iota(jnp.int32, (HD, TM), 0) == 0, 1.0, 0.0)
    ones_pad = jnp.where(lax.broadcasted_iota(jnp.int32, (VROWS - HD, TM), 0) == 0, 1.0, 0.0)
    for hq in range(NQ):
        t = _rms_rope_t(qkv_t[hq * HD:(hq + 1) * HD], qg, cos, sin) * (LOG2E * HD ** -0.5)
        nq = jnp.sqrt(jnp.sum(t * t, axis=0, keepdims=True))
        q_ref[hq // QPK, hq % QPK] = jnp.concatenate(
            [t, ones_row * (-BOUND_MARGIN * nq)], axis=0).astype(jnp.bfloat16)
    for g in range(NKV):
        kt = _rms_rope_t(qkv_t[ATTN_W + g * HD:ATTN_W + (g + 1) * HD], kg, cos, sin)
        kn = jnp.max(jnp.sum(kt * kt, axis=0, keepdims=True), axis=1, keepdims=True)
        kmax_ref[0, g] = jnp.broadcast_to(kn, (8, LANES))
        k_ref[g] = jnp.concatenate([kt, ones_row], axis=0).T.astype(jnp.bfloat16)
        vt = qkv_t[ATTN_W + KV_W + g * HD:ATTN_W + KV_W + (g + 1) * HD]
        v_ref[g] = jnp.concatenate([vt, ones_pad], axis=0).astype(jnp.bfloat16)
    uag = jnp.dot(hb, wn_ref[...], preferred_element_type=jnp.float32)
    u_ref[...] = uag[:, :POOL_W]
    a = uag[:, POOL_W:POOL_W + CONV_W]
    gt = uag[:, POOL_W + CONV_W:]
    ag_ref[...] = a * jax.nn.sigmoid(gt)


def _inproj(xa, mod, w_t, w_n, qg, kg, cos_t, sin_t, l):
    ttot = xa.shape[0]
    nt = ttot // TM
    return pl.pallas_call(
        _inproj_kernel,
        grid=(nt,),
        in_specs=[
            pl.BlockSpec((TM, D), lambda i: (i, 0)),
            _layer((8, N_MOD * D), l),
            _layer((ATTN_W + 2 * KV_W, D), l),
            _layer((D, POOL_W + 2 * CONV_W), l),
            _layer((HD, TM), l),
            _layer((HD, TM), l),
            pl.BlockSpec((32, TM), lambda i: (0, i)),
            pl.BlockSpec((32, TM), lambda i: (0, i)),
        ],
        out_specs=[
            pl.BlockSpec((NKV, QPK, 2 * HD, TM), lambda i: (0, 0, 0, i)),
            pl.BlockSpec((NKV, TM, 2 * HD), lambda i: (0, i, 0)),
            pl.BlockSpec((NKV, VROWS, TM), lambda i: (0, 0, i)),
            pl.BlockSpec((1, NKV, 8, LANES), lambda i: (i, 0, 0, 0)),
            pl.BlockSpec((TM, POOL_W), lambda i: (i, 0)),
            pl.BlockSpec((TM, CONV_W), lambda i: (i, 0)),
        ],
        out_shape=[
            jax.ShapeDtypeStruct((NKV, QPK, 2 * HD, ttot), jnp.bfloat16),
            jax.ShapeDtypeStruct((NKV, ttot, 2 * HD), jnp.bfloat16),
            jax.ShapeDtypeStruct((NKV, VROWS, ttot), jnp.bfloat16),
            jax.ShapeDtypeStruct((nt, NKV, 8, LANES), jnp.float32),
            jax.ShapeDtypeStruct((ttot, POOL_W), jnp.float32),
            jax.ShapeDtypeStruct((ttot, CONV_W), jnp.float32),
        ],
        compiler_params=_cparams(("arbitrary",)),
        name="inproj",
    )(xa, mod, w_t, w_n, qg, kg, cos_t, sin_t)


TK = 256
CH_A, CH_B = 3, 2
PAIR_KEYS = (CH_A + CH_B) * TK
L_MIN = 2.0 ** -60


def _attn_kernel(kmax_ref, q_ref, qn_ref, k_ref, v_ref, o_ref, qs_ref, sa_ref, sb_ref, acc_ref,
                 m_ref, *, n_chunks, n_pairs):
    g = pl.program_id(0)
    i = pl.program_id(1)
    nt = pl.num_programs(1)
    nch = jnp.where(i == 0, 1, n_chunks)
    row = lax.broadcasted_iota(jnp.int32, (2 * HD, TM), 0)
    kmax = kmax_ref[g]

    def prep(src_ref):
        for hh in range(QPK):
            qf = src_ref[0, hh].astype(jnp.float32)
            qs_ref[hh] = jnp.where(row == HD, qf * kmax, qf).astype(jnp.bfloat16)

    acc_ref[...] = jnp.zeros(acc_ref.shape, jnp.float32)

    def step(nxt, cur):
        for kk in range(max(grp[1] for grp in (nxt, cur) if grp is not None)):
            do_nxt = nxt is not None and kk < nxt[1]
            do_cur = cur is not None and kk < cur[1]
            rows = slice(kk * TK, (kk + 1) * TK)
            if do_nxt:
                kc = k_ref[0, pl.ds(pl.multiple_of(nxt[0] + kk * TK, TK), TK), :]
            if do_cur:
                vc = v_ref[0, :, pl.ds(pl.multiple_of(cur[0] + kk * TK, TK), TK)]
            for hh in range(QPK):
                if do_nxt:
                    nxt[2][hh, rows, :] = jnp.dot(kc, qs_ref[hh], preferred_element_type=jnp.float32)
                if do_cur:
                    p = jnp.exp2(cur[2][hh, rows, :]).astype(jnp.bfloat16)
                    acc_ref[hh] += jnp.dot(vc, p, preferred_element_type=jnp.float32)

    def group_a(m):
        return (m * PAIR_KEYS, CH_A, sa_ref)

    def group_b(m):
        return (m * PAIR_KEYS + CH_A * TK, CH_B, sb_ref)

    def pair(m):
        step(group_b(m), group_a(m))
        step(group_a(m + 1), group_b(m))

    @pl.when(i == 0)
    def _():
        prep(q_ref)
        kc = k_ref[0, 0:TM, :]
        vc = v_ref[0, :, 0:TM]
        for hh in range(QPK):
            p = jnp.exp2(jnp.dot(kc, qs_ref[hh], preferred_element_type=jnp.float32))
            acc_ref[hh] = jnp.dot(vc, p.astype(jnp.bfloat16), preferred_element_type=jnp.float32)
        prep(qn_ref)
        step(group_a(0), None)

    @pl.when(i > 0)
    def _():
        n_iters = (n_pairs - 1) // 2

        def body(j, carry):
            pair(2 * j)
            pair(2 * j + 1)
            return carry

        lax.fori_loop(0, n_iters, body, 0)
        for m in range(2 * n_iters, n_pairs - 1):
            pair(m)
        step(group_b(n_pairs - 1), group_a(n_pairs - 1))

        @pl.when(i < nt - 1)
        def _():
            prep(qn_ref)
            step(group_a(0), group_b(n_pairs - 1))

        @pl.when(i == nt - 1)
        def _():
            step(None, group_b(n_pairs - 1))

    l_min = jnp.min(acc_ref[:, HD:HD + 1, :])

    @pl.when(l_min < L_MIN)
    def _():
        for hh in range(QPK):
            qs_ref[hh] = jnp.where(row == HD, 0.0, q_ref[0, hh].astype(jnp.float32)
                                   ).astype(jnp.bfloat16)
        acc_ref[...] = jnp.zeros(acc_ref.shape, jnp.float32)
        m_ref[...] = jnp.full(m_ref.shape, NEG_BIG, jnp.float32)

        def body2(c, carry):
            start = pl.multiple_of(c * TK, TK)
            kc = k_ref[0, pl.ds(start, TK), :]
            vc = v_ref[0, :, pl.ds(start, TK)]
            for hh in range(QPK):
                s = jnp.dot(kc, qs_ref[hh], preferred_element_type=jnp.float32)
                m_old = m_ref[hh]
                m_new = jnp.maximum(m_old, jnp.max(s, axis=0, keepdims=True))
                p = jnp.exp2(s - m_new).astype(jnp.bfloat16)
                acc_ref[hh] = jnp.exp2(m_old - m_new) * acc_ref[hh] + jnp.dot(
                    vc, p, preferred_element_type=jnp.float32)
                m_ref[hh] = m_new
            return carry

        lax.fori_loop(0, nch, body2, 0)

        @pl.when(i < nt - 1)
        def _():
            prep(qn_ref)

    outs = [acc_ref[hh, 0:HD, :] / acc_ref[hh, HD:HD + 1, :] for hh in range(QPK)]
    o_ref[...] = jnp.concatenate(outs, axis=0).T


def _attention(q_t, k2, v_t, kmax):
    ttot = k2.shape[1]
    nt = ttot // TM
    assert ttot % PAIR_KEYS == 0
    kern = functools.partial(_attn_kernel, n_chunks=ttot // TK, n_pairs=ttot // PAIR_KEYS)
    grid_spec = pltpu.PrefetchScalarGridSpec(
        num_scalar_prefetch=1,
        grid=(NKV, nt),
        in_specs=[
            pl.BlockSpec((1, QPK, 2 * HD, TM), lambda g, i, km: (g, 0, 0, i)),
            pl.BlockSpec((1, QPK, 2 * HD, TM),
                         lambda g, i, km: (g, 0, 0, jnp.minimum(i + 1, nt - 1))),
            pl.BlockSpec((1, ttot, 2 * HD), lambda g, i, km: (g, 0, 0)),
            pl.BlockSpec((1, VROWS, ttot), lambda g, i, km: (g, 0, 0)),
        ],
        out_specs=pl.BlockSpec((TM, QPK * HD), lambda g, i, km: (i, g)),
        scratch_shapes=[
            pltpu.VMEM((QPK, 2 * HD, TM), jnp.bfloat16),
            pltpu.VMEM((QPK, CH_A * TK, TM), jnp.float32),
            pltpu.VMEM((QPK, CH_B * TK, TM), jnp.float32),
            pltpu.VMEM((QPK, VROWS, TM), jnp.float32),
            pltpu.VMEM((QPK, 1, TM), jnp.float32),
        ],
    )
    return pl.pallas_call(
        kern,
        grid_spec=grid_spec,
        out_shape=jax.ShapeDtypeStruct((ttot, ATTN_W), jnp.float32),
        compiler_params=_cparams(("arbitrary", "arbitrary")),
        name="attention",
    )(kmax, q_t, q_t, k2, v_t)


def _fill_ext(ext_ref, cur_ref, prev_ref, next_ref, left_ok, right_ok):
    zero = jnp.zeros((HALO, cur_ref.shape[1]), jnp.float32)
    ext_ref[0:HALO, :] = jnp.where(left_ok, prev_ref[...], zero)
    ext_ref[HALO:HALO + TM, :] = cur_ref[...]
    ext_ref[HALO + TM:HALO + TM + HALO, :] = jnp.where(right_ok, next_ref[...], zero)


def _mix_kernel(x_ref, attn_ref, u_ref, up_ref, un_ref, ag_ref, agp_ref, agn_ref, mod_ref,
                wpool_ref, pscale_ref, wdw_ref, bdw_ref, cvg_ref, cvb_ref, wpw_ref, wout_ref,
                ln1g_ref, ln1b_ref, wr_ref, br_ref,
                x1_ref, h2_ref, route_ref, cnt_ref,
                uext_ref, agext_ref, base_ref, *, alpha, n_lat):
    i = pl.program_id(0)
    nt = pl.num_programs(0)
    is_ctx = i == 0
    left_ok = i >= 2
    right_ok = jnp.logical_and(i >= 1, i < nt - 1)

    @pl.when(i == 0)
    def _():
        base_ref[...] = jnp.zeros(base_ref.shape, jnp.float32)

    _fill_ext(uext_ref, u_ref, up_ref, un_ref, left_ok, right_ok)
    _fill_ext(agext_ref, ag_ref, agp_ref, agn_ref, left_ok, right_ok)

    def ush(off, rows=TM):
        return uext_ref[HALO + off:HALO + off + rows, :]

    a2 = ush(-8, TM + 15) + ush(-7, TM + 15)
    a4 = a2[0:TM + 13] + a2[2:TM + 15]
    a8 = a4[0:TM + 9] + a4[4:TM + 13]
    a16 = a8[0:TM] + a8[8:TM + 8]
    sums = (a2[7:7 + TM], a4[6:6 + TM], a8[4:4 + TM], a16)
    pos = lax.broadcasted_iota(jnp.int32, (TM, 1), 0) + jnp.where(is_ctx, 0, (i - 1) * TM)
    seq_n = jnp.where(is_ctx, TM, n_lat)
    u_cur = u_ref[...]
    lane = lax.broadcasted_iota(jnp.int32, (TM, POOL_W), 1)
    pooled = jnp.zeros((TM, POOL_W), jnp.float32)
    for gi, w in enumerate(POOL_WINDOWS):
        lo = jnp.maximum(pos - w // 2, 0)
        hi = jnp.minimum(pos + (w - w // 2), seq_n)
        inv = 1.0 / (hi - lo).astype(jnp.float32)
        in_group = jnp.logical_and(lane >= gi * POOL_GW, lane < (gi + 1) * POOL_GW)
        pooled = jnp.where(in_group, sums[gi] * inv - u_cur, pooled)
    y_pool = jnp.dot(pooled.astype(jnp.bfloat16), wpool_ref[...],
                     preferred_element_type=jnp.float32) * pscale_ref[...]

    conv = jnp.zeros((TM, CONV_W), jnp.float32) + bdw_ref[...]
    for k in range(CONV_K):
        off = HALO + k - CONV_K // 2
        conv = conv + agext_ref[off:off + TM, :] * wdw_ref[k:k + 1, :]
    cn = _ln_plain(conv) * cvg_ref[...] + cvb_ref[...]
    cn = cn * jax.nn.sigmoid(cn)
    y_conv = jnp.dot(cn.astype(jnp.bfloat16), wpw_ref[...], preferred_element_type=jnp.float32)

    y = jnp.dot(attn_ref[...].astype(jnp.bfloat16), wout_ref[0:ATTN_W, :],
                preferred_element_type=jnp.float32)
    y = y + jnp.dot(y_pool.astype(jnp.bfloat16), wout_ref[ATTN_W:ATTN_W + POOL_W, :],
                    preferred_element_type=jnp.float32)
    y = y + jnp.dot(y_conv.astype(jnp.bfloat16), wout_ref[ATTN_W + POOL_W:, :],
                    preferred_element_type=jnp.float32)
    x1 = _ln_plain(alpha * x_ref[...] + _mod_rows(mod_ref, 2) * y) * ln1g_ref[...] + ln1b_ref[...]
    x1_ref[...] = x1
    h2 = _ln_plain(x1) * (1.0 + _mod_rows(mod_ref, 4)) + _mod_rows(mod_ref, 3)
    h2_ref[...] = h2

    h_hi = h2.astype(jnp.bfloat16)
    h_lo = (h2 - h_hi.astype(jnp.float32)).astype(jnp.bfloat16)
    hw = jnp.dot(h_hi, wr_ref[...], preferred_element_type=jnp.float32)
    logits = (hw[:, :LANES] + hw[:, LANES:] + br_ref[...]
              + jnp.dot(h_lo, wr_ref[:, :LANES], preferred_element_type=jnp.float32))
    ln = lax.broadcasted_iota(jnp.int32, (TM, LANES), 1)
    neg = jnp.float32(-jnp.inf)
    is_g = ln < N_GROUPS
    lg = jnp.where(is_g, logits, neg)
    mg = jnp.max(lg, axis=-1, keepdims=True)
    gsel = jnp.min(jnp.where(lg == mg, ln, LANES), axis=-1, keepdims=True)
    p_group = 1.0 / jnp.sum(jnp.exp(lg - mg), axis=-1, keepdims=True)
    e_lane = ln - N_GROUPS
    in_sel = jnp.logical_and(e_lane >= gsel * EPG, e_lane < (gsel + 1) * EPG)
    le = jnp.where(in_sel, logits, neg)
    v1 = jnp.max(le, axis=-1, keepdims=True)
    e1 = jnp.min(jnp.where(le == v1, e_lane, LANES), axis=-1, keepdims=True)
    le2 = jnp.where(e_lane == e1, neg, le)
    v2 = jnp.max(le2, axis=-1, keepdims=True)
    e2 = jnp.min(jnp.where(le2 == v2, e_lane, LANES), axis=-1, keepdims=True)
    ex = jnp.exp(v2 - v1)
    gate1 = p_group / (1.0 + ex)
    gate2 = p_group * ex / (1.0 + ex)

    oh = jnp.logical_or(ln == e1, ln == e2)
    ohb = jnp.where(oh, 1.0, 0.0).astype(jnp.bfloat16)
    r_i = lax.broadcasted_iota(jnp.int32, (TM, TM), 0)
    c_i = lax.broadcasted_iota(jnp.int32, (TM, TM), 1)
    tri = jnp.where(c_i < r_i, 1.0, 0.0).astype(jnp.bfloat16)
    before = jnp.dot(tri, ohb, preferred_element_type=jnp.float32) + base_ref[...]
    rank1 = jnp.sum(jnp.where(ln == e1, before, 0.0), axis=-1, keepdims=True)
    rank2 = jnp.sum(jnp.where(ln == e2, before, 0.0), axis=-1, keepdims=True)
    base_ref[...] = base_ref[...] + jnp.sum(ohb.astype(jnp.float32), axis=0, keepdims=True)
    cnt_ref[...] = jnp.broadcast_to(base_ref[...], cnt_ref.shape)

    route = jnp.zeros((TM, LANES), jnp.float32)
    for j, val in enumerate((e1.astype(jnp.float32), e2.astype(jnp.float32), rank1, rank2,
                             gate1, gate2)):
        route = jnp.where(ln == j, val, route)
    route_ref[...] = route


def _mixer(xa, attn, u, ag, mod, wpool, pscale, wdw, bdw, cvg, cvb, wpw, wout, ln1g, ln1b,
           wr, br, alpha, l):
    ttot = xa.shape[0]
    nt = ttot // TM
    hb = TM // HALO
    nhb = ttot // HALO

    def tile(w):
        return pl.BlockSpec((TM, w), lambda i: (i, 0))

    def prev(w):
        return pl.BlockSpec((HALO, w), lambda i: (jnp.maximum(i * hb - 1, 0), 0))

    def nxt(w):
        return pl.BlockSpec((HALO, w), lambda i: (jnp.minimum((i + 1) * hb, nhb - 1), 0))

    kern = functools.partial(_mix_kernel, alpha=alpha, n_lat=ttot - TM)
    return pl.pallas_call(
        kern,
        grid=(nt,),
        in_specs=[
            tile(D), tile(ATTN_W),
            tile(POOL_W), prev(POOL_W), nxt(POOL_W),
            tile(CONV_W), prev(CONV_W), nxt(CONV_W),
            _layer((8, N_MOD * D), l),
            _layer((POOL_W, POOL_W), l), _layer((1, POOL_W), l),
            _layer((32, CONV_W), l), _layer((1, CONV_W), l), _layer((1, CONV_W), l),
            _layer((1, CONV_W), l),
            _layer((CONV_W, CONV_W), l), _layer((D, D), l),
            _layer((1, D), l), _layer((1, D), l),
            _layer((D, 2 * LANES), l), _layer((1, LANES), l),
        ],
        out_specs=[tile(D), tile(D), tile(LANES), _full((8, LANES))],
        out_shape=[
            jax.ShapeDtypeStruct((ttot, D), jnp.float32),
            jax.ShapeDtypeStruct((ttot, D), jnp.float32),
            jax.ShapeDtypeStruct((ttot, LANES), jnp.float32),
            jax.ShapeDtypeStruct((8, LANES), jnp.float32),
        ],
        scratch_shapes=[
            pltpu.VMEM((TM + 2 * HALO, POOL_W), jnp.float32),
            pltpu.VMEM((TM + 2 * HALO, CONV_W), jnp.float32),
            pltpu.VMEM((1, LANES), jnp.float32),
        ],
        compiler_params=_cparams(("arbitrary",)),
        name="mixer",
    )(xa, attn, u, u, u, ag, ag, ag, mod, wpool, pscale, wdw, bdw, cvg, cvb, wpw, wout,
      ln1g, ln1b, wr, br)


ROW_GROUP = 8


def _dispatch_kernel(d1_ref, d2_ref, h_ref, xs_in_ref, xs_ref, sem):
    del xs_in_ref

    def start(r8, carry):
        base = pl.multiple_of(r8 * ROW_GROUP, ROW_GROUP)
        rows = h_ref.at[pl.ds(base, ROW_GROUP), :]
        for j in range(ROW_GROUP):
            for d_ref in (d1_ref, d2_ref):
                pltpu.make_async_copy(rows.at[pl.ds(j, 1), :],
                                      xs_ref.at[pl.ds(d_ref[0, 0, base + j], 1), :], sem).start()
        return carry

    lax.fori_loop(0, TM // ROW_GROUP, start, 0)
    for _ in range(2):
        pltpu.make_async_copy(h_ref, xs_ref.at[pl.ds(0, TM), :], sem).wait()


def _dispatch(h2, d1, d2, n_slots):
    ttot = h2.shape[0]
    nt = ttot // TM
    smem = pl.BlockSpec((1, 1, TM), lambda i: (i, 0, 0), memory_space=pltpu.SMEM)
    xs0 = jnp.zeros((n_slots, D), jnp.float32)
    return pl.pallas_call(
        _dispatch_kernel,
        grid=(nt,),
        in_specs=[smem, smem, pl.BlockSpec((TM, D), lambda i: (i, 0)),
                  pl.BlockSpec(memory_space=pl.ANY)],
        out_specs=pl.BlockSpec(memory_space=pl.ANY),
        out_shape=jax.ShapeDtypeStruct((n_slots, D), jnp.float32),
        scratch_shapes=[pltpu.SemaphoreType.DMA(())],
        input_output_aliases={3: 0},
        compiler_params=_cparams(("arbitrary",)),
        name="dispatch",
    )(d1, d2, h2, xs0)


def _expert_kernel(be_ref, nu_ref, x_ref, wg_ref, wu_ref, wd_ref, y_ref, wgu_b, wd_b):
    b = pl.program_id(0)
    new_expert = jnp.logical_or(b == 0, be_ref[b] != be_ref[jnp.maximum(b - 1, 0)])

    @pl.when(jnp.logical_and(b < nu_ref[0], new_expert))
    def _():
        wgu_b[:, :D_EXPERT] = wg_ref[0].astype(jnp.bfloat16)
        wgu_b[:, D_EXPERT:] = wu_ref[0].astype(jnp.bfloat16)
        wd_b[...] = wd_ref[0].astype(jnp.bfloat16)

    @pl.when(b < nu_ref[0])
    def _():
        x = x_ref[...].astype(jnp.bfloat16)
        gu = jnp.dot(x, wgu_b[...], preferred_element_type=jnp.float32)
        gt = gu[:, :D_EXPERT]
        hm = (gt * jax.nn.sigmoid(gt) * gu[:, D_EXPERT:]).astype(jnp.bfloat16)
        y_ref[...] = jnp.dot(hm, wd_b[...], preferred_element_type=jnp.float32)

    @pl.when(b >= nu_ref[0])
    def _():
        y_ref[...] = jnp.zeros(y_ref.shape, jnp.float32)


def _experts(xs, blk_e, n_used, wg, wu, wd, l):
    n_slots = xs.shape[0]
    nblk = n_slots // EBLK

    def row_map(b, be, nu):
        return (jnp.minimum(b, nu[0] - 1), 0)

    def w_map(b, be, nu):
        return (l, be[b], 0, 0)

    grid_spec = pltpu.PrefetchScalarGridSpec(
        num_scalar_prefetch=2,
        grid=(nblk,),
        in_specs=[
            pl.BlockSpec((EBLK, D), row_map),
            pl.BlockSpec((None, 1, D, D_EXPERT), w_map),
            pl.BlockSpec((None, 1, D, D_EXPERT), w_map),
            pl.BlockSpec((None, 1, D_EXPERT, D), w_map),
        ],
        out_specs=pl.BlockSpec((EBLK, D), lambda b, be, nu: (b, 0)),
        scratch_shapes=[pltpu.VMEM((D, 2 * D_EXPERT), jnp.bfloat16),
                        pltpu.VMEM((D_EXPERT, D), jnp.bfloat16)],
    )
    return pl.pallas_call(
        _expert_kernel,
        grid_spec=grid_spec,
        out_shape=jax.ShapeDtypeStruct((n_slots, D), jnp.float32),
        compiler_params=_cparams(("arbitrary",)),
        name="experts",
    )(blk_e, n_used, xs, wg, wu, wd)


def _combine_kernel(d1_ref, d2_ref, x1_ref, route_ref, mod_ref, ln2g_ref, ln2b_ref, ys_ref,
                    o_ref, y1_buf, y2_buf, sem, *, alpha):
    def start(r8, carry):
        base = pl.multiple_of(r8 * ROW_GROUP, ROW_GROUP)
        for d_ref, buf in ((d1_ref, y1_buf), (d2_ref, y2_buf)):
            rows = buf.at[pl.ds(base, ROW_GROUP), :]
            for j in range(ROW_GROUP):
                pltpu.make_async_copy(ys_ref.at[pl.ds(d_ref[0, 0, base + j], 1), :],
                                      rows.at[pl.ds(j, 1), :], sem).start()
        return carry

    lax.fori_loop(0, TM // ROW_GROUP, start, 0)
    for buf in (y1_buf, y2_buf):
        pltpu.make_async_copy(ys_ref.at[pl.ds(0, TM), :], buf, sem).wait()

    route = route_ref[...]
    g1 = route[:, 4:5]
    g2 = route[:, 5:6]
    o = g1 * y1_buf[...] + g2 * y2_buf[...]
    z = alpha * x1_ref[...] + _mod_rows(mod_ref, 5) * o
    o_ref[...] = _ln_plain(z) * ln2g_ref[...] + ln2b_ref[...]


def _combine(x1, route, mod, ln2g, ln2b, ys, d1, d2, alpha, l, latents_only):
    ttot = x1.shape[0]
    nt = ttot // TM
    smem = pl.BlockSpec((1, 1, TM), lambda i: (i, 0, 0), memory_space=pltpu.SMEM)
    kern = functools.partial(_combine_kernel, alpha=alpha)
    if latents_only:
        out_spec = pl.BlockSpec((TM, D), lambda i: (jnp.maximum(i - 1, 0), 0))
        out_rows = ttot - TM
    else:
        out_spec = pl.BlockSpec((TM, D), lambda i: (i, 0))
        out_rows = ttot
    return pl.pallas_call(
        kern,
        grid=(nt,),
        in_specs=[smem, smem,
                  pl.BlockSpec((TM, D), lambda i: (i, 0)),
                  pl.BlockSpec((TM, LANES), lambda i: (i, 0)),
                  _layer((8, N_MOD * D), l), _layer((1, D), l), _layer((1, D), l),
                  pl.BlockSpec(memory_space=pl.ANY)],
        out_specs=out_spec,
        out_shape=jax.ShapeDtypeStruct((out_rows, D), jnp.float32),
        scratch_shapes=[pltpu.VMEM((TM, D), jnp.float32), pltpu.VMEM((TM, D), jnp.float32),
                        pltpu.SemaphoreType.DMA(())],
        compiler_params=_cparams(("arbitrary",)),
        name="combine",
    )(d1, d2, x1, route, mod, ln2g, ln2b, ys)


def _rope_tables_t(n_lat, n_ctx):
    rows = n_lat // GRID_W
    row = jnp.broadcast_to(jnp.arange(rows)[:, None], (rows, GRID_W)).reshape(-1)
    col = jnp.broadcast_to(jnp.arange(GRID_W)[None, :], (rows, GRID_W)).reshape(-1)
    n_freq = HD // 4
    inv = ROPE_THETA ** (-jnp.arange(n_freq, dtype=jnp.float32) / n_freq)
    pos = jnp.stack([row, col], axis=-1).astype(jnp.float32)
    ang = (pos[:, :, None] * inv).reshape(n_lat, 2 * n_freq)
    cos = jnp.concatenate([jnp.ones((n_ctx, 2 * n_freq), jnp.float32), jnp.cos(ang)], axis=0)
    sin = jnp.concatenate([jnp.zeros((n_ctx, 2 * n_freq), jnp.float32), jnp.sin(ang)], axis=0)
    return cos.T, sin.T


def _row(v):
    return v.reshape(1, -1)


def kernel(x, c, ctx, c_ctx, w_mod, b_mod, w_in, q_gain, k_gain, w_pool, pool_scale, w_dw, b_dw,
           cv_ln_g, cv_ln_b, w_cv_pw, w_out, ln1_g, ln1_b, ln2_g, ln2_b, w_rg, b_rg, w_re, b_re,
           w_e_gate, w_e_up, w_e_down):
    depth = w_mod.shape[0]
    n_lat = x.shape[1]
    n_ctx = ctx.shape[1]
    assert x.shape[0] == 1 and n_ctx == TM and n_lat % TM == 0 and n_lat % GRID_W == 0
    ttot = n_ctx + n_lat
    alpha = float((2 * depth) ** 0.25)
    n_blk = -(-(2 * ttot) // EBLK) + N_EXPERTS
    n_slots = n_blk * EBLK

    xa = jnp.concatenate([ctx[0], x[0]], axis=0)
    cc = jnp.concatenate([c, c_ctx[None], jnp.zeros((6, D), jnp.float32)], axis=0)
    mod = _modulation(cc, w_mod, b_mod)
    cos_t, sin_t = _rope_tables_t(n_lat, n_ctx)

    n_qkv = ATTN_W + 2 * KV_W
    bf = jnp.bfloat16
    w_t = jnp.swapaxes(w_in[:, :, :n_qkv], 1, 2).astype(bf)
    w_n = w_in[:, :, n_qkv:].astype(bf)
    qg = jnp.broadcast_to(q_gain[:, :, None], (depth, HD, TM))
    kg = jnp.broadcast_to(k_gain[:, :, None], (depth, HD, TM))
    n_pg = len(POOL_WINDOWS)
    wpool = (jnp.eye(n_pg, dtype=jnp.float32)[None, :, None, :, None] * w_pool[:, :, :, None, :]
             ).reshape(depth, POOL_W, POOL_W).astype(bf)
    wdw = jnp.pad(w_dw, ((0, 0), (0, 32 - CONV_K), (0, 0)))
    n_r = N_GROUPS + N_EXPERTS
    wr = jnp.concatenate([w_rg, w_re, jnp.zeros((depth, D, LANES - n_r), jnp.float32)], axis=2)
    br = jnp.concatenate([b_rg, b_re, jnp.zeros((depth, LANES - n_r), jnp.float32)], axis=1)
    wr_hi = wr.astype(bf)
    wr = jnp.concatenate([wr_hi, (wr - wr_hi.astype(jnp.float32)).astype(bf)], axis=2)
    wpw = w_cv_pw.astype(bf)
    wout = w_out.astype(bf)

    def rows(v):
        return v[:, None, :]

    e_ids = jnp.arange(N_EXPERTS, dtype=jnp.int32)
    blk_start = jnp.arange(n_blk, dtype=jnp.int32) * EBLK
    for l in range(depth):
        q_t, k2, v_t, kn2, u, ag = _inproj(xa, mod, w_t, w_n, qg, kg, cos_t, sin_t, l)
        kmax = jnp.sqrt(jnp.max(kn2[:, :, 0, 0], axis=0))
        attn = _attention(q_t, k2, v_t, kmax)
        x1, h2, route, cnt = _mixer(
            xa, attn, u, ag, mod, wpool, rows(pool_scale), wdw, rows(b_dw), rows(cv_ln_g),
            rows(cv_ln_b), wpw, wout, rows(ln1_g), rows(ln1_b), wr, rows(br), alpha, l)

        counts = cnt[0, :N_EXPERTS].astype(jnp.int32)
        padded = (counts + EBLK - 1) // EBLK * EBLK
        pends = jnp.cumsum(padded)
        pstarts = pends - padded
        ri = route[:, :4].astype(jnp.int32)

        def dest(e, r):
            start = jnp.sum(jnp.where(e[:, None] == e_ids[None, :], pstarts[None, :], 0), axis=1)
            return (start + r).reshape(ttot // TM, 1, TM)

        d1 = dest(ri[:, 0], ri[:, 2])
        d2 = dest(ri[:, 1], ri[:, 3])
        n_used = (pends[-1] // EBLK).astype(jnp.int32).reshape(1)
        first_row = jnp.minimum(blk_start, pends[-1] - 1)
        blk_e = jnp.sum((pends[None, :] <= first_row[:, None]).astype(jnp.int32), axis=1)
        blk_e = jnp.minimum(blk_e, N_EXPERTS - 1)

        xs = _dispatch(h2, d1, d2, n_slots)
        ys = _experts(xs, blk_e, n_used, w_e_gate, w_e_up, w_e_down, l)
        xa = _combine(x1, route, mod, rows(ln2_g), rows(ln2_b), ys, d1, d2, alpha, l,
                      latents_only=(l == depth - 1))
    return xa[None]
```

```python
import functools

import jax
import jax.numpy as jnp
from jax import lax
from jax.experimental import pallas as pl
from jax.experimental.pallas import tpu as pltpu

D = 1024
GRID_W = 64
HD = 64
NQ = 8
NKV = 2
QPK = NQ // NKV
ATTN_W = NQ * HD
KV_W = NKV * HD
POOL_W = 256
POOL_WINDOWS = (2, 4, 8, 16)
POOL_GW = POOL_W // len(POOL_WINDOWS)
CONV_W = 256
CONV_K = 31
N_GROUPS = 4
EPG = 8
N_EXPERTS = N_GROUPS * EPG
D_EXPERT = 512
N_MOD = 6
ROPE_THETA = 10000.0
EPS = 1e-6

TM = 256
HALO = 16
EBLK = 256
LANES = 128
VROWS = HD + 16
LOG2E = 1.4426950408889634
BOUND_MARGIN = 1.0 + 2.0 ** -5
VMEM_LIMIT = 56 * 1024 * 1024
NEG_BIG = -1e30

_HI = lax.Precision.HIGHEST


def _cparams(sem):
    return pltpu.CompilerParams(dimension_semantics=sem, vmem_limit_bytes=VMEM_LIMIT)


def _full(shape):
    n = len(shape)
    return pl.BlockSpec(shape, lambda *a, n=n: (0,) * n)


def _layer(shape, l):
    n = len(shape)
    return pl.BlockSpec((None,) + tuple(shape), lambda *a, n=n: (l,) + (0,) * n)


def _mod_kernel(cc_ref, w_ref, b_ref, o_ref):
    cc = cc_ref[...]
    a = cc * jax.nn.sigmoid(cc)
    o_ref[0] = jnp.dot(a, w_ref[0], precision=_HI, preferred_element_type=jnp.float32) + b_ref[0]


def _modulation(cc, w_mod, b_mod):
    depth = w_mod.shape[0]
    nw = w_mod.shape[2]
    bn = 1024
    return pl.pallas_call(
        _mod_kernel,
        grid=(depth, nw // bn),
        in_specs=[
            pl.BlockSpec((8, D), lambda l, j: (0, 0)),
            pl.BlockSpec((1, D, bn), lambda l, j: (l, 0, j)),
            pl.BlockSpec((1, 1, bn), lambda l, j: (l, 0, j)),
        ],
        out_specs=pl.BlockSpec((1, 8, bn), lambda l, j: (l, 0, j)),
        out_shape=jax.ShapeDtypeStruct((depth, 8, nw), jnp.float32),
        compiler_params=_cparams(("arbitrary", "arbitrary")),
        name="modulation",
    )(cc, w_mod, b_mod.reshape(depth, 1, nw))


def _ln_plain(x):
    mu = jnp.mean(x, axis=-1, keepdims=True)
    xc = x - mu
    var = jnp.mean(xc * xc, axis=-1, keepdims=True)
    return xc * lax.rsqrt(var + EPS)


def _mod_rows(mod_ref, idx):
    row = jnp.where(pl.program_id(0) == 0, 1, 0)
    return mod_ref[pl.ds(row, 1), idx * D:(idx + 1) * D]


def _rms_rope_t(t, gain, cos, sin):
    ms = jnp.mean(t * t, axis=0, keepdims=True)
    t = t * lax.rsqrt(ms + EPS) * gain
    outs = []
    for a in range(2):
        t1 = t[32 * a:32 * a + 16]
        t2 = t[32 * a + 16:32 * a + 32]
        c = cos[16 * a:16 * a + 16]
        s = sin[16 * a:16 * a + 16]
        outs.append(t1 * c - t2 * s)
        outs.append(t2 * c + t1 * s)
    return jnp.concatenate(outs, axis=0)


def _inproj_kernel(x_ref, mod_ref, wt_ref, wn_ref, qg_ref, kg_ref, cos_ref, sin_ref,
                   q_ref, k_ref, v_ref, kmax_ref, u_ref, ag_ref):
    x = x_ref[...]
    h = _ln_plain(x) * (1.0 + _mod_rows(mod_ref, 1)) + _mod_rows(mod_ref, 0)
    hb = h.astype(jnp.bfloat16)
    qkv_t = lax.dot_general(wt_ref[...], hb, (((1,), (1,)), ((), ())),
                            preferred_element_type=jnp.float32)
    cos = cos_ref[...]
    sin = sin_ref[...]
    qg = qg_ref[...]
    kg = kg_ref[...]
    ones_row = jnp.where(lax.broadcasted_iota(jnp.int32, (HD, TM), 0) == 0, 1.0, 0.0)
    ones_pad = jnp.where(lax.broadcasted_iota(jnp.int32, (VROWS - HD, TM), 0) == 0, 1.0, 0.0)
    for hq in range(NQ):
        t = _rms_rope_t(qkv_t[hq * HD:(hq + 1) * HD], qg, cos, sin) * (LOG2E * HD ** -0.5)
        nq = jnp.sqrt(jnp.sum(t * t, axis=0, keepdims=True))
        q_ref[hq // QPK, hq % QPK] = jnp.concatenate(
            [t, ones_row * (-BOUND_MARGIN * nq)], axis=0).astype(jnp.bfloat16)
    for g in range(NKV):
        kt = _rms_rope_t(qkv_t[ATTN_W + g * HD:ATTN_W + (g + 1) * HD], kg, cos, sin)
        kn = jnp.max(jnp.sum(kt * kt, axis=0, keepdims=True), axis=1, keepdims=True)
        kmax_ref[0, g] = jnp.broadcast_to(kn, (8, LANES))
        k_ref[g] = jnp.concatenate([kt, ones_row], axis=0).T.astype(jnp.bfloat16)
        vt = qkv_t[ATTN_W + KV_W + g * HD:ATTN_W + KV_W + (g + 1) * HD]
        v_ref[g] = jnp.concatenate([vt, ones_pad], axis=0).astype(jnp.bfloat16)
    uag = jnp.dot(hb, wn_ref[...], preferred_element_type=jnp.float32)
    u_ref[...] = uag[:, :POOL_W]
    a = uag[:, POOL_W:POOL_W + CONV_W]
    gt = uag[:, POOL_W + CONV_W:]
    ag_ref[...] = a * jax.nn.sigmoid(gt)


def _inproj(xa, mod, w_t, w_n, qg, kg, cos_t, sin_t, l):
    ttot = xa.shape[0]
    nt = ttot // TM
    return pl.pallas_call(
        _inproj_kernel,
        grid=(nt,),
        in_specs=[
            pl.BlockSpec((TM, D), lambda i: (i, 0)),
            _layer((8, N_MOD * D), l),
            _layer((ATTN_W + 2 * KV_W, D), l),
            _layer((D, POOL_W + 2 * CONV_W), l),
            _layer((HD, TM), l),
            _layer((HD, TM), l),
            pl.BlockSpec((32, TM), lambda i: (0, i)),
            pl.BlockSpec((32, TM), lambda i: (0, i)),
        ],
        out_specs=[
            pl.BlockSpec((NKV, QPK, 2 * HD, TM), lambda i: (0, 0, 0, i)),
            pl.BlockSpec((NKV, TM, 2 * HD), lambda i: (0, i, 0)),
            pl.BlockSpec((NKV, VROWS, TM), lambda i: (0, 0, i)),
            pl.BlockSpec((1, NKV, 8, LANES), lambda i: (i, 0, 0, 0)),
            pl.BlockSpec((TM, POOL_W), lambda i: (i, 0)),
            pl.BlockSpec((TM, CONV_W), lambda i: (i, 0)),
        ],
        out_shape=[
            jax.ShapeDtypeStruct((NKV, QPK, 2 * HD, ttot), jnp.bfloat16),
            jax.ShapeDtypeStruct((NKV, ttot, 2 * HD), jnp.bfloat16),
            jax.ShapeDtypeStruct((NKV, VROWS, ttot), jnp.bfloat16),
            jax.ShapeDtypeStruct((nt, NKV, 8, LANES), jnp.float32),
            jax.ShapeDtypeStruct((ttot, POOL_W), jnp.float32),
            jax.ShapeDtypeStruct((ttot, CONV_W), jnp.float32),
        ],
        compiler_params=_cparams(("arbitrary",)),
        name="inproj",
    )(xa, mod, w_t, w_n, qg, kg, cos_t, sin_t)


TK = 256
GK = 1280
L_MIN = 2.0 ** -60


def _attn_kernel(kmax_ref, q_ref, k_ref, v_ref, o_ref, qs_ref, s_ref, acc_ref, m_ref, *,
                 n_chunks, n_groups):
    g = pl.program_id(0)
    i = pl.program_id(1)
    nch = jnp.where(i == 0, 1, n_chunks)
    row = lax.broadcasted_iota(jnp.int32, (2 * HD, TM), 0)
    kmax = kmax_ref[g]

    for hh in range(QPK):
        qf = q_ref[0, hh].astype(jnp.float32)
        qs_ref[hh] = jnp.where(row == HD, qf * kmax, qf).astype(jnp.bfloat16)
    acc_ref[...] = jnp.zeros(acc_ref.shape, jnp.float32)

    def step(nxt, cur):
        for kk in range(GK // TK):
            rows = slice(kk * TK, (kk + 1) * TK)
            if nxt is not None:
                kc = k_ref[0, pl.ds(pl.multiple_of(nxt[0] * GK + kk * TK, TK), TK), :]
            if cur is not None:
                vc = v_ref[0, :, pl.ds(pl.multiple_of(cur[0] * GK + kk * TK, TK), TK)]
            for hh in range(QPK):
                if nxt is not None:
                    s_ref[nxt[1], hh, rows, :] = jnp.dot(
                        kc, qs_ref[hh], preferred_element_type=jnp.float32)
                if cur is not None:
                    p = jnp.exp2(s_ref[cur[1], hh, rows, :]).astype(jnp.bfloat16)
                    acc_ref[hh] += jnp.dot(vc, p, preferred_element_type=jnp.float32)

    @pl.when(i == 0)
    def _():
        kc = k_ref[0, 0:TM, :]
        vc = v_ref[0, :, 0:TM]
        for hh in range(QPK):
            p = jnp.exp2(jnp.dot(kc, qs_ref[hh], preferred_element_type=jnp.float32))
            acc_ref[hh] = jnp.dot(vc, p.astype(jnp.bfloat16), preferred_element_type=jnp.float32)

    @pl.when(i > 0)
    def _():
        step((0, 0), None)
        n_pairs = (n_groups - 1) // 2

        def pair(j, carry):
            step((2 * j + 1, 1), (2 * j, 0))
            step((2 * j + 2, 0), (2 * j + 1, 1))
            return carry

        lax.fori_loop(0, n_pairs, pair, 0)
        for n in range(2 * n_pairs, n_groups - 1):
            step((n + 1, (n + 1) % 2), (n, n % 2))
        step(None, (n_groups - 1, (n_groups - 1) % 2))

    l_min = jnp.min(acc_ref[:, HD:HD + 1, :])

    @pl.when(l_min < L_MIN)
    def _():
        for hh in range(QPK):
            qs_ref[hh] = jnp.where(row == HD, 0.0, q_ref[0, hh].astype(jnp.float32)
                                   ).astype(jnp.bfloat16)
        acc_ref[...] = jnp.zeros(acc_ref.shape, jnp.float32)
        m_ref[...] = jnp.full(m_ref.shape, NEG_BIG, jnp.float32)

        def body2(c, carry):
            start = pl.multiple_of(c * TK, TK)
            kc = k_ref[0, pl.ds(start, TK), :]
            vc = v_ref[0, :, pl.ds(start, TK)]
            for hh in range(QPK):
                s = jnp.dot(kc, qs_ref[hh], preferred_element_type=jnp.float32)
                m_old = m_ref[hh]
                m_new = jnp.maximum(m_old, jnp.max(s, axis=0, keepdims=True))
                p = jnp.exp2(s - m_new).astype(jnp.bfloat16)
                acc_ref[hh] = jnp.exp2(m_old - m_new) * acc_ref[hh] + jnp.dot(
                    vc, p, preferred_element_type=jnp.float32)
                m_ref[hh] = m_new
            return carry

        lax.fori_loop(0, nch, body2, 0)

    outs = [acc_ref[hh, 0:HD, :] / acc_ref[hh, HD:HD + 1, :] for hh in range(QPK)]
    o_ref[...] = jnp.concatenate(outs, axis=0).T


def _attention(q_t, k2, v_t, kmax):
    ttot = k2.shape[1]
    nt = ttot // TM
    assert ttot % GK == 0 and GK % TK == 0
    kern = functools.partial(_attn_kernel, n_chunks=ttot // TK, n_groups=ttot // GK)
    grid_spec = pltpu.PrefetchScalarGridSpec(
        num_scalar_prefetch=1,
        grid=(NKV, nt),
        in_specs=[
            pl.BlockSpec((1, QPK, 2 * HD, TM), lambda g, i, km: (g, 0, 0, i)),
            pl.BlockSpec((1, ttot, 2 * HD), lambda g, i, km: (g, 0, 0)),
            pl.BlockSpec((1, VROWS, ttot), lambda g, i, km: (g, 0, 0)),
        ],
        out_specs=pl.BlockSpec((TM, QPK * HD), lambda g, i, km: (i, g)),
        scratch_shapes=[
            pltpu.VMEM((QPK, 2 * HD, TM), jnp.bfloat16),
            pltpu.VMEM((2, QPK, GK, TM), jnp.float32),
            pltpu.VMEM((QPK, VROWS, TM), jnp.float32),
            pltpu.VMEM((QPK, 1, TM), jnp.float32),
        ],
    )
    return pl.pallas_call(
        kern,
        grid_spec=grid_spec,
        out_shape=jax.ShapeDtypeStruct((ttot, ATTN_W), jnp.float32),
        compiler_params=_cparams(("arbitrary", "arbitrary")),
        name="attention",
    )(kmax, q_t, k2, v_t)


def _fill_ext(ext_ref, cur_ref, prev_ref, next_ref, left_ok, right_ok):
    zero = jnp.zeros((HALO, cur_ref.shape[1]), jnp.float32)
    ext_ref[0:HALO, :] = jnp.where(left_ok, prev_ref[...], zero)
    ext_ref[HALO:HALO + TM, :] = cur_ref[...]
    ext_ref[HALO + TM:HALO + TM + HALO, :] = jnp.where(right_ok, next_ref[...], zero)


def _mix_kernel(x_ref, attn_ref, u_ref, up_ref, un_ref, ag_ref, agp_ref, agn_ref, mod_ref,
                wpool_ref, pscale_ref, wdw_ref, bdw_ref, cvg_ref, cvb_ref, wpw_ref, wout_ref,
                ln1g_ref, ln1b_ref, wr_ref, br_ref,
                x1_ref, h2_ref, route_ref, cnt_ref,
                uext_ref, agext_ref, base_ref, *, alpha, n_lat):
    i = pl.program_id(0)
    nt = pl.num_programs(0)
    is_ctx = i == 0
    left_ok = i >= 2
    right_ok = jnp.logical_and(i >= 1, i < nt - 1)

    @pl.when(i == 0)
    def _():
        base_ref[...] = jnp.zeros(base_ref.shape, jnp.float32)

    _fill_ext(uext_ref, u_ref, up_ref, un_ref, left_ok, right_ok)
    _fill_ext(agext_ref, ag_ref, agp_ref, agn_ref, left_ok, right_ok)

    def ush(off, rows=TM):
        return uext_ref[HALO + off:HALO + off + rows, :]

    a2 = ush(-8, TM + 15) + ush(-7, TM + 15)
    a4 = a2[0:TM + 13] + a2[2:TM + 15]
    a8 = a4[0:TM + 9] + a4[4:TM + 13]
    a16 = a8[0:TM] + a8[8:TM + 8]
    sums = (a2[7:7 + TM], a4[6:6 + TM], a8[4:4 + TM], a16)
    pos = lax.broadcasted_iota(jnp.int32, (TM, 1), 0) + jnp.where(is_ctx, 0, (i - 1) * TM)
    seq_n = jnp.where(is_ctx, TM, n_lat)
    u_cur = u_ref[...]
    lane = lax.broadcasted_iota(jnp.int32, (TM, POOL_W), 1)
    pooled = jnp.zeros((TM, POOL_W), jnp.float32)
    for gi, w in enumerate(POOL_WINDOWS):
        lo = jnp.maximum(pos - w // 2, 0)
        hi = jnp.minimum(pos + (w - w // 2), seq_n)
        inv = 1.0 / (hi - lo).astype(jnp.float32)
        in_group = jnp.logical_and(lane >= gi * POOL_GW, lane < (gi + 1) * POOL_GW)
        pooled = jnp.where(in_group, sums[gi] * inv - u_cur, pooled)
    y_pool = jnp.dot(pooled.astype(jnp.bfloat16), wpool_ref[...],
                     preferred_element_type=jnp.float32) * pscale_ref[...]

    conv = jnp.zeros((TM, CONV_W), jnp.float32) + bdw_ref[...]
    for k in range(CONV_K):
        off = HALO + k - CONV_K // 2
        conv = conv + agext_ref[off:off + TM, :] * wdw_ref[k:k + 1, :]
    cn = _ln_plain(conv) * cvg_ref[...] + cvb_ref[...]
    cn = cn * jax.nn.sigmoid(cn)
    y_conv = jnp.dot(cn.astype(jnp.bfloat16), wpw_ref[...], preferred_element_type=jnp.float32)

    y = jnp.dot(attn_ref[...].astype(jnp.bfloat16), wout_ref[0:ATTN_W, :],
                preferred_element_type=jnp.float32)
    y = y + jnp.dot(y_pool.astype(jnp.bfloat16), wout_ref[ATTN_W:ATTN_W + POOL_W, :],
                    preferred_element_type=jnp.float32)
    y = y + jnp.dot(y_conv.astype(jnp.bfloat16), wout_ref[ATTN_W + POOL_W:, :],
                    preferred_element_type=jnp.float32)
    x1 = _ln_plain(alpha * x_ref[...] + _mod_rows(mod_ref, 2) * y) * ln1g_ref[...] + ln1b_ref[...]
    x1_ref[...] = x1
    h2 = _ln_plain(x1) * (1.0 + _mod_rows(mod_ref, 4)) + _mod_rows(mod_ref, 3)
    h2_ref[...] = h2

    h_hi = h2.astype(jnp.bfloat16)
    h_lo = (h2 - h_hi.astype(jnp.float32)).astype(jnp.bfloat16)
    hw = jnp.dot(h_hi, wr_ref[...], preferred_element_type=jnp.float32)
    logits = (hw[:, :LANES] + hw[:, LANES:] + br_ref[...]
              + jnp.dot(h_lo, wr_ref[:, :LANES], preferred_element_type=jnp.float32))
    ln = lax.broadcasted_iota(jnp.int32, (TM, LANES), 1)
    neg = jnp.float32(-jnp.inf)
    is_g = ln < N_GROUPS
    lg = jnp.where(is_g, logits, neg)
    mg = jnp.max(lg, axis=-1, keepdims=True)
    gsel = jnp.min(jnp.where(lg == mg, ln, LANES), axis=-1, keepdims=True)
    p_group = 1.0 / jnp.sum(jnp.exp(lg - mg), axis=-1, keepdims=True)
    e_lane = ln - N_GROUPS
    in_sel = jnp.logical_and(e_lane >= gsel * EPG, e_lane < (gsel + 1) * EPG)
    le = jnp.where(in_sel, logits, neg)
    v1 = jnp.max(le, axis=-1, keepdims=True)
    e1 = jnp.min(jnp.where(le == v1, e_lane, LANES), axis=-1, keepdims=True)
    le2 = jnp.where(e_lane == e1, neg, le)
    v2 = jnp.max(le2, axis=-1, keepdims=True)
    e2 = jnp.min(jnp.where(le2 == v2, e_lane, LANES), axis=-1, keepdims=True)
    ex = jnp.exp(v2 - v1)
    gate1 = p_group / (1.0 + ex)
    gate2 = p_group * ex / (1.0 + ex)

    oh = jnp.logical_or(ln == e1, ln == e2)
    ohb = jnp.where(oh, 1.0, 0.0).astype(jnp.bfloat16)
    r_i = lax.broadcasted_iota(jnp.int32, (TM, TM), 0)
    c_i = lax.broadcasted_iota(jnp.int32, (TM, TM), 1)
    tri = jnp.where(c_i < r_i, 1.0, 0.0).astype(jnp.bfloat16)
    before = jnp.dot(tri, ohb, preferred_element_type=jnp.float32) + base_ref[...]
    rank1 = jnp.sum(jnp.where(ln == e1, before, 0.0), axis=-1, keepdims=True)
    rank2 = jnp.sum(jnp.where(ln == e2, before, 0.0), axis=-1, keepdims=True)
    base_ref[...] = base_ref[...] + jnp.sum(ohb.astype(jnp.float32), axis=0, keepdims=True)
    cnt_ref[...] = jnp.broadcast_to(base_ref[...], cnt_ref.shape)

    route = jnp.zeros((TM, LANES), jnp.float32)
    for j, val in enumerate((e1.astype(jnp.float32), e2.astype(jnp.float32), rank1, rank2,
                             gate1, gate2)):
        route = jnp.where(ln == j, val, route)
    route_ref[...] = route


def _mixer(xa, attn, u, ag, mod, wpool, pscale, wdw, bdw, cvg, cvb, wpw, wout, ln1g, ln1b,
           wr, br, alpha, l):
    ttot = xa.shape[0]
    nt = ttot // TM
    hb = TM // HALO
    nhb = ttot // HALO

    def tile(w):
        return pl.BlockSpec((TM, w), lambda i: (i, 0))

    def prev(w):
        return pl.BlockSpec((HALO, w), lambda i: (jnp.maximum(i * hb - 1, 0), 0))

    def nxt(w):
        return pl.BlockSpec((HALO, w), lambda i: (jnp.minimum((i + 1) * hb, nhb - 1), 0))

    kern = functools.partial(_mix_kernel, alpha=alpha, n_lat=ttot - TM)
    return pl.pallas_call(
        kern,
        grid=(nt,),
        in_specs=[
            tile(D), tile(ATTN_W),
            tile(POOL_W), prev(POOL_W), nxt(POOL_W),
            tile(CONV_W), prev(CONV_W), nxt(CONV_W),
            _layer((8, N_MOD * D), l),
            _layer((POOL_W, POOL_W), l), _layer((1, POOL_W), l),
            _layer((32, CONV_W), l), _layer((1, CONV_W), l), _layer((1, CONV_W), l),
            _layer((1, CONV_W), l),
            _layer((CONV_W, CONV_W), l), _layer((D, D), l),
            _layer((1, D), l), _layer((1, D), l),
            _layer((D, 2 * LANES), l), _layer((1, LANES), l),
        ],
        out_specs=[tile(D), tile(D), tile(LANES), _full((8, LANES))],
        out_shape=[
            jax.ShapeDtypeStruct((ttot, D), jnp.float32),
            jax.ShapeDtypeStruct((ttot, D), jnp.float32),
            jax.ShapeDtypeStruct((ttot, LANES), jnp.float32),
            jax.ShapeDtypeStruct((8, LANES), jnp.float32),
        ],
        scratch_shapes=[
            pltpu.VMEM((TM + 2 * HALO, POOL_W), jnp.float32),
            pltpu.VMEM((TM + 2 * HALO, CONV_W), jnp.float32),
            pltpu.VMEM((1, LANES), jnp.float32),
        ],
        compiler_params=_cparams(("arbitrary",)),
        name="mixer",
    )(xa, attn, u, u, u, ag, ag, ag, mod, wpool, pscale, wdw, bdw, cvg, cvb, wpw, wout,
      ln1g, ln1b, wr, br)


ROW_GROUP = 8


def _dispatch_kernel(pends_ref, d1_ref, d2_ref, h_ref, xs_ref, zero_buf, sem, zsem, *, n_blk):
    @pl.when(pl.program_id(0) == 0)
    def _():
        zero_buf[...] = jnp.zeros(zero_buf.shape, jnp.float32)

        def zero_block(first_row):
            return pltpu.make_async_copy(
                zero_buf, xs_ref.at[pl.ds(pl.multiple_of(first_row, EBLK), EBLK), :], zsem)

        def seg_nonempty(e):
            return pends_ref[e] > (pends_ref[e - 1] if e > 0 else 0)

        for e in range(N_EXPERTS):
            @pl.when(seg_nonempty(e))
            def _():
                zero_block(pends_ref[e] - EBLK).start()
        n_used = pends_ref[N_EXPERTS - 1] // EBLK

        def tail_start(b, carry):
            zero_block(b * EBLK).start()
            return carry

        lax.fori_loop(n_used, n_blk, tail_start, 0)
        for e in range(N_EXPERTS):
            @pl.when(seg_nonempty(e))
            def _():
                zero_block(pends_ref[e] - EBLK).wait()

        def tail_wait(b, carry):
            zero_block(b * EBLK).wait()
            return carry

        lax.fori_loop(n_used, n_blk, tail_wait, 0)

    def start(r8, carry):
        base = pl.multiple_of(r8 * ROW_GROUP, ROW_GROUP)
        rows = h_ref.at[pl.ds(base, ROW_GROUP), :]
        for j in range(ROW_GROUP):
            for d_ref in (d1_ref, d2_ref):
                pltpu.make_async_copy(rows.at[pl.ds(j, 1), :],
                                      xs_ref.at[pl.ds(d_ref[0, 0, base + j], 1), :], sem).start()
        return carry

    lax.fori_loop(0, TM // ROW_GROUP, start, 0)
    for _ in range(2):
        pltpu.make_async_copy(h_ref, xs_ref.at[pl.ds(0, TM), :], sem).wait()


def _dispatch(h2, d1, d2, pends, n_slots):
    ttot = h2.shape[0]
    nt = ttot // TM
    smem = pl.BlockSpec((1, 1, TM), lambda i, pe: (i, 0, 0), memory_space=pltpu.SMEM)
    grid_spec = pltpu.PrefetchScalarGridSpec(
        num_scalar_prefetch=1,
        grid=(nt,),
        in_specs=[smem, smem, pl.BlockSpec((TM, D), lambda i, pe: (i, 0))],
        out_specs=pl.BlockSpec(memory_space=pl.ANY),
        scratch_shapes=[pltpu.VMEM((EBLK, D), jnp.float32),
                        pltpu.SemaphoreType.DMA(()), pltpu.SemaphoreType.DMA(())],
    )
    return pl.pallas_call(
        functools.partial(_dispatch_kernel, n_blk=n_slots // EBLK),
        grid_spec=grid_spec,
        out_shape=jax.ShapeDtypeStruct((n_slots, D), jnp.float32),
        compiler_params=_cparams(("arbitrary",)),
        name="dispatch",
    )(pends, d1, d2, h2)


def _expert_kernel(be_ref, nu_ref, x_ref, wg_ref, wu_ref, wd_ref, y_ref, wgu_b, wd_b):
    b = pl.program_id(0)
    new_expert = jnp.logical_or(b == 0, be_ref[b] != be_ref[jnp.maximum(b - 1, 0)])

    @pl.when(jnp.logical_and(b < nu_ref[0], new_expert))
    def _():
        wgu_b[:, :D_EXPERT] = wg_ref[0].astype(jnp.bfloat16)
        wgu_b[:, D_EXPERT:] = wu_ref[0].astype(jnp.bfloat16)
        wd_b[...] = wd_ref[0].astype(jnp.bfloat16)

    @pl.when(b < nu_ref[0])
    def _():
        x = x_ref[...].astype(jnp.bfloat16)
        gu = jnp.dot(x, wgu_b[...], preferred_element_type=jnp.float32)
        gt = gu[:, :D_EXPERT]
        hm = (gt * jax.nn.sigmoid(gt) * gu[:, D_EXPERT:]).astype(jnp.bfloat16)
        y_ref[...] = jnp.dot(hm, wd_b[...], preferred_element_type=jnp.float32)

    @pl.when(b >= nu_ref[0])
    def _():
        y_ref[...] = jnp.zeros(y_ref.shape, jnp.float32)


def _experts(xs, blk_e, n_used, wg, wu, wd, l):
    n_slots = xs.shape[0]
    nblk = n_slots // EBLK

    def row_map(b, be, nu):
        return (jnp.minimum(b, nu[0] - 1), 0)

    def w_map(b, be, nu):
        return (l, be[b], 0, 0)

    grid_spec = pltpu.PrefetchScalarGridSpec(
        num_scalar_prefetch=2,
        grid=(nblk,),
        in_specs=[
            pl.BlockSpec((EBLK, D), row_map),
            pl.BlockSpec((None, 1, D, D_EXPERT), w_map),
            pl.BlockSpec((None, 1, D, D_EXPERT), w_map),
            pl.BlockSpec((None, 1, D_EXPERT, D), w_map),
        ],
        out_specs=pl.BlockSpec((EBLK, D), lambda b, be, nu: (b, 0)),
        scratch_shapes=[pltpu.VMEM((D, 2 * D_EXPERT), jnp.bfloat16),
                        pltpu.VMEM((D_EXPERT, D), jnp.bfloat16)],
    )
    return pl.pallas_call(
        _expert_kernel,
        grid_spec=grid_spec,
        out_shape=jax.ShapeDtypeStruct((n_slots, D), jnp.float32),
        compiler_params=_cparams(("arbitrary",)),
        name="experts",
    )(blk_e, n_used, xs, wg, wu, wd)


def _combine_kernel(d1_ref, d2_ref, d1n_ref, d2n_ref, x1_ref, route_ref, mod_ref, ln2g_ref,
                    ln2b_ref, ys_ref, o_ref, y_buf, sems, *, alpha):
    i = pl.program_id(0)
    nt = pl.num_programs(0)
    slot = i % 2

    def gather(da_ref, db_ref, to_slot):
        def start(r8, carry):
            base = pl.multiple_of(r8 * ROW_GROUP, ROW_GROUP)
            for k, d_ref in enumerate((da_ref, db_ref)):
                rows = y_buf.at[to_slot, k, pl.ds(base, ROW_GROUP), :]
                for j in range(ROW_GROUP):
                    pltpu.make_async_copy(ys_ref.at[pl.ds(d_ref[0, 0, base + j], 1), :],
                                          rows.at[pl.ds(j, 1), :], sems.at[to_slot]).start()
            return carry

        lax.fori_loop(0, TM // ROW_GROUP, start, 0)

    @pl.when(i == 0)
    def _():
        gather(d1_ref, d2_ref, 0)

    @pl.when(i < nt - 1)
    def _():
        gather(d1n_ref, d2n_ref, 1 - slot)

    for k in range(2):
        pltpu.make_async_copy(ys_ref.at[pl.ds(0, TM), :], y_buf.at[slot, k], sems.at[slot]).wait()

    route = route_ref[...]
    g1 = route[:, 4:5]
    g2 = route[:, 5:6]
    o = g1 * y_buf[slot, 0] + g2 * y_buf[slot, 1]
    z = alpha * x1_ref[...] + _mod_rows(mod_ref, 5) * o
    o_ref[...] = _ln_plain(z) * ln2g_ref[...] + ln2b_ref[...]


def _combine(x1, route, mod, ln2g, ln2b, ys, d1, d2, alpha, l, latents_only):
    ttot = x1.shape[0]
    nt = ttot // TM
    smem = pl.BlockSpec((1, 1, TM), lambda i: (i, 0, 0), memory_space=pltpu.SMEM)
    smem_next = pl.BlockSpec((1, 1, TM), lambda i: (jnp.minimum(i + 1, nt - 1), 0, 0),
                             memory_space=pltpu.SMEM)
    kern = functools.partial(_combine_kernel, alpha=alpha)
    if latents_only:
        out_spec = pl.BlockSpec((TM, D), lambda i: (jnp.maximum(i - 1, 0), 0))
        out_rows = ttot - TM
    else:
        out_spec = pl.BlockSpec((TM, D), lambda i: (i, 0))
        out_rows = ttot
    return pl.pallas_call(
        kern,
        grid=(nt,),
        in_specs=[smem, smem, smem_next, smem_next,
                  pl.BlockSpec((TM, D), lambda i: (i, 0)),
                  pl.BlockSpec((TM, LANES), lambda i: (i, 0)),
                  _layer((8, N_MOD * D), l), _layer((1, D), l), _layer((1, D), l),
                  pl.BlockSpec(memory_space=pl.ANY)],
        out_specs=out_spec,
        out_shape=jax.ShapeDtypeStruct((out_rows, D), jnp.float32),
        scratch_shapes=[pltpu.VMEM((2, 2, TM, D), jnp.float32),
                        pltpu.SemaphoreType.DMA((2,))],
        compiler_params=_cparams(("arbitrary",)),
        name="combine",
    )(d1, d2, d1, d2, x1, route, mod, ln2g, ln2b, ys)


def _rope_tables_t(n_lat, n_ctx):
    rows = n_lat // GRID_W
    row = jnp.broadcast_to(jnp.arange(rows)[:, None], (rows, GRID_W)).reshape(-1)
    col = jnp.broadcast_to(jnp.arange(GRID_W)[None, :], (rows, GRID_W)).reshape(-1)
    n_freq = HD // 4
    inv = ROPE_THETA ** (-jnp.arange(n_freq, dtype=jnp.float32) / n_freq)
    pos = jnp.stack([row, col], axis=-1).astype(jnp.float32)
    ang = (pos[:, :, None] * inv).reshape(n_lat, 2 * n_freq)
    cos = jnp.concatenate([jnp.ones((n_ctx, 2 * n_freq), jnp.float32), jnp.cos(ang)], axis=0)
    sin = jnp.concatenate([jnp.zeros((n_ctx, 2 * n_freq), jnp.float32), jnp.sin(ang)], axis=0)
    return cos.T, sin.T


def _row(v):
    return v.reshape(1, -1)


def kernel(x, c, ctx, c_ctx, w_mod, b_mod, w_in, q_gain, k_gain, w_pool, pool_scale, w_dw, b_dw,
           cv_ln_g, cv_ln_b, w_cv_pw, w_out, ln1_g, ln1_b, ln2_g, ln2_b, w_rg, b_rg, w_re, b_re,
           w_e_gate, w_e_up, w_e_down):
    depth = w_mod.shape[0]
    n_lat = x.shape[1]
    n_ctx = ctx.shape[1]
    assert x.shape[0] == 1 and n_ctx == TM and n_lat % TM == 0 and n_lat % GRID_W == 0
    ttot = n_ctx + n_lat
    alpha = float((2 * depth) ** 0.25)
    n_blk = -(-(2 * ttot) // EBLK) + N_EXPERTS
    n_slots = n_blk * EBLK

    xa = jnp.concatenate([ctx[0], x[0]], axis=0)
    cc = jnp.concatenate([c, c_ctx[None], jnp.zeros((6, D), jnp.float32)], axis=0)
    mod = _modulation(cc, w_mod, b_mod)
    cos_t, sin_t = _rope_tables_t(n_lat, n_ctx)

    n_qkv = ATTN_W + 2 * KV_W
    bf = jnp.bfloat16
    w_t = jnp.swapaxes(w_in[:, :, :n_qkv], 1, 2).astype(bf)
    w_n = w_in[:, :, n_qkv:].astype(bf)
    qg = jnp.broadcast_to(q_gain[:, :, None], (depth, HD, TM))
    kg = jnp.broadcast_to(k_gain[:, :, None], (depth, HD, TM))
    n_pg = len(POOL_WINDOWS)
    wpool = (jnp.eye(n_pg, dtype=jnp.float32)[None, :, None, :, None] * w_pool[:, :, :, None, :]
             ).reshape(depth, POOL_W, POOL_W).astype(bf)
    wdw = jnp.pad(w_dw, ((0, 0), (0, 32 - CONV_K), (0, 0)))
    n_r = N_GROUPS + N_EXPERTS
    wr = jnp.concatenate([w_rg, w_re, jnp.zeros((depth, D, LANES - n_r), jnp.float32)], axis=2)
    br = jnp.concatenate([b_rg, b_re, jnp.zeros((depth, LANES - n_r), jnp.float32)], axis=1)
    wr_hi = wr.astype(bf)
    wr = jnp.concatenate([wr_hi, (wr - wr_hi.astype(jnp.float32)).astype(bf)], axis=2)
    wpw = w_cv_pw.astype(bf)
    wout = w_out.astype(bf)

    def rows(v):
        return v[:, None, :]

    e_ids = jnp.arange(N_EXPERTS, dtype=jnp.int32)
    blk_start = jnp.arange(n_blk, dtype=jnp.int32) * EBLK
    for l in range(depth):
        q_t, k2, v_t, kn2, u, ag = _inproj(xa, mod, w_t, w_n, qg, kg, cos_t, sin_t, l)
        kmax = jnp.sqrt(jnp.max(kn2[:, :, 0, 0], axis=0))
        attn = _attention(q_t, k2, v_t, kmax)
        x1, h2, route, cnt = _mixer(
            xa, attn, u, ag, mod, wpool, rows(pool_scale), wdw, rows(b_dw), rows(cv_ln_g),
            rows(cv_ln_b), wpw, wout, rows(ln1_g), rows(ln1_b), wr, rows(br), alpha, l)

        counts = cnt[0, :N_EXPERTS].astype(jnp.int32)
        padded = (counts + EBLK - 1) // EBLK * EBLK
        pends = jnp.cumsum(padded)
        pstarts = pends - padded
        ri = route[:, :4].astype(jnp.int32)

        def dest(e, r):
            start = jnp.sum(jnp.where(e[:, None] == e_ids[None, :], pstarts[None, :], 0), axis=1)
            return (start + r).reshape(ttot // TM, 1, TM)

        d1 = dest(ri[:, 0], ri[:, 2])
        d2 = dest(ri[:, 1], ri[:, 3])
        n_used = (pends[-1] // EBLK).astype(jnp.int32).reshape(1)
        first_row = jnp.minimum(blk_start, pends[-1] - 1)
        blk_e = jnp.sum((pends[None, :] <= first_row[:, None]).astype(jnp.int32), axis=1)
        blk_e = jnp.minimum(blk_e, N_EXPERTS - 1)

        xs = _dispatch(h2, d1, d2, pends, n_slots)
        ys = _experts(xs, blk_e, n_used, w_e_gate, w_e_up, w_e_down, l)
        xa = _combine(x1, route, mod, rows(ln2_g), rows(ln2_b), ys, d1, d2, alpha, l,
                      latents_only=(l == depth - 1))
    return xa[None]
```

```python
import functools

import jax
import jax.numpy as jnp
from jax import lax
from jax.experimental import pallas as pl
from jax.experimental.pallas import tpu as pltpu

D = 1024
GRID_W = 64
HD = 64
NQ = 8
NKV = 2
QPK = NQ // NKV
ATTN_W = NQ * HD
KV_W = NKV * HD
POOL_W = 256
POOL_WINDOWS = (2, 4, 8, 16)
POOL_GW = POOL_W // len(POOL_WINDOWS)
CONV_W = 256
CONV_K = 31
N_GROUPS = 4
EPG = 8
N_EXPERTS = N_GROUPS * EPG
D_EXPERT = 512
N_MOD = 6
ROPE_THETA = 10000.0
EPS = 1e-6

TM = 256
HALO = 16
EBLK = 256
LANES = 128
VROWS = HD + 16
LOG2E = 1.4426950408889634
BOUND_MARGIN = 1.0 + 2.0 ** -5
VMEM_LIMIT = 56 * 1024 * 1024
NEG_BIG = -1e30

_HI = lax.Precision.HIGHEST


def _cparams(sem):
    return pltpu.CompilerParams(dimension_semantics=sem, vmem_limit_bytes=VMEM_LIMIT)


def _full(shape):
    n = len(shape)
    return pl.BlockSpec(shape, lambda *a, n=n: (0,) * n)


def _layer(shape, l):
    n = len(shape)
    return pl.BlockSpec((None,) + tuple(shape), lambda *a, n=n: (l,) + (0,) * n)


def _mod_kernel(cc_ref, w_ref, b_ref, o_ref):
    cc = cc_ref[...]
    a = cc * jax.nn.sigmoid(cc)
    o_ref[0] = jnp.dot(a, w_ref[0], precision=_HI, preferred_element_type=jnp.float32) + b_ref[0]


def _modulation(cc, w_mod, b_mod):
    depth = w_mod.shape[0]
    nw = w_mod.shape[2]
    bn = 1024
    return pl.pallas_call(
        _mod_kernel,
        grid=(depth, nw // bn),
        in_specs=[
            pl.BlockSpec((8, D), lambda l, j: (0, 0)),
            pl.BlockSpec((1, D, bn), lambda l, j: (l, 0, j)),
            pl.BlockSpec((1, 1, bn), lambda l, j: (l, 0, j)),
        ],
        out_specs=pl.BlockSpec((1, 8, bn), lambda l, j: (l, 0, j)),
        out_shape=jax.ShapeDtypeStruct((depth, 8, nw), jnp.float32),
        compiler_params=_cparams(("arbitrary", "arbitrary")),
        name="modulation",
    )(cc, w_mod, b_mod.reshape(depth, 1, nw))


def _ln_plain(x):
    mu = jnp.mean(x, axis=-1, keepdims=True)
    xc = x - mu
    var = jnp.mean(xc * xc, axis=-1, keepdims=True)
    return xc * lax.rsqrt(var + EPS)


def _mod_rows(mod_ref, idx):
    row = jnp.where(pl.program_id(0) == 0, 1, 0)
    return mod_ref[pl.ds(row, 1), idx * D:(idx + 1) * D]


def _rms_rope_t(t, gain, cos, sin):
    ms = jnp.mean(t * t, axis=0, keepdims=True)
    t = t * lax.rsqrt(ms + EPS) * gain
    outs = []
    for a in range(2):
        t1 = t[32 * a:32 * a + 16]
        t2 = t[32 * a + 16:32 * a + 32]
        c = cos[16 * a:16 * a + 16]
        s = sin[16 * a:16 * a + 16]
        outs.append(t1 * c - t2 * s)
        outs.append(t2 * c + t1 * s)
    return jnp.concatenate(outs, axis=0)


def _inproj_kernel(x_ref, mod_ref, wt_ref, wn_ref, qg_ref, kg_ref, cos_ref, sin_ref,
                   q_ref, k_ref, v_ref, kmax_ref, u_ref, ag_ref):
    x = x_ref[...]
    h = _ln_plain(x) * (1.0 + _mod_rows(mod_ref, 1)) + _mod_rows(mod_ref, 0)
    hb = h.astype(jnp.bfloat16)
    qkv_t = lax.dot_general(wt_ref[...], hb, (((1,), (1,)), ((), ())),
                            preferred_element_type=jnp.float32)
    cos = cos_ref[...]
    sin = sin_ref[...]
    qg = qg_ref[...]
    kg = kg_ref[...]
    ones_row = jnp.where(lax.broadcasted_iota(jnp.int32, (HD, TM), 0) == 0, 1.0, 0.0)
    ones_pad = jnp.where(lax.broadcasted_iota(jnp.int32, (VROWS - HD, TM), 0) == 0, 1.0, 0.0)
    for hq in range(NQ):
        t = _rms_rope_t(qkv_t[hq * HD:(hq + 1) * HD], qg, cos, sin) * (LOG2E * HD ** -0.5)
        nq = jnp.sqrt(jnp.sum(t * t, axis=0, keepdims=True))
        q_ref[hq // QPK, hq % QPK] = jnp.concatenate(
            [t, ones_row * (-BOUND_MARGIN * nq)], axis=0).astype(jnp.bfloat16)
    for g in range(NKV):
        kt = _rms_rope_t(qkv_t[ATTN_W + g * HD:ATTN_W + (g + 1) * HD], kg, cos, sin)
        kn = jnp.max(jnp.sum(kt * kt, axis=0, keepdims=True), axis=1, keepdims=True)
        kmax_ref[0, g] = jnp.broadcast_to(kn, (8, LANES))
        k_ref[g] = jnp.concatenate([kt, ones_row], axis=0).T.astype(jnp.bfloat16)
        vt = qkv_t[ATTN_W + KV_W + g * HD:ATTN_W + KV_W + (g + 1) * HD]
        v_ref[g] = jnp.concatenate([vt, ones_pad], axis=0).astype(jnp.bfloat16)
    uag = jnp.dot(hb, wn_ref[...], preferred_element_type=jnp.float32)
    u_ref[...] = uag[:, :POOL_W]
    a = uag[:, POOL_W:POOL_W + CONV_W]
    gt = uag[:, POOL_W + CONV_W:]
    ag_ref[...] = a * jax.nn.sigmoid(gt)


def _inproj(xa, mod, w_t, w_n, qg, kg, cos_t, sin_t, l):
    ttot = xa.shape[0]
    nt = ttot // TM
    return pl.pallas_call(
        _inproj_kernel,
        grid=(nt,),
        in_specs=[
            pl.BlockSpec((TM, D), lambda i: (i, 0)),
            _layer((8, N_MOD * D), l),
            _layer((ATTN_W + 2 * KV_W, D), l),
            _layer((D, POOL_W + 2 * CONV_W), l),
            _layer((HD, TM), l),
            _layer((HD, TM), l),
            pl.BlockSpec((32, TM), lambda i: (0, i)),
            pl.BlockSpec((32, TM), lambda i: (0, i)),
        ],
        out_specs=[
            pl.BlockSpec((NKV, QPK, 2 * HD, TM), lambda i: (0, 0, 0, i)),
            pl.BlockSpec((NKV, TM, 2 * HD), lambda i: (0, i, 0)),
            pl.BlockSpec((NKV, VROWS, TM), lambda i: (0, 0, i)),
            pl.BlockSpec((1, NKV, 8, LANES), lambda i: (i, 0, 0, 0)),
            pl.BlockSpec((TM, POOL_W), lambda i: (i, 0)),
            pl.BlockSpec((TM, CONV_W), lambda i: (i, 0)),
        ],
        out_shape=[
            jax.ShapeDtypeStruct((NKV, QPK, 2 * HD, ttot), jnp.bfloat16),
            jax.ShapeDtypeStruct((NKV, ttot, 2 * HD), jnp.bfloat16),
            jax.ShapeDtypeStruct((NKV, VROWS, ttot), jnp.bfloat16),
            jax.ShapeDtypeStruct((nt, NKV, 8, LANES), jnp.float32),
            jax.ShapeDtypeStruct((ttot, POOL_W), jnp.float32),
            jax.ShapeDtypeStruct((ttot, CONV_W), jnp.float32),
        ],
        compiler_params=_cparams(("arbitrary",)),
        name="inproj",
    )(xa, mod, w_t, w_n, qg, kg, cos_t, sin_t)


TK = 256
GCH = 5
L_MIN = 2.0 ** -60


def _attn_kernel(kmax_ref, q_ref, k_ref, v_ref, o_ref, qs_ref, s_ref, acc_ref, m_ref, *,
                 n_chunks):
    g = pl.program_id(0)
    i = pl.program_id(1)
    nch = jnp.where(i == 0, 1, n_chunks)
    row = lax.broadcasted_iota(jnp.int32, (2 * HD, TM), 0)
    kmax = kmax_ref[g]

    for hh in range(QPK):
        qf = q_ref[0, hh].astype(jnp.float32)
        qs_ref[hh] = jnp.where(row == HD, qf * kmax, qf).astype(jnp.bfloat16)
    acc_ref[...] = jnp.zeros(acc_ref.shape, jnp.float32)

    def step(nxt, cur):
        for kk in range(max(grp[1] for grp in (nxt, cur) if grp is not None)):
            do_nxt = nxt is not None and kk < nxt[1]
            do_cur = cur is not None and kk < cur[1]
            rows = slice(kk * TK, (kk + 1) * TK)
            if do_nxt:
                kc = k_ref[0, pl.ds(pl.multiple_of((nxt[0] + kk) * TK, TK), TK), :]
            if do_cur:
                vc = v_ref[0, :, pl.ds(pl.multiple_of((cur[0] + kk) * TK, TK), TK)]
            for hh in range(QPK):
                if do_nxt:
                    s_ref[nxt[2], hh, rows, :] = jnp.dot(
                        kc, qs_ref[hh], preferred_element_type=jnp.float32)
                if do_cur:
                    p = jnp.exp2(s_ref[cur[2], hh, rows, :]).astype(jnp.bfloat16)
                    acc_ref[hh] += jnp.dot(vc, p, preferred_element_type=jnp.float32)

    @pl.when(i == 0)
    def _():
        kc = k_ref[0, 0:TM, :]
        vc = v_ref[0, :, 0:TM]
        for hh in range(QPK):
            p = jnp.exp2(jnp.dot(kc, qs_ref[hh], preferred_element_type=jnp.float32))
            acc_ref[hh] = jnp.dot(vc, p.astype(jnp.bfloat16), preferred_element_type=jnp.float32)

    @pl.when(i > 0)
    def _():
        n_mid = (n_chunks - 2) // GCH
        tail = n_chunks - 1 - GCH * n_mid
        sizes = [1] + [GCH] * n_mid + ([tail - 1, 1] if tail >= 2 else [tail])
        firsts = [sum(sizes[:n]) for n in range(len(sizes))]

        def group(n):
            return (firsts[n], sizes[n], n % 2)

        step(group(0), None)
        step(group(1), group(0))
        n_loop = max(n_mid - 1, 0) // 2

        def pair(j, carry):
            c0 = 1 + 2 * GCH * j
            step((c0 + GCH, GCH, 0), (c0, GCH, 1))
            step((c0 + 2 * GCH, GCH, 1), (c0 + GCH, GCH, 0))
            return carry

        lax.fori_loop(0, n_loop, pair, 0)
        for n in range(1 + 2 * n_loop, len(sizes) - 1):
            step(group(n + 1), group(n))
        step(None, group(len(sizes) - 1))

    l_min = jnp.min(acc_ref[:, HD:HD + 1, :])

    @pl.when(l_min < L_MIN)
    def _():
        for hh in range(QPK):
            qs_ref[hh] = jnp.where(row == HD, 0.0, q_ref[0, hh].astype(jnp.float32)
                                   ).astype(jnp.bfloat16)
        acc_ref[...] = jnp.zeros(acc_ref.shape, jnp.float32)
        m_ref[...] = jnp.full(m_ref.shape, NEG_BIG, jnp.float32)

        def body2(c, carry):
            start = pl.multiple_of(c * TK, TK)
            kc = k_ref[0, pl.ds(start, TK), :]
            vc = v_ref[0, :, pl.ds(start, TK)]
            for hh in range(QPK):
                s = jnp.dot(kc, qs_ref[hh], preferred_element_type=jnp.float32)
                m_old = m_ref[hh]
                m_new = jnp.maximum(m_old, jnp.max(s, axis=0, keepdims=True))
                p = jnp.exp2(s - m_new).astype(jnp.bfloat16)
                acc_ref[hh] = jnp.exp2(m_old - m_new) * acc_ref[hh] + jnp.dot(
                    vc, p, preferred_element_type=jnp.float32)
                m_ref[hh] = m_new
            return carry

        lax.fori_loop(0, nch, body2, 0)

    outs = [acc_ref[hh, 0:HD, :] / acc_ref[hh, HD:HD + 1, :] for hh in range(QPK)]
    o_ref[...] = jnp.concatenate(outs, axis=0).T


def _attention(q_t, k2, v_t, kmax):
    ttot = k2.shape[1]
    nt = ttot // TM
    assert ttot % TK == 0 and ttot // TK >= 3
    kern = functools.partial(_attn_kernel, n_chunks=ttot // TK)
    grid_spec = pltpu.PrefetchScalarGridSpec(
        num_scalar_prefetch=1,
        grid=(NKV, nt),
        in_specs=[
            pl.BlockSpec((1, QPK, 2 * HD, TM), lambda g, i, km: (g, 0, 0, i)),
            pl.BlockSpec((1, ttot, 2 * HD), lambda g, i, km: (g, 0, 0)),
            pl.BlockSpec((1, VROWS, ttot), lambda g, i, km: (g, 0, 0)),
        ],
        out_specs=pl.BlockSpec((TM, QPK * HD), lambda g, i, km: (i, g)),
        scratch_shapes=[
            pltpu.VMEM((QPK, 2 * HD, TM), jnp.bfloat16),
            pltpu.VMEM((2, QPK, GCH * TK, TM), jnp.float32),
            pltpu.VMEM((QPK, VROWS, TM), jnp.float32),
            pltpu.VMEM((QPK, 1, TM), jnp.float32),
        ],
    )
    return pl.pallas_call(
        kern,
        grid_spec=grid_spec,
        out_shape=jax.ShapeDtypeStruct((ttot, ATTN_W), jnp.float32),
        compiler_params=_cparams(("arbitrary", "arbitrary")),
        name="attention",
    )(kmax, q_t, k2, v_t)


def _fill_ext(ext_ref, cur_ref, prev_ref, next_ref, left_ok, right_ok):
    zero = jnp.zeros((HALO, cur_ref.shape[1]), jnp.float32)
    ext_ref[0:HALO, :] = jnp.where(left_ok, prev_ref[...], zero)
    ext_ref[HALO:HALO + TM, :] = cur_ref[...]
    ext_ref[HALO + TM:HALO + TM + HALO, :] = jnp.where(right_ok, next_ref[...], zero)


def _mix_kernel(x_ref, attn_ref, u_ref, up_ref, un_ref, ag_ref, agp_ref, agn_ref, mod_ref,
                wpool_ref, pscale_ref, wdw_ref, bdw_ref, cvg_ref, cvb_ref, wpw_ref, wout_ref,
                ln1g_ref, ln1b_ref, wr_ref, br_ref,
                x1_ref, h2_ref, route_ref, cnt_ref,
                uext_ref, agext_ref, base_ref, *, alpha, n_lat):
    i = pl.program_id(0)
    nt = pl.num_programs(0)
    is_ctx = i == 0
    left_ok = i >= 2
    right_ok = jnp.logical_and(i >= 1, i < nt - 1)

    @pl.when(i == 0)
    def _():
        base_ref[...] = jnp.zeros(base_ref.shape, jnp.float32)

    _fill_ext(uext_ref, u_ref, up_ref, un_ref, left_ok, right_ok)
    _fill_ext(agext_ref, ag_ref, agp_ref, agn_ref, left_ok, right_ok)

    def ush(off, rows=TM):
        return uext_ref[HALO + off:HALO + off + rows, :]

    a2 = ush(-8, TM + 15) + ush(-7, TM + 15)
    a4 = a2[0:TM + 13] + a2[2:TM + 15]
    a8 = a4[0:TM + 9] + a4[4:TM + 13]
    a16 = a8[0:TM] + a8[8:TM + 8]
    sums = (a2[7:7 + TM], a4[6:6 + TM], a8[4:4 + TM], a16)
    pos = lax.broadcasted_iota(jnp.int32, (TM, 1), 0) + jnp.where(is_ctx, 0, (i - 1) * TM)
    seq_n = jnp.where(is_ctx, TM, n_lat)
    u_cur = u_ref[...]
    lane = lax.broadcasted_iota(jnp.int32, (TM, POOL_W), 1)
    pooled = jnp.zeros((TM, POOL_W), jnp.float32)
    for gi, w in enumerate(POOL_WINDOWS):
        lo = jnp.maximum(pos - w // 2, 0)
        hi = jnp.minimum(pos + (w - w // 2), seq_n)
        inv = 1.0 / (hi - lo).astype(jnp.float32)
        in_group = jnp.logical_and(lane >= gi * POOL_GW, lane < (gi + 1) * POOL_GW)
        pooled = jnp.where(in_group, sums[gi] * inv - u_cur, pooled)
    y_pool = jnp.dot(pooled.astype(jnp.bfloat16), wpool_ref[...],
                     preferred_element_type=jnp.float32) * pscale_ref[...]

    conv = jnp.zeros((TM, CONV_W), jnp.float32) + bdw_ref[...]
    for k in range(CONV_K):
        off = HALO + k - CONV_K // 2
        conv = conv + agext_ref[off:off + TM, :] * wdw_ref[k:k + 1, :]
    cn = _ln_plain(conv) * cvg_ref[...] + cvb_ref[...]
    cn = cn * jax.nn.sigmoid(cn)
    y_conv = jnp.dot(cn.astype(jnp.bfloat16), wpw_ref[...], preferred_element_type=jnp.float32)

    y = jnp.dot(attn_ref[...].astype(jnp.bfloat16), wout_ref[0:ATTN_W, :],
                preferred_element_type=jnp.float32)
    y = y + jnp.dot(y_pool.astype(jnp.bfloat16), wout_ref[ATTN_W:ATTN_W + POOL_W, :],
                    preferred_element_type=jnp.float32)
    y = y + jnp.dot(y_conv.astype(jnp.bfloat16), wout_ref[ATTN_W + POOL_W:, :],
                    preferred_element_type=jnp.float32)
    x1 = _ln_plain(alpha * x_ref[...] + _mod_rows(mod_ref, 2) * y) * ln1g_ref[...] + ln1b_ref[...]
    x1_ref[...] = x1
    h2 = _ln_plain(x1) * (1.0 + _mod_rows(mod_ref, 4)) + _mod_rows(mod_ref, 3)
    h2_ref[...] = h2

    h_hi = h2.astype(jnp.bfloat16)
    h_lo = (h2 - h_hi.astype(jnp.float32)).astype(jnp.bfloat16)
    hw = jnp.dot(h_hi, wr_ref[...], preferred_element_type=jnp.float32)
    logits = (hw[:, :LANES] + hw[:, LANES:] + br_ref[...]
              + jnp.dot(h_lo, wr_ref[:, :LANES], preferred_element_type=jnp.float32))
    ln = lax.broadcasted_iota(jnp.int32, (TM, LANES), 1)
    neg = jnp.float32(-jnp.inf)
    is_g = ln < N_GROUPS
    lg = jnp.where(is_g, logits, neg)
    mg = jnp.max(lg, axis=-1, keepdims=True)
    gsel = jnp.min(jnp.where(lg == mg, ln, LANES), axis=-1, keepdims=True)
    p_group = 1.0 / jnp.sum(jnp.exp(lg - mg), axis=-1, keepdims=True)
    e_lane = ln - N_GROUPS
    in_sel = jnp.logical_and(e_lane >= gsel * EPG, e_lane < (gsel + 1) * EPG)
    le = jnp.where(in_sel, logits, neg)
    v1 = jnp.max(le, axis=-1, keepdims=True)
    e1 = jnp.min(jnp.where(le == v1, e_lane, LANES), axis=-1, keepdims=True)
    le2 = jnp.where(e_lane == e1, neg, le)
    v2 = jnp.max(le2, axis=-1, keepdims=True)
    e2 = jnp.min(jnp.where(le2 == v2, e_lane, LANES), axis=-1, keepdims=True)
    ex = jnp.exp(v2 - v1)
    gate1 = p_group / (1.0 + ex)
    gate2 = p_group * ex / (1.0 + ex)

    oh = jnp.logical_or(ln == e1, ln == e2)
    ohb = jnp.where(oh, 1.0, 0.0).astype(jnp.bfloat16)
    r_i = lax.broadcasted_iota(jnp.int32, (TM, TM), 0)
    c_i = lax.broadcasted_iota(jnp.int32, (TM, TM), 1)
    tri = jnp.where(c_i < r_i, 1.0, 0.0).astype(jnp.bfloat16)
    before = jnp.dot(tri, ohb, preferred_element_type=jnp.float32) + base_ref[...]
    rank1 = jnp.sum(jnp.where(ln == e1, before, 0.0), axis=-1, keepdims=True)
    rank2 = jnp.sum(jnp.where(ln == e2, before, 0.0), axis=-1, keepdims=True)
    base_ref[...] = base_ref[...] + jnp.sum(ohb.astype(jnp.float32), axis=0, keepdims=True)
    cnt_ref[...] = jnp.broadcast_to(base_ref[...], cnt_ref.shape)

    route = jnp.zeros((TM, LANES), jnp.float32)
    for j, val in enumerate((e1.astype(jnp.float32), e2.astype(jnp.float32), rank1, rank2,
                             gate1, gate2)):
        route = jnp.where(ln == j, val, route)
    route_ref[...] = route


def _mixer(xa, attn, u, ag, mod, wpool, pscale, wdw, bdw, cvg, cvb, wpw, wout, ln1g, ln1b,
           wr, br, alpha, l):
    ttot = xa.shape[0]
    nt = ttot // TM
    hb = TM // HALO
    nhb = ttot // HALO

    def tile(w):
        return pl.BlockSpec((TM, w), lambda i: (i, 0))

    def prev(w):
        return pl.BlockSpec((HALO, w), lambda i: (jnp.maximum(i * hb - 1, 0), 0))

    def nxt(w):
        return pl.BlockSpec((HALO, w), lambda i: (jnp.minimum((i + 1) * hb, nhb - 1), 0))

    kern = functools.partial(_mix_kernel, alpha=alpha, n_lat=ttot - TM)
    return pl.pallas_call(
        kern,
        grid=(nt,),
        in_specs=[
            tile(D), tile(ATTN_W),
            tile(POOL_W), prev(POOL_W), nxt(POOL_W),
            tile(CONV_W), prev(CONV_W), nxt(CONV_W),
            _layer((8, N_MOD * D), l),
            _layer((POOL_W, POOL_W), l), _layer((1, POOL_W), l),
            _layer((32, CONV_W), l), _layer((1, CONV_W), l), _layer((1, CONV_W), l),
            _layer((1, CONV_W), l),
            _layer((CONV_W, CONV_W), l), _layer((D, D), l),
            _layer((1, D), l), _layer((1, D), l),
            _layer((D, 2 * LANES), l), _layer((1, LANES), l),
        ],
        out_specs=[tile(D), tile(D), tile(LANES), _full((8, LANES))],
        out_shape=[
            jax.ShapeDtypeStruct((ttot, D), jnp.float32),
            jax.ShapeDtypeStruct((ttot, D), jnp.float32),
            jax.ShapeDtypeStruct((ttot, LANES), jnp.float32),
            jax.ShapeDtypeStruct((8, LANES), jnp.float32),
        ],
        scratch_shapes=[
            pltpu.VMEM((TM + 2 * HALO, POOL_W), jnp.float32),
            pltpu.VMEM((TM + 2 * HALO, CONV_W), jnp.float32),
            pltpu.VMEM((1, LANES), jnp.float32),
        ],
        compiler_params=_cparams(("arbitrary",)),
        name="mixer",
    )(xa, attn, u, u, u, ag, ag, ag, mod, wpool, pscale, wdw, bdw, cvg, cvb, wpw, wout,
      ln1g, ln1b, wr, br)


ROW_GROUP = 8


def _dispatch_kernel(pends_ref, d1_ref, d2_ref, h_ref, xs_ref, zero_buf, sem, zsem, *, n_blk):
    @pl.when(pl.program_id(0) == 0)
    def _():
        zero_buf[...] = jnp.zeros(zero_buf.shape, jnp.float32)

        def zero_block(first_row):
            return pltpu.make_async_copy(
                zero_buf, xs_ref.at[pl.ds(pl.multiple_of(first_row, EBLK), EBLK), :], zsem)

        def seg_nonempty(e):
            return pends_ref[e] > (pends_ref[e - 1] if e > 0 else 0)

        for e in range(N_EXPERTS):
            @pl.when(seg_nonempty(e))
            def _():
                zero_block(pends_ref[e] - EBLK).start()
        n_used = pends_ref[N_EXPERTS - 1] // EBLK

        def tail_start(b, carry):
            zero_block(b * EBLK).start()
            return carry

        lax.fori_loop(n_used, n_blk, tail_start, 0)
        for e in range(N_EXPERTS):
            @pl.when(seg_nonempty(e))
            def _():
                zero_block(pends_ref[e] - EBLK).wait()

        def tail_wait(b, carry):
            zero_block(b * EBLK).wait()
            return carry

        lax.fori_loop(n_used, n_blk, tail_wait, 0)

    def start(r8, carry):
        base = pl.multiple_of(r8 * ROW_GROUP, ROW_GROUP)
        rows = h_ref.at[pl.ds(base, ROW_GROUP), :]
        for j in range(ROW_GROUP):
            for d_ref in (d1_ref, d2_ref):
                pltpu.make_async_copy(rows.at[pl.ds(j, 1), :],
                                      xs_ref.at[pl.ds(d_ref[0, 0, base + j], 1), :], sem).start()
        return carry

    lax.fori_loop(0, TM // ROW_GROUP, start, 0)
    for _ in range(2):
        pltpu.make_async_copy(h_ref, xs_ref.at[pl.ds(0, TM), :], sem).wait()


def _dispatch(h2, d1, d2, pends, n_slots):
    ttot = h2.shape[0]
    nt = ttot // TM
    smem = pl.BlockSpec((1, 1, TM), lambda i, pe: (i, 0, 0), memory_space=pltpu.SMEM)
    grid_spec = pltpu.PrefetchScalarGridSpec(
        num_scalar_prefetch=1,
        grid=(nt,),
        in_specs=[smem, smem, pl.BlockSpec((TM, D), lambda i, pe: (i, 0))],
        out_specs=pl.BlockSpec(memory_space=pl.ANY),
        scratch_shapes=[pltpu.VMEM((EBLK, D), jnp.float32),
                        pltpu.SemaphoreType.DMA(()), pltpu.SemaphoreType.DMA(())],
    )
    return pl.pallas_call(
        functools.partial(_dispatch_kernel, n_blk=n_slots // EBLK),
        grid_spec=grid_spec,
        out_shape=jax.ShapeDtypeStruct((n_slots, D), jnp.float32),
        compiler_params=_cparams(("arbitrary",)),
        name="dispatch",
    )(pends, d1, d2, h2)


def _expert_kernel(be_ref, nu_ref, x_ref, wg_ref, wu_ref, wd_ref, y_ref, wgu_b, wd_b):
    b = pl.program_id(0)
    new_expert = jnp.logical_or(b == 0, be_ref[b] != be_ref[jnp.maximum(b - 1, 0)])

    @pl.when(jnp.logical_and(b < nu_ref[0], new_expert))
    def _():
        wgu_b[:, :D_EXPERT] = wg_ref[0].astype(jnp.bfloat16)
        wgu_b[:, D_EXPERT:] = wu_ref[0].astype(jnp.bfloat16)
        wd_b[...] = wd_ref[0].astype(jnp.bfloat16)

    @pl.when(b < nu_ref[0])
    def _():
        x = x_ref[...].astype(jnp.bfloat16)
        gu = jnp.dot(x, wgu_b[...], preferred_element_type=jnp.float32)
        gt = gu[:, :D_EXPERT]
        hm = (gt * jax.nn.sigmoid(gt) * gu[:, D_EXPERT:]).astype(jnp.bfloat16)
        y_ref[...] = jnp.dot(hm, wd_b[...], preferred_element_type=jnp.float32)

    @pl.when(b >= nu_ref[0])
    def _():
        y_ref[...] = jnp.zeros(y_ref.shape, jnp.float32)


def _experts(xs, blk_e, n_used, wg, wu, wd, l):
    n_slots = xs.shape[0]
    nblk = n_slots // EBLK

    def row_map(b, be, nu):
        return (jnp.minimum(b, nu[0] - 1), 0)

    def w_map(b, be, nu):
        return (l, be[b], 0, 0)

    grid_spec = pltpu.PrefetchScalarGridSpec(
        num_scalar_prefetch=2,
        grid=(nblk,),
        in_specs=[
            pl.BlockSpec((EBLK, D), row_map),
            pl.BlockSpec((None, 1, D, D_EXPERT), w_map),
            pl.BlockSpec((None, 1, D, D_EXPERT), w_map),
            pl.BlockSpec((None, 1, D_EXPERT, D), w_map),
        ],
        out_specs=pl.BlockSpec((EBLK, D), lambda b, be, nu: (b, 0)),
        scratch_shapes=[pltpu.VMEM((D, 2 * D_EXPERT), jnp.bfloat16),
                        pltpu.VMEM((D_EXPERT, D), jnp.bfloat16)],
    )
    return pl.pallas_call(
        _expert_kernel,
        grid_spec=grid_spec,
        out_shape=jax.ShapeDtypeStruct((n_slots, D), jnp.float32),
        compiler_params=_cparams(("arbitrary",)),
        name="experts",
    )(blk_e, n_used, xs, wg, wu, wd)


def _combine_kernel(d1_ref, d2_ref, d1n_ref, d2n_ref, x1_ref, route_ref, mod_ref, ln2g_ref,
                    ln2b_ref, ys_ref, o_ref, y_buf, sems, *, alpha):
    i = pl.program_id(0)
    nt = pl.num_programs(0)
    slot = i % 2

    def gather(da_ref, db_ref, to_slot):
        def start(r8, carry):
            base = pl.multiple_of(r8 * ROW_GROUP, ROW_GROUP)
            for k, d_ref in enumerate((da_ref, db_ref)):
                rows = y_buf.at[to_slot, k, pl.ds(base, ROW_GROUP), :]
                for j in range(ROW_GROUP):
                    pltpu.make_async_copy(ys_ref.at[pl.ds(d_ref[0, 0, base + j], 1), :],
                                          rows.at[pl.ds(j, 1), :], sems.at[to_slot]).start()
            return carry

        lax.fori_loop(0, TM // ROW_GROUP, start, 0)

    @pl.when(i == 0)
    def _():
        gather(d1_ref, d2_ref, 0)

    @pl.when(i < nt - 1)
    def _():
        gather(d1n_ref, d2n_ref, 1 - slot)

    for k in range(2):
        pltpu.make_async_copy(ys_ref.at[pl.ds(0, TM), :], y_buf.at[slot, k], sems.at[slot]).wait()

    route = route_ref[...]
    g1 = route[:, 4:5]
    g2 = route[:, 5:6]
    o = g1 * y_buf[slot, 0] + g2 * y_buf[slot, 1]
    z = alpha * x1_ref[...] + _mod_rows(mod_ref, 5) * o
    o_ref[...] = _ln_plain(z) * ln2g_ref[...] + ln2b_ref[...]


def _combine(x1, route, mod, ln2g, ln2b, ys, d1, d2, alpha, l, latents_only):
    ttot = x1.shape[0]
    nt = ttot // TM
    smem = pl.BlockSpec((1, 1, TM), lambda i: (i, 0, 0), memory_space=pltpu.SMEM)
    smem_next = pl.BlockSpec((1, 1, TM), lambda i: (jnp.minimum(i + 1, nt - 1), 0, 0),
                             memory_space=pltpu.SMEM)
    kern = functools.partial(_combine_kernel, alpha=alpha)
    if latents_only:
        out_spec = pl.BlockSpec((TM, D), lambda i: (jnp.maximum(i - 1, 0), 0))
        out_rows = ttot - TM
    else:
        out_spec = pl.BlockSpec((TM, D), lambda i: (i, 0))
        out_rows = ttot
    return pl.pallas_call(
        kern,
        grid=(nt,),
        in_specs=[smem, smem, smem_next, smem_next,
                  pl.BlockSpec((TM, D), lambda i: (i, 0)),
                  pl.BlockSpec((TM, LANES), lambda i: (i, 0)),
                  _layer((8, N_MOD * D), l), _layer((1, D), l), _layer((1, D), l),
                  pl.BlockSpec(memory_space=pl.ANY)],
        out_specs=out_spec,
        out_shape=jax.ShapeDtypeStruct((out_rows, D), jnp.float32),
        scratch_shapes=[pltpu.VMEM((2, 2, TM, D), jnp.float32),
                        pltpu.SemaphoreType.DMA((2,))],
        compiler_params=_cparams(("arbitrary",)),
        name="combine",
    )(d1, d2, d1, d2, x1, route, mod, ln2g, ln2b, ys)


def _rope_tables_t(n_lat, n_ctx):
    rows = n_lat // GRID_W
    row = jnp.broadcast_to(jnp.arange(rows)[:, None], (rows, GRID_W)).reshape(-1)
    col = jnp.broadcast_to(jnp.arange(GRID_W)[None, :], (rows, GRID_W)).reshape(-1)
    n_freq = HD // 4
    inv = ROPE_THETA ** (-jnp.arange(n_freq, dtype=jnp.float32) / n_freq)
    pos = jnp.stack([row, col], axis=-1).astype(jnp.float32)
    ang = (pos[:, :, None] * inv).reshape(n_lat, 2 * n_freq)
    cos = jnp.concatenate([jnp.ones((n_ctx, 2 * n_freq), jnp.float32), jnp.cos(ang)], axis=0)
    sin = jnp.concatenate([jnp.zeros((n_ctx, 2 * n_freq), jnp.float32), jnp.sin(ang)], axis=0)
    return cos.T, sin.T


def _row(v):
    return v.reshape(1, -1)


def kernel(x, c, ctx, c_ctx, w_mod, b_mod, w_in, q_gain, k_gain, w_pool, pool_scale, w_dw, b_dw,
           cv_ln_g, cv_ln_b, w_cv_pw, w_out, ln1_g, ln1_b, ln2_g, ln2_b, w_rg, b_rg, w_re, b_re,
           w_e_gate, w_e_up, w_e_down):
    depth = w_mod.shape[0]
    n_lat = x.shape[1]
    n_ctx = ctx.shape[1]
    assert x.shape[0] == 1 and n_ctx == TM and n_lat % TM == 0 and n_lat % GRID_W == 0
    ttot = n_ctx + n_lat
    alpha = float((2 * depth) ** 0.25)
    n_blk = -(-(2 * ttot) // EBLK) + N_EXPERTS
    n_slots = n_blk * EBLK

    xa = jnp.concatenate([ctx[0], x[0]], axis=0)
    cc = jnp.concatenate([c, c_ctx[None], jnp.zeros((6, D), jnp.float32)], axis=0)
    mod = _modulation(cc, w_mod, b_mod)
    cos_t, sin_t = _rope_tables_t(n_lat, n_ctx)

    n_qkv = ATTN_W + 2 * KV_W
    bf = jnp.bfloat16
    w_t = jnp.swapaxes(w_in[:, :, :n_qkv], 1, 2).astype(bf)
    w_n = w_in[:, :, n_qkv:].astype(bf)
    qg = jnp.broadcast_to(q_gain[:, :, None], (depth, HD, TM))
    kg = jnp.broadcast_to(k_gain[:, :, None], (depth, HD, TM))
    n_pg = len(POOL_WINDOWS)
    wpool = (jnp.eye(n_pg, dtype=jnp.float32)[None, :, None, :, None] * w_pool[:, :, :, None, :]
             ).reshape(depth, POOL_W, POOL_W).astype(bf)
    wdw = jnp.pad(w_dw, ((0, 0), (0, 32 - CONV_K), (0, 0)))
    n_r = N_GROUPS + N_EXPERTS
    wr = jnp.concatenate([w_rg, w_re, jnp.zeros((depth, D, LANES - n_r), jnp.float32)], axis=2)
    br = jnp.concatenate([b_rg, b_re, jnp.zeros((depth, LANES - n_r), jnp.float32)], axis=1)
    wr_hi = wr.astype(bf)
    wr = jnp.concatenate([wr_hi, (wr - wr_hi.astype(jnp.float32)).astype(bf)], axis=2)
    wpw = w_cv_pw.astype(bf)
    wout = w_out.astype(bf)

    def rows(v):
        return v[:, None, :]

    e_ids = jnp.arange(N_EXPERTS, dtype=jnp.int32)
    blk_start = jnp.arange(n_blk, dtype=jnp.int32) * EBLK
    for l in range(depth):
        q_t, k2, v_t, kn2, u, ag = _inproj(xa, mod, w_t, w_n, qg, kg, cos_t, sin_t, l)
        kmax = jnp.sqrt(jnp.max(kn2[:, :, 0, 0], axis=0))
        attn = _attention(q_t, k2, v_t, kmax)
        x1, h2, route, cnt = _mixer(
            xa, attn, u, ag, mod, wpool, rows(pool_scale), wdw, rows(b_dw), rows(cv_ln_g),
            rows(cv_ln_b), wpw, wout, rows(ln1_g), rows(ln1_b), wr, rows(br), alpha, l)

        counts = cnt[0, :N_EXPERTS].astype(jnp.int32)
        padded = (counts + EBLK - 1) // EBLK * EBLK
        pends = jnp.cumsum(padded)
        pstarts = pends - padded
        ri = route[:, :4].astype(jnp.int32)

        def dest(e, r):
            start = jnp.sum(jnp.where(e[:, None] == e_ids[None, :], pstarts[None, :], 0), axis=1)
            return (start + r).reshape(ttot // TM, 1, TM)

        d1 = dest(ri[:, 0], ri[:, 2])
        d2 = dest(ri[:, 1], ri[:, 3])
        n_used = (pends[-1] // EBLK).astype(jnp.int32).reshape(1)
        first_row = jnp.minimum(blk_start, pends[-1] - 1)
        blk_e = jnp.sum((pends[None, :] <= first_row[:, None]).astype(jnp.int32), axis=1)
        blk_e = jnp.minimum(blk_e, N_EXPERTS - 1)

        xs = _dispatch(h2, d1, d2, pends, n_slots)
        ys = _experts(xs, blk_e, n_used, w_e_gate, w_e_up, w_e_down, l)
        xa = _combine(x1, route, mod, rows(ln2_g), rows(ln2_b), ys, d1, d2, alpha, l,
                      latents_only=(l == depth - 1))
    return xa[None]
```

```python
import functools

import jax
import jax.numpy as jnp
from jax import lax
from jax.experimental import pallas as pl
from jax.experimental.pallas import tpu as pltpu

D = 1024
GRID_W = 64
HD = 64
NQ = 8
NKV = 2
QPK = NQ // NKV
ATTN_W = NQ * HD
KV_W = NKV * HD
POOL_W = 256
POOL_WINDOWS = (2, 4, 8, 16)
POOL_GW = POOL_W // len(POOL_WINDOWS)
CONV_W = 256
CONV_K = 31
N_GROUPS = 4
EPG = 8
N_EXPERTS = N_GROUPS * EPG
D_EXPERT = 512
N_MOD = 6
ROPE_THETA = 10000.0
EPS = 1e-6

TM = 256
HALO = 16
EBLK = 256
LANES = 128
LOG2E = 1.4426950408889634
BOUND_MARGIN = 1.0 + 2.0 ** -5
VMEM_LIMIT = 56 * 1024 * 1024
NEG_BIG = -1e30

_HI = lax.Precision.HIGHEST


def _cparams(sem):
    return pltpu.CompilerParams(dimension_semantics=sem, vmem_limit_bytes=VMEM_LIMIT)


def _full(shape):
    n = len(shape)
    return pl.BlockSpec(shape, lambda *a, n=n: (0,) * n)


def _layer(shape, l):
    n = len(shape)
    return pl.BlockSpec((None,) + tuple(shape), lambda *a, n=n: (l,) + (0,) * n)


def _mod_kernel(cc_ref, w_ref, b_ref, o_ref):
    cc = cc_ref[...]
    a = cc * jax.nn.sigmoid(cc)
    o_ref[0] = jnp.dot(a, w_ref[0], precision=_HI, preferred_element_type=jnp.float32) + b_ref[0]


def _modulation(cc, w_mod, b_mod):
    depth = w_mod.shape[0]
    nw = w_mod.shape[2]
    bn = 1024
    return pl.pallas_call(
        _mod_kernel,
        grid=(depth, nw // bn),
        in_specs=[
            pl.BlockSpec((8, D), lambda l, j: (0, 0)),
            pl.BlockSpec((1, D, bn), lambda l, j: (l, 0, j)),
            pl.BlockSpec((1, 1, bn), lambda l, j: (l, 0, j)),
        ],
        out_specs=pl.BlockSpec((1, 8, bn), lambda l, j: (l, 0, j)),
        out_shape=jax.ShapeDtypeStruct((depth, 8, nw), jnp.float32),
        compiler_params=_cparams(("arbitrary", "arbitrary")),
        name="modulation",
    )(cc, w_mod, b_mod.reshape(depth, 1, nw))


def _ln_plain(x):
    mu = jnp.mean(x, axis=-1, keepdims=True)
    xc = x - mu
    var = jnp.mean(xc * xc, axis=-1, keepdims=True)
    return xc * lax.rsqrt(var + EPS)


def _mod_rows(mod_ref, idx):
    row = jnp.where(pl.program_id(0) == 0, 1, 0)
    return mod_ref[pl.ds(row, 1), idx * D:(idx + 1) * D]


def _rms_rope_t(t, gain, cos, sin):
    ms = jnp.mean(t * t, axis=0, keepdims=True)
    t = t * lax.rsqrt(ms + EPS) * gain
    outs = []
    for a in range(2):
        t1 = t[32 * a:32 * a + 16]
        t2 = t[32 * a + 16:32 * a + 32]
        c = cos[16 * a:16 * a + 16]
        s = sin[16 * a:16 * a + 16]
        outs.append(t1 * c - t2 * s)
        outs.append(t2 * c + t1 * s)
    return jnp.concatenate(outs, axis=0)


def _inproj_kernel(x_ref, mod_ref, wt_ref, wn_ref, qg_ref, kg_ref, cos_ref, sin_ref,
                   q_ref, k_ref, v_ref, kmax_ref, u_ref, ag_ref):
    x = x_ref[...]
    h = _ln_plain(x) * (1.0 + _mod_rows(mod_ref, 1)) + _mod_rows(mod_ref, 0)
    hb = h.astype(jnp.bfloat16)
    qkv_t = lax.dot_general(wt_ref[...], hb, (((1,), (1,)), ((), ())),
                            preferred_element_type=jnp.float32)
    cos = cos_ref[...]
    sin = sin_ref[...]
    qg = qg_ref[...]
    kg = kg_ref[...]
    ones_row = jnp.where(lax.broadcasted_iota(jnp.int32, (HD, TM), 0) == 0, 1.0, 0.0)
    for hq in range(NQ):
        t = _rms_rope_t(qkv_t[hq * HD:(hq + 1) * HD], qg, cos, sin) * (LOG2E * HD ** -0.5)
        nq = jnp.sqrt(jnp.sum(t * t, axis=0, keepdims=True))
        q_ref[hq // QPK, hq % QPK] = jnp.concatenate(
            [t, ones_row * (-BOUND_MARGIN * nq)], axis=0).astype(jnp.bfloat16)
    for g in range(NKV):
        kt = _rms_rope_t(qkv_t[ATTN_W + g * HD:ATTN_W + (g + 1) * HD], kg, cos, sin)
        kn = jnp.max(jnp.sum(kt * kt, axis=0, keepdims=True), axis=1, keepdims=True)
        kmax_ref[0, g] = jnp.broadcast_to(kn, (8, LANES))
        k_ref[g] = jnp.concatenate([kt, ones_row], axis=0).T.astype(jnp.bfloat16)
        vt = qkv_t[ATTN_W + KV_W + g * HD:ATTN_W + KV_W + (g + 1) * HD]
        v_ref[g] = vt.astype(jnp.bfloat16)
    uag = jnp.dot(hb, wn_ref[...], preferred_element_type=jnp.float32)
    u_ref[...] = uag[:, :POOL_W]
    a = uag[:, POOL_W:POOL_W + CONV_W]
    gt = uag[:, POOL_W + CONV_W:]
    ag_ref[...] = a * jax.nn.sigmoid(gt)


def _inproj(xa, mod, w_t, w_n, qg, kg, cos_t, sin_t, l):
    ttot = xa.shape[0]
    nt = ttot // TM
    return pl.pallas_call(
        _inproj_kernel,
        grid=(nt,),
        in_specs=[
            pl.BlockSpec((TM, D), lambda i: (i, 0)),
            _layer((8, N_MOD * D), l),
            _layer((ATTN_W + 2 * KV_W, D), l),
            _layer((D, POOL_W + 2 * CONV_W), l),
            _layer((HD, TM), l),
            _layer((HD, TM), l),
            pl.BlockSpec((32, TM), lambda i: (0, i)),
            pl.BlockSpec((32, TM), lambda i: (0, i)),
        ],
        out_specs=[
            pl.BlockSpec((NKV, QPK, 2 * HD, TM), lambda i: (0, 0, 0, i)),
            pl.BlockSpec((NKV, TM, 2 * HD), lambda i: (0, i, 0)),
            pl.BlockSpec((NKV, HD, TM), lambda i: (0, 0, i)),
            pl.BlockSpec((1, NKV, 8, LANES), lambda i: (i, 0, 0, 0)),
            pl.BlockSpec((TM, POOL_W), lambda i: (i, 0)),
            pl.BlockSpec((TM, CONV_W), lambda i: (i, 0)),
        ],
        out_shape=[
            jax.ShapeDtypeStruct((NKV, QPK, 2 * HD, ttot), jnp.bfloat16),
            jax.ShapeDtypeStruct((NKV, ttot, 2 * HD), jnp.bfloat16),
            jax.ShapeDtypeStruct((NKV, HD, ttot), jnp.bfloat16),
            jax.ShapeDtypeStruct((nt, NKV, 8, LANES), jnp.float32),
            jax.ShapeDtypeStruct((ttot, POOL_W), jnp.float32),
            jax.ShapeDtypeStruct((ttot, CONV_W), jnp.float32),
        ],
        compiler_params=_cparams(("arbitrary",)),
        name="inproj",
    )(xa, mod, w_t, w_n, qg, kg, cos_t, sin_t)


TK = 256
GCH = 5
L_MIN = 2.0 ** -60


def _attn_kernel(kmax_ref, q_ref, k_ref, v_ref, o_ref, qs_ref, s_ref, acc_ref, l_ref, m_ref, *,
                 n_chunks):
    g = pl.program_id(0)
    i = pl.program_id(1)
    nch = jnp.where(i == 0, 1, n_chunks)
    row = lax.broadcasted_iota(jnp.int32, (2 * HD, TM), 0)
    kmax = kmax_ref[g]

    for hh in range(QPK):
        qf = q_ref[0, hh].astype(jnp.float32)
        qs_ref[hh] = jnp.where(row == HD, qf * kmax, qf).astype(jnp.bfloat16)
    acc_ref[...] = jnp.zeros(acc_ref.shape, jnp.float32)
    l_ref[...] = jnp.zeros(l_ref.shape, jnp.float32)

    def fold(p):
        return jnp.sum(p.reshape(TK // 8, 8, TM), axis=0)

    def step(nxt, cur):
        for kk in range(max(grp[1] for grp in (nxt, cur) if grp is not None)):
            do_nxt = nxt is not None and kk < nxt[1]
            do_cur = cur is not None and kk < cur[1]
            rows = slice(kk * TK, (kk + 1) * TK)
            if do_nxt:
                kc = k_ref[0, pl.ds(pl.multiple_of((nxt[0] + kk) * TK, TK), TK), :]
            if do_cur:
                vc = v_ref[0, :, pl.ds(pl.multiple_of((cur[0] + kk) * TK, TK), TK)]
            for hh in range(QPK):
                if do_nxt:
                    s_ref[nxt[2], hh, rows, :] = jnp.dot(
                        kc, qs_ref[hh], preferred_element_type=jnp.float32)
                if do_cur:
                    p = jnp.exp2(s_ref[cur[2], hh, rows, :])
                    l_ref[hh] += fold(p)
                    acc_ref[hh] += jnp.dot(vc, p.astype(jnp.bfloat16),
                                           preferred_element_type=jnp.float32)

    @pl.when(i == 0)
    def _():
        kc = k_ref[0, 0:TM, :]
        vc = v_ref[0, :, 0:TM]
        for hh in range(QPK):
            p = jnp.exp2(jnp.dot(kc, qs_ref[hh], preferred_element_type=jnp.float32))
            l_ref[hh] = fold(p)
            acc_ref[hh] = jnp.dot(vc, p.astype(jnp.bfloat16), preferred_element_type=jnp.float32)

    @pl.when(i > 0)
    def _():
        n_mid = (n_chunks - 2) // GCH
        tail = n_chunks - 1 - GCH * n_mid
        sizes = [1] + [GCH] * n_mid + ([tail - 1, 1] if tail >= 2 else [tail])
        firsts = [sum(sizes[:n]) for n in range(len(sizes))]

        def group(n):
            return (firsts[n], sizes[n], n % 2)

        step(group(0), None)
        step(group(1), group(0))
        n_loop = max(n_mid - 1, 0) // 2

        def pair(j, carry):
            c0 = 1 + 2 * GCH * j
            step((c0 + GCH, GCH, 0), (c0, GCH, 1))
            step((c0 + 2 * GCH, GCH, 1), (c0 + GCH, GCH, 0))
            return carry

        lax.fori_loop(0, n_loop, pair, 0)
        for n in range(1 + 2 * n_loop, len(sizes) - 1):
            step(group(n + 1), group(n))
        step(None, group(len(sizes) - 1))

    l_min = jnp.min(jnp.sum(l_ref[...], axis=1))

    @pl.when(l_min < L_MIN)
    def _():
        for hh in range(QPK):
            qs_ref[hh] = jnp.where(row == HD, 0.0, q_ref[0, hh].astype(jnp.float32)
                                   ).astype(jnp.bfloat16)
        acc_ref[...] = jnp.zeros(acc_ref.shape, jnp.float32)
        l_ref[...] = jnp.zeros(l_ref.shape, jnp.float32)
        m_ref[...] = jnp.full(m_ref.shape, NEG_BIG, jnp.float32)

        def body2(c, carry):
            start = pl.multiple_of(c * TK, TK)
            kc = k_ref[0, pl.ds(start, TK), :]
            vc = v_ref[0, :, pl.ds(start, TK)]
            for hh in range(QPK):
                s = jnp.dot(kc, qs_ref[hh], preferred_element_type=jnp.float32)
                m_old = m_ref[hh]
                m_new = jnp.maximum(m_old, jnp.max(s, axis=0, keepdims=True))
                p = jnp.exp2(s - m_new)
                alpha = jnp.exp2(m_old - m_new)
                l_ref[hh] = alpha * l_ref[hh] + fold(p)
                acc_ref[hh] = alpha * acc_ref[hh] + jnp.dot(
                    vc, p.astype(jnp.bfloat16), preferred_element_type=jnp.float32)
                m_ref[hh] = m_new
            return carry

        lax.fori_loop(0, nch, body2, 0)

    outs = [acc_ref[hh] / jnp.sum(l_ref[hh], axis=0, keepdims=True) for hh in range(QPK)]
    o_ref[...] = jnp.concatenate(outs, axis=0).T


def _attention(q_t, k2, v_t, kmax):
    ttot = k2.shape[1]
    nt = ttot // TM
    assert ttot % TK == 0 and ttot // TK >= 3
    kern = functools.partial(_attn_kernel, n_chunks=ttot // TK)
    grid_spec = pltpu.PrefetchScalarGridSpec(
        num_scalar_prefetch=1,
        grid=(NKV, nt),
        in_specs=[
            pl.BlockSpec((1, QPK, 2 * HD, TM), lambda g, i, km: (g, 0, 0, i)),
            pl.BlockSpec((1, ttot, 2 * HD), lambda g, i, km: (g, 0, 0)),
            pl.BlockSpec((1, HD, ttot), lambda g, i, km: (g, 0, 0)),
        ],
        out_specs=pl.BlockSpec((TM, QPK * HD), lambda g, i, km: (i, g)),
        scratch_shapes=[
            pltpu.VMEM((QPK, 2 * HD, TM), jnp.bfloat16),
            pltpu.VMEM((2, QPK, GCH * TK, TM), jnp.float32),
            pltpu.VMEM((QPK, HD, TM), jnp.float32),
            pltpu.VMEM((QPK, 8, TM), jnp.float32),
            pltpu.VMEM((QPK, 1, TM), jnp.float32),
        ],
    )
    return pl.pallas_call(
        kern,
        grid_spec=grid_spec,
        out_shape=jax.ShapeDtypeStruct((ttot, ATTN_W), jnp.float32),
        compiler_params=_cparams(("arbitrary", "arbitrary")),
        name="attention",
    )(kmax, q_t, k2, v_t)


def _fill_ext(ext_ref, cur_ref, prev_ref, next_ref, left_ok, right_ok):
    zero = jnp.zeros((HALO, cur_ref.shape[1]), jnp.float32)
    ext_ref[0:HALO, :] = jnp.where(left_ok, prev_ref[...], zero)
    ext_ref[HALO:HALO + TM, :] = cur_ref[...]
    ext_ref[HALO + TM:HALO + TM + HALO, :] = jnp.where(right_ok, next_ref[...], zero)


def _mix_kernel(x_ref, attn_ref, u_ref, up_ref, un_ref, ag_ref, agp_ref, agn_ref, mod_ref,
                wpool_ref, pscale_ref, wdw_ref, bdw_ref, cvg_ref, cvb_ref, wpw_ref, wout_ref,
                ln1g_ref, ln1b_ref, wr_ref, br_ref,
                x1_ref, h2_ref, route_ref, cnt_ref,
                uext_ref, agext_ref, base_ref, *, alpha, n_lat):
    i = pl.program_id(0)
    nt = pl.num_programs(0)
    is_ctx = i == 0
    left_ok = i >= 2
    right_ok = jnp.logical_and(i >= 1, i < nt - 1)

    @pl.when(i == 0)
    def _():
        base_ref[...] = jnp.zeros(base_ref.shape, jnp.float32)

    _fill_ext(uext_ref, u_ref, up_ref, un_ref, left_ok, right_ok)
    _fill_ext(agext_ref, ag_ref, agp_ref, agn_ref, left_ok, right_ok)

    def ush(off, rows=TM):
        return uext_ref[HALO + off:HALO + off + rows, :]

    a2 = ush(-8, TM + 15) + ush(-7, TM + 15)
    a4 = a2[0:TM + 13] + a2[2:TM + 15]
    a8 = a4[0:TM + 9] + a4[4:TM + 13]
    a16 = a8[0:TM] + a8[8:TM + 8]
    sums = (a2[7:7 + TM], a4[6:6 + TM], a8[4:4 + TM], a16)
    pos = lax.broadcasted_iota(jnp.int32, (TM, 1), 0) + jnp.where(is_ctx, 0, (i - 1) * TM)
    seq_n = jnp.where(is_ctx, TM, n_lat)
    u_cur = u_ref[...]
    lane = lax.broadcasted_iota(jnp.int32, (TM, POOL_W), 1)
    pooled = jnp.zeros((TM, POOL_W), jnp.float32)
    for gi, w in enumerate(POOL_WINDOWS):
        lo = jnp.maximum(pos - w // 2, 0)
        hi = jnp.minimum(pos + (w - w // 2), seq_n)
        inv = 1.0 / (hi - lo).astype(jnp.float32)
        in_group = jnp.logical_and(lane >= gi * POOL_GW, lane < (gi + 1) * POOL_GW)
        pooled = jnp.where(in_group, sums[gi] * inv - u_cur, pooled)
    y_pool = jnp.dot(pooled.astype(jnp.bfloat16), wpool_ref[...],
                     preferred_element_type=jnp.float32) * pscale_ref[...]

    conv = jnp.zeros((TM, CONV_W), jnp.float32) + bdw_ref[...]
    for k in range(CONV_K):
        off = HALO + k - CONV_K // 2
        conv = conv + agext_ref[off:off + TM, :] * wdw_ref[k:k + 1, :]
    cn = _ln_plain(conv) * cvg_ref[...] + cvb_ref[...]
    cn = cn * jax.nn.sigmoid(cn)
    y_conv = jnp.dot(cn.astype(jnp.bfloat16), wpw_ref[...], preferred_element_type=jnp.float32)

    y = jnp.dot(attn_ref[...].astype(jnp.bfloat16), wout_ref[0:ATTN_W, :],
                preferred_element_type=jnp.float32)
    y = y + jnp.dot(y_pool.astype(jnp.bfloat16), wout_ref[ATTN_W:ATTN_W + POOL_W, :],
                    preferred_element_type=jnp.float32)
    y = y + jnp.dot(y_conv.astype(jnp.bfloat16), wout_ref[ATTN_W + POOL_W:, :],
                    preferred_element_type=jnp.float32)
    x1 = _ln_plain(alpha * x_ref[...] + _mod_rows(mod_ref, 2) * y) * ln1g_ref[...] + ln1b_ref[...]
    x1_ref[...] = x1
    h2 = _ln_plain(x1) * (1.0 + _mod_rows(mod_ref, 4)) + _mod_rows(mod_ref, 3)
    h2_ref[...] = h2

    h_hi = h2.astype(jnp.bfloat16)
    h_lo = (h2 - h_hi.astype(jnp.float32)).astype(jnp.bfloat16)
    hw = jnp.dot(h_hi, wr_ref[...], preferred_element_type=jnp.float32)
    logits = (hw[:, :LANES] + hw[:, LANES:] + br_ref[...]
              + jnp.dot(h_lo, wr_ref[:, :LANES], preferred_element_type=jnp.float32))
    ln = lax.broadcasted_iota(jnp.int32, (TM, LANES), 1)
    neg = jnp.float32(-jnp.inf)
    is_g = ln < N_GROUPS
    lg = jnp.where(is_g, logits, neg)
    mg = jnp.max(lg, axis=-1, keepdims=True)
    gsel = jnp.min(jnp.where(lg == mg, ln, LANES), axis=-1, keepdims=True)
    p_group = 1.0 / jnp.sum(jnp.exp(lg - mg), axis=-1, keepdims=True)
    e_lane = ln - N_GROUPS
    in_sel = jnp.logical_and(e_lane >= gsel * EPG, e_lane < (gsel + 1) * EPG)
    le = jnp.where(in_sel, logits, neg)
    v1 = jnp.max(le, axis=-1, keepdims=True)
    e1 = jnp.min(jnp.where(le == v1, e_lane, LANES), axis=-1, keepdims=True)
    le2 = jnp.where(e_lane == e1, neg, le)
    v2 = jnp.max(le2, axis=-1, keepdims=True)
    e2 = jnp.min(jnp.where(le2 == v2, e_lane, LANES), axis=-1, keepdims=True)
    ex = jnp.exp(v2 - v1)
    gate1 = p_group / (1.0 + ex)
    gate2 = p_group * ex / (1.0 + ex)

    oh = jnp.logical_or(ln == e1, ln == e2)
    ohb = jnp.where(oh, 1.0, 0.0).astype(jnp.bfloat16)
    r_i = lax.broadcasted_iota(jnp.int32, (TM, TM), 0)
    c_i = lax.broadcasted_iota(jnp.int32, (TM, TM), 1)
    tri = jnp.where(c_i < r_i, 1.0, 0.0).astype(jnp.bfloat16)
    before = jnp.dot(tri, ohb, preferred_element_type=jnp.float32) + base_ref[...]
    rank1 = jnp.sum(jnp.where(ln == e1, before, 0.0), axis=-1, keepdims=True)
    rank2 = jnp.sum(jnp.where(ln == e2, before, 0.0), axis=-1, keepdims=True)
    base_ref[...] = base_ref[...] + jnp.sum(ohb.astype(jnp.float32), axis=0, keepdims=True)
    cnt_ref[...] = jnp.broadcast_to(base_ref[...], cnt_ref.shape)

    route = jnp.zeros((TM, LANES), jnp.float32)
    for j, val in enumerate((e1.astype(jnp.float32), e2.astype(jnp.float32), rank1, rank2,
                             gate1, gate2)):
        route = jnp.where(ln == j, val, route)
    route_ref[...] = route


def _mixer(xa, attn, u, ag, mod, wpool, pscale, wdw, bdw, cvg, cvb, wpw, wout, ln1g, ln1b,
           wr, br, alpha, l):
    ttot = xa.shape[0]
    nt = ttot // TM
    hb = TM // HALO
    nhb = ttot // HALO

    def tile(w):
        return pl.BlockSpec((TM, w), lambda i: (i, 0))

    def prev(w):
        return pl.BlockSpec((HALO, w), lambda i: (jnp.maximum(i * hb - 1, 0), 0))

    def nxt(w):
        return pl.BlockSpec((HALO, w), lambda i: (jnp.minimum((i + 1) * hb, nhb - 1), 0))

    kern = functools.partial(_mix_kernel, alpha=alpha, n_lat=ttot - TM)
    return pl.pallas_call(
        kern,
        grid=(nt,),
        in_specs=[
            tile(D), tile(ATTN_W),
            tile(POOL_W), prev(POOL_W), nxt(POOL_W),
            tile(CONV_W), prev(CONV_W), nxt(CONV_W),
            _layer((8, N_MOD * D), l),
            _layer((POOL_W, POOL_W), l), _layer((1, POOL_W), l),
            _layer((32, CONV_W), l), _layer((1, CONV_W), l), _layer((1, CONV_W), l),
            _layer((1, CONV_W), l),
            _layer((CONV_W, CONV_W), l), _layer((D, D), l),
            _layer((1, D), l), _layer((1, D), l),
            _layer((D, 2 * LANES), l), _layer((1, LANES), l),
        ],
        out_specs=[tile(D), tile(D), tile(LANES), _full((8, LANES))],
        out_shape=[
            jax.ShapeDtypeStruct((ttot, D), jnp.float32),
            jax.ShapeDtypeStruct((ttot, D), jnp.float32),
            jax.ShapeDtypeStruct((ttot, LANES), jnp.float32),
            jax.ShapeDtypeStruct((8, LANES), jnp.float32),
        ],
        scratch_shapes=[
            pltpu.VMEM((TM + 2 * HALO, POOL_W), jnp.float32),
            pltpu.VMEM((TM + 2 * HALO, CONV_W), jnp.float32),
            pltpu.VMEM((1, LANES), jnp.float32),
        ],
        compiler_params=_cparams(("arbitrary",)),
        name="mixer",
    )(xa, attn, u, u, u, ag, ag, ag, mod, wpool, pscale, wdw, bdw, cvg, cvb, wpw, wout,
      ln1g, ln1b, wr, br)


ROW_GROUP = 8


def _dispatch_kernel(pends_ref, d1_ref, d2_ref, h_ref, xs_ref, zero_buf, sem, zsem, *, n_blk):
    @pl.when(pl.program_id(0) == 0)
    def _():
        zero_buf[...] = jnp.zeros(zero_buf.shape, jnp.float32)

        def zero_block(first_row):
            return pltpu.make_async_copy(
                zero_buf, xs_ref.at[pl.ds(pl.multiple_of(first_row, EBLK), EBLK), :], zsem)

        def seg_nonempty(e):
            return pends_ref[e] > (pends_ref[e - 1] if e > 0 else 0)

        for e in range(N_EXPERTS):
            @pl.when(seg_nonempty(e))
            def _():
                zero_block(pends_ref[e] - EBLK).start()
        n_used = pends_ref[N_EXPERTS - 1] // EBLK

        def tail_start(b, carry):
            zero_block(b * EBLK).start()
            return carry

        lax.fori_loop(n_used, n_blk, tail_start, 0)
        for e in range(N_EXPERTS):
            @pl.when(seg_nonempty(e))
            def _():
                zero_block(pends_ref[e] - EBLK).wait()

        def tail_wait(b, carry):
            zero_block(b * EBLK).wait()
            return carry

        lax.fori_loop(n_used, n_blk, tail_wait, 0)

    def start(r8, carry):
        base = pl.multiple_of(r8 * ROW_GROUP, ROW_GROUP)
        rows = h_ref.at[pl.ds(base, ROW_GROUP), :]
        for j in range(ROW_GROUP):
            for d_ref in (d1_ref, d2_ref):
                pltpu.make_async_copy(rows.at[pl.ds(j, 1), :],
                                      xs_ref.at[pl.ds(d_ref[0, 0, base + j], 1), :], sem).start()
        return carry

    lax.fori_loop(0, TM // ROW_GROUP, start, 0)
    for _ in range(2):
        pltpu.make_async_copy(h_ref, xs_ref.at[pl.ds(0, TM), :], sem).wait()


def _dispatch(h2, d1, d2, pends, n_slots):
    ttot = h2.shape[0]
    nt = ttot // TM
    smem = pl.BlockSpec((1, 1, TM), lambda i, pe: (i, 0, 0), memory_space=pltpu.SMEM)
    grid_spec = pltpu.PrefetchScalarGridSpec(
        num_scalar_prefetch=1,
        grid=(nt,),
        in_specs=[smem, smem, pl.BlockSpec((TM, D), lambda i, pe: (i, 0))],
        out_specs=pl.BlockSpec(memory_space=pl.ANY),
        scratch_shapes=[pltpu.VMEM((EBLK, D), jnp.float32),
                        pltpu.SemaphoreType.DMA(()), pltpu.SemaphoreType.DMA(())],
    )
    return pl.pallas_call(
        functools.partial(_dispatch_kernel, n_blk=n_slots // EBLK),
        grid_spec=grid_spec,
        out_shape=jax.ShapeDtypeStruct((n_slots, D), jnp.float32),
        compiler_params=_cparams(("arbitrary",)),
        name="dispatch",
    )(pends, d1, d2, h2)


def _expert_kernel(be_ref, nu_ref, x_ref, wg_ref, wu_ref, wd_ref, y_ref, wgu_b, wd_b):
    b = pl.program_id(0)
    new_expert = jnp.logical_or(b == 0, be_ref[b] != be_ref[jnp.maximum(b - 1, 0)])

    @pl.when(jnp.logical_and(b < nu_ref[0], new_expert))
    def _():
        wgu_b[:, :D_EXPERT] = wg_ref[0].astype(jnp.bfloat16)
        wgu_b[:, D_EXPERT:] = wu_ref[0].astype(jnp.bfloat16)
        wd_b[...] = wd_ref[0].astype(jnp.bfloat16)

    @pl.when(b < nu_ref[0])
    def _():
        x = x_ref[...].astype(jnp.bfloat16)
        gu = jnp.dot(x, wgu_b[...], preferred_element_type=jnp.float32)
        gt = gu[:, :D_EXPERT]
        hm = (gt * jax.nn.sigmoid(gt) * gu[:, D_EXPERT:]).astype(jnp.bfloat16)
        y_ref[...] = jnp.dot(hm, wd_b[...], preferred_element_type=jnp.float32)

    @pl.when(b >= nu_ref[0])
    def _():
        y_ref[...] = jnp.zeros(y_ref.shape, jnp.float32)


def _experts(xs, blk_e, n_used, wg, wu, wd, l):
    n_slots = xs.shape[0]
    nblk = n_slots // EBLK

    def row_map(b, be, nu):
        return (jnp.minimum(b, nu[0] - 1), 0)

    def w_map(b, be, nu):
        return (l, be[b], 0, 0)

    grid_spec = pltpu.PrefetchScalarGridSpec(
        num_scalar_prefetch=2,
        grid=(nblk,),
        in_specs=[
            pl.BlockSpec((EBLK, D), row_map),
            pl.BlockSpec((None, 1, D, D_EXPERT), w_map),
            pl.BlockSpec((None, 1, D, D_EXPERT), w_map),
            pl.BlockSpec((None, 1, D_EXPERT, D), w_map),
        ],
        out_specs=pl.BlockSpec((EBLK, D), lambda b, be, nu: (b, 0)),
        scratch_shapes=[pltpu.VMEM((D, 2 * D_EXPERT), jnp.bfloat16),
                        pltpu.VMEM((D_EXPERT, D), jnp.bfloat16)],
    )
    return pl.pallas_call(
        _expert_kernel,
        grid_spec=grid_spec,
        out_shape=jax.ShapeDtypeStruct((n_slots, D), jnp.float32),
        compiler_params=_cparams(("arbitrary",)),
        name="experts",
    )(blk_e, n_used, xs, wg, wu, wd)


def _combine_kernel(d1_ref, d2_ref, d1n_ref, d2n_ref, x1_ref, route_ref, mod_ref, ln2g_ref,
                    ln2b_ref, ys_ref, o_ref, y_buf, sems, *, alpha):
    i = pl.program_id(0)
    nt = pl.num_programs(0)
    slot = i % 2

    def gather(da_ref, db_ref, to_slot):
        def start(r8, carry):
            base = pl.multiple_of(r8 * ROW_GROUP, ROW_GROUP)
            for k, d_ref in enumerate((da_ref, db_ref)):
                rows = y_buf.at[to_slot, k, pl.ds(base, ROW_GROUP), :]
                for j in range(ROW_GROUP):
                    pltpu.make_async_copy(ys_ref.at[pl.ds(d_ref[0, 0, base + j], 1), :],
                                          rows.at[pl.ds(j, 1), :], sems.at[to_slot]).start()
            return carry

        lax.fori_loop(0, TM // ROW_GROUP, start, 0)

    @pl.when(i == 0)
    def _():
        gather(d1_ref, d2_ref, 0)

    @pl.when(i < nt - 1)
    def _():
        gather(d1n_ref, d2n_ref, 1 - slot)

    for k in range(2):
        pltpu.make_async_copy(ys_ref.at[pl.ds(0, TM), :], y_buf.at[slot, k], sems.at[slot]).wait()

    route = route_ref[...]
    g1 = route[:, 4:5]
    g2 = route[:, 5:6]
    o = g1 * y_buf[slot, 0] + g2 * y_buf[slot, 1]
    z = alpha * x1_ref[...] + _mod_rows(mod_ref, 5) * o
    o_ref[...] = _ln_plain(z) * ln2g_ref[...] + ln2b_ref[...]


def _combine(x1, route, mod, ln2g, ln2b, ys, d1, d2, alpha, l, latents_only):
    ttot = x1.shape[0]
    nt = ttot // TM
    smem = pl.BlockSpec((1, 1, TM), lambda i: (i, 0, 0), memory_space=pltpu.SMEM)
    smem_next = pl.BlockSpec((1, 1, TM), lambda i: (jnp.minimum(i + 1, nt - 1), 0, 0),
                             memory_space=pltpu.SMEM)
    kern = functools.partial(_combine_kernel, alpha=alpha)
    if latents_only:
        out_spec = pl.BlockSpec((TM, D), lambda i: (jnp.maximum(i - 1, 0), 0))
        out_rows = ttot - TM
    else:
        out_spec = pl.BlockSpec((TM, D), lambda i: (i, 0))
        out_rows = ttot
    return pl.pallas_call(
        kern,
        grid=(nt,),
        in_specs=[smem, smem, smem_next, smem_next,
                  pl.BlockSpec((TM, D), lambda i: (i, 0)),
                  pl.BlockSpec((TM, LANES), lambda i: (i, 0)),
                  _layer((8, N_MOD * D), l), _layer((1, D), l), _layer((1, D), l),
                  pl.BlockSpec(memory_space=pl.ANY)],
        out_specs=out_spec,
        out_shape=jax.ShapeDtypeStruct((out_rows, D), jnp.float32),
        scratch_shapes=[pltpu.VMEM((2, 2, TM, D), jnp.float32),
                        pltpu.SemaphoreType.DMA((2,))],
        compiler_params=_cparams(("arbitrary",)),
        name="combine",
    )(d1, d2, d1, d2, x1, route, mod, ln2g, ln2b, ys)


def _rope_tables_t(n_lat, n_ctx):
    rows = n_lat // GRID_W
    row = jnp.broadcast_to(jnp.arange(rows)[:, None], (rows, GRID_W)).reshape(-1)
    col = jnp.broadcast_to(jnp.arange(GRID_W)[None, :], (rows, GRID_W)).reshape(-1)
    n_freq = HD // 4
    inv = ROPE_THETA ** (-jnp.arange(n_freq, dtype=jnp.float32) / n_freq)
    pos = jnp.stack([row, col], axis=-1).astype(jnp.float32)
    ang = (pos[:, :, None] * inv).reshape(n_lat, 2 * n_freq)
    cos = jnp.concatenate([jnp.ones((n_ctx, 2 * n_freq), jnp.float32), jnp.cos(ang)], axis=0)
    sin = jnp.concatenate([jnp.zeros((n_ctx, 2 * n_freq), jnp.float32), jnp.sin(ang)], axis=0)
    return cos.T, sin.T


def _row(v):
    return v.reshape(1, -1)


def kernel(x, c, ctx, c_ctx, w_mod, b_mod, w_in, q_gain, k_gain, w_pool, pool_scale, w_dw, b_dw,
           cv_ln_g, cv_ln_b, w_cv_pw, w_out, ln1_g, ln1_b, ln2_g, ln2_b, w_rg, b_rg, w_re, b_re,
           w_e_gate, w_e_up, w_e_down):
    depth = w_mod.shape[0]
    n_lat = x.shape[1]
    n_ctx = ctx.shape[1]
    assert x.shape[0] == 1 and n_ctx == TM and n_lat % TM == 0 and n_lat % GRID_W == 0
    ttot = n_ctx + n_lat
    alpha = float((2 * depth) ** 0.25)
    n_blk = -(-(2 * ttot) // EBLK) + N_EXPERTS
    n_slots = n_blk * EBLK

    xa = jnp.concatenate([ctx[0], x[0]], axis=0)
    cc = jnp.concatenate([c, c_ctx[None], jnp.zeros((6, D), jnp.float32)], axis=0)
    mod = _modulation(cc, w_mod, b_mod)
    cos_t, sin_t = _rope_tables_t(n_lat, n_ctx)

    n_qkv = ATTN_W + 2 * KV_W
    bf = jnp.bfloat16
    w_t = jnp.swapaxes(w_in[:, :, :n_qkv], 1, 2).astype(bf)
    w_n = w_in[:, :, n_qkv:].astype(bf)
    qg = jnp.broadcast_to(q_gain[:, :, None], (depth, HD, TM))
    kg = jnp.broadcast_to(k_gain[:, :, None], (depth, HD, TM))
    n_pg = len(POOL_WINDOWS)
    wpool = (jnp.eye(n_pg, dtype=jnp.float32)[None, :, None, :, None] * w_pool[:, :, :, None, :]
             ).reshape(depth, POOL_W, POOL_W).astype(bf)
    wdw = jnp.pad(w_dw, ((0, 0), (0, 32 - CONV_K), (0, 0)))
    n_r = N_GROUPS + N_EXPERTS
    wr = jnp.concatenate([w_rg, w_re, jnp.zeros((depth, D, LANES - n_r), jnp.float32)], axis=2)
    br = jnp.concatenate([b_rg, b_re, jnp.zeros((depth, LANES - n_r), jnp.float32)], axis=1)
    wr_hi = wr.astype(bf)
    wr = jnp.concatenate([wr_hi, (wr - wr_hi.astype(jnp.float32)).astype(bf)], axis=2)
    wpw = w_cv_pw.astype(bf)
    wout = w_out.astype(bf)

    def rows(v):
        return v[:, None, :]

    e_ids = jnp.arange(N_EXPERTS, dtype=jnp.int32)
    blk_start = jnp.arange(n_blk, dtype=jnp.int32) * EBLK
    for l in range(depth):
        q_t, k2, v_t, kn2, u, ag = _inproj(xa, mod, w_t, w_n, qg, kg, cos_t, sin_t, l)
        kmax = jnp.sqrt(jnp.max(kn2[:, :, 0, 0], axis=0))
        attn = _attention(q_t, k2, v_t, kmax)
        x1, h2, route, cnt = _mixer(
            xa, attn, u, ag, mod, wpool, rows(pool_scale), wdw, rows(b_dw), rows(cv_ln_g),
            rows(cv_ln_b), wpw, wout, rows(ln1_g), rows(ln1_b), wr, rows(br), alpha, l)

        counts = cnt[0, :N_EXPERTS].astype(jnp.int32)
        padded = (counts + EBLK - 1) // EBLK * EBLK
        pends = jnp.cumsum(padded)
        pstarts = pends - padded
        ri = route[:, :4].astype(jnp.int32)

        def dest(e, r):
            start = jnp.sum(jnp.where(e[:, None] == e_ids[None, :], pstarts[None, :], 0), axis=1)
            return (start + r).reshape(ttot // TM, 1, TM)

        d1 = dest(ri[:, 0], ri[:, 2])
        d2 = dest(ri[:, 1], ri[:, 3])
        n_used = (pends[-1] // EBLK).astype(jnp.int32).reshape(1)
        first_row = jnp.minimum(blk_start, pends[-1] - 1)
        blk_e = jnp.sum((pends[None, :] <= first_row[:, None]).astype(jnp.int32), axis=1)
        blk_e = jnp.minimum(blk_e, N_EXPERTS - 1)

        xs = _dispatch(h2, d1, d2, pends, n_slots)
        ys = _experts(xs, blk_e, n_used, w_e_gate, w_e_up, w_e_down, l)
        xa = _combine(x1, route, mod, rows(ln2_g), rows(ln2_b), ys, d1, d2, alpha, l,
                      latents_only=(l == depth - 1))
    return xa[None]
```

```python
import functools

import jax
import jax.numpy as jnp
from jax import lax
from jax.experimental import pallas as pl
from jax.experimental.pallas import tpu as pltpu

D = 1024
GRID_W = 64
HD = 64
NQ = 8
NKV = 2
QPK = NQ // NKV
ATTN_W = NQ * HD
KV_W = NKV * HD
POOL_W = 256
POOL_WINDOWS = (2, 4, 8, 16)
POOL_GW = POOL_W // len(POOL_WINDOWS)
CONV_W = 256
CONV_K = 31
N_GROUPS = 4
EPG = 8
N_EXPERTS = N_GROUPS * EPG
D_EXPERT = 512
N_MOD = 6
ROPE_THETA = 10000.0
EPS = 1e-6

TM = 256
HALO = 16
EBLK = 256
LANES = 128
VROWS = HD + 16
LOG2E = 1.4426950408889634
BOUND_MARGIN = 1.0 + 2.0 ** -5
VMEM_LIMIT = 56 * 1024 * 1024
NEG_BIG = -1e30

_HI = lax.Precision.HIGHEST


def _cparams(sem):
    return pltpu.CompilerParams(dimension_semantics=sem, vmem_limit_bytes=VMEM_LIMIT)


def _full(shape):
    n = len(shape)
    return pl.BlockSpec(shape, lambda *a, n=n: (0,) * n)


def _layer(shape, l):
    n = len(shape)
    return pl.BlockSpec((None,) + tuple(shape), lambda *a, n=n: (l,) + (0,) * n)


def _mod_kernel(cc_ref, w_ref, b_ref, o_ref):
    cc = cc_ref[...]
    a = cc * jax.nn.sigmoid(cc)
    o_ref[0] = jnp.dot(a, w_ref[0], precision=_HI, preferred_element_type=jnp.float32) + b_ref[0]


def _modulation(cc, w_mod, b_mod):
    depth = w_mod.shape[0]
    nw = w_mod.shape[2]
    bn = 1024
    return pl.pallas_call(
        _mod_kernel,
        grid=(depth, nw // bn),
        in_specs=[
            pl.BlockSpec((8, D), lambda l, j: (0, 0)),
            pl.BlockSpec((1, D, bn), lambda l, j: (l, 0, j)),
            pl.BlockSpec((1, 1, bn), lambda l, j: (l, 0, j)),
        ],
        out_specs=pl.BlockSpec((1, 8, bn), lambda l, j: (l, 0, j)),
        out_shape=jax.ShapeDtypeStruct((depth, 8, nw), jnp.float32),
        compiler_params=_cparams(("arbitrary", "arbitrary")),
        name="modulation",
    )(cc, w_mod, b_mod.reshape(depth, 1, nw))


def _ln_plain(x):
    mu = jnp.mean(x, axis=-1, keepdims=True)
    xc = x - mu
    var = jnp.mean(xc * xc, axis=-1, keepdims=True)
    return xc * lax.rsqrt(var + EPS)


def _mod_rows(mod_ref, idx):
    row = jnp.where(pl.program_id(0) == 0, 1, 0)
    return mod_ref[pl.ds(row, 1), idx * D:(idx + 1) * D]


def _rms_rope_t(t, gain, cos, sin):
    ms = jnp.mean(t * t, axis=0, keepdims=True)
    t = t * lax.rsqrt(ms + EPS) * gain
    outs = []
    for a in range(2):
        t1 = t[32 * a:32 * a + 16]
        t2 = t[32 * a + 16:32 * a + 32]
        c = cos[16 * a:16 * a + 16]
        s = sin[16 * a:16 * a + 16]
        outs.append(t1 * c - t2 * s)
        outs.append(t2 * c + t1 * s)
    return jnp.concatenate(outs, axis=0)


def _inproj_kernel(x_ref, mod_ref, wt_ref, wn_ref, qg_ref, kg_ref, cos_ref, sin_ref,
                   q_ref, k_ref, v_ref, kmax_ref, u_ref, ag_ref):
    x = x_ref[...]
    h = _ln_plain(x) * (1.0 + _mod_rows(mod_ref, 1)) + _mod_rows(mod_ref, 0)
    hb = h.astype(jnp.bfloat16)
    qkv_t = lax.dot_general(wt_ref[...], hb, (((1,), (1,)), ((), ())),
                            preferred_element_type=jnp.float32)
    cos = cos_ref[...]
    sin = sin_ref[...]
    qg = qg_ref[...]
    kg = kg_ref[...]
    ones_row = jnp.where(lax.broadcasted_iota(jnp.int32, (HD, TM), 0) == 0, 1.0, 0.0)
    ones_pad = jnp.where(lax.broadcasted_iota(jnp.int32, (VROWS - HD, TM), 0) == 0, 1.0, 0.0)
    for hq in range(NQ):
        t = _rms_rope_t(qkv_t[hq * HD:(hq + 1) * HD], qg, cos, sin) * (LOG2E * HD ** -0.5)
        nq = jnp.sqrt(jnp.sum(t * t, axis=0, keepdims=True))
        q_ref[hq // QPK, hq % QPK] = jnp.concatenate(
            [t, ones_row * (-BOUND_MARGIN * nq)], axis=0).astype(jnp.bfloat16)
    for g in range(NKV):
        kt = _rms_rope_t(qkv_t[ATTN_W + g * HD:ATTN_W + (g + 1) * HD], kg, cos, sin)
        kn = jnp.max(jnp.sum(kt * kt, axis=0, keepdims=True), axis=1, keepdims=True)
        kmax_ref[0, g] = jnp.broadcast_to(kn, (8, LANES))
        k_ref[g] = jnp.concatenate([kt, ones_row], axis=0).T.astype(jnp.bfloat16)
        vt = qkv_t[ATTN_W + KV_W + g * HD:ATTN_W + KV_W + (g + 1) * HD]
        v_ref[g] = jnp.concatenate([vt, ones_pad], axis=0).astype(jnp.bfloat16)
    uag = jnp.dot(hb, wn_ref[...], preferred_element_type=jnp.float32)
    u_ref[...] = uag[:, :POOL_W]
    a = uag[:, POOL_W:POOL_W + CONV_W]
    gt = uag[:, POOL_W + CONV_W:]
    ag_ref[...] = a * jax.nn.sigmoid(gt)


def _inproj(xa, mod, w_t, w_n, qg, kg, cos_t, sin_t, l):
    ttot = xa.shape[0]
    nt = ttot // TM
    return pl.pallas_call(
        _inproj_kernel,
        grid=(nt,),
        in_specs=[
            pl.BlockSpec((TM, D), lambda i: (i, 0)),
            _layer((8, N_MOD * D), l),
            _layer((ATTN_W + 2 * KV_W, D), l),
            _layer((D, POOL_W + 2 * CONV_W), l),
            _layer((HD, TM), l),
            _layer((HD, TM), l),
            pl.BlockSpec((32, TM), lambda i: (0, i)),
            pl.BlockSpec((32, TM), lambda i: (0, i)),
        ],
        out_specs=[
            pl.BlockSpec((NKV, QPK, 2 * HD, TM), lambda i: (0, 0, 0, i)),
            pl.BlockSpec((NKV, TM, 2 * HD), lambda i: (0, i, 0)),
            pl.BlockSpec((NKV, VROWS, TM), lambda i: (0, 0, i)),
            pl.BlockSpec((1, NKV, 8, LANES), lambda i: (i, 0, 0, 0)),
            pl.BlockSpec((TM, POOL_W), lambda i: (i, 0)),
            pl.BlockSpec((TM, CONV_W), lambda i: (i, 0)),
        ],
        out_shape=[
            jax.ShapeDtypeStruct((NKV, QPK, 2 * HD, ttot), jnp.bfloat16),
            jax.ShapeDtypeStruct((NKV, ttot, 2 * HD), jnp.bfloat16),
            jax.ShapeDtypeStruct((NKV, VROWS, ttot), jnp.bfloat16),
            jax.ShapeDtypeStruct((nt, NKV, 8, LANES), jnp.float32),
            jax.ShapeDtypeStruct((ttot, POOL_W), jnp.float32),
            jax.ShapeDtypeStruct((ttot, CONV_W), jnp.float32),
        ],
        compiler_params=_cparams(("arbitrary",)),
        name="inproj",
    )(xa, mod, w_t, w_n, qg, kg, cos_t, sin_t)


TK = 256
GCH = 5
L_MIN = 2.0 ** -60


def _attn_kernel(kmax_ref, q_ref, k_ref, v_ref, o_ref, qs_ref, s_ref, acc_ref, m_ref, *,
                 n_chunks):
    g = pl.program_id(0)
    i = pl.program_id(1)
    nch = jnp.where(i == 0, 1, n_chunks)
    row = lax.broadcasted_iota(jnp.int32, (2 * HD, TM), 0)
    kmax = kmax_ref[g]

    for hh in range(QPK):
        qf = q_ref[0, hh].astype(jnp.float32)
        qs_ref[hh] = jnp.where(row == HD, qf * kmax, qf).astype(jnp.bfloat16)
    acc_ref[...] = jnp.zeros(acc_ref.shape, jnp.float32)

    def step(nxt, cur):
        for kk in range(max(grp[1] for grp in (nxt, cur) if grp is not None)):
            do_nxt = nxt is not None and kk < nxt[1]
            do_cur = cur is not None and kk < cur[1]
            rows = slice(kk * TK, (kk + 1) * TK)
            if do_nxt:
                kc = k_ref[0, pl.ds(pl.multiple_of((nxt[0] + kk) * TK, TK), TK), :]
            if do_cur:
                vc = v_ref[0, :, pl.ds(pl.multiple_of((cur[0] + kk) * TK, TK), TK)]
            for hh in range(QPK):
                if do_nxt:
                    s_ref[nxt[2], hh, rows, :] = jnp.dot(
                        kc, qs_ref[hh], preferred_element_type=jnp.float32)
                if do_cur:
                    p = jnp.exp2(s_ref[cur[2], hh, rows, :]).astype(jnp.bfloat16)
                    acc_ref[hh] += jnp.dot(vc, p, preferred_element_type=jnp.float32)

    @pl.when(i == 0)
    def _():
        kc = k_ref[0, 0:TM, :]
        vc = v_ref[0, :, 0:TM]
        for hh in range(QPK):
            p = jnp.exp2(jnp.dot(kc, qs_ref[hh], preferred_element_type=jnp.float32))
            acc_ref[hh] = jnp.dot(vc, p.astype(jnp.bfloat16), preferred_element_type=jnp.float32)

    @pl.when(i > 0)
    def _():
        n_mid = (n_chunks - 2) // GCH
        tail = n_chunks - 1 - GCH * n_mid
        sizes = [1] + [GCH] * n_mid + ([tail - 1, 1] if tail >= 2 else [tail])
        firsts = [sum(sizes[:n]) for n in range(len(sizes))]

        def group(n):
            return (firsts[n], sizes[n], n % 2)

        step(group(0), None)
        step(group(1), group(0))
        n_loop = max(n_mid - 1, 0) // 2

        def pair(j, carry):
            c0 = 1 + 2 * GCH * j
            step((c0 + GCH, GCH, 0), (c0, GCH, 1))
            step((c0 + 2 * GCH, GCH, 1), (c0 + GCH, GCH, 0))
            return carry

        lax.fori_loop(0, n_loop, pair, 0)
        for n in range(1 + 2 * n_loop, len(sizes) - 1):
            step(group(n + 1), group(n))
        step(None, group(len(sizes) - 1))

    l_min = jnp.min(acc_ref[:, HD:HD + 1, :])

    @pl.when(l_min < L_MIN)
    def _():
        for hh in range(QPK):
            qs_ref[hh] = jnp.where(row == HD, 0.0, q_ref[0, hh].astype(jnp.float32)
                                   ).astype(jnp.bfloat16)
        acc_ref[...] = jnp.zeros(acc_ref.shape, jnp.float32)
        m_ref[...] = jnp.full(m_ref.shape, NEG_BIG, jnp.float32)

        def body2(c, carry):
            start = pl.multiple_of(c * TK, TK)
            kc = k_ref[0, pl.ds(start, TK), :]
            vc = v_ref[0, :, pl.ds(start, TK)]
            for hh in range(QPK):
                s = jnp.dot(kc, qs_ref[hh], preferred_element_type=jnp.float32)
                m_old = m_ref[hh]
                m_new = jnp.maximum(m_old, jnp.max(s, axis=0, keepdims=True))
                p = jnp.exp2(s - m_new).astype(jnp.bfloat16)
                acc_ref[hh] = jnp.exp2(m_old - m_new) * acc_ref[hh] + jnp.dot(
                    vc, p, preferred_element_type=jnp.float32)
                m_ref[hh] = m_new
            return carry

        lax.fori_loop(0, nch, body2, 0)

    outs = [acc_ref[hh, 0:HD, :] / acc_ref[hh, HD:HD + 1, :] for hh in range(QPK)]
    o_ref[...] = jnp.concatenate(outs, axis=0).T


def _attention(q_t, k2, v_t, kmax):
    ttot = k2.shape[1]
    nt = ttot // TM
    assert ttot % TK == 0 and ttot // TK >= 3
    kern = functools.partial(_attn_kernel, n_chunks=ttot // TK)
    grid_spec = pltpu.PrefetchScalarGridSpec(
        num_scalar_prefetch=1,
        grid=(NKV, nt),
        in_specs=[
            pl.BlockSpec((1, QPK, 2 * HD, TM), lambda g, i, km: (g, 0, 0, i)),
            pl.BlockSpec((1, ttot, 2 * HD), lambda g, i, km: (g, 0, 0)),
            pl.BlockSpec((1, VROWS, ttot), lambda g, i, km: (g, 0, 0)),
        ],
        out_specs=pl.BlockSpec((TM, QPK * HD), lambda g, i, km: (i, g)),
        scratch_shapes=[
            pltpu.VMEM((QPK, 2 * HD, TM), jnp.bfloat16),
            pltpu.VMEM((2, QPK, GCH * TK, TM), jnp.float32),
            pltpu.VMEM((QPK, VROWS, TM), jnp.float32),
            pltpu.VMEM((QPK, 1, TM), jnp.float32),
        ],
    )
    return pl.pallas_call(
        kern,
        grid_spec=grid_spec,
        out_shape=jax.ShapeDtypeStruct((ttot, ATTN_W), jnp.float32),
        compiler_params=_cparams(("arbitrary", "arbitrary")),
        name="attention",
    )(kmax, q_t, k2, v_t)


def _fill_ext(ext_ref, cur_ref, prev_ref, next_ref, left_ok, right_ok):
    zero = jnp.zeros((HALO, cur_ref.shape[1]), jnp.float32)
    ext_ref[0:HALO, :] = jnp.where(left_ok, prev_ref[...], zero)
    ext_ref[HALO:HALO + TM, :] = cur_ref[...]
    ext_ref[HALO + TM:HALO + TM + HALO, :] = jnp.where(right_ok, next_ref[...], zero)


def _mix_kernel(x_ref, attn_ref, u_ref, up_ref, un_ref, ag_ref, agp_ref, agn_ref, mod_ref,
                wpool_ref, pscale_ref, wdw_ref, bdw_ref, cvg_ref, cvb_ref, wpw_ref, wout_ref,
                ln1g_ref, ln1b_ref, wr_ref, br_ref,
                x1_ref, h2_ref, route_ref, cnt_ref,
                uext_ref, agext_ref, base_ref, *, alpha, n_lat):
    i = pl.program_id(0)
    nt = pl.num_programs(0)
    is_ctx = i == 0
    left_ok = i >= 2
    right_ok = jnp.logical_and(i >= 1, i < nt - 1)

    @pl.when(i == 0)
    def _():
        base_ref[...] = jnp.zeros(base_ref.shape, jnp.float32)

    _fill_ext(uext_ref, u_ref, up_ref, un_ref, left_ok, right_ok)
    _fill_ext(agext_ref, ag_ref, agp_ref, agn_ref, left_ok, right_ok)

    def ush(off, rows=TM):
        return uext_ref[HALO + off:HALO + off + rows, :]

    a2 = ush(-8, TM + 15) + ush(-7, TM + 15)
    a4 = a2[0:TM + 13] + a2[2:TM + 15]
    a8 = a4[0:TM + 9] + a4[4:TM + 13]
    a16 = a8[0:TM] + a8[8:TM + 8]
    sums = (a2[7:7 + TM], a4[6:6 + TM], a8[4:4 + TM], a16)
    pos = lax.broadcasted_iota(jnp.int32, (TM, 1), 0) + jnp.where(is_ctx, 0, (i - 1) * TM)
    seq_n = jnp.where(is_ctx, TM, n_lat)
    u_cur = u_ref[...]
    lane = lax.broadcasted_iota(jnp.int32, (TM, POOL_W), 1)
    pooled = jnp.zeros((TM, POOL_W), jnp.float32)
    for gi, w in enumerate(POOL_WINDOWS):
        lo = jnp.maximum(pos - w // 2, 0)
        hi = jnp.minimum(pos + (w - w // 2), seq_n)
        inv = 1.0 / (hi - lo).astype(jnp.float32)
        in_group = jnp.logical_and(lane >= gi * POOL_GW, lane < (gi + 1) * POOL_GW)
        pooled = jnp.where(in_group, sums[gi] * inv - u_cur, pooled)
    y_pool = jnp.dot(pooled.astype(jnp.bfloat16), wpool_ref[...],
                     preferred_element_type=jnp.float32) * pscale_ref[...]

    conv = jnp.zeros((TM, CONV_W), jnp.float32) + bdw_ref[...]
    for k in range(CONV_K):
        off = HALO + k - CONV_K // 2
        conv = conv + agext_ref[off:off + TM, :] * wdw_ref[k:k + 1, :]
    cn = _ln_plain(conv) * cvg_ref[...] + cvb_ref[...]
    cn = cn * jax.nn.sigmoid(cn)
    y_conv = jnp.dot(cn.astype(jnp.bfloat16), wpw_ref[...], preferred_element_type=jnp.float32)

    y = jnp.dot(attn_ref[...].astype(jnp.bfloat16), wout_ref[0:ATTN_W, :],
                preferred_element_type=jnp.float32)
    y = y + jnp.dot(y_pool.astype(jnp.bfloat16), wout_ref[ATTN_W:ATTN_W + POOL_W, :],
                    preferred_element_type=jnp.float32)
    y = y + jnp.dot(y_conv.astype(jnp.bfloat16), wout_ref[ATTN_W + POOL_W:, :],
                    preferred_element_type=jnp.float32)
    x1 = _ln_plain(alpha * x_ref[...] + _mod_rows(mod_ref, 2) * y) * ln1g_ref[...] + ln1b_ref[...]
    x1_ref[...] = x1
    h2 = _ln_plain(x1) * (1.0 + _mod_rows(mod_ref, 4)) + _mod_rows(mod_ref, 3)
    h2_ref[...] = h2

    h_hi = h2.astype(jnp.bfloat16)
    h_lo = (h2 - h_hi.astype(jnp.float32)).astype(jnp.bfloat16)
    hw = jnp.dot(h_hi, wr_ref[...], preferred_element_type=jnp.float32)
    logits = (hw[:, :LANES] + hw[:, LANES:] + br_ref[...]
              + jnp.dot(h_lo, wr_ref[:, :LANES], preferred_element_type=jnp.float32))
    ln = lax.broadcasted_iota(jnp.int32, (TM, LANES), 1)
    neg = jnp.float32(-jnp.inf)
    is_g = ln < N_GROUPS
    lg = jnp.where(is_g, logits, neg)
    mg = jnp.max(lg, axis=-1, keepdims=True)
    gsel = jnp.min(jnp.where(lg == mg, ln, LANES), axis=-1, keepdims=True)
    p_group = 1.0 / jnp.sum(jnp.exp(lg - mg), axis=-1, keepdims=True)
    e_lane = ln - N_GROUPS
    in_sel = jnp.logical_and(e_lane >= gsel * EPG, e_lane < (gsel + 1) * EPG)
    le = jnp.where(in_sel, logits, neg)
    v1 = jnp.max(le, axis=-1, keepdims=True)
    e1 = jnp.min(jnp.where(le == v1, e_lane, LANES), axis=-1, keepdims=True)
    le2 = jnp.where(e_lane == e1, neg, le)
    v2 = jnp.max(le2, axis=-1, keepdims=True)
    e2 = jnp.min(jnp.where(le2 == v2, e_lane, LANES), axis=-1, keepdims=True)
    ex = jnp.exp(v2 - v1)
    gate1 = p_group / (1.0 + ex)
    gate2 = p_group * ex / (1.0 + ex)

    oh = jnp.logical_or(ln == e1, ln == e2)
    ohb = jnp.where(oh, 1.0, 0.0).astype(jnp.bfloat16)
    r_i = lax.broadcasted_iota(jnp.int32, (TM, TM), 0)
    c_i = lax.broadcasted_iota(jnp.int32, (TM, TM), 1)
    tri = jnp.where(c_i < r_i, 1.0, 0.0).astype(jnp.bfloat16)
    before = jnp.dot(tri, ohb, preferred_element_type=jnp.float32) + base_ref[...]
    rank1 = jnp.sum(jnp.where(ln == e1, before, 0.0), axis=-1, keepdims=True)
    rank2 = jnp.sum(jnp.where(ln == e2, before, 0.0), axis=-1, keepdims=True)
    base_ref[...] = base_ref[...] + jnp.sum(ohb.astype(jnp.float32), axis=0, keepdims=True)
    cnt_ref[...] = jnp.broadcast_to(base_ref[...], cnt_ref.shape)

    route = jnp.zeros((TM, LANES), jnp.float32)
    for j, val in enumerate((e1.astype(jnp.float32), e2.astype(jnp.float32), rank1, rank2,
                             gate1, gate2)):
        route = jnp.where(ln == j, val, route)
    route_ref[...] = route


def _mixer(xa, attn, u, ag, mod, wpool, pscale, wdw, bdw, cvg, cvb, wpw, wout, ln1g, ln1b,
           wr, br, alpha, l):
    ttot = xa.shape[0]
    nt = ttot // TM
    hb = TM // HALO
    nhb = ttot // HALO

    def tile(w):
        return pl.BlockSpec((TM, w), lambda i: (i, 0))

    def prev(w):
        return pl.BlockSpec((HALO, w), lambda i: (jnp.maximum(i * hb - 1, 0), 0))

    def nxt(w):
        return pl.BlockSpec((HALO, w), lambda i: (jnp.minimum((i + 1) * hb, nhb - 1), 0))

    kern = functools.partial(_mix_kernel, alpha=alpha, n_lat=ttot - TM)
    return pl.pallas_call(
        kern,
        grid=(nt,),
        in_specs=[
            tile(D), tile(ATTN_W),
            tile(POOL_W), prev(POOL_W), nxt(POOL_W),
            tile(CONV_W), prev(CONV_W), nxt(CONV_W),
            _layer((8, N_MOD * D), l),
            _layer((POOL_W, POOL_W), l), _layer((1, POOL_W), l),
            _layer((32, CONV_W), l), _layer((1, CONV_W), l), _layer((1, CONV_W), l),
            _layer((1, CONV_W), l),
            _layer((CONV_W, CONV_W), l), _layer((D, D), l),
            _layer((1, D), l), _layer((1, D), l),
            _layer((D, 2 * LANES), l), _layer((1, LANES), l),
        ],
        out_specs=[tile(D), tile(D), tile(LANES), _full((8, LANES))],
        out_shape=[
            jax.ShapeDtypeStruct((ttot, D), jnp.float32),
            jax.ShapeDtypeStruct((ttot, D), jnp.float32),
            jax.ShapeDtypeStruct((ttot, LANES), jnp.float32),
            jax.ShapeDtypeStruct((8, LANES), jnp.float32),
        ],
        scratch_shapes=[
            pltpu.VMEM((TM + 2 * HALO, POOL_W), jnp.float32),
            pltpu.VMEM((TM + 2 * HALO, CONV_W), jnp.float32),
            pltpu.VMEM((1, LANES), jnp.float32),
        ],
        compiler_params=_cparams(("arbitrary",)),
        name="mixer",
    )(xa, attn, u, u, u, ag, ag, ag, mod, wpool, pscale, wdw, bdw, cvg, cvb, wpw, wout,
      ln1g, ln1b, wr, br)


ROW_GROUP = 8


def _dispatch_kernel(pends_ref, d1_ref, d2_ref, h_hbm, xs_ref, zero_buf, h_buf, lsems, ssems,
                     zsem, *, n_blk):
    i = pl.program_id(0)
    nt = pl.num_programs(0)
    slot = i % 3

    def load(t, to_slot):
        return pltpu.make_async_copy(h_hbm.at[pl.ds(pl.multiple_of(t * TM, TM), TM), :],
                                     h_buf.at[to_slot], lsems.at[to_slot])

    def wait_scatter(of_slot):
        for _ in range(2):
            pltpu.make_async_copy(h_buf.at[of_slot], xs_ref.at[pl.ds(0, TM), :],
                                  ssems.at[of_slot]).wait()

    @pl.when(i == 0)
    def _():
        load(0, 0).start()

        @pl.when(nt > 1)
        def _():
            load(1, 1).start()

        zero_buf[...] = jnp.zeros(zero_buf.shape, jnp.float32)

        def zero_block(first_row):
            return pltpu.make_async_copy(
                zero_buf, xs_ref.at[pl.ds(pl.multiple_of(first_row, EBLK), EBLK), :], zsem)

        def seg_nonempty(e):
            return pends_ref[e] > (pends_ref[e - 1] if e > 0 else 0)

        for e in range(N_EXPERTS):
            @pl.when(seg_nonempty(e))
            def _():
                zero_block(pends_ref[e] - EBLK).start()
        n_used = pends_ref[N_EXPERTS - 1] // EBLK

        def tail_start(b, carry):
            zero_block(b * EBLK).start()
            return carry

        lax.fori_loop(n_used, n_blk, tail_start, 0)
        for e in range(N_EXPERTS):
            @pl.when(seg_nonempty(e))
            def _():
                zero_block(pends_ref[e] - EBLK).wait()

        def tail_wait(b, carry):
            zero_block(b * EBLK).wait()
            return carry

        lax.fori_loop(n_used, n_blk, tail_wait, 0)

    load(i, slot).wait()

    def start(r8, carry):
        base = pl.multiple_of(r8 * ROW_GROUP, ROW_GROUP)
        rows = h_buf.at[slot, pl.ds(base, ROW_GROUP), :]
        for j in range(ROW_GROUP):
            for d_ref in (d1_ref, d2_ref):
                pltpu.make_async_copy(rows.at[pl.ds(j, 1), :],
                                      xs_ref.at[pl.ds(d_ref[0, 0, base + j], 1), :],
                                      ssems.at[slot]).start()
        return carry

    lax.fori_loop(0, TM // ROW_GROUP, start, 0)

    @pl.when(i >= 1)
    def _():
        wait_scatter((i + 2) % 3)

    @pl.when(i + 2 < nt)
    def _():
        load(i + 2, (i + 2) % 3).start()

    @pl.when(i == nt - 1)
    def _():
        wait_scatter(slot)


def _dispatch(h2, d1, d2, pends, n_slots):
    ttot = h2.shape[0]
    nt = ttot // TM
    smem = pl.BlockSpec((1, 1, TM), lambda i, pe: (i, 0, 0), memory_space=pltpu.SMEM)
    grid_spec = pltpu.PrefetchScalarGridSpec(
        num_scalar_prefetch=1,
        grid=(nt,),
        in_specs=[smem, smem, pl.BlockSpec(memory_space=pl.ANY)],
        out_specs=pl.BlockSpec(memory_space=pl.ANY),
        scratch_shapes=[pltpu.VMEM((EBLK, D), jnp.float32),
                        pltpu.VMEM((3, TM, D), jnp.float32),
                        pltpu.SemaphoreType.DMA((3,)), pltpu.SemaphoreType.DMA((3,)),
                        pltpu.SemaphoreType.DMA(())],
    )
    return pl.pallas_call(
        functools.partial(_dispatch_kernel, n_blk=n_slots // EBLK),
        grid_spec=grid_spec,
        out_shape=jax.ShapeDtypeStruct((n_slots, D), jnp.float32),
        compiler_params=_cparams(("arbitrary",)),
        name="dispatch",
    )(pends, d1, d2, h2)


def _expert_kernel(be_ref, ne_ref, nu_ref, x_ref, wg_hbm, wu_hbm, wd_hbm, y_ref,
                   wg_f, wu_f, wd_f, wgu_b, wd_b, seg_ref, sems, *, l):
    b = pl.program_id(0)
    e = be_ref[b]
    new_expert = jnp.logical_or(b == 0, e != be_ref[jnp.maximum(b - 1, 0)])

    def fetch(expert, slot):
        return [pltpu.make_async_copy(src.at[l, expert], dst.at[slot], sems.at[slot])
                for src, dst in ((wg_hbm, wg_f), (wu_hbm, wu_f), (wd_hbm, wd_f))]

    @pl.when(b == 0)
    def _():
        seg_ref[0] = 0
        for cp in fetch(e, 0):
            cp.start()

    @pl.when(jnp.logical_and(b < nu_ref[0], new_expert))
    def _():
        seg = seg_ref[0]
        slot = seg % 2
        for cp in fetch(e, slot):
            cp.wait()

        @pl.when(ne_ref[b] != e)
        def _():
            for cp in fetch(ne_ref[b], 1 - slot):
                cp.start()

        wgu_b[:, :D_EXPERT] = wg_f[slot].astype(jnp.bfloat16)
        wgu_b[:, D_EXPERT:] = wu_f[slot].astype(jnp.bfloat16)
        wd_b[...] = wd_f[slot].astype(jnp.bfloat16)
        seg_ref[0] = seg + 1

    @pl.when(b < nu_ref[0])
    def _():
        x = x_ref[...].astype(jnp.bfloat16)
        gu = jnp.dot(x, wgu_b[...], preferred_element_type=jnp.float32)
        gt = gu[:, :D_EXPERT]
        hm = (gt * jax.nn.sigmoid(gt) * gu[:, D_EXPERT:]).astype(jnp.bfloat16)
        y_ref[...] = jnp.dot(hm, wd_b[...], preferred_element_type=jnp.float32)

    @pl.when(b >= nu_ref[0])
    def _():
        y_ref[...] = jnp.zeros(y_ref.shape, jnp.float32)


def _experts(xs, blk_e, nxt_e, n_used, wg, wu, wd, l):
    n_slots = xs.shape[0]
    nblk = n_slots // EBLK

    def row_map(b, be, ne, nu):
        return (jnp.minimum(b, nu[0] - 1), 0)

    hbm = pl.BlockSpec(memory_space=pl.ANY)
    grid_spec = pltpu.PrefetchScalarGridSpec(
        num_scalar_prefetch=3,
        grid=(nblk,),
        in_specs=[pl.BlockSpec((EBLK, D), row_map), hbm, hbm, hbm],
        out_specs=pl.BlockSpec((EBLK, D), lambda b, be, ne, nu: (b, 0)),
        scratch_shapes=[pltpu.VMEM((2, D, D_EXPERT), jnp.float32),
                        pltpu.VMEM((2, D, D_EXPERT), jnp.float32),
                        pltpu.VMEM((2, D_EXPERT, D), jnp.float32),
                        pltpu.VMEM((D, 2 * D_EXPERT), jnp.bfloat16),
                        pltpu.VMEM((D_EXPERT, D), jnp.bfloat16),
                        pltpu.SMEM((1,), jnp.int32),
                        pltpu.SemaphoreType.DMA((2,))],
    )
    return pl.pallas_call(
        functools.partial(_expert_kernel, l=l),
        grid_spec=grid_spec,
        out_shape=jax.ShapeDtypeStruct((n_slots, D), jnp.float32),
        compiler_params=_cparams(("arbitrary",)),
        name="experts",
    )(blk_e, nxt_e, n_used, xs, wg, wu, wd)


def _combine_kernel(d1_ref, d2_ref, d1n_ref, d2n_ref, x1_ref, route_ref, mod_ref, ln2g_ref,
                    ln2b_ref, ys_ref, o_ref, y_buf, sems, *, alpha):
    i = pl.program_id(0)
    nt = pl.num_programs(0)
    slot = i % 2

    def gather(da_ref, db_ref, to_slot):
        def start(r8, carry):
            base = pl.multiple_of(r8 * ROW_GROUP, ROW_GROUP)
            for k, d_ref in enumerate((da_ref, db_ref)):
                rows = y_buf.at[to_slot, k, pl.ds(base, ROW_GROUP), :]
                for j in range(ROW_GROUP):
                    pltpu.make_async_copy(ys_ref.at[pl.ds(d_ref[0, 0, base + j], 1), :],
                                          rows.at[pl.ds(j, 1), :], sems.at[to_slot]).start()
            return carry

        lax.fori_loop(0, TM // ROW_GROUP, start, 0)

    @pl.when(i == 0)
    def _():
        gather(d1_ref, d2_ref, 0)

    @pl.when(i < nt - 1)
    def _():
        gather(d1n_ref, d2n_ref, 1 - slot)

    for k in range(2):
        pltpu.make_async_copy(ys_ref.at[pl.ds(0, TM), :], y_buf.at[slot, k], sems.at[slot]).wait()

    route = route_ref[...]
    g1 = route[:, 4:5]
    g2 = route[:, 5:6]
    o = g1 * y_buf[slot, 0] + g2 * y_buf[slot, 1]
    z = alpha * x1_ref[...] + _mod_rows(mod_ref, 5) * o
    o_ref[...] = _ln_plain(z) * ln2g_ref[...] + ln2b_ref[...]


def _combine(x1, route, mod, ln2g, ln2b, ys, d1, d2, alpha, l, latents_only):
    ttot = x1.shape[0]
    nt = ttot // TM
    smem = pl.BlockSpec((1, 1, TM), lambda i: (i, 0, 0), memory_space=pltpu.SMEM)
    smem_next = pl.BlockSpec((1, 1, TM), lambda i: (jnp.minimum(i + 1, nt - 1), 0, 0),
                             memory_space=pltpu.SMEM)
    kern = functools.partial(_combine_kernel, alpha=alpha)
    if latents_only:
        out_spec = pl.BlockSpec((TM, D), lambda i: (jnp.maximum(i - 1, 0), 0))
        out_rows = ttot - TM
    else:
        out_spec = pl.BlockSpec((TM, D), lambda i: (i, 0))
        out_rows = ttot
    return pl.pallas_call(
        kern,
        grid=(nt,),
        in_specs=[smem, smem, smem_next, smem_next,
                  pl.BlockSpec((TM, D), lambda i: (i, 0)),
                  pl.BlockSpec((TM, LANES), lambda i: (i, 0)),
                  _layer((8, N_MOD * D), l), _layer((1, D), l), _layer((1, D), l),
                  pl.BlockSpec(memory_space=pl.ANY)],
        out_specs=out_spec,
        out_shape=jax.ShapeDtypeStruct((out_rows, D), jnp.float32),
        scratch_shapes=[pltpu.VMEM((2, 2, TM, D), jnp.float32),
                        pltpu.SemaphoreType.DMA((2,))],
        compiler_params=_cparams(("arbitrary",)),
        name="combine",
    )(d1, d2, d1, d2, x1, route, mod, ln2g, ln2b, ys)


def _rope_tables_t(n_lat, n_ctx):
    rows = n_lat // GRID_W
    row = jnp.broadcast_to(jnp.arange(rows)[:, None], (rows, GRID_W)).reshape(-1)
    col = jnp.broadcast_to(jnp.arange(GRID_W)[None, :], (rows, GRID_W)).reshape(-1)
    n_freq = HD // 4
    inv = ROPE_THETA ** (-jnp.arange(n_freq, dtype=jnp.float32) / n_freq)
    pos = jnp.stack([row, col], axis=-1).astype(jnp.float32)
    ang = (pos[:, :, None] * inv).reshape(n_lat, 2 * n_freq)
    cos = jnp.concatenate([jnp.ones((n_ctx, 2 * n_freq), jnp.float32), jnp.cos(ang)], axis=0)
    sin = jnp.concatenate([jnp.zeros((n_ctx, 2 * n_freq), jnp.float32), jnp.sin(ang)], axis=0)
    return cos.T, sin.T


def _row(v):
    return v.reshape(1, -1)


def kernel(x, c, ctx, c_ctx, w_mod, b_mod, w_in, q_gain, k_gain, w_pool, pool_scale, w_dw, b_dw,
           cv_ln_g, cv_ln_b, w_cv_pw, w_out, ln1_g, ln1_b, ln2_g, ln2_b, w_rg, b_rg, w_re, b_re,
           w_e_gate, w_e_up, w_e_down):
    depth = w_mod.shape[0]
    n_lat = x.shape[1]
    n_ctx = ctx.shape[1]
    assert x.shape[0] == 1 and n_ctx == TM and n_lat % TM == 0 and n_lat % GRID_W == 0
    ttot = n_ctx + n_lat
    alpha = float((2 * depth) ** 0.25)
    n_blk = -(-(2 * ttot) // EBLK) + N_EXPERTS
    n_slots = n_blk * EBLK

    xa = jnp.concatenate([ctx[0], x[0]], axis=0)
    cc = jnp.concatenate([c, c_ctx[None], jnp.zeros((6, D), jnp.float32)], axis=0)
    mod = _modulation(cc, w_mod, b_mod)
    cos_t, sin_t = _rope_tables_t(n_lat, n_ctx)

    n_qkv = ATTN_W + 2 * KV_W
    bf = jnp.bfloat16
    w_t = jnp.swapaxes(w_in[:, :, :n_qkv], 1, 2).astype(bf)
    w_n = w_in[:, :, n_qkv:].astype(bf)
    qg = jnp.broadcast_to(q_gain[:, :, None], (depth, HD, TM))
    kg = jnp.broadcast_to(k_gain[:, :, None], (depth, HD, TM))
    n_pg = len(POOL_WINDOWS)
    wpool = (jnp.eye(n_pg, dtype=jnp.float32)[None, :, None, :, None] * w_pool[:, :, :, None, :]
             ).reshape(depth, POOL_W, POOL_W).astype(bf)
    wdw = jnp.pad(w_dw, ((0, 0), (0, 32 - CONV_K), (0, 0)))
    n_r = N_GROUPS + N_EXPERTS
    wr = jnp.concatenate([w_rg, w_re, jnp.zeros((depth, D, LANES - n_r), jnp.float32)], axis=2)
    br = jnp.concatenate([b_rg, b_re, jnp.zeros((depth, LANES - n_r), jnp.float32)], axis=1)
    wr_hi = wr.astype(bf)
    wr = jnp.concatenate([wr_hi, (wr - wr_hi.astype(jnp.float32)).astype(bf)], axis=2)
    wpw = w_cv_pw.astype(bf)
    wout = w_out.astype(bf)

    def rows(v):
        return v[:, None, :]

    e_ids = jnp.arange(N_EXPERTS, dtype=jnp.int32)
    blk_start = jnp.arange(n_blk, dtype=jnp.int32) * EBLK
    for l in range(depth):
        q_t, k2, v_t, kn2, u, ag = _inproj(xa, mod, w_t, w_n, qg, kg, cos_t, sin_t, l)
        kmax = jnp.sqrt(jnp.max(kn2[:, :, 0, 0], axis=0))
        attn = _attention(q_t, k2, v_t, kmax)
        x1, h2, route, cnt = _mixer(
            xa, attn, u, ag, mod, wpool, rows(pool_scale), wdw, rows(b_dw), rows(cv_ln_g),
            rows(cv_ln_b), wpw, wout, rows(ln1_g), rows(ln1_b), wr, rows(br), alpha, l)

        counts = cnt[0, :N_EXPERTS].astype(jnp.int32)
        padded = (counts + EBLK - 1) // EBLK * EBLK
        pends = jnp.cumsum(padded)
        pstarts = pends - padded
        ri = route[:, :4].astype(jnp.int32)

        def dest(e, r):
            start = jnp.sum(jnp.where(e[:, None] == e_ids[None, :], pstarts[None, :], 0), axis=1)
            return (start + r).reshape(ttot // TM, 1, TM)

        d1 = dest(ri[:, 0], ri[:, 2])
        d2 = dest(ri[:, 1], ri[:, 3])
        n_used = (pends[-1] // EBLK).astype(jnp.int32).reshape(1)
        first_row = jnp.minimum(blk_start, pends[-1] - 1)
        blk_e = jnp.sum((pends[None, :] <= first_row[:, None]).astype(jnp.int32), axis=1)
        blk_e = jnp.minimum(blk_e, N_EXPERTS - 1)
        seg_end = jnp.sum(jnp.where(blk_e[:, None] == e_ids[None, :], pends[None, :], 0), axis=1)
        nxt_row = jnp.minimum(seg_end, pends[-1] - 1)
        nxt_e = jnp.sum((pends[None, :] <= nxt_row[:, None]).astype(jnp.int32), axis=1)
        nxt_e = jnp.minimum(nxt_e, N_EXPERTS - 1)

        xs = _dispatch(h2, d1, d2, pends, n_slots)
        ys = _experts(xs, blk_e, nxt_e, n_used, w_e_gate, w_e_up, w_e_down, l)
        xa = _combine(x1, route, mod, rows(ln2_g), rows(ln2_b), ys, d1, d2, alpha, l,
                      latents_only=(l == depth - 1))
    return xa[None]
```

```python
import functools

import jax
import jax.numpy as jnp
from jax import lax
from jax.experimental import pallas as pl
from jax.experimental.pallas import tpu as pltpu

D = 1024
GRID_W = 64
HD = 64
NQ = 8
NKV = 2
QPK = NQ // NKV
ATTN_W = NQ * HD
KV_W = NKV * HD
POOL_W = 256
POOL_WINDOWS = (2, 4, 8, 16)
POOL_GW = POOL_W // len(POOL_WINDOWS)
CONV_W = 256
CONV_K = 31
N_GROUPS = 4
EPG = 8
N_EXPERTS = N_GROUPS * EPG
D_EXPERT = 512
N_MOD = 6
ROPE_THETA = 10000.0
EPS = 1e-6

TM = 256
HALO = 16
EBLK = 256
LANES = 128
VROWS = HD + 16
LOG2E = 1.4426950408889634
BOUND_MARGIN = 1.0 + 2.0 ** -5
VMEM_LIMIT = 56 * 1024 * 1024
NEG_BIG = -1e30

_HI = lax.Precision.HIGHEST


def _cparams(sem):
    return pltpu.CompilerParams(dimension_semantics=sem, vmem_limit_bytes=VMEM_LIMIT)


def _full(shape):
    n = len(shape)
    return pl.BlockSpec(shape, lambda *a, n=n: (0,) * n)


def _layer(shape, l):
    n = len(shape)
    return pl.BlockSpec((None,) + tuple(shape), lambda *a, n=n: (l,) + (0,) * n)


def _mod_kernel(cc_ref, w_ref, b_ref, o_ref):
    cc = cc_ref[...]
    a = cc * jax.nn.sigmoid(cc)
    o_ref[0] = jnp.dot(a, w_ref[0], precision=_HI, preferred_element_type=jnp.float32) + b_ref[0]


def _modulation(cc, w_mod, b_mod):
    depth = w_mod.shape[0]
    nw = w_mod.shape[2]
    bn = 1024
    return pl.pallas_call(
        _mod_kernel,
        grid=(depth, nw // bn),
        in_specs=[
            pl.BlockSpec((8, D), lambda l, j: (0, 0)),
            pl.BlockSpec((1, D, bn), lambda l, j: (l, 0, j)),
            pl.BlockSpec((1, 1, bn), lambda l, j: (l, 0, j)),
        ],
        out_specs=pl.BlockSpec((1, 8, bn), lambda l, j: (l, 0, j)),
        out_shape=jax.ShapeDtypeStruct((depth, 8, nw), jnp.float32),
        compiler_params=_cparams(("arbitrary", "arbitrary")),
        name="modulation",
    )(cc, w_mod, b_mod.reshape(depth, 1, nw))


def _ln_plain(x):
    mu = jnp.mean(x, axis=-1, keepdims=True)
    xc = x - mu
    var = jnp.mean(xc * xc, axis=-1, keepdims=True)
    return xc * lax.rsqrt(var + EPS)


def _mod_rows(mod_ref, idx):
    row = jnp.where(pl.program_id(0) == 0, 1, 0)
    return mod_ref[pl.ds(row, 1), idx * D:(idx + 1) * D]


def _rms_rope_t(t, gain, cos, sin):
    ms = jnp.mean(t * t, axis=0, keepdims=True)
    t = t * lax.rsqrt(ms + EPS) * gain
    outs = []
    for a in range(2):
        t1 = t[32 * a:32 * a + 16]
        t2 = t[32 * a + 16:32 * a + 32]
        c = cos[16 * a:16 * a + 16]
        s = sin[16 * a:16 * a + 16]
        outs.append(t1 * c - t2 * s)
        outs.append(t2 * c + t1 * s)
    return jnp.concatenate(outs, axis=0)


def _inproj_kernel(x_ref, mod_ref, wt_ref, wn_ref, qg_ref, kg_ref, cos_ref, sin_ref,
                   q_ref, k_ref, v_ref, kmax_ref, u_ref, ag_ref):
    x = x_ref[...]
    h = _ln_plain(x) * (1.0 + _mod_rows(mod_ref, 1)) + _mod_rows(mod_ref, 0)
    hb = h.astype(jnp.bfloat16)
    qkv_t = lax.dot_general(wt_ref[...], hb, (((1,), (1,)), ((), ())),
                            preferred_element_type=jnp.float32)
    cos = cos_ref[...]
    sin = sin_ref[...]
    qg = qg_ref[...]
    kg = kg_ref[...]
    ones_row = jnp.where(lax.broadcasted_iota(jnp.int32, (HD, TM), 0) == 0, 1.0, 0.0)
    ones_pad = jnp.where(lax.broadcasted_iota(jnp.int32, (VROWS - HD, TM), 0) == 0, 1.0, 0.0)
    for hq in range(NQ):
        t = _rms_rope_t(qkv_t[hq * HD:(hq + 1) * HD], qg, cos, sin) * (LOG2E * HD ** -0.5)
        nq = jnp.sqrt(jnp.sum(t * t, axis=0, keepdims=True))
        q_ref[hq // QPK, hq % QPK] = jnp.concatenate(
            [t, ones_row * (-BOUND_MARGIN * nq)], axis=0).astype(jnp.bfloat16)
    for g in range(NKV):
        kt = _rms_rope_t(qkv_t[ATTN_W + g * HD:ATTN_W + (g + 1) * HD], kg, cos, sin)
        kn = jnp.max(jnp.sum(kt * kt, axis=0, keepdims=True), axis=1, keepdims=True)
        kmax_ref[0, g] = jnp.broadcast_to(kn, (8, LANES))
        k_ref[g] = jnp.concatenate([kt, ones_row], axis=0).T.astype(jnp.bfloat16)
        vt = qkv_t[ATTN_W + KV_W + g * HD:ATTN_W + KV_W + (g + 1) * HD]
        v_ref[g] = jnp.concatenate([vt, ones_pad], axis=0).astype(jnp.bfloat16)
    uag = jnp.dot(hb, wn_ref[...], preferred_element_type=jnp.float32)
    u_ref[...] = uag[:, :POOL_W]
    a = uag[:, POOL_W:POOL_W + CONV_W]
    gt = uag[:, POOL_W + CONV_W:]
    ag_ref[...] = a * jax.nn.sigmoid(gt)


def _inproj(xa, mod, w_t, w_n, qg, kg, cos_t, sin_t, l):
    ttot = xa.shape[0]
    nt = ttot // TM
    return pl.pallas_call(
        _inproj_kernel,
        grid=(nt,),
        in_specs=[
            pl.BlockSpec((TM, D), lambda i: (i, 0)),
            _layer((8, N_MOD * D), l),
            _layer((ATTN_W + 2 * KV_W, D), l),
            _layer((D, POOL_W + 2 * CONV_W), l),
            _layer((HD, TM), l),
            _layer((HD, TM), l),
            pl.BlockSpec((32, TM), lambda i: (0, i)),
            pl.BlockSpec((32, TM), lambda i: (0, i)),
        ],
        out_specs=[
            pl.BlockSpec((NKV, QPK, 2 * HD, TM), lambda i: (0, 0, 0, i)),
            pl.BlockSpec((NKV, TM, 2 * HD), lambda i: (0, i, 0)),
            pl.BlockSpec((NKV, VROWS, TM), lambda i: (0, 0, i)),
            pl.BlockSpec((1, NKV, 8, LANES), lambda i: (i, 0, 0, 0)),
            pl.BlockSpec((TM, POOL_W), lambda i: (i, 0)),
            pl.BlockSpec((TM, CONV_W), lambda i: (i, 0)),
        ],
        out_shape=[
            jax.ShapeDtypeStruct((NKV, QPK, 2 * HD, ttot), jnp.bfloat16),
            jax.ShapeDtypeStruct((NKV, ttot, 2 * HD), jnp.bfloat16),
            jax.ShapeDtypeStruct((NKV, VROWS, ttot), jnp.bfloat16),
            jax.ShapeDtypeStruct((nt, NKV, 8, LANES), jnp.float32),
            jax.ShapeDtypeStruct((ttot, POOL_W), jnp.float32),
            jax.ShapeDtypeStruct((ttot, CONV_W), jnp.float32),
        ],
        compiler_params=_cparams(("arbitrary",)),
        name="inproj",
    )(xa, mod, w_t, w_n, qg, kg, cos_t, sin_t)


TK = 256
GCH = 5
L_MIN = 2.0 ** -60


def _attn_kernel(kmax_ref, q_ref, k_ref, v_ref, o_ref, qs_ref, s_ref, acc_ref, m_ref, *,
                 n_chunks):
    g = pl.program_id(0)
    i = pl.program_id(1)
    nch = jnp.where(i == 0, 1, n_chunks)
    row = lax.broadcasted_iota(jnp.int32, (2 * HD, TM), 0)
    kmax = kmax_ref[g]

    for hh in range(QPK):
        qf = q_ref[0, hh].astype(jnp.float32)
        qs_ref[hh] = jnp.where(row == HD, qf * kmax, qf).astype(jnp.bfloat16)
    acc_ref[...] = jnp.zeros(acc_ref.shape, jnp.float32)

    def step(nxt, cur):
        for kk in range(max(grp[1] for grp in (nxt, cur) if grp is not None)):
            do_nxt = nxt is not None and kk < nxt[1]
            do_cur = cur is not None and kk < cur[1]
            rows = slice(kk * TK, (kk + 1) * TK)
            if do_nxt:
                kc = k_ref[0, pl.ds(pl.multiple_of((nxt[0] + kk) * TK, TK), TK), :]
            if do_cur:
                vc = v_ref[0, :, pl.ds(pl.multiple_of((cur[0] + kk) * TK, TK), TK)]
            for hh in range(QPK):
                if do_nxt:
                    s_ref[nxt[2], hh, rows, :] = jnp.dot(
                        kc, qs_ref[hh], preferred_element_type=jnp.float32)
                if do_cur:
                    p = jnp.exp2(s_ref[cur[2], hh, rows, :]).astype(jnp.bfloat16)
                    acc_ref[hh] += jnp.dot(vc, p, preferred_element_type=jnp.float32)

    @pl.when(i == 0)
    def _():
        kc = k_ref[0, 0:TM, :]
        vc = v_ref[0, :, 0:TM]
        for hh in range(QPK):
            p = jnp.exp2(jnp.dot(kc, qs_ref[hh], preferred_element_type=jnp.float32))
            acc_ref[hh] = jnp.dot(vc, p.astype(jnp.bfloat16), preferred_element_type=jnp.float32)

    @pl.when(i > 0)
    def _():
        n_mid = (n_chunks - 2) // GCH
        tail = n_chunks - 1 - GCH * n_mid
        sizes = [1] + [GCH] * n_mid + ([tail - 1, 1] if tail >= 2 else [tail])
        firsts = [sum(sizes[:n]) for n in range(len(sizes))]

        def group(n):
            return (firsts[n], sizes[n], n % 2)

        step(group(0), None)
        step(group(1), group(0))
        n_loop = max(n_mid - 1, 0) // 2

        def pair(j, carry):
            c0 = 1 + 2 * GCH * j
            step((c0 + GCH, GCH, 0), (c0, GCH, 1))
            step((c0 + 2 * GCH, GCH, 1), (c0 + GCH, GCH, 0))
            return carry

        lax.fori_loop(0, n_loop, pair, 0)
        for n in range(1 + 2 * n_loop, len(sizes) - 1):
            step(group(n + 1), group(n))
        step(None, group(len(sizes) - 1))

    l_min = jnp.min(acc_ref[:, HD:HD + 1, :])

    @pl.when(l_min < L_MIN)
    def _():
        for hh in range(QPK):
            qs_ref[hh] = jnp.where(row == HD, 0.0, q_ref[0, hh].astype(jnp.float32)
                                   ).astype(jnp.bfloat16)
        acc_ref[...] = jnp.zeros(acc_ref.shape, jnp.float32)
        m_ref[...] = jnp.full(m_ref.shape, NEG_BIG, jnp.float32)

        def body2(c, carry):
            start = pl.multiple_of(c * TK, TK)
            kc = k_ref[0, pl.ds(start, TK), :]
            vc = v_ref[0, :, pl.ds(start, TK)]
            for hh in range(QPK):
                s = jnp.dot(kc, qs_ref[hh], preferred_element_type=jnp.float32)
                m_old = m_ref[hh]
                m_new = jnp.maximum(m_old, jnp.max(s, axis=0, keepdims=True))
                p = jnp.exp2(s - m_new).astype(jnp.bfloat16)
                acc_ref[hh] = jnp.exp2(m_old - m_new) * acc_ref[hh] + jnp.dot(
                    vc, p, preferred_element_type=jnp.float32)
                m_ref[hh] = m_new
            return carry

        lax.fori_loop(0, nch, body2, 0)

    outs = [acc_ref[hh, 0:HD, :] / acc_ref[hh, HD:HD + 1, :] for hh in range(QPK)]
    o_ref[...] = jnp.concatenate(outs, axis=0).T


def _attention(q_t, k2, v_t, kmax):
    ttot = k2.shape[1]
    nt = ttot // TM
    assert ttot % TK == 0 and ttot // TK >= 3
    kern = functools.partial(_attn_kernel, n_chunks=ttot // TK)
    grid_spec = pltpu.PrefetchScalarGridSpec(
        num_scalar_prefetch=1,
        grid=(NKV, nt),
        in_specs=[
            pl.BlockSpec((1, QPK, 2 * HD, TM), lambda g, i, km: (g, 0, 0, i)),
            pl.BlockSpec((1, ttot, 2 * HD), lambda g, i, km: (g, 0, 0)),
            pl.BlockSpec((1, VROWS, ttot), lambda g, i, km: (g, 0, 0)),
        ],
        out_specs=pl.BlockSpec((TM, QPK * HD), lambda g, i, km: (i, g)),
        scratch_shapes=[
            pltpu.VMEM((QPK, 2 * HD, TM), jnp.bfloat16),
            pltpu.VMEM((2, QPK, GCH * TK, TM), jnp.float32),
            pltpu.VMEM((QPK, VROWS, TM), jnp.float32),
            pltpu.VMEM((QPK, 1, TM), jnp.float32),
        ],
    )
    return pl.pallas_call(
        kern,
        grid_spec=grid_spec,
        out_shape=jax.ShapeDtypeStruct((ttot, ATTN_W), jnp.float32),
        compiler_params=_cparams(("arbitrary", "arbitrary")),
        name="attention",
    )(kmax, q_t, k2, v_t)


def _fill_ext(ext_ref, cur_ref, prev_ref, next_ref, left_ok, right_ok):
    zero = jnp.zeros((HALO, cur_ref.shape[1]), jnp.float32)
    ext_ref[0:HALO, :] = jnp.where(left_ok, prev_ref[...], zero)
    ext_ref[HALO:HALO + TM, :] = cur_ref[...]
    ext_ref[HALO + TM:HALO + TM + HALO, :] = jnp.where(right_ok, next_ref[...], zero)


def _mix_kernel(x_ref, attn_ref, u_ref, up_ref, un_ref, ag_ref, agp_ref, agn_ref, mod_ref,
                wpool_ref, pscale_ref, wdw_ref, bdw_ref, cvg_ref, cvb_ref, wpw_ref, wout_ref,
                ln1g_ref, ln1b_ref, wr_ref, br_ref,
                x1_ref, h2_ref, route_ref, cnt_ref,
                uext_ref, agext_ref, agsh_ref, base_ref, *, alpha, n_lat):
    i = pl.program_id(0)
    nt = pl.num_programs(0)
    is_ctx = i == 0
    left_ok = i >= 2
    right_ok = jnp.logical_and(i >= 1, i < nt - 1)

    @pl.when(i == 0)
    def _():
        base_ref[...] = jnp.zeros(base_ref.shape, jnp.float32)

    _fill_ext(uext_ref, u_ref, up_ref, un_ref, left_ok, right_ok)
    _fill_ext(agext_ref, ag_ref, agp_ref, agn_ref, left_ok, right_ok)

    def ush(off, rows=TM):
        return uext_ref[HALO + off:HALO + off + rows, :]

    a2 = ush(-8, TM + 15) + ush(-7, TM + 15)
    a4 = a2[0:TM + 13] + a2[2:TM + 15]
    a8 = a4[0:TM + 9] + a4[4:TM + 13]
    a16 = a8[0:TM] + a8[8:TM + 8]
    sums = (a2[7:7 + TM], a4[6:6 + TM], a8[4:4 + TM], a16)
    pos = lax.broadcasted_iota(jnp.int32, (TM, 1), 0) + jnp.where(is_ctx, 0, (i - 1) * TM)
    seq_n = jnp.where(is_ctx, TM, n_lat)
    u_cur = u_ref[...]
    lane = lax.broadcasted_iota(jnp.int32, (TM, POOL_W), 1)
    pooled = jnp.zeros((TM, POOL_W), jnp.float32)
    for gi, w in enumerate(POOL_WINDOWS):
        lo = jnp.maximum(pos - w // 2, 0)
        hi = jnp.minimum(pos + (w - w // 2), seq_n)
        inv = 1.0 / (hi - lo).astype(jnp.float32)
        in_group = jnp.logical_and(lane >= gi * POOL_GW, lane < (gi + 1) * POOL_GW)
        pooled = jnp.where(in_group, sums[gi] * inv - u_cur, pooled)
    y_pool = jnp.dot(pooled.astype(jnp.bfloat16), wpool_ref[...],
                     preferred_element_type=jnp.float32) * pscale_ref[...]

    sh_rows = TM + 2 * HALO - 8
    for b in range(1, 8):
        agsh_ref[b - 1] = agext_ref[b:b + sh_rows, :]
    conv = jnp.zeros((TM, CONV_W), jnp.float32) + bdw_ref[...]
    for k in range(CONV_K):
        off = HALO + k - CONV_K // 2
        b, a8 = off % 8, off // 8 * 8
        tap = agext_ref[a8:a8 + TM, :] if b == 0 else agsh_ref[b - 1, a8:a8 + TM, :]
        conv = conv + tap * wdw_ref[k:k + 1, :]
    cn = _ln_plain(conv) * cvg_ref[...] + cvb_ref[...]
    cn = cn * jax.nn.sigmoid(cn)
    y_conv = jnp.dot(cn.astype(jnp.bfloat16), wpw_ref[...], preferred_element_type=jnp.float32)

    y = jnp.dot(attn_ref[...].astype(jnp.bfloat16), wout_ref[0:ATTN_W, :],
                preferred_element_type=jnp.float32)
    y = y + jnp.dot(y_pool.astype(jnp.bfloat16), wout_ref[ATTN_W:ATTN_W + POOL_W, :],
                    preferred_element_type=jnp.float32)
    y = y + jnp.dot(y_conv.astype(jnp.bfloat16), wout_ref[ATTN_W + POOL_W:, :],
                    preferred_element_type=jnp.float32)
    x1 = _ln_plain(alpha * x_ref[...] + _mod_rows(mod_ref, 2) * y) * ln1g_ref[...] + ln1b_ref[...]
    x1_ref[...] = x1
    h2 = _ln_plain(x1) * (1.0 + _mod_rows(mod_ref, 4)) + _mod_rows(mod_ref, 3)
    h2_ref[...] = h2

    h_hi = h2.astype(jnp.bfloat16)
    h_lo = (h2 - h_hi.astype(jnp.float32)).astype(jnp.bfloat16)
    hw = jnp.dot(h_hi, wr_ref[...], preferred_element_type=jnp.float32)
    logits = (hw[:, :LANES] + hw[:, LANES:] + br_ref[...]
              + jnp.dot(h_lo, wr_ref[:, :LANES], preferred_element_type=jnp.float32))
    ln = lax.broadcasted_iota(jnp.int32, (TM, LANES), 1)
    lnf = ln.astype(jnp.float32)
    no_lane = jnp.float32(LANES)
    neg = jnp.float32(-jnp.inf)
    is_g = ln < N_GROUPS
    lg = jnp.where(is_g, logits, neg)
    mg = jnp.max(lg, axis=-1, keepdims=True)
    gsel = jnp.min(jnp.where(lg == mg, lnf, no_lane), axis=-1, keepdims=True)
    p_group = 1.0 / jnp.sum(jnp.exp(lg - mg), axis=-1, keepdims=True)
    e_lane = lnf - N_GROUPS
    in_sel = jnp.logical_and(e_lane >= gsel * EPG, e_lane < (gsel + 1) * EPG)
    le = jnp.where(in_sel, logits, neg)
    v1 = jnp.max(le, axis=-1, keepdims=True)
    e1 = jnp.min(jnp.where(le == v1, e_lane, no_lane), axis=-1, keepdims=True)
    le2 = jnp.where(e_lane == e1, neg, le)
    v2 = jnp.max(le2, axis=-1, keepdims=True)
    e2 = jnp.min(jnp.where(le2 == v2, e_lane, no_lane), axis=-1, keepdims=True)
    ex = jnp.exp(v2 - v1)
    gate1 = p_group / (1.0 + ex)
    gate2 = p_group * ex / (1.0 + ex)

    oh = jnp.logical_or(lnf == e1, lnf == e2)
    ohb = jnp.where(oh, 1.0, 0.0).astype(jnp.bfloat16)
    r_i = lax.broadcasted_iota(jnp.int32, (TM, TM), 0)
    c_i = lax.broadcasted_iota(jnp.int32, (TM, TM), 1)
    tri = jnp.where(c_i < r_i, 1.0, 0.0).astype(jnp.bfloat16)
    before = jnp.dot(tri, ohb, preferred_element_type=jnp.float32) + base_ref[...]
    rank1 = jnp.sum(jnp.where(lnf == e1, before, 0.0), axis=-1, keepdims=True)
    rank2 = jnp.sum(jnp.where(lnf == e2, before, 0.0), axis=-1, keepdims=True)
    base_ref[...] = base_ref[...] + jnp.sum(ohb.astype(jnp.float32), axis=0, keepdims=True)
    cnt_ref[...] = jnp.broadcast_to(base_ref[...], cnt_ref.shape)

    route = jnp.zeros((TM, LANES), jnp.float32)
    for j, val in enumerate((e1, e2, rank1, rank2, gate1, gate2)):
        route = jnp.where(ln == j, val, route)
    route_ref[...] = route


def _mixer(xa, attn, u, ag, mod, wpool, pscale, wdw, bdw, cvg, cvb, wpw, wout, ln1g, ln1b,
           wr, br, alpha, l):
    ttot = xa.shape[0]
    nt = ttot // TM
    hb = TM // HALO
    nhb = ttot // HALO

    def tile(w):
        return pl.BlockSpec((TM, w), lambda i: (i, 0))

    def prev(w):
        return pl.BlockSpec((HALO, w), lambda i: (jnp.maximum(i * hb - 1, 0), 0))

    def nxt(w):
        return pl.BlockSpec((HALO, w), lambda i: (jnp.minimum((i + 1) * hb, nhb - 1), 0))

    kern = functools.partial(_mix_kernel, alpha=alpha, n_lat=ttot - TM)
    return pl.pallas_call(
        kern,
        grid=(nt,),
        in_specs=[
            tile(D), tile(ATTN_W),
            tile(POOL_W), prev(POOL_W), nxt(POOL_W),
            tile(CONV_W), prev(CONV_W), nxt(CONV_W),
            _layer((8, N_MOD * D), l),
            _layer((POOL_W, POOL_W), l), _layer((1, POOL_W), l),
            _layer((32, CONV_W), l), _layer((1, CONV_W), l), _layer((1, CONV_W), l),
            _layer((1, CONV_W), l),
            _layer((CONV_W, CONV_W), l), _layer((D, D), l),
            _layer((1, D), l), _layer((1, D), l),
            _layer((D, 2 * LANES), l), _layer((1, LANES), l),
        ],
        out_specs=[tile(D), tile(D), tile(LANES), _full((8, LANES))],
        out_shape=[
            jax.ShapeDtypeStruct((ttot, D), jnp.float32),
            jax.ShapeDtypeStruct((ttot, D), jnp.float32),
            jax.ShapeDtypeStruct((ttot, LANES), jnp.float32),
            jax.ShapeDtypeStruct((8, LANES), jnp.float32),
        ],
        scratch_shapes=[
            pltpu.VMEM((TM + 2 * HALO, POOL_W), jnp.float32),
            pltpu.VMEM((TM + 2 * HALO, CONV_W), jnp.float32),
            pltpu.VMEM((7, TM + 2 * HALO - 8, CONV_W), jnp.float32),
            pltpu.VMEM((1, LANES), jnp.float32),
        ],
        compiler_params=_cparams(("arbitrary",)),
        name="mixer",
    )(xa, attn, u, u, u, ag, ag, ag, mod, wpool, pscale, wdw, bdw, cvg, cvb, wpw, wout,
      ln1g, ln1b, wr, br)


ROW_GROUP = 8


def _dispatch_kernel(pends_ref, d1_ref, d2_ref, h_hbm, xs_ref, zero_buf, h_buf, lsems, ssems,
                     zsem, *, n_blk):
    i = pl.program_id(0)
    nt = pl.num_programs(0)
    slot = i % 3

    def load(t, to_slot):
        return pltpu.make_async_copy(h_hbm.at[pl.ds(pl.multiple_of(t * TM, TM), TM), :],
                                     h_buf.at[to_slot], lsems.at[to_slot])

    def wait_scatter(of_slot):
        for _ in range(2):
            pltpu.make_async_copy(h_buf.at[of_slot], xs_ref.at[pl.ds(0, TM), :],
                                  ssems.at[of_slot]).wait()

    @pl.when(i == 0)
    def _():
        load(0, 0).start()

        @pl.when(nt > 1)
        def _():
            load(1, 1).start()

        zero_buf[...] = jnp.zeros(zero_buf.shape, jnp.float32)

        def zero_block(first_row):
            return pltpu.make_async_copy(
                zero_buf, xs_ref.at[pl.ds(pl.multiple_of(first_row, EBLK), EBLK), :], zsem)

        def seg_nonempty(e):
            return pends_ref[e] > (pends_ref[e - 1] if e > 0 else 0)

        for e in range(N_EXPERTS):
            @pl.when(seg_nonempty(e))
            def _():
                zero_block(pends_ref[e] - EBLK).start()
        n_used = pends_ref[N_EXPERTS - 1] // EBLK

        def tail_start(b, carry):
            zero_block(b * EBLK).start()
            return carry

        lax.fori_loop(n_used, n_blk, tail_start, 0)
        for e in range(N_EXPERTS):
            @pl.when(seg_nonempty(e))
            def _():
                zero_block(pends_ref[e] - EBLK).wait()

        def tail_wait(b, carry):
            zero_block(b * EBLK).wait()
            return carry

        lax.fori_loop(n_used, n_blk, tail_wait, 0)

    load(i, slot).wait()

    def start(r8, carry):
        base = pl.multiple_of(r8 * ROW_GROUP, ROW_GROUP)
        rows = h_buf.at[slot, pl.ds(base, ROW_GROUP), :]
        for j in range(ROW_GROUP):
            for d_ref in (d1_ref, d2_ref):
                pltpu.make_async_copy(rows.at[pl.ds(j, 1), :],
                                      xs_ref.at[pl.ds(d_ref[0, 0, base + j], 1), :],
                                      ssems.at[slot]).start()
        return carry

    lax.fori_loop(0, TM // ROW_GROUP, start, 0)

    @pl.when(i >= 1)
    def _():
        wait_scatter((i + 2) % 3)

    @pl.when(i + 2 < nt)
    def _():
        load(i + 2, (i + 2) % 3).start()

    @pl.when(i == nt - 1)
    def _():
        wait_scatter(slot)


def _dispatch(h2, d1, d2, pends, n_slots):
    ttot = h2.shape[0]
    nt = ttot // TM
    smem = pl.BlockSpec((1, 1, TM), lambda i, pe: (i, 0, 0), memory_space=pltpu.SMEM)
    grid_spec = pltpu.PrefetchScalarGridSpec(
        num_scalar_prefetch=1,
        grid=(nt,),
        in_specs=[smem, smem, pl.BlockSpec(memory_space=pl.ANY)],
        out_specs=pl.BlockSpec(memory_space=pl.ANY),
        scratch_shapes=[pltpu.VMEM((EBLK, D), jnp.float32),
                        pltpu.VMEM((3, TM, D), jnp.float32),
                        pltpu.SemaphoreType.DMA((3,)), pltpu.SemaphoreType.DMA((3,)),
                        pltpu.SemaphoreType.DMA(())],
    )
    return pl.pallas_call(
        functools.partial(_dispatch_kernel, n_blk=n_slots // EBLK),
        grid_spec=grid_spec,
        out_shape=jax.ShapeDtypeStruct((n_slots, D), jnp.float32),
        compiler_params=_cparams(("arbitrary",)),
        name="dispatch",
    )(pends, d1, d2, h2)


def _expert_kernel(be_ref, ne_ref, nu_ref, x_ref, wg_hbm, wu_hbm, wd_hbm, y_ref,
                   wg_f, wu_f, wd_f, wgu_b, wd_b, seg_ref, sems, *, l):
    b = pl.program_id(0)
    e = be_ref[b]
    new_expert = jnp.logical_or(b == 0, e != be_ref[jnp.maximum(b - 1, 0)])

    def fetch(expert, slot):
        return [pltpu.make_async_copy(src.at[l, expert], dst.at[slot], sems.at[slot])
                for src, dst in ((wg_hbm, wg_f), (wu_hbm, wu_f), (wd_hbm, wd_f))]

    @pl.when(b == 0)
    def _():
        seg_ref[0] = 0
        for cp in fetch(e, 0):
            cp.start()

    @pl.when(jnp.logical_and(b < nu_ref[0], new_expert))
    def _():
        seg = seg_ref[0]
        slot = seg % 2
        for cp in fetch(e, slot):
            cp.wait()

        @pl.when(ne_ref[b] != e)
        def _():
            for cp in fetch(ne_ref[b], 1 - slot):
                cp.start()

        wgu_b[:, :D_EXPERT] = wg_f[slot].astype(jnp.bfloat16)
        wgu_b[:, D_EXPERT:] = wu_f[slot].astype(jnp.bfloat16)
        wd_b[...] = wd_f[slot].astype(jnp.bfloat16)
        seg_ref[0] = seg + 1

    @pl.when(b < nu_ref[0])
    def _():
        x = x_ref[...].astype(jnp.bfloat16)
        gu = jnp.dot(x, wgu_b[...], preferred_element_type=jnp.float32)
        gt = gu[:, :D_EXPERT]
        hm = (gt * jax.nn.sigmoid(gt) * gu[:, D_EXPERT:]).astype(jnp.bfloat16)
        y_ref[...] = jnp.dot(hm, wd_b[...], preferred_element_type=jnp.float32)

    @pl.when(b >= nu_ref[0])
    def _():
        y_ref[...] = jnp.zeros(y_ref.shape, jnp.float32)


def _experts(xs, blk_e, nxt_e, n_used, wg, wu, wd, l):
    n_slots = xs.shape[0]
    nblk = n_slots // EBLK

    def row_map(b, be, ne, nu):
        return (jnp.minimum(b, nu[0] - 1), 0)

    hbm = pl.BlockSpec(memory_space=pl.ANY)
    grid_spec = pltpu.PrefetchScalarGridSpec(
        num_scalar_prefetch=3,
        grid=(nblk,),
        in_specs=[pl.BlockSpec((EBLK, D), row_map), hbm, hbm, hbm],
        out_specs=pl.BlockSpec((EBLK, D), lambda b, be, ne, nu: (b, 0)),
        scratch_shapes=[pltpu.VMEM((2, D, D_EXPERT), jnp.float32),
                        pltpu.VMEM((2, D, D_EXPERT), jnp.float32),
                        pltpu.VMEM((2, D_EXPERT, D), jnp.float32),
                        pltpu.VMEM((D, 2 * D_EXPERT), jnp.bfloat16),
                        pltpu.VMEM((D_EXPERT, D), jnp.bfloat16),
                        pltpu.SMEM((1,), jnp.int32),
                        pltpu.SemaphoreType.DMA((2,))],
    )
    return pl.pallas_call(
        functools.partial(_expert_kernel, l=l),
        grid_spec=grid_spec,
        out_shape=jax.ShapeDtypeStruct((n_slots, D), jnp.float32),
        compiler_params=_cparams(("arbitrary",)),
        name="experts",
    )(blk_e, nxt_e, n_used, xs, wg, wu, wd)


def _combine_kernel(d1_ref, d2_ref, d1n_ref, d2n_ref, x1_ref, route_ref, mod_ref, ln2g_ref,
                    ln2b_ref, ys_ref, o_ref, y_buf, sems, *, alpha):
    i = pl.program_id(0)
    nt = pl.num_programs(0)
    slot = i % 2

    def gather(da_ref, db_ref, to_slot):
        def start(r8, carry):
            base = pl.multiple_of(r8 * ROW_GROUP, ROW_GROUP)
            for k, d_ref in enumerate((da_ref, db_ref)):
                rows = y_buf.at[to_slot, k, pl.ds(base, ROW_GROUP), :]
                for j in range(ROW_GROUP):
                    pltpu.make_async_copy(ys_ref.at[pl.ds(d_ref[0, 0, base + j], 1), :],
                                          rows.at[pl.ds(j, 1), :], sems.at[to_slot]).start()
            return carry

        lax.fori_loop(0, TM // ROW_GROUP, start, 0)

    @pl.when(i == 0)
    def _():
        gather(d1_ref, d2_ref, 0)

    @pl.when(i < nt - 1)
    def _():
        gather(d1n_ref, d2n_ref, 1 - slot)

    for k in range(2):
        pltpu.make_async_copy(ys_ref.at[pl.ds(0, TM), :], y_buf.at[slot, k], sems.at[slot]).wait()

    route = route_ref[...]
    g1 = route[:, 4:5]
    g2 = route[:, 5:6]
    o = g1 * y_buf[slot, 0] + g2 * y_buf[slot, 1]
    z = alpha * x1_ref[...] + _mod_rows(mod_ref, 5) * o
    o_ref[...] = _ln_plain(z) * ln2g_ref[...] + ln2b_ref[...]


def _combine(x1, route, mod, ln2g, ln2b, ys, d1, d2, alpha, l, latents_only):
    ttot = x1.shape[0]
    nt = ttot // TM
    smem = pl.BlockSpec((1, 1, TM), lambda i: (i, 0, 0), memory_space=pltpu.SMEM)
    smem_next = pl.BlockSpec((1, 1, TM), lambda i: (jnp.minimum(i + 1, nt - 1), 0, 0),
                             memory_space=pltpu.SMEM)
    kern = functools.partial(_combine_kernel, alpha=alpha)
    if latents_only:
        out_spec = pl.BlockSpec((TM, D), lambda i: (jnp.maximum(i - 1, 0), 0))
        out_rows = ttot - TM
    else:
        out_spec = pl.BlockSpec((TM, D), lambda i: (i, 0))
        out_rows = ttot
    return pl.pallas_call(
        kern,
        grid=(nt,),
        in_specs=[smem, smem, smem_next, smem_next,
                  pl.BlockSpec((TM, D), lambda i: (i, 0)),
                  pl.BlockSpec((TM, LANES), lambda i: (i, 0)),
                  _layer((8, N_MOD * D), l), _layer((1, D), l), _layer((1, D), l),
                  pl.BlockSpec(memory_space=pl.ANY)],
        out_specs=out_spec,
        out_shape=jax.ShapeDtypeStruct((out_rows, D), jnp.float32),
        scratch_shapes=[pltpu.VMEM((2, 2, TM, D), jnp.float32),
                        pltpu.SemaphoreType.DMA((2,))],
        compiler_params=_cparams(("arbitrary",)),
        name="combine",
    )(d1, d2, d1, d2, x1, route, mod, ln2g, ln2b, ys)


def _rope_tables_t(n_lat, n_ctx):
    rows = n_lat // GRID_W
    row = jnp.broadcast_to(jnp.arange(rows)[:, None], (rows, GRID_W)).reshape(-1)
    col = jnp.broadcast_to(jnp.arange(GRID_W)[None, :], (rows, GRID_W)).reshape(-1)
    n_freq = HD // 4
    inv = ROPE_THETA ** (-jnp.arange(n_freq, dtype=jnp.float32) / n_freq)
    pos = jnp.stack([row, col], axis=-1).astype(jnp.float32)
    ang = (pos[:, :, None] * inv).reshape(n_lat, 2 * n_freq)
    cos = jnp.concatenate([jnp.ones((n_ctx, 2 * n_freq), jnp.float32), jnp.cos(ang)], axis=0)
    sin = jnp.concatenate([jnp.zeros((n_ctx, 2 * n_freq), jnp.float32), jnp.sin(ang)], axis=0)
    return cos.T, sin.T


def _row(v):
    return v.reshape(1, -1)


def kernel(x, c, ctx, c_ctx, w_mod, b_mod, w_in, q_gain, k_gain, w_pool, pool_scale, w_dw, b_dw,
           cv_ln_g, cv_ln_b, w_cv_pw, w_out, ln1_g, ln1_b, ln2_g, ln2_b, w_rg, b_rg, w_re, b_re,
           w_e_gate, w_e_up, w_e_down):
    depth = w_mod.shape[0]
    n_lat = x.shape[1]
    n_ctx = ctx.shape[1]
    assert x.shape[0] == 1 and n_ctx == TM and n_lat % TM == 0 and n_lat % GRID_W == 0
    ttot = n_ctx + n_lat
    alpha = float((2 * depth) ** 0.25)
    n_blk = -(-(2 * ttot) // EBLK) + N_EXPERTS
    n_slots = n_blk * EBLK

    xa = jnp.concatenate([ctx[0], x[0]], axis=0)
    cc = jnp.concatenate([c, c_ctx[None], jnp.zeros((6, D), jnp.float32)], axis=0)
    mod = _modulation(cc, w_mod, b_mod)
    cos_t, sin_t = _rope_tables_t(n_lat, n_ctx)

    n_qkv = ATTN_W + 2 * KV_W
    bf = jnp.bfloat16
    w_t = jnp.swapaxes(w_in[:, :, :n_qkv], 1, 2).astype(bf)
    w_n = w_in[:, :, n_qkv:].astype(bf)
    qg = jnp.broadcast_to(q_gain[:, :, None], (depth, HD, TM))
    kg = jnp.broadcast_to(k_gain[:, :, None], (depth, HD, TM))
    n_pg = len(POOL_WINDOWS)
    wpool = (jnp.eye(n_pg, dtype=jnp.float32)[None, :, None, :, None] * w_pool[:, :, :, None, :]
             ).reshape(depth, POOL_W, POOL_W).astype(bf)
    wdw = jnp.pad(w_dw, ((0, 0), (0, 32 - CONV_K), (0, 0)))
    n_r = N_GROUPS + N_EXPERTS
    wr = jnp.concatenate([w_rg, w_re, jnp.zeros((depth, D, LANES - n_r), jnp.float32)], axis=2)
    br = jnp.concatenate([b_rg, b_re, jnp.zeros((depth, LANES - n_r), jnp.float32)], axis=1)
    wr_hi = wr.astype(bf)
    wr = jnp.concatenate([wr_hi, (wr - wr_hi.astype(jnp.float32)).astype(bf)], axis=2)
    wpw = w_cv_pw.astype(bf)
    wout = w_out.astype(bf)

    def rows(v):
        return v[:, None, :]

    e_ids = jnp.arange(N_EXPERTS, dtype=jnp.int32)
    blk_start = jnp.arange(n_blk, dtype=jnp.int32) * EBLK
    for l in range(depth):
        q_t, k2, v_t, kn2, u, ag = _inproj(xa, mod, w_t, w_n, qg, kg, cos_t, sin_t, l)
        kmax = jnp.sqrt(jnp.max(kn2[:, :, 0, 0], axis=0))
        attn = _attention(q_t, k2, v_t, kmax)
        x1, h2, route, cnt = _mixer(
            xa, attn, u, ag, mod, wpool, rows(pool_scale), wdw, rows(b_dw), rows(cv_ln_g),
            rows(cv_ln_b), wpw, wout, rows(ln1_g), rows(ln1_b), wr, rows(br), alpha, l)

        counts = cnt[0, :N_EXPERTS].astype(jnp.int32)
        padded = (counts + EBLK - 1) // EBLK * EBLK
        pends = jnp.cumsum(padded)
        pstarts = pends - padded
        ri = route[:, :4].astype(jnp.int32)

        def dest(e, r):
            start = jnp.sum(jnp.where(e[:, None] == e_ids[None, :], pstarts[None, :], 0), axis=1)
            return (start + r).reshape(ttot // TM, 1, TM)

        d1 = dest(ri[:, 0], ri[:, 2])
        d2 = dest(ri[:, 1], ri[:, 3])
        n_used = (pends[-1] // EBLK).astype(jnp.int32).reshape(1)
        first_row = jnp.minimum(blk_start, pends[-1] - 1)
        blk_e = jnp.sum((pends[None, :] <= first_row[:, None]).astype(jnp.int32), axis=1)
        blk_e = jnp.minimum(blk_e, N_EXPERTS - 1)
        seg_end = jnp.sum(jnp.where(blk_e[:, None] == e_ids[None, :], pends[None, :], 0), axis=1)
        nxt_row = jnp.minimum(seg_end, pends[-1] - 1)
        nxt_e = jnp.sum((pends[None, :] <= nxt_row[:, None]).astype(jnp.int32), axis=1)
        nxt_e = jnp.minimum(nxt_e, N_EXPERTS - 1)

        xs = _dispatch(h2, d1, d2, pends, n_slots)
        ys = _experts(xs, blk_e, nxt_e, n_used, w_e_gate, w_e_up, w_e_down, l)
        xa = _combine(x1, route, mod, rows(ln2_g), rows(ln2_b), ys, d1, d2, alpha, l,
                      latents_only=(l == depth - 1))
    return xa[None]
```

```python
import functools

import jax
import jax.numpy as jnp
from jax import lax
from jax.experimental import pallas as pl
from jax.experimental.pallas import tpu as pltpu

D = 1024
GRID_W = 64
HD = 64
NQ = 8
NKV = 2
QPK = NQ // NKV
ATTN_W = NQ * HD
KV_W = NKV * HD
POOL_W = 256
POOL_WINDOWS = (2, 4, 8, 16)
POOL_GW = POOL_W // len(POOL_WINDOWS)
CONV_W = 256
CONV_K = 31
N_GROUPS = 4
EPG = 8
N_EXPERTS = N_GROUPS * EPG
D_EXPERT = 512
N_MOD = 6
ROPE_THETA = 10000.0
EPS = 1e-6

TM = 256
HALO = 16
EBLK = 256
LANES = 128
VROWS = HD + 16
LOG2E = 1.4426950408889634
BOUND_MARGIN = 1.0 + 2.0 ** -5
VMEM_LIMIT = 56 * 1024 * 1024
NEG_BIG = -1e30

_HI = lax.Precision.HIGHEST


def _cparams(sem):
    return pltpu.CompilerParams(dimension_semantics=sem, vmem_limit_bytes=VMEM_LIMIT)


def _full(shape):
    n = len(shape)
    return pl.BlockSpec(shape, lambda *a, n=n: (0,) * n)


def _layer(shape, l):
    n = len(shape)
    return pl.BlockSpec((None,) + tuple(shape), lambda *a, n=n: (l,) + (0,) * n)


def _mod_kernel(cc_ref, w_ref, b_ref, o_ref):
    cc = cc_ref[...]
    a = cc * jax.nn.sigmoid(cc)
    o_ref[0] = jnp.dot(a, w_ref[0], precision=_HI, preferred_element_type=jnp.float32) + b_ref[0]


def _modulation(cc, w_mod, b_mod):
    depth = w_mod.shape[0]
    nw = w_mod.shape[2]
    bn = 1024
    return pl.pallas_call(
        _mod_kernel,
        grid=(depth, nw // bn),
        in_specs=[
            pl.BlockSpec((8, D), lambda l, j: (0, 0)),
            pl.BlockSpec((1, D, bn), lambda l, j: (l, 0, j)),
            pl.BlockSpec((1, 1, bn), lambda l, j: (l, 0, j)),
        ],
        out_specs=pl.BlockSpec((1, 8, bn), lambda l, j: (l, 0, j)),
        out_shape=jax.ShapeDtypeStruct((depth, 8, nw), jnp.float32),
        compiler_params=_cparams(("arbitrary", "arbitrary")),
        name="modulation",
    )(cc, w_mod, b_mod.reshape(depth, 1, nw))


def _ln_plain(x):
    mu = jnp.mean(x, axis=-1, keepdims=True)
    xc = x - mu
    var = jnp.mean(xc * xc, axis=-1, keepdims=True)
    return xc * lax.rsqrt(var + EPS)


def _mod_rows(mod_ref, idx):
    row = jnp.where(pl.program_id(0) == 0, 1, 0)
    return mod_ref[pl.ds(row, 1), idx * D:(idx + 1) * D]


def _rms_rope_t(t, gain, cos, sin):
    ms = jnp.mean(t * t, axis=0, keepdims=True)
    t = t * lax.rsqrt(ms + EPS) * gain
    outs = []
    for a in range(2):
        t1 = t[32 * a:32 * a + 16]
        t2 = t[32 * a + 16:32 * a + 32]
        c = cos[16 * a:16 * a + 16]
        s = sin[16 * a:16 * a + 16]
        outs.append(t1 * c - t2 * s)
        outs.append(t2 * c + t1 * s)
    return jnp.concatenate(outs, axis=0)


def _stream_tile(x_ref, ctx_ref):
    if ctx_ref is None:
        return x_ref[...]
    return jnp.where(pl.program_id(0) == 0, ctx_ref[...], x_ref[...])


def _stream_specs(xa, ctx):
    if ctx is None:
        return [pl.BlockSpec((TM, D), lambda i: (i, 0))], [xa]
    return [pl.BlockSpec((TM, D), lambda i: (jnp.maximum(i - 1, 0), 0)), _full((TM, D))], [xa, ctx]


def _inproj_kernel(*refs, split_stream):
    x_ref, ctx_ref = (refs[0], refs[1]) if split_stream else (refs[0], None)
    (mod_ref, wt_ref, wn_ref, qg_ref, kg_ref, cos_ref, sin_ref,
     q_ref, k_ref, v_ref, kmax_ref, u_ref, ag_ref) = refs[2 if split_stream else 1:]
    x = _stream_tile(x_ref, ctx_ref)
    h = _ln_plain(x) * (1.0 + _mod_rows(mod_ref, 1)) + _mod_rows(mod_ref, 0)
    hb = h.astype(jnp.bfloat16)
    qkv_t = lax.dot_general(wt_ref[...], hb, (((1,), (1,)), ((), ())),
                            preferred_element_type=jnp.float32)
    cos = cos_ref[...]
    sin = sin_ref[...]
    qg = qg_ref[...]
    kg = kg_ref[...]
    ones_row = jnp.where(lax.broadcasted_iota(jnp.int32, (HD, TM), 0) == 0, 1.0, 0.0)
    ones_pad = jnp.where(lax.broadcasted_iota(jnp.int32, (VROWS - HD, TM), 0) == 0, 1.0, 0.0)
    for hq in range(NQ):
        t = _rms_rope_t(qkv_t[hq * HD:(hq + 1) * HD], qg, cos, sin) * (LOG2E * HD ** -0.5)
        nq = jnp.sqrt(jnp.sum(t * t, axis=0, keepdims=True))
        q_ref[hq // QPK, hq % QPK] = jnp.concatenate(
            [t, ones_row * (-BOUND_MARGIN * nq)], axis=0).astype(jnp.bfloat16)
    for g in range(NKV):
        kt = _rms_rope_t(qkv_t[ATTN_W + g * HD:ATTN_W + (g + 1) * HD], kg, cos, sin)
        kn = jnp.max(jnp.sum(kt * kt, axis=0, keepdims=True), axis=1, keepdims=True)
        kmax_ref[0, g] = jnp.broadcast_to(kn, (8, LANES))
        k_ref[g] = jnp.concatenate([kt, ones_row], axis=0).T.astype(jnp.bfloat16)
        vt = qkv_t[ATTN_W + KV_W + g * HD:ATTN_W + KV_W + (g + 1) * HD]
        v_ref[g] = jnp.concatenate([vt, ones_pad], axis=0).astype(jnp.bfloat16)
    uag = jnp.dot(hb, wn_ref[...], preferred_element_type=jnp.float32)
    u_ref[...] = uag[:, :POOL_W]
    a = uag[:, POOL_W:POOL_W + CONV_W]
    gt = uag[:, POOL_W + CONV_W:]
    ag_ref[...] = a * jax.nn.sigmoid(gt)


def _inproj(xa, ctx, mod, w_t, w_n, qg, kg, cos_t, sin_t, l):
    ttot = xa.shape[0] + (0 if ctx is None else ctx.shape[0])
    nt = ttot // TM
    stream_specs, stream_args = _stream_specs(xa, ctx)
    return pl.pallas_call(
        functools.partial(_inproj_kernel, split_stream=ctx is not None),
        grid=(nt,),
        in_specs=stream_specs + [
            _layer((8, N_MOD * D), l),
            _layer((ATTN_W + 2 * KV_W, D), l),
            _layer((D, POOL_W + 2 * CONV_W), l),
            _layer((HD, TM), l),
            _layer((HD, TM), l),
            pl.BlockSpec((32, TM), lambda i: (0, i)),
            pl.BlockSpec((32, TM), lambda i: (0, i)),
        ],
        out_specs=[
            pl.BlockSpec((NKV, QPK, 2 * HD, TM), lambda i: (0, 0, 0, i)),
            pl.BlockSpec((NKV, TM, 2 * HD), lambda i: (0, i, 0)),
            pl.BlockSpec((NKV, VROWS, TM), lambda i: (0, 0, i)),
            pl.BlockSpec((1, NKV, 8, LANES), lambda i: (i, 0, 0, 0)),
            pl.BlockSpec((TM, POOL_W), lambda i: (i, 0)),
            pl.BlockSpec((TM, CONV_W), lambda i: (i, 0)),
        ],
        out_shape=[
            jax.ShapeDtypeStruct((NKV, QPK, 2 * HD, ttot), jnp.bfloat16),
            jax.ShapeDtypeStruct((NKV, ttot, 2 * HD), jnp.bfloat16),
            jax.ShapeDtypeStruct((NKV, VROWS, ttot), jnp.bfloat16),
            jax.ShapeDtypeStruct((nt, NKV, 8, LANES), jnp.float32),
            jax.ShapeDtypeStruct((ttot, POOL_W), jnp.float32),
            jax.ShapeDtypeStruct((ttot, CONV_W), jnp.float32),
        ],
        compiler_params=_cparams(("arbitrary",)),
        name="inproj",
    )(*stream_args, mod, w_t, w_n, qg, kg, cos_t, sin_t)


TK = 256
GCH = 5
L_MIN = 2.0 ** -60


def _attn_kernel(kmax_ref, q_ref, k_ref, v_ref, o_ref, qs_ref, s_ref, acc_ref, m_ref, *,
                 n_chunks):
    g = pl.program_id(0)
    i = pl.program_id(1)
    nch = jnp.where(i == 0, 1, n_chunks)
    row = lax.broadcasted_iota(jnp.int32, (2 * HD, TM), 0)
    kmax = kmax_ref[g]

    for hh in range(QPK):
        qf = q_ref[0, hh].astype(jnp.float32)
        qs_ref[hh] = jnp.where(row == HD, qf * kmax, qf).astype(jnp.bfloat16)
    acc_ref[...] = jnp.zeros(acc_ref.shape, jnp.float32)

    def step(nxt, cur):
        for kk in range(max(grp[1] for grp in (nxt, cur) if grp is not None)):
            do_nxt = nxt is not None and kk < nxt[1]
            do_cur = cur is not None and kk < cur[1]
            rows = slice(kk * TK, (kk + 1) * TK)
            if do_nxt:
                kc = k_ref[0, pl.ds(pl.multiple_of((nxt[0] + kk) * TK, TK), TK), :]
            if do_cur:
                vc = v_ref[0, :, pl.ds(pl.multiple_of((cur[0] + kk) * TK, TK), TK)]
            for hh in range(QPK):
                if do_nxt:
                    s_ref[nxt[2], hh, rows, :] = jnp.dot(
                        kc, qs_ref[hh], preferred_element_type=jnp.float32)
                if do_cur:
                    p = jnp.exp2(s_ref[cur[2], hh, rows, :]).astype(jnp.bfloat16)
                    acc_ref[hh] += jnp.dot(vc, p, preferred_element_type=jnp.float32)

    @pl.when(i == 0)
    def _():
        kc = k_ref[0, 0:TM, :]
        vc = v_ref[0, :, 0:TM]
        for hh in range(QPK):
            p = jnp.exp2(jnp.dot(kc, qs_ref[hh], preferred_element_type=jnp.float32))
            acc_ref[hh] = jnp.dot(vc, p.astype(jnp.bfloat16), preferred_element_type=jnp.float32)

    @pl.when(i > 0)
    def _():
        n_mid = (n_chunks - 2) // GCH
        tail = n_chunks - 1 - GCH * n_mid
        sizes = [1] + [GCH] * n_mid + ([tail - 1, 1] if tail >= 2 else [tail])
        firsts = [sum(sizes[:n]) for n in range(len(sizes))]

        def group(n):
            return (firsts[n], sizes[n], n % 2)

        step(group(0), None)
        step(group(1), group(0))
        n_loop = max(n_mid - 1, 0) // 2

        def pair(j, carry):
            c0 = 1 + 2 * GCH * j
            step((c0 + GCH, GCH, 0), (c0, GCH, 1))
            step((c0 + 2 * GCH, GCH, 1), (c0 + GCH, GCH, 0))
            return carry

        lax.fori_loop(0, n_loop, pair, 0)
        for n in range(1 + 2 * n_loop, len(sizes) - 1):
            step(group(n + 1), group(n))
        step(None, group(len(sizes) - 1))

    l_min = jnp.min(acc_ref[:, HD:HD + 1, :])

    @pl.when(l_min < L_MIN)
    def _():
        for hh in range(QPK):
            qs_ref[hh] = jnp.where(row == HD, 0.0, q_ref[0, hh].astype(jnp.float32)
                                   ).astype(jnp.bfloat16)
        acc_ref[...] = jnp.zeros(acc_ref.shape, jnp.float32)
        m_ref[...] = jnp.full(m_ref.shape, NEG_BIG, jnp.float32)

        def body2(c, carry):
            start = pl.multiple_of(c * TK, TK)
            kc = k_ref[0, pl.ds(start, TK), :]
            vc = v_ref[0, :, pl.ds(start, TK)]
            for hh in range(QPK):
                s = jnp.dot(kc, qs_ref[hh], preferred_element_type=jnp.float32)
                m_old = m_ref[hh]
                m_new = jnp.maximum(m_old, jnp.max(s, axis=0, keepdims=True))
                p = jnp.exp2(s - m_new).astype(jnp.bfloat16)
                acc_ref[hh] = jnp.exp2(m_old - m_new) * acc_ref[hh] + jnp.dot(
                    vc, p, preferred_element_type=jnp.float32)
                m_ref[hh] = m_new
            return carry

        lax.fori_loop(0, nch, body2, 0)

    outs = [acc_ref[hh, 0:HD, :] / acc_ref[hh, HD:HD + 1, :] for hh in range(QPK)]
    o_ref[...] = jnp.concatenate(outs, axis=0).T


def _attention(q_t, k2, v_t, kmax):
    ttot = k2.shape[1]
    nt = ttot // TM
    assert ttot % TK == 0 and ttot // TK >= 3
    kern = functools.partial(_attn_kernel, n_chunks=ttot // TK)
    grid_spec = pltpu.PrefetchScalarGridSpec(
        num_scalar_prefetch=1,
        grid=(NKV, nt),
        in_specs=[
            pl.BlockSpec((1, QPK, 2 * HD, TM), lambda g, i, km: (g, 0, 0, i)),
            pl.BlockSpec((1, ttot, 2 * HD), lambda g, i, km: (g, 0, 0)),
            pl.BlockSpec((1, VROWS, ttot), lambda g, i, km: (g, 0, 0)),
        ],
        out_specs=pl.BlockSpec((TM, QPK * HD), lambda g, i, km: (i, g)),
        scratch_shapes=[
            pltpu.VMEM((QPK, 2 * HD, TM), jnp.bfloat16),
            pltpu.VMEM((2, QPK, GCH * TK, TM), jnp.float32),
            pltpu.VMEM((QPK, VROWS, TM), jnp.float32),
            pltpu.VMEM((QPK, 1, TM), jnp.float32),
        ],
    )
    return pl.pallas_call(
        kern,
        grid_spec=grid_spec,
        out_shape=jax.ShapeDtypeStruct((ttot, ATTN_W), jnp.float32),
        compiler_params=_cparams(("arbitrary", "arbitrary")),
        name="attention",
    )(kmax, q_t, k2, v_t)


def _fill_ext(ext_ref, cur_ref, prev_ref, next_ref, left_ok, right_ok):
    zero = jnp.zeros((HALO, cur_ref.shape[1]), jnp.float32)
    ext_ref[0:HALO, :] = jnp.where(left_ok, prev_ref[...], zero)
    ext_ref[HALO:HALO + TM, :] = cur_ref[...]
    ext_ref[HALO + TM:HALO + TM + HALO, :] = jnp.where(right_ok, next_ref[...], zero)


def _mix_kernel(*refs, alpha, n_lat, split_stream):
    x_ref, ctx_ref = (refs[0], refs[1]) if split_stream else (refs[0], None)
    (attn_ref, u_ref, up_ref, un_ref, ag_ref, agp_ref, agn_ref, mod_ref,
     wpool_ref, pscale_ref, wdw_ref, bdw_ref, cvg_ref, cvb_ref, wpw_ref, wout_ref,
     ln1g_ref, ln1b_ref, wr_ref, br_ref,
     x1_ref, h2_ref, route_ref, cnt_ref,
     uext_ref, agext_ref, agsh_ref, base_ref) = refs[2 if split_stream else 1:]
    i = pl.program_id(0)
    nt = pl.num_programs(0)
    is_ctx = i == 0
    left_ok = i >= 2
    right_ok = jnp.logical_and(i >= 1, i < nt - 1)

    @pl.when(i == 0)
    def _():
        base_ref[...] = jnp.zeros(base_ref.shape, jnp.float32)

    _fill_ext(uext_ref, u_ref, up_ref, un_ref, left_ok, right_ok)
    _fill_ext(agext_ref, ag_ref, agp_ref, agn_ref, left_ok, right_ok)

    def ush(off, rows=TM):
        return uext_ref[HALO + off:HALO + off + rows, :]

    a2 = ush(-8, TM + 15) + ush(-7, TM + 15)
    a4 = a2[0:TM + 13] + a2[2:TM + 15]
    a8 = a4[0:TM + 9] + a4[4:TM + 13]
    a16 = a8[0:TM] + a8[8:TM + 8]
    sums = (a2[7:7 + TM], a4[6:6 + TM], a8[4:4 + TM], a16)
    pos = lax.broadcasted_iota(jnp.int32, (TM, 1), 0) + jnp.where(is_ctx, 0, (i - 1) * TM)
    seq_n = jnp.where(is_ctx, TM, n_lat)
    u_cur = u_ref[...]
    lane = lax.broadcasted_iota(jnp.int32, (TM, POOL_W), 1)
    pooled = jnp.zeros((TM, POOL_W), jnp.float32)
    for gi, w in enumerate(POOL_WINDOWS):
        lo = jnp.maximum(pos - w // 2, 0)
        hi = jnp.minimum(pos + (w - w // 2), seq_n)
        inv = 1.0 / (hi - lo).astype(jnp.float32)
        in_group = jnp.logical_and(lane >= gi * POOL_GW, lane < (gi + 1) * POOL_GW)
        pooled = jnp.where(in_group, sums[gi] * inv - u_cur, pooled)
    y_pool = jnp.dot(pooled.astype(jnp.bfloat16), wpool_ref[...],
                     preferred_element_type=jnp.float32) * pscale_ref[...]

    sh_rows = TM + 2 * HALO - 8
    for b in range(1, 8):
        agsh_ref[b - 1] = agext_ref[b:b + sh_rows, :]
    conv = jnp.zeros((TM, CONV_W), jnp.float32) + bdw_ref[...]
    for k in range(CONV_K):
        off = HALO + k - CONV_K // 2
        b, a8 = off % 8, off // 8 * 8
        tap = agext_ref[a8:a8 + TM, :] if b == 0 else agsh_ref[b - 1, a8:a8 + TM, :]
        conv = conv + tap * wdw_ref[k:k + 1, :]
    cn = _ln_plain(conv) * cvg_ref[...] + cvb_ref[...]
    cn = cn * jax.nn.sigmoid(cn)
    y_conv = jnp.dot(cn.astype(jnp.bfloat16), wpw_ref[...], preferred_element_type=jnp.float32)

    y = jnp.dot(attn_ref[...].astype(jnp.bfloat16), wout_ref[0:ATTN_W, :],
                preferred_element_type=jnp.float32)
    y = y + jnp.dot(y_pool.astype(jnp.bfloat16), wout_ref[ATTN_W:ATTN_W + POOL_W, :],
                    preferred_element_type=jnp.float32)
    y = y + jnp.dot(y_conv.astype(jnp.bfloat16), wout_ref[ATTN_W + POOL_W:, :],
                    preferred_element_type=jnp.float32)
    x_res = _stream_tile(x_ref, ctx_ref)
    x1 = _ln_plain(alpha * x_res + _mod_rows(mod_ref, 2) * y) * ln1g_ref[...] + ln1b_ref[...]
    x1_ref[...] = x1
    h2 = _ln_plain(x1) * (1.0 + _mod_rows(mod_ref, 4)) + _mod_rows(mod_ref, 3)
    h_hi = h2.astype(jnp.bfloat16)
    h_rounded = h_hi.astype(jnp.float32)
    h_lo = (h2 - h_rounded).astype(jnp.bfloat16)
    bits = pltpu.bitcast(h_rounded, jnp.uint32)
    h2_ref[...] = bits[:, D // 2:] | lax.shift_right_logical(bits[:, :D // 2], jnp.uint32(16))
    hw = jnp.dot(h_hi, wr_ref[...], preferred_element_type=jnp.float32)
    logits = (hw[:, :LANES] + hw[:, LANES:] + br_ref[...]
              + jnp.dot(h_lo, wr_ref[:, :LANES], preferred_element_type=jnp.float32))
    ln = lax.broadcasted_iota(jnp.int32, (TM, LANES), 1)
    lnf = ln.astype(jnp.float32)
    no_lane = jnp.float32(LANES)
    neg = jnp.float32(-jnp.inf)
    is_g = ln < N_GROUPS
    lg = jnp.where(is_g, logits, neg)
    mg = jnp.max(lg, axis=-1, keepdims=True)
    gsel = jnp.min(jnp.where(lg == mg, lnf, no_lane), axis=-1, keepdims=True)
    p_group = 1.0 / jnp.sum(jnp.exp(lg - mg), axis=-1, keepdims=True)
    e_lane = lnf - N_GROUPS
    in_sel = jnp.logical_and(e_lane >= gsel * EPG, e_lane < (gsel + 1) * EPG)
    le = jnp.where(in_sel, logits, neg)
    v1 = jnp.max(le, axis=-1, keepdims=True)
    e1 = jnp.min(jnp.where(le == v1, e_lane, no_lane), axis=-1, keepdims=True)
    le2 = jnp.where(e_lane == e1, neg, le)
    v2 = jnp.max(le2, axis=-1, keepdims=True)
    e2 = jnp.min(jnp.where(le2 == v2, e_lane, no_lane), axis=-1, keepdims=True)
    ex = jnp.exp(v2 - v1)
    gate1 = p_group / (1.0 + ex)
    gate2 = p_group * ex / (1.0 + ex)

    oh = jnp.logical_or(lnf == e1, lnf == e2)
    ohb = jnp.where(oh, 1.0, 0.0).astype(jnp.bfloat16)
    r_i = lax.broadcasted_iota(jnp.int32, (TM, TM), 0)
    c_i = lax.broadcasted_iota(jnp.int32, (TM, TM), 1)
    tri = jnp.where(c_i < r_i, 1.0, 0.0).astype(jnp.bfloat16)
    before = jnp.dot(tri, ohb, preferred_element_type=jnp.float32) + base_ref[...]
    rank1 = jnp.sum(jnp.where(lnf == e1, before, 0.0), axis=-1, keepdims=True)
    rank2 = jnp.sum(jnp.where(lnf == e2, before, 0.0), axis=-1, keepdims=True)
    base_ref[...] = base_ref[...] + jnp.sum(ohb.astype(jnp.float32), axis=0, keepdims=True)
    cnt_ref[...] = jnp.broadcast_to(base_ref[...], cnt_ref.shape)

    route = jnp.zeros((TM, LANES), jnp.float32)
    for j, val in enumerate((e1, e2, rank1, rank2, gate1, gate2)):
        route = jnp.where(ln == j, val, route)
    route_ref[...] = route


def _mixer(xa, ctx, attn, u, ag, mod, wpool, pscale, wdw, bdw, cvg, cvb, wpw, wout, ln1g, ln1b,
           wr, br, alpha, l):
    ttot = attn.shape[0]
    nt = ttot // TM
    stream_specs, stream_args = _stream_specs(xa, ctx)
    hb = TM // HALO
    nhb = ttot // HALO

    def tile(w):
        return pl.BlockSpec((TM, w), lambda i: (i, 0))

    def prev(w):
        return pl.BlockSpec((HALO, w), lambda i: (jnp.maximum(i * hb - 1, 0), 0))

    def nxt(w):
        return pl.BlockSpec((HALO, w), lambda i: (jnp.minimum((i + 1) * hb, nhb - 1), 0))

    kern = functools.partial(_mix_kernel, alpha=alpha, n_lat=ttot - TM,
                             split_stream=ctx is not None)
    return pl.pallas_call(
        kern,
        grid=(nt,),
        in_specs=stream_specs + [
            tile(ATTN_W),
            tile(POOL_W), prev(POOL_W), nxt(POOL_W),
            tile(CONV_W), prev(CONV_W), nxt(CONV_W),
            _layer((8, N_MOD * D), l),
            _layer((POOL_W, POOL_W), l), _layer((1, POOL_W), l),
            _layer((32, CONV_W), l), _layer((1, CONV_W), l), _layer((1, CONV_W), l),
            _layer((1, CONV_W), l),
            _layer((CONV_W, CONV_W), l), _layer((D, D), l),
            _layer((1, D), l), _layer((1, D), l),
            _layer((D, 2 * LANES), l), _layer((1, LANES), l),
        ],
        out_specs=[tile(D), tile(D // 2), tile(LANES), _full((8, LANES))],
        out_shape=[
            jax.ShapeDtypeStruct((ttot, D), jnp.float32),
            jax.ShapeDtypeStruct((ttot, D // 2), jnp.uint32),
            jax.ShapeDtypeStruct((ttot, LANES), jnp.float32),
            jax.ShapeDtypeStruct((8, LANES), jnp.float32),
        ],
        scratch_shapes=[
            pltpu.VMEM((TM + 2 * HALO, POOL_W), jnp.float32),
            pltpu.VMEM((TM + 2 * HALO, CONV_W), jnp.float32),
            pltpu.VMEM((7, TM + 2 * HALO - 8, CONV_W), jnp.float32),
            pltpu.VMEM((1, LANES), jnp.float32),
        ],
        compiler_params=_cparams(("arbitrary",)),
        name="mixer",
    )(*stream_args, attn, u, u, u, ag, ag, ag, mod, wpool, pscale, wdw, bdw, cvg, cvb, wpw, wout,
      ln1g, ln1b, wr, br)


ROW_GROUP = 8


def _dispatch_kernel(pends_ref, d1_ref, d2_ref, h_hbm, xs_ref, zero_buf, h_buf, lsems, ssems,
                     zsem, *, n_blk):
    i = pl.program_id(0)
    nt = pl.num_programs(0)
    slot = i % 3

    def load(t, to_slot):
        return pltpu.make_async_copy(h_hbm.at[pl.ds(pl.multiple_of(t * TM, TM), TM), :],
                                     h_buf.at[to_slot], lsems.at[to_slot])

    def wait_scatter(of_slot):
        for _ in range(2):
            pltpu.make_async_copy(h_buf.at[of_slot], xs_ref.at[pl.ds(0, TM), :],
                                  ssems.at[of_slot]).wait()

    @pl.when(i == 0)
    def _():
        load(0, 0).start()

        @pl.when(nt > 1)
        def _():
            load(1, 1).start()

        zero_buf[...] = jnp.zeros(zero_buf.shape, zero_buf.dtype)

        def zero_block(first_row):
            return pltpu.make_async_copy(
                zero_buf, xs_ref.at[pl.ds(pl.multiple_of(first_row, EBLK), EBLK), :], zsem)

        def seg_nonempty(e):
            return pends_ref[e] > (pends_ref[e - 1] if e > 0 else 0)

        for e in range(N_EXPERTS):
            @pl.when(seg_nonempty(e))
            def _():
                zero_block(pends_ref[e] - EBLK).start()
        n_used = pends_ref[N_EXPERTS - 1] // EBLK

        def tail_start(b, carry):
            zero_block(b * EBLK).start()
            return carry

        lax.fori_loop(n_used, n_blk, tail_start, 0)
        for e in range(N_EXPERTS):
            @pl.when(seg_nonempty(e))
            def _():
                zero_block(pends_ref[e] - EBLK).wait()

        def tail_wait(b, carry):
            zero_block(b * EBLK).wait()
            return carry

        lax.fori_loop(n_used, n_blk, tail_wait, 0)

    load(i, slot).wait()

    def start(r8, carry):
        base = pl.multiple_of(r8 * ROW_GROUP, ROW_GROUP)
        rows = h_buf.at[slot, pl.ds(base, ROW_GROUP), :]
        for j in range(ROW_GROUP):
            for d_ref in (d1_ref, d2_ref):
                pltpu.make_async_copy(rows.at[pl.ds(j, 1), :],
                                      xs_ref.at[pl.ds(d_ref[0, 0, base + j], 1), :],
                                      ssems.at[slot]).start()
        return carry

    lax.fori_loop(0, TM // ROW_GROUP, start, 0)

    @pl.when(i >= 1)
    def _():
        wait_scatter((i + 2) % 3)

    @pl.when(i + 2 < nt)
    def _():
        load(i + 2, (i + 2) % 3).start()

    @pl.when(i == nt - 1)
    def _():
        wait_scatter(slot)


def _dispatch(h2, d1, d2, pends, n_slots):
    ttot = h2.shape[0]
    nt = ttot // TM
    smem = pl.BlockSpec((1, 1, TM), lambda i, pe: (i, 0, 0), memory_space=pltpu.SMEM)
    grid_spec = pltpu.PrefetchScalarGridSpec(
        num_scalar_prefetch=1,
        grid=(nt,),
        in_specs=[smem, smem, pl.BlockSpec(memory_space=pl.ANY)],
        out_specs=pl.BlockSpec(memory_space=pl.ANY),
        scratch_shapes=[pltpu.VMEM((EBLK, D // 2), jnp.uint32),
                        pltpu.VMEM((3, TM, D // 2), jnp.uint32),
                        pltpu.SemaphoreType.DMA((3,)), pltpu.SemaphoreType.DMA((3,)),
                        pltpu.SemaphoreType.DMA(())],
    )
    return pl.pallas_call(
        functools.partial(_dispatch_kernel, n_blk=n_slots // EBLK),
        grid_spec=grid_spec,
        out_shape=jax.ShapeDtypeStruct((n_slots, D // 2), jnp.uint32),
        compiler_params=_cparams(("arbitrary",)),
        name="dispatch",
    )(pends, d1, d2, h2)


def _expert_kernel(be_ref, ne_ref, nu_ref, x_ref, wg_hbm, wu_hbm, wd_hbm, y_ref,
                   wg_f, wu_f, wd_f, wgu_b, wd_b, seg_ref, sems, *, l):
    b = pl.program_id(0)
    e = be_ref[b]
    new_expert = jnp.logical_or(b == 0, e != be_ref[jnp.maximum(b - 1, 0)])

    def fetch(expert, slot):
        return [pltpu.make_async_copy(src.at[l, expert], dst.at[slot], sems.at[slot])
                for src, dst in ((wg_hbm, wg_f), (wu_hbm, wu_f), (wd_hbm, wd_f))]

    @pl.when(b == 0)
    def _():
        seg_ref[0] = 0
        for cp in fetch(e, 0):
            cp.start()

    @pl.when(jnp.logical_and(b < nu_ref[0], new_expert))
    def _():
        seg = seg_ref[0]
        slot = seg % 2
        for cp in fetch(e, slot):
            cp.wait()

        @pl.when(ne_ref[b] != e)
        def _():
            for cp in fetch(ne_ref[b], 1 - slot):
                cp.start()

        wgu_b[:, :D_EXPERT] = wg_f[slot].astype(jnp.bfloat16)
        wgu_b[:, D_EXPERT:] = wu_f[slot].astype(jnp.bfloat16)
        wd_b[...] = wd_f[slot].astype(jnp.bfloat16)
        seg_ref[0] = seg + 1

    @pl.when(b < nu_ref[0])
    def _():
        xp = x_ref[...]
        x_lo = pltpu.bitcast(lax.shift_left(xp, jnp.uint32(16)), jnp.float32).astype(jnp.bfloat16)
        x_hi = pltpu.bitcast(xp & jnp.uint32(0xFFFF0000), jnp.float32).astype(jnp.bfloat16)
        gu = (jnp.dot(x_lo, wgu_b[:D // 2, :], preferred_element_type=jnp.float32)
              + jnp.dot(x_hi, wgu_b[D // 2:, :], preferred_element_type=jnp.float32))
        gt = gu[:, :D_EXPERT]
        hm = (gt * jax.nn.sigmoid(gt) * gu[:, D_EXPERT:]).astype(jnp.bfloat16)
        y_ref[...] = jnp.dot(hm, wd_b[...], preferred_element_type=jnp.float32)

    @pl.when(b >= nu_ref[0])
    def _():
        y_ref[...] = jnp.zeros(y_ref.shape, jnp.float32)


def _experts(xs, blk_e, nxt_e, n_used, wg, wu, wd, l):
    n_slots = xs.shape[0]
    nblk = n_slots // EBLK

    def row_map(b, be, ne, nu):
        return (jnp.minimum(b, nu[0] - 1), 0)

    hbm = pl.BlockSpec(memory_space=pl.ANY)
    grid_spec = pltpu.PrefetchScalarGridSpec(
        num_scalar_prefetch=3,
        grid=(nblk,),
        in_specs=[pl.BlockSpec((EBLK, D // 2), row_map), hbm, hbm, hbm],
        out_specs=pl.BlockSpec((EBLK, D), lambda b, be, ne, nu: (b, 0)),
        scratch_shapes=[pltpu.VMEM((2, D, D_EXPERT), jnp.float32),
                        pltpu.VMEM((2, D, D_EXPERT), jnp.float32),
                        pltpu.VMEM((2, D_EXPERT, D), jnp.float32),
                        pltpu.VMEM((D, 2 * D_EXPERT), jnp.bfloat16),
                        pltpu.VMEM((D_EXPERT, D), jnp.bfloat16),
                        pltpu.SMEM((1,), jnp.int32),
                        pltpu.SemaphoreType.DMA((2,))],
    )
    return pl.pallas_call(
        functools.partial(_expert_kernel, l=l),
        grid_spec=grid_spec,
        out_shape=jax.ShapeDtypeStruct((n_slots, D), jnp.float32),
        compiler_params=_cparams(("arbitrary",)),
        name="experts",
    )(blk_e, nxt_e, n_used, xs, wg, wu, wd)


def _combine_kernel(d1_ref, d2_ref, d1n_ref, d2n_ref, x1_ref, route_ref, mod_ref, ln2g_ref,
                    ln2b_ref, ys_ref, o_ref, y_buf, sems, *, alpha):
    i = pl.program_id(0)
    nt = pl.num_programs(0)
    slot = i % 2

    def gather(da_ref, db_ref, to_slot):
        def start(r8, carry):
            base = pl.multiple_of(r8 * ROW_GROUP, ROW_GROUP)
            for k, d_ref in enumerate((da_ref, db_ref)):
                rows = y_buf.at[to_slot, k, pl.ds(base, ROW_GROUP), :]
                for j in range(ROW_GROUP):
                    pltpu.make_async_copy(ys_ref.at[pl.ds(d_ref[0, 0, base + j], 1), :],
                                          rows.at[pl.ds(j, 1), :], sems.at[to_slot]).start()
            return carry

        lax.fori_loop(0, TM // ROW_GROUP, start, 0)

    @pl.when(i == 0)
    def _():
        gather(d1_ref, d2_ref, 0)

    @pl.when(i < nt - 1)
    def _():
        gather(d1n_ref, d2n_ref, 1 - slot)

    for k in range(2):
        pltpu.make_async_copy(ys_ref.at[pl.ds(0, TM), :], y_buf.at[slot, k], sems.at[slot]).wait()

    route = route_ref[...]
    g1 = route[:, 4:5]
    g2 = route[:, 5:6]
    o = g1 * y_buf[slot, 0] + g2 * y_buf[slot, 1]
    z = alpha * x1_ref[...] + _mod_rows(mod_ref, 5) * o
    o_ref[...] = _ln_plain(z) * ln2g_ref[...] + ln2b_ref[...]


def _combine(x1, route, mod, ln2g, ln2b, ys, d1, d2, alpha, l, latents_only):
    ttot = x1.shape[0]
    nt = ttot // TM
    smem = pl.BlockSpec((1, 1, TM), lambda i: (i, 0, 0), memory_space=pltpu.SMEM)
    smem_next = pl.BlockSpec((1, 1, TM), lambda i: (jnp.minimum(i + 1, nt - 1), 0, 0),
                             memory_space=pltpu.SMEM)
    kern = functools.partial(_combine_kernel, alpha=alpha)
    if latents_only:
        out_spec = pl.BlockSpec((TM, D), lambda i: (jnp.maximum(i - 1, 0), 0))
        out_rows = ttot - TM
    else:
        out_spec = pl.BlockSpec((TM, D), lambda i: (i, 0))
        out_rows = ttot
    return pl.pallas_call(
        kern,
        grid=(nt,),
        in_specs=[smem, smem, smem_next, smem_next,
                  pl.BlockSpec((TM, D), lambda i: (i, 0)),
                  pl.BlockSpec((TM, LANES), lambda i: (i, 0)),
                  _layer((8, N_MOD * D), l), _layer((1, D), l), _layer((1, D), l),
                  pl.BlockSpec(memory_space=pl.ANY)],
        out_specs=out_spec,
        out_shape=jax.ShapeDtypeStruct((out_rows, D), jnp.float32),
        scratch_shapes=[pltpu.VMEM((2, 2, TM, D), jnp.float32),
                        pltpu.SemaphoreType.DMA((2,))],
        compiler_params=_cparams(("arbitrary",)),
        name="combine",
    )(d1, d2, d1, d2, x1, route, mod, ln2g, ln2b, ys)


def _rope_tables_t(n_lat, n_ctx):
    rows = n_lat // GRID_W
    row = jnp.broadcast_to(jnp.arange(rows)[:, None], (rows, GRID_W)).reshape(-1)
    col = jnp.broadcast_to(jnp.arange(GRID_W)[None, :], (rows, GRID_W)).reshape(-1)
    n_freq = HD // 4
    inv = ROPE_THETA ** (-jnp.arange(n_freq, dtype=jnp.float32) / n_freq)
    pos = jnp.stack([row, col], axis=-1).astype(jnp.float32)
    ang = (pos[:, :, None] * inv).reshape(n_lat, 2 * n_freq)
    cos = jnp.concatenate([jnp.ones((n_ctx, 2 * n_freq), jnp.float32), jnp.cos(ang)], axis=0)
    sin = jnp.concatenate([jnp.zeros((n_ctx, 2 * n_freq), jnp.float32), jnp.sin(ang)], axis=0)
    return cos.T, sin.T


def _row(v):
    return v.reshape(1, -1)


def kernel(x, c, ctx, c_ctx, w_mod, b_mod, w_in, q_gain, k_gain, w_pool, pool_scale, w_dw, b_dw,
           cv_ln_g, cv_ln_b, w_cv_pw, w_out, ln1_g, ln1_b, ln2_g, ln2_b, w_rg, b_rg, w_re, b_re,
           w_e_gate, w_e_up, w_e_down):
    depth = w_mod.shape[0]
    n_lat = x.shape[1]
    n_ctx = ctx.shape[1]
    assert x.shape[0] == 1 and n_ctx == TM and n_lat % TM == 0 and n_lat % GRID_W == 0
    ttot = n_ctx + n_lat
    alpha = float((2 * depth) ** 0.25)
    n_blk = -(-(2 * ttot) // EBLK) + N_EXPERTS
    n_slots = n_blk * EBLK

    xa, xctx = x[0], ctx[0]
    cc = jnp.concatenate([c, c_ctx[None], jnp.zeros((6, D), jnp.float32)], axis=0)
    mod = _modulation(cc, w_mod, b_mod)
    cos_t, sin_t = _rope_tables_t(n_lat, n_ctx)

    n_qkv = ATTN_W + 2 * KV_W
    bf = jnp.bfloat16
    w_t = jnp.swapaxes(w_in[:, :, :n_qkv], 1, 2).astype(bf)
    w_n = w_in[:, :, n_qkv:].astype(bf)
    qg = jnp.broadcast_to(q_gain[:, :, None], (depth, HD, TM))
    kg = jnp.broadcast_to(k_gain[:, :, None], (depth, HD, TM))
    n_pg = len(POOL_WINDOWS)
    wpool = (jnp.eye(n_pg, dtype=jnp.float32)[None, :, None, :, None] * w_pool[:, :, :, None, :]
             ).reshape(depth, POOL_W, POOL_W).astype(bf)
    wdw = jnp.pad(w_dw, ((0, 0), (0, 32 - CONV_K), (0, 0)))
    n_r = N_GROUPS + N_EXPERTS
    wr = jnp.concatenate([w_rg, w_re, jnp.zeros((depth, D, LANES - n_r), jnp.float32)], axis=2)
    br = jnp.concatenate([b_rg, b_re, jnp.zeros((depth, LANES - n_r), jnp.float32)], axis=1)
    wr_hi = wr.astype(bf)
    wr = jnp.concatenate([wr_hi, (wr - wr_hi.astype(jnp.float32)).astype(bf)], axis=2)
    wpw = w_cv_pw.astype(bf)
    wout = w_out.astype(bf)

    def rows(v):
        return v[:, None, :]

    e_ids = jnp.arange(N_EXPERTS, dtype=jnp.int32)
    blk_start = jnp.arange(n_blk, dtype=jnp.int32) * EBLK
    for l in range(depth):
        q_t, k2, v_t, kn2, u, ag = _inproj(xa, xctx, mod, w_t, w_n, qg, kg, cos_t, sin_t, l)
        kmax = jnp.sqrt(jnp.max(kn2[:, :, 0, 0], axis=0))
        attn = _attention(q_t, k2, v_t, kmax)
        x1, h2, route, cnt = _mixer(
            xa, xctx, attn, u, ag, mod, wpool, rows(pool_scale), wdw, rows(b_dw), rows(cv_ln_g),
            rows(cv_ln_b), wpw, wout, rows(ln1_g), rows(ln1_b), wr, rows(br), alpha, l)

        counts = cnt[0, :N_EXPERTS].astype(jnp.int32)
        padded = (counts + EBLK - 1) // EBLK * EBLK
        pends = jnp.cumsum(padded)
        pstarts = pends - padded
        ri = route[:, :4].astype(jnp.int32)

        def dest(e, r):
            start = jnp.sum(jnp.where(e[:, None] == e_ids[None, :], pstarts[None, :], 0), axis=1)
            return (start + r).reshape(ttot // TM, 1, TM)

        d1 = dest(ri[:, 0], ri[:, 2])
        d2 = dest(ri[:, 1], ri[:, 3])
        n_used = (pends[-1] // EBLK).astype(jnp.int32).reshape(1)
        first_row = jnp.minimum(blk_start, pends[-1] - 1)
        blk_e = jnp.sum((pends[None, :] <= first_row[:, None]).astype(jnp.int32), axis=1)
        blk_e = jnp.minimum(blk_e, N_EXPERTS - 1)
        seg_end = jnp.sum(jnp.where(blk_e[:, None] == e_ids[None, :], pends[None, :], 0), axis=1)
        nxt_row = jnp.minimum(seg_end, pends[-1] - 1)
        nxt_e = jnp.sum((pends[None, :] <= nxt_row[:, None]).astype(jnp.int32), axis=1)
        nxt_e = jnp.minimum(nxt_e, N_EXPERTS - 1)

        xs = _dispatch(h2, d1, d2, pends, n_slots)
        ys = _experts(xs, blk_e, nxt_e, n_used, w_e_gate, w_e_up, w_e_down, l)
        xa = _combine(x1, route, mod, rows(ln2_g), rows(ln2_b), ys, d1, d2, alpha, l,
                      latents_only=(l == depth - 1))
        xctx = None
    return xa[None]
```

```python
import functools

import jax
import jax.numpy as jnp
from jax import lax
from jax.experimental import pallas as pl
from jax.experimental.pallas import tpu as pltpu

D = 1024
GRID_W = 64
HD = 64
NQ = 8
NKV = 2
QPK = NQ // NKV
ATTN_W = NQ * HD
KV_W = NKV * HD
POOL_W = 256
POOL_WINDOWS = (2, 4, 8, 16)
POOL_GW = POOL_W // len(POOL_WINDOWS)
CONV_W = 256
CONV_K = 31
N_GROUPS = 4
EPG = 8
N_EXPERTS = N_GROUPS * EPG
D_EXPERT = 512
N_MOD = 6
ROPE_THETA = 10000.0
EPS = 1e-6

TM = 256
HALO = 16
EBLK = 256
LANES = 128
VROWS = HD + 16
LOG2E = 1.4426950408889634
BOUND_MARGIN = 1.0 + 2.0 ** -5
VMEM_LIMIT = 56 * 1024 * 1024
NEG_BIG = -1e30

_HI = lax.Precision.HIGHEST


def _cparams(sem):
    return pltpu.CompilerParams(dimension_semantics=sem, vmem_limit_bytes=VMEM_LIMIT)


def _full(shape):
    n = len(shape)
    return pl.BlockSpec(shape, lambda *a, n=n: (0,) * n)


def _layer(shape, l):
    n = len(shape)
    return pl.BlockSpec((None,) + tuple(shape), lambda *a, n=n: (l,) + (0,) * n)


def _mod_kernel(cc_ref, w_ref, b_ref, o_ref):
    cc = cc_ref[...]
    a = cc * jax.nn.sigmoid(cc)
    o_ref[0] = jnp.dot(a, w_ref[0], precision=_HI, preferred_element_type=jnp.float32) + b_ref[0]


def _modulation(cc, w_mod, b_mod):
    depth = w_mod.shape[0]
    nw = w_mod.shape[2]
    bn = 1024
    return pl.pallas_call(
        _mod_kernel,
        grid=(depth, nw // bn),
        in_specs=[
            pl.BlockSpec((8, D), lambda l, j: (0, 0)),
            pl.BlockSpec((1, D, bn), lambda l, j: (l, 0, j)),
            pl.BlockSpec((1, 1, bn), lambda l, j: (l, 0, j)),
        ],
        out_specs=pl.BlockSpec((1, 8, bn), lambda l, j: (l, 0, j)),
        out_shape=jax.ShapeDtypeStruct((depth, 8, nw), jnp.float32),
        compiler_params=_cparams(("arbitrary", "arbitrary")),
        name="modulation",
    )(cc, w_mod, b_mod.reshape(depth, 1, nw))


def _ln_plain(x):
    mu = jnp.mean(x, axis=-1, keepdims=True)
    xc = x - mu
    var = jnp.mean(xc * xc, axis=-1, keepdims=True)
    return xc * lax.rsqrt(var + EPS)


def _mod_rows(mod_ref, idx):
    row = jnp.where(pl.program_id(0) == 0, 1, 0)
    return mod_ref[pl.ds(row, 1), idx * D:(idx + 1) * D]


def _rms_rope_t(t, gain, cos, sin):
    ms = jnp.mean(t * t, axis=0, keepdims=True)
    t = t * lax.rsqrt(ms + EPS) * gain
    outs = []
    for a in range(2):
        t1 = t[32 * a:32 * a + 16]
        t2 = t[32 * a + 16:32 * a + 32]
        c = cos[16 * a:16 * a + 16]
        s = sin[16 * a:16 * a + 16]
        outs.append(t1 * c - t2 * s)
        outs.append(t2 * c + t1 * s)
    return jnp.concatenate(outs, axis=0)


def _stream_tile(x_ref, ctx_ref):
    if ctx_ref is None:
        return x_ref[...]
    return jnp.where(pl.program_id(0) == 0, ctx_ref[...], x_ref[...])


def _stream_specs(xa, ctx):
    if ctx is None:
        return [pl.BlockSpec((TM, D), lambda i: (i, 0))], [xa]
    return [pl.BlockSpec((TM, D), lambda i: (jnp.maximum(i - 1, 0), 0)), _full((TM, D))], [xa, ctx]


def _inproj_kernel(*refs, split_stream):
    x_ref, ctx_ref = (refs[0], refs[1]) if split_stream else (refs[0], None)
    (mod_ref, wt_ref, wn_ref, qg_ref, kg_ref, cos_ref, sin_ref,
     q_ref, k_ref, v_ref, kmax_ref, u_ref, ag_ref) = refs[2 if split_stream else 1:]
    x = _stream_tile(x_ref, ctx_ref)
    h = _ln_plain(x) * (1.0 + _mod_rows(mod_ref, 1)) + _mod_rows(mod_ref, 0)
    hb = h.astype(jnp.bfloat16)
    qkv_t = lax.dot_general(wt_ref[...], hb, (((1,), (1,)), ((), ())),
                            preferred_element_type=jnp.float32)
    cos = cos_ref[...]
    sin = sin_ref[...]
    qg = qg_ref[...]
    kg = kg_ref[...]
    ones_row = jnp.where(lax.broadcasted_iota(jnp.int32, (HD, TM), 0) == 0, 1.0, 0.0)
    ones_pad = jnp.where(lax.broadcasted_iota(jnp.int32, (VROWS - HD, TM), 0) == 0, 1.0, 0.0)
    for hq in range(NQ):
        t = _rms_rope_t(qkv_t[hq * HD:(hq + 1) * HD], qg, cos, sin) * (LOG2E * HD ** -0.5)
        nq = jnp.sqrt(jnp.sum(t * t, axis=0, keepdims=True))
        q_ref[hq // QPK, hq % QPK] = jnp.concatenate(
            [t, ones_row * (-BOUND_MARGIN * nq)], axis=0).astype(jnp.bfloat16)
    for g in range(NKV):
        kt = _rms_rope_t(qkv_t[ATTN_W + g * HD:ATTN_W + (g + 1) * HD], kg, cos, sin)
        kn = jnp.max(jnp.sum(kt * kt, axis=0, keepdims=True), axis=1, keepdims=True)
        kmax_ref[0, g] = jnp.broadcast_to(kn, (8, LANES))
        k_ref[g] = jnp.concatenate([kt, ones_row], axis=0).T.astype(jnp.bfloat16)
        vt = qkv_t[ATTN_W + KV_W + g * HD:ATTN_W + KV_W + (g + 1) * HD]
        v_ref[g] = jnp.concatenate([vt, ones_pad], axis=0).astype(jnp.bfloat16)
    uag = jnp.dot(hb, wn_ref[...], preferred_element_type=jnp.float32)
    u_ref[...] = uag[:, :POOL_W]
    a = uag[:, POOL_W:POOL_W + CONV_W]
    gt = uag[:, POOL_W + CONV_W:]
    ag_ref[...] = a * jax.nn.sigmoid(gt)


def _inproj(xa, ctx, mod, w_t, w_n, qg, kg, cos_t, sin_t, l):
    ttot = xa.shape[0] + (0 if ctx is None else ctx.shape[0])
    nt = ttot // TM
    stream_specs, stream_args = _stream_specs(xa, ctx)
    return pl.pallas_call(
        functools.partial(_inproj_kernel, split_stream=ctx is not None),
        grid=(nt,),
        in_specs=stream_specs + [
            _layer((8, N_MOD * D), l),
            _layer((ATTN_W + 2 * KV_W, D), l),
            _layer((D, POOL_W + 2 * CONV_W), l),
            _layer((HD, TM), l),
            _layer((HD, TM), l),
            pl.BlockSpec((32, TM), lambda i: (0, i)),
            pl.BlockSpec((32, TM), lambda i: (0, i)),
        ],
        out_specs=[
            pl.BlockSpec((NKV, QPK, 2 * HD, TM), lambda i: (0, 0, 0, i)),
            pl.BlockSpec((NKV, TM, 2 * HD), lambda i: (0, i, 0)),
            pl.BlockSpec((NKV, VROWS, TM), lambda i: (0, 0, i)),
            pl.BlockSpec((1, NKV, 8, LANES), lambda i: (i, 0, 0, 0)),
            pl.BlockSpec((TM, POOL_W), lambda i: (i, 0)),
            pl.BlockSpec((TM, CONV_W), lambda i: (i, 0)),
        ],
        out_shape=[
            jax.ShapeDtypeStruct((NKV, QPK, 2 * HD, ttot), jnp.bfloat16),
            jax.ShapeDtypeStruct((NKV, ttot, 2 * HD), jnp.bfloat16),
            jax.ShapeDtypeStruct((NKV, VROWS, ttot), jnp.bfloat16),
            jax.ShapeDtypeStruct((nt, NKV, 8, LANES), jnp.float32),
            jax.ShapeDtypeStruct((ttot, POOL_W), jnp.float32),
            jax.ShapeDtypeStruct((ttot, CONV_W), jnp.float32),
        ],
        compiler_params=_cparams(("arbitrary",)),
        name="inproj",
    )(*stream_args, mod, w_t, w_n, qg, kg, cos_t, sin_t)


TK = 256
GCH = 5
LOOP_STEPS = 4
L_MIN = 2.0 ** -60


def _attn_kernel(kmax_ref, q_ref, k_ref, v_ref, o_ref, qs_ref, s_ref, acc_ref, m_ref, *,
                 n_chunks):
    g = pl.program_id(0)
    i = pl.program_id(1)
    nch = jnp.where(i == 0, 1, n_chunks)
    row = lax.broadcasted_iota(jnp.int32, (2 * HD, TM), 0)
    kmax = kmax_ref[g]

    for hh in range(QPK):
        qf = q_ref[0, hh].astype(jnp.float32)
        qs_ref[hh] = jnp.where(row == HD, qf * kmax, qf).astype(jnp.bfloat16)
    acc_ref[...] = jnp.zeros(acc_ref.shape, jnp.float32)

    def step(nxt, cur):
        for kk in range(max(grp[1] for grp in (nxt, cur) if grp is not None)):
            do_nxt = nxt is not None and kk < nxt[1]
            do_cur = cur is not None and kk < cur[1]
            rows = slice(kk * TK, (kk + 1) * TK)
            if do_nxt:
                kc = k_ref[0, pl.ds(pl.multiple_of((nxt[0] + kk) * TK, TK), TK), :]
            if do_cur:
                vc = v_ref[0, :, pl.ds(pl.multiple_of((cur[0] + kk) * TK, TK), TK)]
            for hh in range(QPK):
                if do_nxt:
                    s_ref[nxt[2], hh, rows, :] = jnp.dot(
                        kc, qs_ref[hh], preferred_element_type=jnp.float32)
                if do_cur:
                    p = jnp.exp2(s_ref[cur[2], hh, rows, :]).astype(jnp.bfloat16)
                    acc_ref[hh] += jnp.dot(vc, p, preferred_element_type=jnp.float32)

    @pl.when(i == 0)
    def _():
        kc = k_ref[0, 0:TM, :]
        vc = v_ref[0, :, 0:TM]
        for hh in range(QPK):
            p = jnp.exp2(jnp.dot(kc, qs_ref[hh], preferred_element_type=jnp.float32))
            acc_ref[hh] = jnp.dot(vc, p.astype(jnp.bfloat16), preferred_element_type=jnp.float32)

    @pl.when(i > 0)
    def _():
        n_mid = (n_chunks - 2) // GCH
        tail = n_chunks - 1 - GCH * n_mid
        sizes = [1] + [GCH] * n_mid + ([tail - 1, 1] if tail >= 2 else [tail])
        firsts = [sum(sizes[:n]) for n in range(len(sizes))]

        def group(n):
            return (firsts[n], sizes[n], n % 2)

        step(group(0), None)
        step(group(1), group(0))
        n_loop = max(n_mid - 1, 0) // LOOP_STEPS

        def body(j, carry):
            c0 = 1 + LOOP_STEPS * GCH * j
            for u in range(LOOP_STEPS):
                step((c0 + (u + 1) * GCH, GCH, u % 2), (c0 + u * GCH, GCH, (u + 1) % 2))
            return carry

        lax.fori_loop(0, n_loop, body, 0)
        for n in range(1 + LOOP_STEPS * n_loop, len(sizes) - 1):
            step(group(n + 1), group(n))
        step(None, group(len(sizes) - 1))

    l_min = jnp.min(acc_ref[:, HD:HD + 1, :])

    @pl.when(l_min < L_MIN)
    def _():
        for hh in range(QPK):
            qs_ref[hh] = jnp.where(row == HD, 0.0, q_ref[0, hh].astype(jnp.float32)
                                   ).astype(jnp.bfloat16)
        acc_ref[...] = jnp.zeros(acc_ref.shape, jnp.float32)
        m_ref[...] = jnp.full(m_ref.shape, NEG_BIG, jnp.float32)

        def body2(c, carry):
            start = pl.multiple_of(c * TK, TK)
            kc = k_ref[0, pl.ds(start, TK), :]
            vc = v_ref[0, :, pl.ds(start, TK)]
            for hh in range(QPK):
                s = jnp.dot(kc, qs_ref[hh], preferred_element_type=jnp.float32)
                m_old = m_ref[hh]
                m_new = jnp.maximum(m_old, jnp.max(s, axis=0, keepdims=True))
                p = jnp.exp2(s - m_new).astype(jnp.bfloat16)
                acc_ref[hh] = jnp.exp2(m_old - m_new) * acc_ref[hh] + jnp.dot(
                    vc, p, preferred_element_type=jnp.float32)
                m_ref[hh] = m_new
            return carry

        lax.fori_loop(0, nch, body2, 0)

    outs = [acc_ref[hh, 0:HD, :] / acc_ref[hh, HD:HD + 1, :] for hh in range(QPK)]
    o_ref[...] = jnp.concatenate(outs, axis=0).T


def _attention(q_t, k2, v_t, kmax):
    ttot = k2.shape[1]
    nt = ttot // TM
    assert ttot % TK == 0 and ttot // TK >= 3
    kern = functools.partial(_attn_kernel, n_chunks=ttot // TK)
    grid_spec = pltpu.PrefetchScalarGridSpec(
        num_scalar_prefetch=1,
        grid=(NKV, nt),
        in_specs=[
            pl.BlockSpec((1, QPK, 2 * HD, TM), lambda g, i, km: (g, 0, 0, i)),
            pl.BlockSpec((1, ttot, 2 * HD), lambda g, i, km: (g, 0, 0)),
            pl.BlockSpec((1, VROWS, ttot), lambda g, i, km: (g, 0, 0)),
        ],
        out_specs=pl.BlockSpec((TM, QPK * HD), lambda g, i, km: (i, g)),
        scratch_shapes=[
            pltpu.VMEM((QPK, 2 * HD, TM), jnp.bfloat16),
            pltpu.VMEM((2, QPK, GCH * TK, TM), jnp.float32),
            pltpu.VMEM((QPK, VROWS, TM), jnp.float32),
            pltpu.VMEM((QPK, 1, TM), jnp.float32),
        ],
    )
    return pl.pallas_call(
        kern,
        grid_spec=grid_spec,
        out_shape=jax.ShapeDtypeStruct((ttot, ATTN_W), jnp.float32),
        compiler_params=_cparams(("arbitrary", "arbitrary")),
        name="attention",
    )(kmax, q_t, k2, v_t)


def _fill_ext(ext_ref, cur_ref, prev_ref, next_ref, left_ok, right_ok):
    zero = jnp.zeros((HALO, cur_ref.shape[1]), jnp.float32)
    ext_ref[0:HALO, :] = jnp.where(left_ok, prev_ref[...], zero)
    ext_ref[HALO:HALO + TM, :] = cur_ref[...]
    ext_ref[HALO + TM:HALO + TM + HALO, :] = jnp.where(right_ok, next_ref[...], zero)


def _mix_kernel(*refs, alpha, n_lat, split_stream):
    x_ref, ctx_ref = (refs[0], refs[1]) if split_stream else (refs[0], None)
    (attn_ref, u_ref, up_ref, un_ref, ag_ref, agp_ref, agn_ref, mod_ref,
     wpool_ref, pscale_ref, wdw_ref, bdw_ref, cvg_ref, cvb_ref, wpw_ref, wout_ref,
     ln1g_ref, ln1b_ref, wr_ref, br_ref,
     x1_ref, h2_ref, route_ref, cnt_ref,
     uext_ref, agext_ref, agsh_ref, base_ref) = refs[2 if split_stream else 1:]
    i = pl.program_id(0)
    nt = pl.num_programs(0)
    is_ctx = i == 0
    left_ok = i >= 2
    right_ok = jnp.logical_and(i >= 1, i < nt - 1)

    @pl.when(i == 0)
    def _():
        base_ref[...] = jnp.zeros(base_ref.shape, jnp.float32)

    _fill_ext(uext_ref, u_ref, up_ref, un_ref, left_ok, right_ok)
    _fill_ext(agext_ref, ag_ref, agp_ref, agn_ref, left_ok, right_ok)

    def ush(off, rows=TM):
        return uext_ref[HALO + off:HALO + off + rows, :]

    a2 = ush(-8, TM + 15) + ush(-7, TM + 15)
    a4 = a2[0:TM + 13] + a2[2:TM + 15]
    a8 = a4[0:TM + 9] + a4[4:TM + 13]
    a16 = a8[0:TM] + a8[8:TM + 8]
    sums = (a2[7:7 + TM], a4[6:6 + TM], a8[4:4 + TM], a16)
    pos = lax.broadcasted_iota(jnp.int32, (TM, 1), 0) + jnp.where(is_ctx, 0, (i - 1) * TM)
    seq_n = jnp.where(is_ctx, TM, n_lat)
    u_cur = u_ref[...]
    lane = lax.broadcasted_iota(jnp.int32, (TM, POOL_W), 1)
    pooled = jnp.zeros((TM, POOL_W), jnp.float32)
    for gi, w in enumerate(POOL_WINDOWS):
        lo = jnp.maximum(pos - w // 2, 0)
        hi = jnp.minimum(pos + (w - w // 2), seq_n)
        inv = 1.0 / (hi - lo).astype(jnp.float32)
        in_group = jnp.logical_and(lane >= gi * POOL_GW, lane < (gi + 1) * POOL_GW)
        pooled = jnp.where(in_group, sums[gi] * inv - u_cur, pooled)
    y_pool = jnp.dot(pooled.astype(jnp.bfloat16), wpool_ref[...],
                     preferred_element_type=jnp.float32) * pscale_ref[...]

    sh_rows = TM + 2 * HALO - 8
    for b in range(1, 8):
        agsh_ref[b - 1] = agext_ref[b:b + sh_rows, :]
    conv = jnp.zeros((TM, CONV_W), jnp.float32) + bdw_ref[...]
    for k in range(CONV_K):
        off = HALO + k - CONV_K // 2
        b, a8 = off % 8, off // 8 * 8
        tap = agext_ref[a8:a8 + TM, :] if b == 0 else agsh_ref[b - 1, a8:a8 + TM, :]
        conv = conv + tap * wdw_ref[k:k + 1, :]
    cn = _ln_plain(conv) * cvg_ref[...] + cvb_ref[...]
    cn = cn * jax.nn.sigmoid(cn)
    y_conv = jnp.dot(cn.astype(jnp.bfloat16), wpw_ref[...], preferred_element_type=jnp.float32)

    y = jnp.dot(attn_ref[...].astype(jnp.bfloat16), wout_ref[0:ATTN_W, :],
                preferred_element_type=jnp.float32)
    y = y + jnp.dot(y_pool.astype(jnp.bfloat16), wout_ref[ATTN_W:ATTN_W + POOL_W, :],
                    preferred_element_type=jnp.float32)
    y = y + jnp.dot(y_conv.astype(jnp.bfloat16), wout_ref[ATTN_W + POOL_W:, :],
                    preferred_element_type=jnp.float32)
    x_res = _stream_tile(x_ref, ctx_ref)
    x1 = _ln_plain(alpha * x_res + _mod_rows(mod_ref, 2) * y) * ln1g_ref[...] + ln1b_ref[...]
    x1_ref[...] = x1
    h2 = _ln_plain(x1) * (1.0 + _mod_rows(mod_ref, 4)) + _mod_rows(mod_ref, 3)
    h_hi = h2.astype(jnp.bfloat16)
    h_rounded = h_hi.astype(jnp.float32)
    h_lo = (h2 - h_rounded).astype(jnp.bfloat16)
    bits = pltpu.bitcast(h_rounded, jnp.uint32)
    h2_ref[...] = bits[:, D // 2:] | lax.shift_right_logical(bits[:, :D // 2], jnp.uint32(16))
    hw = jnp.dot(h_hi, wr_ref[...], preferred_element_type=jnp.float32)
    logits = (hw[:, :LANES] + hw[:, LANES:] + br_ref[...]
              + jnp.dot(h_lo, wr_ref[:, :LANES], preferred_element_type=jnp.float32))
    ln = lax.broadcasted_iota(jnp.int32, (TM, LANES), 1)
    lnf = ln.astype(jnp.float32)
    no_lane = jnp.float32(LANES)
    neg = jnp.float32(-jnp.inf)
    is_g = ln < N_GROUPS
    lg = jnp.where(is_g, logits, neg)
    mg = jnp.max(lg, axis=-1, keepdims=True)
    gsel = jnp.min(jnp.where(lg == mg, lnf, no_lane), axis=-1, keepdims=True)
    p_group = 1.0 / jnp.sum(jnp.exp(lg - mg), axis=-1, keepdims=True)
    e_lane = lnf - N_GROUPS
    in_sel = jnp.logical_and(e_lane >= gsel * EPG, e_lane < (gsel + 1) * EPG)
    le = jnp.where(in_sel, logits, neg)
    v1 = jnp.max(le, axis=-1, keepdims=True)
    e1 = jnp.min(jnp.where(le == v1, e_lane, no_lane), axis=-1, keepdims=True)
    le2 = jnp.where(e_lane == e1, neg, le)
    v2 = jnp.max(le2, axis=-1, keepdims=True)
    e2 = jnp.min(jnp.where(le2 == v2, e_lane, no_lane), axis=-1, keepdims=True)
    ex = jnp.exp(v2 - v1)
    gate1 = p_group / (1.0 + ex)
    gate2 = p_group * ex / (1.0 + ex)

    oh = jnp.logical_or(lnf == e1, lnf == e2)
    ohb = jnp.where(oh, 1.0, 0.0).astype(jnp.bfloat16)
    r_i = lax.broadcasted_iota(jnp.int32, (TM, TM), 0)
    c_i = lax.broadcasted_iota(jnp.int32, (TM, TM), 1)
    tri = jnp.where(c_i < r_i, 1.0, 0.0).astype(jnp.bfloat16)
    before = jnp.dot(tri, ohb, preferred_element_type=jnp.float32) + base_ref[...]
    rank1 = jnp.sum(jnp.where(lnf == e1, before, 0.0), axis=-1, keepdims=True)
    rank2 = jnp.sum(jnp.where(lnf == e2, before, 0.0), axis=-1, keepdims=True)
    base_ref[...] = base_ref[...] + jnp.sum(ohb.astype(jnp.float32), axis=0, keepdims=True)
    cnt_ref[...] = jnp.broadcast_to(base_ref[...], cnt_ref.shape)

    route = jnp.zeros((TM, LANES), jnp.float32)
    for j, val in enumerate((e1, e2, rank1, rank2, gate1, gate2)):
        route = jnp.where(ln == j, val, route)
    route_ref[...] = route


def _mixer(xa, ctx, attn, u, ag, mod, wpool, pscale, wdw, bdw, cvg, cvb, wpw, wout, ln1g, ln1b,
           wr, br, alpha, l):
    ttot = attn.shape[0]
    nt = ttot // TM
    stream_specs, stream_args = _stream_specs(xa, ctx)
    hb = TM // HALO
    nhb = ttot // HALO

    def tile(w):
        return pl.BlockSpec((TM, w), lambda i: (i, 0))

    def prev(w):
        return pl.BlockSpec((HALO, w), lambda i: (jnp.maximum(i * hb - 1, 0), 0))

    def nxt(w):
        return pl.BlockSpec((HALO, w), lambda i: (jnp.minimum((i + 1) * hb, nhb - 1), 0))

    kern = functools.partial(_mix_kernel, alpha=alpha, n_lat=ttot - TM,
                             split_stream=ctx is not None)
    return pl.pallas_call(
        kern,
        grid=(nt,),
        in_specs=stream_specs + [
            tile(ATTN_W),
            tile(POOL_W), prev(POOL_W), nxt(POOL_W),
            tile(CONV_W), prev(CONV_W), nxt(CONV_W),
            _layer((8, N_MOD * D), l),
            _layer((POOL_W, POOL_W), l), _layer((1, POOL_W), l),
            _layer((32, CONV_W), l), _layer((1, CONV_W), l), _layer((1, CONV_W), l),
            _layer((1, CONV_W), l),
            _layer((CONV_W, CONV_W), l), _layer((D, D), l),
            _layer((1, D), l), _layer((1, D), l),
            _layer((D, 2 * LANES), l), _layer((1, LANES), l),
        ],
        out_specs=[tile(D), tile(D // 2), tile(LANES), _full((8, LANES))],
        out_shape=[
            jax.ShapeDtypeStruct((ttot, D), jnp.float32),
            jax.ShapeDtypeStruct((ttot, D // 2), jnp.uint32),
            jax.ShapeDtypeStruct((ttot, LANES), jnp.float32),
            jax.ShapeDtypeStruct((8, LANES), jnp.float32),
        ],
        scratch_shapes=[
            pltpu.VMEM((TM + 2 * HALO, POOL_W), jnp.float32),
            pltpu.VMEM((TM + 2 * HALO, CONV_W), jnp.float32),
            pltpu.VMEM((7, TM + 2 * HALO - 8, CONV_W), jnp.float32),
            pltpu.VMEM((1, LANES), jnp.float32),
        ],
        compiler_params=_cparams(("arbitrary",)),
        name="mixer",
    )(*stream_args, attn, u, u, u, ag, ag, ag, mod, wpool, pscale, wdw, bdw, cvg, cvb, wpw, wout,
      ln1g, ln1b, wr, br)


ROW_GROUP = 8


def _dispatch_kernel(pends_ref, d1_ref, d2_ref, h_hbm, xs_ref, zero_buf, h_buf, lsems, ssems,
                     zsem, *, n_blk):
    i = pl.program_id(0)
    nt = pl.num_programs(0)
    slot = i % 3

    def load(t, to_slot):
        return pltpu.make_async_copy(h_hbm.at[pl.ds(pl.multiple_of(t * TM, TM), TM), :],
                                     h_buf.at[to_slot], lsems.at[to_slot])

    def wait_scatter(of_slot):
        for _ in range(2):
            pltpu.make_async_copy(h_buf.at[of_slot], xs_ref.at[pl.ds(0, TM), :],
                                  ssems.at[of_slot]).wait()

    @pl.when(i == 0)
    def _():
        load(0, 0).start()

        @pl.when(nt > 1)
        def _():
            load(1, 1).start()

        zero_buf[...] = jnp.zeros(zero_buf.shape, zero_buf.dtype)

        def zero_block(first_row):
            return pltpu.make_async_copy(
                zero_buf, xs_ref.at[pl.ds(pl.multiple_of(first_row, EBLK), EBLK), :], zsem)

        def seg_nonempty(e):
            return pends_ref[e] > (pends_ref[e - 1] if e > 0 else 0)

        for e in range(N_EXPERTS):
            @pl.when(seg_nonempty(e))
            def _():
                zero_block(pends_ref[e] - EBLK).start()
        n_used = pends_ref[N_EXPERTS - 1] // EBLK

        def tail_start(b, carry):
            zero_block(b * EBLK).start()
            return carry

        lax.fori_loop(n_used, n_blk, tail_start, 0)
        for e in range(N_EXPERTS):
            @pl.when(seg_nonempty(e))
            def _():
                zero_block(pends_ref[e] - EBLK).wait()

        def tail_wait(b, carry):
            zero_block(b * EBLK).wait()
            return carry

        lax.fori_loop(n_used, n_blk, tail_wait, 0)

    load(i, slot).wait()

    def start(r8, carry):
        base = pl.multiple_of(r8 * ROW_GROUP, ROW_GROUP)
        rows = h_buf.at[slot, pl.ds(base, ROW_GROUP), :]
        for j in range(ROW_GROUP):
            for d_ref in (d1_ref, d2_ref):
                pltpu.make_async_copy(rows.at[pl.ds(j, 1), :],
                                      xs_ref.at[pl.ds(d_ref[0, 0, base + j], 1), :],
                                      ssems.at[slot]).start()
        return carry

    lax.fori_loop(0, TM // ROW_GROUP, start, 0)

    @pl.when(i >= 1)
    def _():
        wait_scatter((i + 2) % 3)

    @pl.when(i + 2 < nt)
    def _():
        load(i + 2, (i + 2) % 3).start()

    @pl.when(i == nt - 1)
    def _():
        wait_scatter(slot)


def _dispatch(h2, d1, d2, pends, n_slots):
    ttot = h2.shape[0]
    nt = ttot // TM
    smem = pl.BlockSpec((1, 1, TM), lambda i, pe: (i, 0, 0), memory_space=pltpu.SMEM)
    grid_spec = pltpu.PrefetchScalarGridSpec(
        num_scalar_prefetch=1,
        grid=(nt,),
        in_specs=[smem, smem, pl.BlockSpec(memory_space=pl.ANY)],
        out_specs=pl.BlockSpec(memory_space=pl.ANY),
        scratch_shapes=[pltpu.VMEM((EBLK, D // 2), jnp.uint32),
                        pltpu.VMEM((3, TM, D // 2), jnp.uint32),
                        pltpu.SemaphoreType.DMA((3,)), pltpu.SemaphoreType.DMA((3,)),
                        pltpu.SemaphoreType.DMA(())],
    )
    return pl.pallas_call(
        functools.partial(_dispatch_kernel, n_blk=n_slots // EBLK),
        grid_spec=grid_spec,
        out_shape=jax.ShapeDtypeStruct((n_slots, D // 2), jnp.uint32),
        compiler_params=_cparams(("arbitrary",)),
        name="dispatch",
    )(pends, d1, d2, h2)


def _expert_kernel(be_ref, ne_ref, nu_ref, x_ref, wg_hbm, wu_hbm, wd_hbm, y_ref,
                   wg_f, wu_f, wd_f, wgu_b, wd_b, seg_ref, sems, *, l):
    b = pl.program_id(0)
    e = be_ref[b]
    new_expert = jnp.logical_or(b == 0, e != be_ref[jnp.maximum(b - 1, 0)])

    def fetch(expert, slot):
        return [pltpu.make_async_copy(src.at[l, expert], dst.at[slot], sems.at[slot])
                for src, dst in ((wg_hbm, wg_f), (wu_hbm, wu_f), (wd_hbm, wd_f))]

    @pl.when(b == 0)
    def _():
        seg_ref[0] = 0
        for cp in fetch(e, 0):
            cp.start()

    @pl.when(jnp.logical_and(b < nu_ref[0], new_expert))
    def _():
        seg = seg_ref[0]
        slot = seg % 2
        for cp in fetch(e, slot):
            cp.wait()

        @pl.when(ne_ref[b] != e)
        def _():
            for cp in fetch(ne_ref[b], 1 - slot):
                cp.start()

        wgu_b[:, :D_EXPERT] = wg_f[slot].astype(jnp.bfloat16)
        wgu_b[:, D_EXPERT:] = wu_f[slot].astype(jnp.bfloat16)
        wd_b[...] = wd_f[slot].astype(jnp.bfloat16)
        seg_ref[0] = seg + 1

    @pl.when(b < nu_ref[0])
    def _():
        xp = x_ref[...]
        x_lo = pltpu.bitcast(lax.shift_left(xp, jnp.uint32(16)), jnp.float32).astype(jnp.bfloat16)
        x_hi = pltpu.bitcast(xp & jnp.uint32(0xFFFF0000), jnp.float32).astype(jnp.bfloat16)
        gu = (jnp.dot(x_lo, wgu_b[:D // 2, :], preferred_element_type=jnp.float32)
              + jnp.dot(x_hi, wgu_b[D // 2:, :], preferred_element_type=jnp.float32))
        gt = gu[:, :D_EXPERT]
        hm = (gt * jax.nn.sigmoid(gt) * gu[:, D_EXPERT:]).astype(jnp.bfloat16)
        y_ref[...] = jnp.dot(hm, wd_b[...], preferred_element_type=jnp.float32)

    @pl.when(b >= nu_ref[0])
    def _():
        y_ref[...] = jnp.zeros(y_ref.shape, jnp.float32)


def _experts(xs, blk_e, nxt_e, n_used, wg, wu, wd, l):
    n_slots = xs.shape[0]
    nblk = n_slots // EBLK

    def row_map(b, be, ne, nu):
        return (jnp.minimum(b, nu[0] - 1), 0)

    hbm = pl.BlockSpec(memory_space=pl.ANY)
    grid_spec = pltpu.PrefetchScalarGridSpec(
        num_scalar_prefetch=3,
        grid=(nblk,),
        in_specs=[pl.BlockSpec((EBLK, D // 2), row_map), hbm, hbm, hbm],
        out_specs=pl.BlockSpec((EBLK, D), lambda b, be, ne, nu: (b, 0)),
        scratch_shapes=[pltpu.VMEM((2, D, D_EXPERT), jnp.float32),
                        pltpu.VMEM((2, D, D_EXPERT), jnp.float32),
                        pltpu.VMEM((2, D_EXPERT, D), jnp.float32),
                        pltpu.VMEM((D, 2 * D_EXPERT), jnp.bfloat16),
                        pltpu.VMEM((D_EXPERT, D), jnp.bfloat16),
                        pltpu.SMEM((1,), jnp.int32),
                        pltpu.SemaphoreType.DMA((2,))],
    )
    return pl.pallas_call(
        functools.partial(_expert_kernel, l=l),
        grid_spec=grid_spec,
        out_shape=jax.ShapeDtypeStruct((n_slots, D), jnp.float32),
        compiler_params=_cparams(("arbitrary",)),
        name="experts",
    )(blk_e, nxt_e, n_used, xs, wg, wu, wd)


def _combine_kernel(d1_ref, d2_ref, d1n_ref, d2n_ref, x1_ref, route_ref, mod_ref, ln2g_ref,
                    ln2b_ref, ys_ref, o_ref, y_buf, sems, *, alpha):
    i = pl.program_id(0)
    nt = pl.num_programs(0)
    slot = i % 2

    def gather(da_ref, db_ref, to_slot):
        def start(r8, carry):
            base = pl.multiple_of(r8 * ROW_GROUP, ROW_GROUP)
            for k, d_ref in enumerate((da_ref, db_ref)):
                rows = y_buf.at[to_slot, k, pl.ds(base, ROW_GROUP), :]
                for j in range(ROW_GROUP):
                    pltpu.make_async_copy(ys_ref.at[pl.ds(d_ref[0, 0, base + j], 1), :],
                                          rows.at[pl.ds(j, 1), :], sems.at[to_slot]).start()
            return carry

        lax.fori_loop(0, TM // ROW_GROUP, start, 0)

    @pl.when(i == 0)
    def _():
        gather(d1_ref, d2_ref, 0)

    @pl.when(i < nt - 1)
    def _():
        gather(d1n_ref, d2n_ref, 1 - slot)

    for k in range(2):
        pltpu.make_async_copy(ys_ref.at[pl.ds(0, TM), :], y_buf.at[slot, k], sems.at[slot]).wait()

    route = route_ref[...]
    g1 = route[:, 4:5]
    g2 = route[:, 5:6]
    o = g1 * y_buf[slot, 0] + g2 * y_buf[slot, 1]
    z = alpha * x1_ref[...] + _mod_rows(mod_ref, 5) * o
    o_ref[...] = _ln_plain(z) * ln2g_ref[...] + ln2b_ref[...]


def _combine(x1, route, mod, ln2g, ln2b, ys, d1, d2, alpha, l, latents_only):
    ttot = x1.shape[0]
    nt = ttot // TM
    smem = pl.BlockSpec((1, 1, TM), lambda i: (i, 0, 0), memory_space=pltpu.SMEM)
    smem_next = pl.BlockSpec((1, 1, TM), lambda i: (jnp.minimum(i + 1, nt - 1), 0, 0),
                             memory_space=pltpu.SMEM)
    kern = functools.partial(_combine_kernel, alpha=alpha)
    if latents_only:
        out_spec = pl.BlockSpec((TM, D), lambda i: (jnp.maximum(i - 1, 0), 0))
        out_rows = ttot - TM
    else:
        out_spec = pl.BlockSpec((TM, D), lambda i: (i, 0))
        out_rows = ttot
    return pl.pallas_call(
        kern,
        grid=(nt,),
        in_specs=[smem, smem, smem_next, smem_next,
                  pl.BlockSpec((TM, D), lambda i: (i, 0)),
                  pl.BlockSpec((TM, LANES), lambda i: (i, 0)),
                  _layer((8, N_MOD * D), l), _layer((1, D), l), _layer((1, D), l),
                  pl.BlockSpec(memory_space=pl.ANY)],
        out_specs=out_spec,
        out_shape=jax.ShapeDtypeStruct((out_rows, D), jnp.float32),
        scratch_shapes=[pltpu.VMEM((2, 2, TM, D), jnp.float32),
                        pltpu.SemaphoreType.DMA((2,))],
        compiler_params=_cparams(("arbitrary",)),
        name="combine",
    )(d1, d2, d1, d2, x1, route, mod, ln2g, ln2b, ys)


def _rope_tables_t(n_lat, n_ctx):
    rows = n_lat // GRID_W
    row = jnp.broadcast_to(jnp.arange(rows)[:, None], (rows, GRID_W)).reshape(-1)
    col = jnp.broadcast_to(jnp.arange(GRID_W)[None, :], (rows, GRID_W)).reshape(-1)
    n_freq = HD // 4
    inv = ROPE_THETA ** (-jnp.arange(n_freq, dtype=jnp.float32) / n_freq)
    pos = jnp.stack([row, col], axis=-1).astype(jnp.float32)
    ang = (pos[:, :, None] * inv).reshape(n_lat, 2 * n_freq)
    cos = jnp.concatenate([jnp.ones((n_ctx, 2 * n_freq), jnp.float32), jnp.cos(ang)], axis=0)
    sin = jnp.concatenate([jnp.zeros((n_ctx, 2 * n_freq), jnp.float32), jnp.sin(ang)], axis=0)
    return cos.T, sin.T


def _row(v):
    return v.reshape(1, -1)


def kernel(x, c, ctx, c_ctx, w_mod, b_mod, w_in, q_gain, k_gain, w_pool, pool_scale, w_dw, b_dw,
           cv_ln_g, cv_ln_b, w_cv_pw, w_out, ln1_g, ln1_b, ln2_g, ln2_b, w_rg, b_rg, w_re, b_re,
           w_e_gate, w_e_up, w_e_down):
    depth = w_mod.shape[0]
    n_lat = x.shape[1]
    n_ctx = ctx.shape[1]
    assert x.shape[0] == 1 and n_ctx == TM and n_lat % TM == 0 and n_lat % GRID_W == 0
    ttot = n_ctx + n_lat
    alpha = float((2 * depth) ** 0.25)
    n_blk = -(-(2 * ttot) // EBLK) + N_EXPERTS
    n_slots = n_blk * EBLK

    xa, xctx = x[0], ctx[0]
    cc = jnp.concatenate([c, c_ctx[None], jnp.zeros((6, D), jnp.float32)], axis=0)
    mod = _modulation(cc, w_mod, b_mod)
    cos_t, sin_t = _rope_tables_t(n_lat, n_ctx)

    n_qkv = ATTN_W + 2 * KV_W
    bf = jnp.bfloat16
    w_t = jnp.swapaxes(w_in[:, :, :n_qkv], 1, 2).astype(bf)
    w_n = w_in[:, :, n_qkv:].astype(bf)
    qg = jnp.broadcast_to(q_gain[:, :, None], (depth, HD, TM))
    kg = jnp.broadcast_to(k_gain[:, :, None], (depth, HD, TM))
    n_pg = len(POOL_WINDOWS)
    wpool = (jnp.eye(n_pg, dtype=jnp.float32)[None, :, None, :, None] * w_pool[:, :, :, None, :]
             ).reshape(depth, POOL_W, POOL_W).astype(bf)
    wdw = jnp.pad(w_dw, ((0, 0), (0, 32 - CONV_K), (0, 0)))
    n_r = N_GROUPS + N_EXPERTS
    wr = jnp.concatenate([w_rg, w_re, jnp.zeros((depth, D, LANES - n_r), jnp.float32)], axis=2)
    br = jnp.concatenate([b_rg, b_re, jnp.zeros((depth, LANES - n_r), jnp.float32)], axis=1)
    wr_hi = wr.astype(bf)
    wr = jnp.concatenate([wr_hi, (wr - wr_hi.astype(jnp.float32)).astype(bf)], axis=2)
    wpw = w_cv_pw.astype(bf)
    wout = w_out.astype(bf)

    def rows(v):
        return v[:, None, :]

    e_ids = jnp.arange(N_EXPERTS, dtype=jnp.int32)
    blk_start = jnp.arange(n_blk, dtype=jnp.int32) * EBLK
    for l in range(depth):
        q_t, k2, v_t, kn2, u, ag = _inproj(xa, xctx, mod, w_t, w_n, qg, kg, cos_t, sin_t, l)
        kmax = jnp.sqrt(jnp.max(kn2[:, :, 0, 0], axis=0))
        attn = _attention(q_t, k2, v_t, kmax)
        x1, h2, route, cnt = _mixer(
            xa, xctx, attn, u, ag, mod, wpool, rows(pool_scale), wdw, rows(b_dw), rows(cv_ln_g),
            rows(cv_ln_b), wpw, wout, rows(ln1_g), rows(ln1_b), wr, rows(br), alpha, l)

        counts = cnt[0, :N_EXPERTS].astype(jnp.int32)
        padded = (counts + EBLK - 1) // EBLK * EBLK
        pends = jnp.cumsum(padded)
        pstarts = pends - padded
        ri = route[:, :4].astype(jnp.int32)

        def dest(e, r):
            start = jnp.sum(jnp.where(e[:, None] == e_ids[None, :], pstarts[None, :], 0), axis=1)
            return (start + r).reshape(ttot // TM, 1, TM)

        d1 = dest(ri[:, 0], ri[:, 2])
        d2 = dest(ri[:, 1], ri[:, 3])
        n_used = (pends[-1] // EBLK).astype(jnp.int32).reshape(1)
        first_row = jnp.minimum(blk_start, pends[-1] - 1)
        blk_e = jnp.sum((pends[None, :] <= first_row[:, None]).astype(jnp.int32), axis=1)
        blk_e = jnp.minimum(blk_e, N_EXPERTS - 1)
        seg_end = jnp.sum(jnp.where(blk_e[:, None] == e_ids[None, :], pends[None, :], 0), axis=1)
        nxt_row = jnp.minimum(seg_end, pends[-1] - 1)
        nxt_e = jnp.sum((pends[None, :] <= nxt_row[:, None]).astype(jnp.int32), axis=1)
        nxt_e = jnp.minimum(nxt_e, N_EXPERTS - 1)

        xs = _dispatch(h2, d1, d2, pends, n_slots)
        ys = _experts(xs, blk_e, nxt_e, n_used, w_e_gate, w_e_up, w_e_down, l)
        xa = _combine(x1, route, mod, rows(ln2_g), rows(ln2_b), ys, d1, d2, alpha, l,
                      latents_only=(l == depth - 1))
        xctx = None
    return xa[None]
```

```python
import functools

import jax
import jax.numpy as jnp
from jax import lax
from jax.experimental import pallas as pl
from jax.experimental.pallas import tpu as pltpu

D = 1024
GRID_W = 64
HD = 64
NQ = 8
NKV = 2
QPK = NQ // NKV
ATTN_W = NQ * HD
KV_W = NKV * HD
POOL_W = 256
POOL_WINDOWS = (2, 4, 8, 16)
POOL_GW = POOL_W // len(POOL_WINDOWS)
CONV_W = 256
CONV_K = 31
N_GROUPS = 4
EPG = 8
N_EXPERTS = N_GROUPS * EPG
D_EXPERT = 512
N_MOD = 6
ROPE_THETA = 10000.0
EPS = 1e-6

TM = 256
HALO = 16
EBLK = 256
LANES = 128
SUBLANES = 8
BF16_ROWS = 16
VROWS = HD + BF16_ROWS
CONV_K_PAD = 32
LOG2E = 1.4426950408889634
BOUND_MARGIN = 1.0 + 2.0 ** -5
V7X_VMEM_BYTES = 64 * 1024 * 1024
VMEM_LIMIT = V7X_VMEM_BYTES - 8 * 1024 * 1024
NEG_BIG = -1e30

_HI = lax.Precision.HIGHEST


def _cparams(sem):
    return pltpu.CompilerParams(dimension_semantics=sem, vmem_limit_bytes=VMEM_LIMIT)


def _full(shape):
    n = len(shape)
    return pl.BlockSpec(shape, lambda *a, n=n: (0,) * n)


def _layer(shape, l):
    n = len(shape)
    return pl.BlockSpec((None,) + tuple(shape), lambda *a, n=n: (l,) + (0,) * n)


def _mod_kernel(cc_ref, w_ref, b_ref, o_ref):
    cc = cc_ref[...]
    a = cc * jax.nn.sigmoid(cc)
    o_ref[0] = jnp.dot(a, w_ref[0], precision=_HI, preferred_element_type=jnp.float32) + b_ref[0]


def _modulation(cc, w_mod, b_mod):
    depth = w_mod.shape[0]
    nw = w_mod.shape[2]
    bn = D
    return pl.pallas_call(
        _mod_kernel,
        grid=(depth, nw // bn),
        in_specs=[
            pl.BlockSpec((SUBLANES, D), lambda l, j: (0, 0)),
            pl.BlockSpec((1, D, bn), lambda l, j: (l, 0, j)),
            pl.BlockSpec((1, 1, bn), lambda l, j: (l, 0, j)),
        ],
        out_specs=pl.BlockSpec((1, SUBLANES, bn), lambda l, j: (l, 0, j)),
        out_shape=jax.ShapeDtypeStruct((depth, SUBLANES, nw), jnp.float32),
        compiler_params=_cparams(("arbitrary", "arbitrary")),
        name="modulation",
    )(cc, w_mod, b_mod.reshape(depth, 1, nw))


def _ln_plain(x):
    mu = jnp.mean(x, axis=-1, keepdims=True)
    xc = x - mu
    var = jnp.mean(xc * xc, axis=-1, keepdims=True)
    return xc * lax.rsqrt(var + EPS)


def _mod_rows(mod_ref, idx):
    row = jnp.where(pl.program_id(0) == 0, 1, 0)
    return mod_ref[pl.ds(row, 1), idx * D:(idx + 1) * D]


def _rms_rope_t(t, gain, cos, sin):
    ms = jnp.mean(t * t, axis=0, keepdims=True)
    t = t * lax.rsqrt(ms + EPS) * gain
    outs = []
    for a in range(2):
        t1 = t[32 * a:32 * a + 16]
        t2 = t[32 * a + 16:32 * a + 32]
        c = cos[16 * a:16 * a + 16]
        s = sin[16 * a:16 * a + 16]
        outs.append(t1 * c - t2 * s)
        outs.append(t2 * c + t1 * s)
    return jnp.concatenate(outs, axis=0)


def _stream_tile(x_ref, ctx_ref):
    if ctx_ref is None:
        return x_ref[...]
    return jnp.where(pl.program_id(0) == 0, ctx_ref[...], x_ref[...])


def _stream_specs(xa, ctx):
    if ctx is None:
        return [pl.BlockSpec((TM, D), lambda i: (i, 0))], [xa]
    return [pl.BlockSpec((TM, D), lambda i: (jnp.maximum(i - 1, 0), 0)), _full((TM, D))], [xa, ctx]


def _inproj_kernel(*refs, split_stream):
    x_ref, ctx_ref = (refs[0], refs[1]) if split_stream else (refs[0], None)
    (mod_ref, wt_ref, wn_ref, qg_ref, kg_ref, cos_ref, sin_ref,
     q_ref, k_ref, v_ref, kmax_ref, u_ref, ag_ref) = refs[2 if split_stream else 1:]
    x = _stream_tile(x_ref, ctx_ref)
    h = _ln_plain(x) * (1.0 + _mod_rows(mod_ref, 1)) + _mod_rows(mod_ref, 0)
    hb = h.astype(jnp.bfloat16)
    qkv_t = lax.dot_general(wt_ref[...], hb, (((1,), (1,)), ((), ())),
                            preferred_element_type=jnp.float32)
    cos = cos_ref[...]
    sin = sin_ref[...]
    qg = qg_ref[...]
    kg = kg_ref[...]
    ones_row = jnp.where(lax.broadcasted_iota(jnp.int32, (HD, TM), 0) == 0, 1.0, 0.0)
    ones_pad = jnp.where(lax.broadcasted_iota(jnp.int32, (VROWS - HD, TM), 0) == 0, 1.0, 0.0)
    for hq in range(NQ):
        t = _rms_rope_t(qkv_t[hq * HD:(hq + 1) * HD], qg, cos, sin) * (LOG2E * HD ** -0.5)
        nq = jnp.sqrt(jnp.sum(t * t, axis=0, keepdims=True))
        q_ref[hq // QPK, hq % QPK] = jnp.concatenate(
            [t, ones_row * (-BOUND_MARGIN * nq)], axis=0).astype(jnp.bfloat16)
    for g in range(NKV):
        kt = _rms_rope_t(qkv_t[ATTN_W + g * HD:ATTN_W + (g + 1) * HD], kg, cos, sin)
        kn = jnp.max(jnp.sum(kt * kt, axis=0, keepdims=True), axis=1, keepdims=True)
        kmax_ref[0, g] = jnp.broadcast_to(kn, (SUBLANES, LANES))
        k_ref[g] = jnp.concatenate([kt, ones_row], axis=0).T.astype(jnp.bfloat16)
        vt = qkv_t[ATTN_W + KV_W + g * HD:ATTN_W + KV_W + (g + 1) * HD]
        v_ref[g] = jnp.concatenate([vt, ones_pad], axis=0).astype(jnp.bfloat16)
    uag = jnp.dot(hb, wn_ref[...], preferred_element_type=jnp.float32)
    u_ref[...] = uag[:, :POOL_W]
    a = uag[:, POOL_W:POOL_W + CONV_W]
    gt = uag[:, POOL_W + CONV_W:]
    ag_ref[...] = a * jax.nn.sigmoid(gt)


def _inproj(xa, ctx, mod, w_t, w_n, qg, kg, cos_t, sin_t, l):
    ttot = xa.shape[0] + (0 if ctx is None else ctx.shape[0])
    nt = ttot // TM
    stream_specs, stream_args = _stream_specs(xa, ctx)
    return pl.pallas_call(
        functools.partial(_inproj_kernel, split_stream=ctx is not None),
        grid=(nt,),
        in_specs=stream_specs + [
            _layer((SUBLANES, N_MOD * D), l),
            _layer((ATTN_W + 2 * KV_W, D), l),
            _layer((D, POOL_W + 2 * CONV_W), l),
            _layer((HD, TM), l),
            _layer((HD, TM), l),
            pl.BlockSpec((HD // 2, TM), lambda i: (0, i)),
            pl.BlockSpec((HD // 2, TM), lambda i: (0, i)),
        ],
        out_specs=[
            pl.BlockSpec((NKV, QPK, 2 * HD, TM), lambda i: (0, 0, 0, i)),
            pl.BlockSpec((NKV, TM, 2 * HD), lambda i: (0, i, 0)),
            pl.BlockSpec((NKV, VROWS, TM), lambda i: (0, 0, i)),
            pl.BlockSpec((1, NKV, 8, LANES), lambda i: (i, 0, 0, 0)),
            pl.BlockSpec((TM, POOL_W), lambda i: (i, 0)),
            pl.BlockSpec((TM, CONV_W), lambda i: (i, 0)),
        ],
        out_shape=[
            jax.ShapeDtypeStruct((NKV, QPK, 2 * HD, ttot), jnp.bfloat16),
            jax.ShapeDtypeStruct((NKV, ttot, 2 * HD), jnp.bfloat16),
            jax.ShapeDtypeStruct((NKV, VROWS, ttot), jnp.bfloat16),
            jax.ShapeDtypeStruct((nt, NKV, 8, LANES), jnp.float32),
            jax.ShapeDtypeStruct((ttot, POOL_W), jnp.float32),
            jax.ShapeDtypeStruct((ttot, CONV_W), jnp.float32),
        ],
        compiler_params=_cparams(("arbitrary",)),
        name="inproj",
    )(*stream_args, mod, w_t, w_n, qg, kg, cos_t, sin_t)


TK = 256
GCH = 5
LOOP_STEPS = 4
L_MIN = 2.0 ** -60


def _attn_kernel(kmax_ref, q_ref, k_ref, v_ref, o_ref, qs_ref, s_ref, acc_ref, m_ref, *,
                 n_chunks):
    g = pl.program_id(0)
    i = pl.program_id(1)
    nch = jnp.where(i == 0, 1, n_chunks)
    row = lax.broadcasted_iota(jnp.int32, (2 * HD, TM), 0)
    kmax = kmax_ref[g]

    for hh in range(QPK):
        qf = q_ref[0, hh].astype(jnp.float32)
        qs_ref[hh] = jnp.where(row == HD, qf * kmax, qf).astype(jnp.bfloat16)
    acc_ref[...] = jnp.zeros(acc_ref.shape, jnp.float32)

    def step(nxt, cur):
        for kk in range(max(grp[1] for grp in (nxt, cur) if grp is not None)):
            do_nxt = nxt is not None and kk < nxt[1]
            do_cur = cur is not None and kk < cur[1]
            rows = slice(kk * TK, (kk + 1) * TK)
            if do_nxt:
                kc = k_ref[0, pl.ds(pl.multiple_of((nxt[0] + kk) * TK, TK), TK), :]
            if do_cur:
                vc = v_ref[0, :, pl.ds(pl.multiple_of((cur[0] + kk) * TK, TK), TK)]
            for hh in range(QPK):
                if do_nxt:
                    s_ref[nxt[2], hh, rows, :] = jnp.dot(
                        kc, qs_ref[hh], preferred_element_type=jnp.float32)
                if do_cur:
                    p = jnp.exp2(s_ref[cur[2], hh, rows, :]).astype(jnp.bfloat16)
                    acc_ref[hh] += jnp.dot(vc, p, preferred_element_type=jnp.float32)

    @pl.when(i == 0)
    def _():
        kc = k_ref[0, 0:TM, :]
        vc = v_ref[0, :, 0:TM]
        for hh in range(QPK):
            p = jnp.exp2(jnp.dot(kc, qs_ref[hh], preferred_element_type=jnp.float32))
            acc_ref[hh] = jnp.dot(vc, p.astype(jnp.bfloat16), preferred_element_type=jnp.float32)

    @pl.when(i > 0)
    def _():
        n_mid = (n_chunks - 2) // GCH
        tail = n_chunks - 1 - GCH * n_mid
        sizes = [1] + [GCH] * n_mid + ([tail - 1, 1] if tail >= 2 else [tail])
        firsts = [sum(sizes[:n]) for n in range(len(sizes))]

        def group(n):
            return (firsts[n], sizes[n], n % 2)

        step(group(0), None)
        step(group(1), group(0))
        n_loop = max(n_mid - 1, 0) // LOOP_STEPS

        def body(j, carry):
            c0 = 1 + LOOP_STEPS * GCH * j
            for u in range(LOOP_STEPS):
                step((c0 + (u + 1) * GCH, GCH, u % 2), (c0 + u * GCH, GCH, (u + 1) % 2))
            return carry

        lax.fori_loop(0, n_loop, body, 0)
        for n in range(1 + LOOP_STEPS * n_loop, len(sizes) - 1):
            step(group(n + 1), group(n))
        step(None, group(len(sizes) - 1))

    l_min = jnp.min(acc_ref[:, HD:HD + 1, :])

    @pl.when(l_min < L_MIN)
    def _():
        for hh in range(QPK):
            qs_ref[hh] = jnp.where(row == HD, 0.0, q_ref[0, hh].astype(jnp.float32)
                                   ).astype(jnp.bfloat16)
        acc_ref[...] = jnp.zeros(acc_ref.shape, jnp.float32)
        m_ref[...] = jnp.full(m_ref.shape, NEG_BIG, jnp.float32)

        def body2(c, carry):
            start = pl.multiple_of(c * TK, TK)
            kc = k_ref[0, pl.ds(start, TK), :]
            vc = v_ref[0, :, pl.ds(start, TK)]
            for hh in range(QPK):
                s = jnp.dot(kc, qs_ref[hh], preferred_element_type=jnp.float32)
                m_old = m_ref[hh]
                m_new = jnp.maximum(m_old, jnp.max(s, axis=0, keepdims=True))
                p = jnp.exp2(s - m_new).astype(jnp.bfloat16)
                acc_ref[hh] = jnp.exp2(m_old - m_new) * acc_ref[hh] + jnp.dot(
                    vc, p, preferred_element_type=jnp.float32)
                m_ref[hh] = m_new
            return carry

        lax.fori_loop(0, nch, body2, 0)

    outs = [acc_ref[hh, 0:HD, :] / acc_ref[hh, HD:HD + 1, :] for hh in range(QPK)]
    o_ref[...] = jnp.concatenate(outs, axis=0).T


def _attention(q_t, k2, v_t, kmax):
    ttot = k2.shape[1]
    nt = ttot // TM
    assert ttot % TK == 0 and ttot // TK >= 3
    kern = functools.partial(_attn_kernel, n_chunks=ttot // TK)
    grid_spec = pltpu.PrefetchScalarGridSpec(
        num_scalar_prefetch=1,
        grid=(NKV, nt),
        in_specs=[
            pl.BlockSpec((1, QPK, 2 * HD, TM), lambda g, i, km: (g, 0, 0, i)),
            pl.BlockSpec((1, ttot, 2 * HD), lambda g, i, km: (g, 0, 0)),
            pl.BlockSpec((1, VROWS, ttot), lambda g, i, km: (g, 0, 0)),
        ],
        out_specs=pl.BlockSpec((TM, QPK * HD), lambda g, i, km: (i, g)),
        scratch_shapes=[
            pltpu.VMEM((QPK, 2 * HD, TM), jnp.bfloat16),
            pltpu.VMEM((2, QPK, GCH * TK, TM), jnp.float32),
            pltpu.VMEM((QPK, VROWS, TM), jnp.float32),
            pltpu.VMEM((QPK, 1, TM), jnp.float32),
        ],
    )
    return pl.pallas_call(
        kern,
        grid_spec=grid_spec,
        out_shape=jax.ShapeDtypeStruct((ttot, ATTN_W), jnp.float32),
        compiler_params=_cparams(("arbitrary", "arbitrary")),
        name="attention",
    )(kmax, q_t, k2, v_t)


def _fill_ext(ext_ref, cur_ref, prev_ref, next_ref, left_ok, right_ok):
    zero = jnp.zeros((HALO, cur_ref.shape[1]), jnp.float32)
    ext_ref[0:HALO, :] = jnp.where(left_ok, prev_ref[...], zero)
    ext_ref[HALO:HALO + TM, :] = cur_ref[...]
    ext_ref[HALO + TM:HALO + TM + HALO, :] = jnp.where(right_ok, next_ref[...], zero)


def _mix_kernel(*refs, alpha, n_lat, split_stream):
    x_ref, ctx_ref = (refs[0], refs[1]) if split_stream else (refs[0], None)
    (attn_ref, u_ref, up_ref, un_ref, ag_ref, agp_ref, agn_ref, mod_ref,
     wpool_ref, pscale_ref, wdw_ref, bdw_ref, cvg_ref, cvb_ref, wpw_ref, wout_ref,
     ln1g_ref, ln1b_ref, wr_ref, br_ref,
     x1_ref, h2_ref, route_ref, cnt_ref,
     uext_ref, agext_ref, agsh_ref, base_ref) = refs[2 if split_stream else 1:]
    i = pl.program_id(0)
    nt = pl.num_programs(0)
    is_ctx = i == 0
    left_ok = i >= 2
    right_ok = jnp.logical_and(i >= 1, i < nt - 1)

    @pl.when(i == 0)
    def _():
        base_ref[...] = jnp.zeros(base_ref.shape, jnp.float32)

    _fill_ext(uext_ref, u_ref, up_ref, un_ref, left_ok, right_ok)
    _fill_ext(agext_ref, ag_ref, agp_ref, agn_ref, left_ok, right_ok)

    def ush(off, rows=TM):
        return uext_ref[HALO + off:HALO + off + rows, :]

    a2 = ush(-8, TM + 15) + ush(-7, TM + 15)
    a4 = a2[0:TM + 13] + a2[2:TM + 15]
    a8 = a4[0:TM + 9] + a4[4:TM + 13]
    a16 = a8[0:TM] + a8[8:TM + 8]
    sums = (a2[7:7 + TM], a4[6:6 + TM], a8[4:4 + TM], a16)
    pos = lax.broadcasted_iota(jnp.int32, (TM, 1), 0) + jnp.where(is_ctx, 0, (i - 1) * TM)
    seq_n = jnp.where(is_ctx, TM, n_lat)
    u_cur = u_ref[...]
    lane = lax.broadcasted_iota(jnp.int32, (TM, POOL_W), 1)
    pooled = jnp.zeros((TM, POOL_W), jnp.float32)
    for gi, w in enumerate(POOL_WINDOWS):
        lo = jnp.maximum(pos - w // 2, 0)
        hi = jnp.minimum(pos + (w - w // 2), seq_n)
        inv = 1.0 / (hi - lo).astype(jnp.float32)
        in_group = jnp.logical_and(lane >= gi * POOL_GW, lane < (gi + 1) * POOL_GW)
        pooled = jnp.where(in_group, sums[gi] * inv - u_cur, pooled)
    y_pool = jnp.dot(pooled.astype(jnp.bfloat16), wpool_ref[...],
                     preferred_element_type=jnp.float32) * pscale_ref[...]

    sh_rows = TM + 2 * HALO - SUBLANES
    for b in range(1, SUBLANES):
        agsh_ref[b - 1] = agext_ref[b:b + sh_rows, :]
    conv = jnp.zeros((TM, CONV_W), jnp.float32) + bdw_ref[...]
    for k in range(CONV_K):
        off = HALO + k - CONV_K // 2
        b, a8 = off % SUBLANES, off // SUBLANES * SUBLANES
        tap = agext_ref[a8:a8 + TM, :] if b == 0 else agsh_ref[b - 1, a8:a8 + TM, :]
        conv = conv + tap * wdw_ref[k:k + 1, :]
    cn = _ln_plain(conv) * cvg_ref[...] + cvb_ref[...]
    cn = cn * jax.nn.sigmoid(cn)
    y_conv = jnp.dot(cn.astype(jnp.bfloat16), wpw_ref[...], preferred_element_type=jnp.float32)

    y = jnp.dot(attn_ref[...].astype(jnp.bfloat16), wout_ref[0:ATTN_W, :],
                preferred_element_type=jnp.float32)
    y = y + jnp.dot(y_pool.astype(jnp.bfloat16), wout_ref[ATTN_W:ATTN_W + POOL_W, :],
                    preferred_element_type=jnp.float32)
    y = y + jnp.dot(y_conv.astype(jnp.bfloat16), wout_ref[ATTN_W + POOL_W:, :],
                    preferred_element_type=jnp.float32)
    x_res = _stream_tile(x_ref, ctx_ref)
    x1 = _ln_plain(alpha * x_res + _mod_rows(mod_ref, 2) * y) * ln1g_ref[...] + ln1b_ref[...]
    x1_ref[...] = x1
    h2 = _ln_plain(x1) * (1.0 + _mod_rows(mod_ref, 4)) + _mod_rows(mod_ref, 3)
    h_hi = h2.astype(jnp.bfloat16)
    h_rounded = h_hi.astype(jnp.float32)
    h_lo = (h2 - h_rounded).astype(jnp.bfloat16)
    bits = pltpu.bitcast(h_rounded, jnp.uint32)
    h2_ref[...] = bits[:, D // 2:] | lax.shift_right_logical(bits[:, :D // 2], jnp.uint32(16))
    hw = jnp.dot(h_hi, wr_ref[...], preferred_element_type=jnp.float32)
    logits = (hw[:, :LANES] + hw[:, LANES:] + br_ref[...]
              + jnp.dot(h_lo, wr_ref[:, :LANES], preferred_element_type=jnp.float32))
    ln = lax.broadcasted_iota(jnp.int32, (TM, LANES), 1)
    lnf = ln.astype(jnp.float32)
    no_lane = jnp.float32(LANES)
    neg = jnp.float32(-jnp.inf)
    is_g = ln < N_GROUPS
    lg = jnp.where(is_g, logits, neg)
    mg = jnp.max(lg, axis=-1, keepdims=True)
    gsel = jnp.min(jnp.where(lg == mg, lnf, no_lane), axis=-1, keepdims=True)
    p_group = 1.0 / jnp.sum(jnp.exp(lg - mg), axis=-1, keepdims=True)
    e_lane = lnf - N_GROUPS
    in_sel = jnp.logical_and(e_lane >= gsel * EPG, e_lane < (gsel + 1) * EPG)
    le = jnp.where(in_sel, logits, neg)
    v1 = jnp.max(le, axis=-1, keepdims=True)
    e1 = jnp.min(jnp.where(le == v1, e_lane, no_lane), axis=-1, keepdims=True)
    le2 = jnp.where(e_lane == e1, neg, le)
    v2 = jnp.max(le2, axis=-1, keepdims=True)
    e2 = jnp.min(jnp.where(le2 == v2, e_lane, no_lane), axis=-1, keepdims=True)
    ex = jnp.exp(v2 - v1)
    gate1 = p_group / (1.0 + ex)
    gate2 = p_group * ex / (1.0 + ex)

    oh = jnp.logical_or(lnf == e1, lnf == e2)
    ohb = jnp.where(oh, 1.0, 0.0).astype(jnp.bfloat16)
    r_i = lax.broadcasted_iota(jnp.int32, (TM, TM), 0)
    c_i = lax.broadcasted_iota(jnp.int32, (TM, TM), 1)
    tri = jnp.where(c_i < r_i, 1.0, 0.0).astype(jnp.bfloat16)
    before = jnp.dot(tri, ohb, preferred_element_type=jnp.float32) + base_ref[...]
    rank1 = jnp.sum(jnp.where(lnf == e1, before, 0.0), axis=-1, keepdims=True)
    rank2 = jnp.sum(jnp.where(lnf == e2, before, 0.0), axis=-1, keepdims=True)
    base_ref[...] = base_ref[...] + jnp.sum(ohb.astype(jnp.float32), axis=0, keepdims=True)
    cnt_ref[...] = jnp.broadcast_to(base_ref[...], cnt_ref.shape)

    route = jnp.zeros((TM, LANES), jnp.float32)
    for j, val in enumerate((e1, e2, rank1, rank2, gate1, gate2)):
        route = jnp.where(ln == j, val, route)
    route_ref[...] = route


def _mixer(xa, ctx, attn, u, ag, mod, wpool, pscale, wdw, bdw, cvg, cvb, wpw, wout, ln1g, ln1b,
           wr, br, alpha, l):
    ttot = attn.shape[0]
    nt = ttot // TM
    stream_specs, stream_args = _stream_specs(xa, ctx)
    hb = TM // HALO
    nhb = ttot // HALO

    def tile(w):
        return pl.BlockSpec((TM, w), lambda i: (i, 0))

    def prev(w):
        return pl.BlockSpec((HALO, w), lambda i: (jnp.maximum(i * hb - 1, 0), 0))

    def nxt(w):
        return pl.BlockSpec((HALO, w), lambda i: (jnp.minimum((i + 1) * hb, nhb - 1), 0))

    kern = functools.partial(_mix_kernel, alpha=alpha, n_lat=ttot - TM,
                             split_stream=ctx is not None)
    return pl.pallas_call(
        kern,
        grid=(nt,),
        in_specs=stream_specs + [
            tile(ATTN_W),
            tile(POOL_W), prev(POOL_W), nxt(POOL_W),
            tile(CONV_W), prev(CONV_W), nxt(CONV_W),
            _layer((SUBLANES, N_MOD * D), l),
            _layer((POOL_W, POOL_W), l), _layer((1, POOL_W), l),
            _layer((CONV_K_PAD, CONV_W), l), _layer((1, CONV_W), l), _layer((1, CONV_W), l),
            _layer((1, CONV_W), l),
            _layer((CONV_W, CONV_W), l), _layer((D, D), l),
            _layer((1, D), l), _layer((1, D), l),
            _layer((D, 2 * LANES), l), _layer((1, LANES), l),
        ],
        out_specs=[tile(D), tile(D // 2), tile(LANES), _full((SUBLANES, LANES))],
        out_shape=[
            jax.ShapeDtypeStruct((ttot, D), jnp.float32),
            jax.ShapeDtypeStruct((ttot, D // 2), jnp.uint32),
            jax.ShapeDtypeStruct((ttot, LANES), jnp.float32),
            jax.ShapeDtypeStruct((SUBLANES, LANES), jnp.float32),
        ],
        scratch_shapes=[
            pltpu.VMEM((TM + 2 * HALO, POOL_W), jnp.float32),
            pltpu.VMEM((TM + 2 * HALO, CONV_W), jnp.float32),
            pltpu.VMEM((SUBLANES - 1, TM + 2 * HALO - SUBLANES, CONV_W), jnp.float32),
            pltpu.VMEM((1, LANES), jnp.float32),
        ],
        compiler_params=_cparams(("arbitrary",)),
        name="mixer",
    )(*stream_args, attn, u, u, u, ag, ag, ag, mod, wpool, pscale, wdw, bdw, cvg, cvb, wpw, wout,
      ln1g, ln1b, wr, br)


ROW_GROUP = SUBLANES


def _dispatch_kernel(pends_ref, d1_ref, d2_ref, h_hbm, xs_ref, zero_buf, h_buf, lsems, ssems,
                     zsem, *, n_blk):
    i = pl.program_id(0)
    nt = pl.num_programs(0)
    slot = i % 3

    def load(t, to_slot):
        return pltpu.make_async_copy(h_hbm.at[pl.ds(pl.multiple_of(t * TM, TM), TM), :],
                                     h_buf.at[to_slot], lsems.at[to_slot])

    def wait_scatter(of_slot):
        for _ in range(2):
            pltpu.make_async_copy(h_buf.at[of_slot], xs_ref.at[pl.ds(0, TM), :],
                                  ssems.at[of_slot]).wait()

    @pl.when(i == 0)
    def _():
        load(0, 0).start()

        @pl.when(nt > 1)
        def _():
            load(1, 1).start()

        zero_buf[...] = jnp.zeros(zero_buf.shape, zero_buf.dtype)

        def zero_block(first_row):
            return pltpu.make_async_copy(
                zero_buf, xs_ref.at[pl.ds(pl.multiple_of(first_row, EBLK), EBLK), :], zsem)

        def seg_nonempty(e):
            return pends_ref[e] > (pends_ref[e - 1] if e > 0 else 0)

        for e in range(N_EXPERTS):
            @pl.when(seg_nonempty(e))
            def _():
                zero_block(pends_ref[e] - EBLK).start()
        n_used = pends_ref[N_EXPERTS - 1] // EBLK

        def tail_start(b, carry):
            zero_block(b * EBLK).start()
            return carry

        lax.fori_loop(n_used, n_blk, tail_start, 0)
        for e in range(N_EXPERTS):
            @pl.when(seg_nonempty(e))
            def _():
                zero_block(pends_ref[e] - EBLK).wait()

        def tail_wait(b, carry):
            zero_block(b * EBLK).wait()
            return carry

        lax.fori_loop(n_used, n_blk, tail_wait, 0)

    load(i, slot).wait()

    def start(r8, carry):
        base = pl.multiple_of(r8 * ROW_GROUP, ROW_GROUP)
        rows = h_buf.at[slot, pl.ds(base, ROW_GROUP), :]
        for j in range(ROW_GROUP):
            for d_ref in (d1_ref, d2_ref):
                pltpu.make_async_copy(rows.at[pl.ds(j, 1), :],
                                      xs_ref.at[pl.ds(d_ref[0, 0, base + j], 1), :],
                                      ssems.at[slot]).start()
        return carry

    lax.fori_loop(0, TM // ROW_GROUP, start, 0)

    @pl.when(i >= 1)
    def _():
        wait_scatter((i + 2) % 3)

    @pl.when(i + 2 < nt)
    def _():
        load(i + 2, (i + 2) % 3).start()

    @pl.when(i == nt - 1)
    def _():
        wait_scatter(slot)


def _dispatch(h2, d1, d2, pends, n_slots):
    ttot = h2.shape[0]
    nt = ttot // TM
    smem = pl.BlockSpec((1, 1, TM), lambda i, pe: (i, 0, 0), memory_space=pltpu.SMEM)
    grid_spec = pltpu.PrefetchScalarGridSpec(
        num_scalar_prefetch=1,
        grid=(nt,),
        in_specs=[smem, smem, pl.BlockSpec(memory_space=pl.ANY)],
        out_specs=pl.BlockSpec(memory_space=pl.ANY),
        scratch_shapes=[pltpu.VMEM((EBLK, D // 2), jnp.uint32),
                        pltpu.VMEM((3, TM, D // 2), jnp.uint32),
                        pltpu.SemaphoreType.DMA((3,)), pltpu.SemaphoreType.DMA((3,)),
                        pltpu.SemaphoreType.DMA(())],
    )
    return pl.pallas_call(
        functools.partial(_dispatch_kernel, n_blk=n_slots // EBLK),
        grid_spec=grid_spec,
        out_shape=jax.ShapeDtypeStruct((n_slots, D // 2), jnp.uint32),
        compiler_params=_cparams(("arbitrary",)),
        name="dispatch",
    )(pends, d1, d2, h2)


def _expert_kernel(be_ref, ne_ref, nu_ref, x_ref, wg_hbm, wu_hbm, wd_hbm, y_ref,
                   wg_f, wu_f, wd_f, wgu_b, wd_b, seg_ref, sems, *, l):
    b = pl.program_id(0)
    e = be_ref[b]
    new_expert = jnp.logical_or(b == 0, e != be_ref[jnp.maximum(b - 1, 0)])

    def fetch(expert, slot):
        return [pltpu.make_async_copy(src.at[l, expert], dst.at[slot], sems.at[slot])
                for src, dst in ((wg_hbm, wg_f), (wu_hbm, wu_f), (wd_hbm, wd_f))]

    @pl.when(b == 0)
    def _():
        seg_ref[0] = 0
        for cp in fetch(e, 0):
            cp.start()

    @pl.when(jnp.logical_and(b < nu_ref[0], new_expert))
    def _():
        seg = seg_ref[0]
        slot = seg % 2
        for cp in fetch(e, slot):
            cp.wait()

        @pl.when(ne_ref[b] != e)
        def _():
            for cp in fetch(ne_ref[b], 1 - slot):
                cp.start()

        wgu_b[:, :D_EXPERT] = wg_f[slot].astype(jnp.bfloat16)
        wgu_b[:, D_EXPERT:] = wu_f[slot].astype(jnp.bfloat16)
        wd_b[...] = wd_f[slot].astype(jnp.bfloat16)
        seg_ref[0] = seg + 1

    @pl.when(b < nu_ref[0])
    def _():
        xp = x_ref[...]
        x_lo = pltpu.bitcast(lax.shift_left(xp, jnp.uint32(16)), jnp.float32).astype(jnp.bfloat16)
        x_hi = pltpu.bitcast(xp & jnp.uint32(0xFFFF0000), jnp.float32).astype(jnp.bfloat16)
        gu = (jnp.dot(x_lo, wgu_b[:D // 2, :], preferred_element_type=jnp.float32)
              + jnp.dot(x_hi, wgu_b[D // 2:, :], preferred_element_type=jnp.float32))
        gt = gu[:, :D_EXPERT]
        hm = (gt * jax.nn.sigmoid(gt) * gu[:, D_EXPERT:]).astype(jnp.bfloat16)
        y_ref[...] = jnp.dot(hm, wd_b[...], preferred_element_type=jnp.float32)

    @pl.when(b >= nu_ref[0])
    def _():
        y_ref[...] = jnp.zeros(y_ref.shape, jnp.float32)


def _experts(xs, blk_e, nxt_e, n_used, wg, wu, wd, l):
    n_slots = xs.shape[0]
    nblk = n_slots // EBLK

    def row_map(b, be, ne, nu):
        return (jnp.minimum(b, nu[0] - 1), 0)

    hbm = pl.BlockSpec(memory_space=pl.ANY)
    grid_spec = pltpu.PrefetchScalarGridSpec(
        num_scalar_prefetch=3,
        grid=(nblk,),
        in_specs=[pl.BlockSpec((EBLK, D // 2), row_map), hbm, hbm, hbm],
        out_specs=pl.BlockSpec((EBLK, D), lambda b, be, ne, nu: (b, 0)),
        scratch_shapes=[pltpu.VMEM((2, D, D_EXPERT), jnp.float32),
                        pltpu.VMEM((2, D, D_EXPERT), jnp.float32),
                        pltpu.VMEM((2, D_EXPERT, D), jnp.float32),
                        pltpu.VMEM((D, 2 * D_EXPERT), jnp.bfloat16),
                        pltpu.VMEM((D_EXPERT, D), jnp.bfloat16),
                        pltpu.SMEM((1,), jnp.int32),
                        pltpu.SemaphoreType.DMA((2,))],
    )
    return pl.pallas_call(
        functools.partial(_expert_kernel, l=l),
        grid_spec=grid_spec,
        out_shape=jax.ShapeDtypeStruct((n_slots, D), jnp.float32),
        compiler_params=_cparams(("arbitrary",)),
        name="experts",
    )(blk_e, nxt_e, n_used, xs, wg, wu, wd)


def _combine_kernel(d1_ref, d2_ref, d1n_ref, d2n_ref, x1_ref, route_ref, mod_ref, ln2g_ref,
                    ln2b_ref, ys_ref, o_ref, y_buf, sems, *, alpha):
    i = pl.program_id(0)
    nt = pl.num_programs(0)
    slot = i % 2

    def gather(da_ref, db_ref, to_slot):
        def start(r8, carry):
            base = pl.multiple_of(r8 * ROW_GROUP, ROW_GROUP)
            for k, d_ref in enumerate((da_ref, db_ref)):
                rows = y_buf.at[to_slot, k, pl.ds(base, ROW_GROUP), :]
                for j in range(ROW_GROUP):
                    pltpu.make_async_copy(ys_ref.at[pl.ds(d_ref[0, 0, base + j], 1), :],
                                          rows.at[pl.ds(j, 1), :], sems.at[to_slot]).start()
            return carry

        lax.fori_loop(0, TM // ROW_GROUP, start, 0)

    @pl.when(i == 0)
    def _():
        gather(d1_ref, d2_ref, 0)

    @pl.when(i < nt - 1)
    def _():
        gather(d1n_ref, d2n_ref, 1 - slot)

    for k in range(2):
        pltpu.make_async_copy(ys_ref.at[pl.ds(0, TM), :], y_buf.at[slot, k], sems.at[slot]).wait()

    route = route_ref[...]
    g1 = route[:, 4:5]
    g2 = route[:, 5:6]
    o = g1 * y_buf[slot, 0] + g2 * y_buf[slot, 1]
    z = alpha * x1_ref[...] + _mod_rows(mod_ref, 5) * o
    o_ref[...] = _ln_plain(z) * ln2g_ref[...] + ln2b_ref[...]


def _combine(x1, route, mod, ln2g, ln2b, ys, d1, d2, alpha, l, latents_only):
    ttot = x1.shape[0]
    nt = ttot // TM
    smem = pl.BlockSpec((1, 1, TM), lambda i: (i, 0, 0), memory_space=pltpu.SMEM)
    smem_next = pl.BlockSpec((1, 1, TM), lambda i: (jnp.minimum(i + 1, nt - 1), 0, 0),
                             memory_space=pltpu.SMEM)
    kern = functools.partial(_combine_kernel, alpha=alpha)
    if latents_only:
        out_spec = pl.BlockSpec((TM, D), lambda i: (jnp.maximum(i - 1, 0), 0))
        out_rows = ttot - TM
    else:
        out_spec = pl.BlockSpec((TM, D), lambda i: (i, 0))
        out_rows = ttot
    return pl.pallas_call(
        kern,
        grid=(nt,),
        in_specs=[smem, smem, smem_next, smem_next,
                  pl.BlockSpec((TM, D), lambda i: (i, 0)),
                  pl.BlockSpec((TM, LANES), lambda i: (i, 0)),
                  _layer((SUBLANES, N_MOD * D), l), _layer((1, D), l), _layer((1, D), l),
                  pl.BlockSpec(memory_space=pl.ANY)],
        out_specs=out_spec,
        out_shape=jax.ShapeDtypeStruct((out_rows, D), jnp.float32),
        scratch_shapes=[pltpu.VMEM((2, 2, TM, D), jnp.float32),
                        pltpu.SemaphoreType.DMA((2,))],
        compiler_params=_cparams(("arbitrary",)),
        name="combine",
    )(d1, d2, d1, d2, x1, route, mod, ln2g, ln2b, ys)


def _rope_tables_t(n_lat, n_ctx):
    rows = n_lat // GRID_W
    row = jnp.broadcast_to(jnp.arange(rows)[:, None], (rows, GRID_W)).reshape(-1)
    col = jnp.broadcast_to(jnp.arange(GRID_W)[None, :], (rows, GRID_W)).reshape(-1)
    n_freq = HD // 4
    inv = ROPE_THETA ** (-jnp.arange(n_freq, dtype=jnp.float32) / n_freq)
    pos = jnp.stack([row, col], axis=-1).astype(jnp.float32)
    ang = (pos[:, :, None] * inv).reshape(n_lat, 2 * n_freq)
    cos = jnp.concatenate([jnp.ones((n_ctx, 2 * n_freq), jnp.float32), jnp.cos(ang)], axis=0)
    sin = jnp.concatenate([jnp.zeros((n_ctx, 2 * n_freq), jnp.float32), jnp.sin(ang)], axis=0)
    return cos.T, sin.T


def kernel(x, c, ctx, c_ctx, w_mod, b_mod, w_in, q_gain, k_gain, w_pool, pool_scale, w_dw, b_dw,
           cv_ln_g, cv_ln_b, w_cv_pw, w_out, ln1_g, ln1_b, ln2_g, ln2_b, w_rg, b_rg, w_re, b_re,
           w_e_gate, w_e_up, w_e_down):
    depth = w_mod.shape[0]
    n_lat = x.shape[1]
    n_ctx = ctx.shape[1]
    assert x.shape[0] == 1 and n_ctx == TM and n_lat % TM == 0 and n_lat % GRID_W == 0
    ttot = n_ctx + n_lat
    alpha = float((2 * depth) ** 0.25)
    n_blk = -(-(2 * ttot) // EBLK) + N_EXPERTS
    n_slots = n_blk * EBLK

    xa, xctx = x[0], ctx[0]
    cc = jnp.concatenate([c, c_ctx[None], jnp.zeros((SUBLANES - 2, D), jnp.float32)], axis=0)
    mod = _modulation(cc, w_mod, b_mod)
    cos_t, sin_t = _rope_tables_t(n_lat, n_ctx)

    n_qkv = ATTN_W + 2 * KV_W
    bf = jnp.bfloat16
    w_t = jnp.swapaxes(w_in[:, :, :n_qkv], 1, 2).astype(bf)
    w_n = w_in[:, :, n_qkv:].astype(bf)
    qg = jnp.broadcast_to(q_gain[:, :, None], (depth, HD, TM))
    kg = jnp.broadcast_to(k_gain[:, :, None], (depth, HD, TM))
    n_pg = len(POOL_WINDOWS)
    wpool = (jnp.eye(n_pg, dtype=jnp.float32)[None, :, None, :, None] * w_pool[:, :, :, None, :]
             ).reshape(depth, POOL_W, POOL_W).astype(bf)
    wdw = jnp.pad(w_dw, ((0, 0), (0, CONV_K_PAD - CONV_K), (0, 0)))
    n_r = N_GROUPS + N_EXPERTS
    wr = jnp.concatenate([w_rg, w_re, jnp.zeros((depth, D, LANES - n_r), jnp.float32)], axis=2)
    br = jnp.concatenate([b_rg, b_re, jnp.zeros((depth, LANES - n_r), jnp.float32)], axis=1)
    wr_hi = wr.astype(bf)
    wr = jnp.concatenate([wr_hi, (wr - wr_hi.astype(jnp.float32)).astype(bf)], axis=2)
    wpw = w_cv_pw.astype(bf)
    wout = w_out.astype(bf)

    def rows(v):
        return v[:, None, :]

    e_ids = jnp.arange(N_EXPERTS, dtype=jnp.int32)
    blk_start = jnp.arange(n_blk, dtype=jnp.int32) * EBLK
    for l in range(depth):
        q_t, k2, v_t, kn2, u, ag = _inproj(xa, xctx, mod, w_t, w_n, qg, kg, cos_t, sin_t, l)
        kmax = jnp.sqrt(jnp.max(kn2[:, :, 0, 0], axis=0))
        attn = _attention(q_t, k2, v_t, kmax)
        x1, h2, route, cnt = _mixer(
            xa, xctx, attn, u, ag, mod, wpool, rows(pool_scale), wdw, rows(b_dw), rows(cv_ln_g),
            rows(cv_ln_b), wpw, wout, rows(ln1_g), rows(ln1_b), wr, rows(br), alpha, l)

        counts = cnt[0, :N_EXPERTS].astype(jnp.int32)
        padded = (counts + EBLK - 1) // EBLK * EBLK
        pends = jnp.cumsum(padded)
        pstarts = pends - padded
        ri = route[:, :4].astype(jnp.int32)

        def dest(e, r):
            start = jnp.sum(jnp.where(e[:, None] == e_ids[None, :], pstarts[None, :], 0), axis=1)
            return (start + r).reshape(ttot // TM, 1, TM)

        d1 = dest(ri[:, 0], ri[:, 2])
        d2 = dest(ri[:, 1], ri[:, 3])
        n_used = (pends[-1] // EBLK).astype(jnp.int32).reshape(1)
        first_row = jnp.minimum(blk_start, pends[-1] - 1)
        blk_e = jnp.sum((pends[None, :] <= first_row[:, None]).astype(jnp.int32), axis=1)
        blk_e = jnp.minimum(blk_e, N_EXPERTS - 1)
        seg_end = jnp.sum(jnp.where(blk_e[:, None] == e_ids[None, :], pends[None, :], 0), axis=1)
        nxt_row = jnp.minimum(seg_end, pends[-1] - 1)
        nxt_e = jnp.sum((pends[None, :] <= nxt_row[:, None]).astype(jnp.int32), axis=1)
        nxt_e = jnp.minimum(nxt_e, N_EXPERTS - 1)

        xs = _dispatch(h2, d1, d2, pends, n_slots)
        ys = _experts(xs, blk_e, nxt_e, n_used, w_e_gate, w_e_up, w_e_down, l)
        xa = _combine(x1, route, mod, rows(ln2_g), rows(ln2_b), ys, d1, d2, alpha, l,
                      latents_only=(l == depth - 1))
        xctx = None
    return xa[None]
```

```python
import functools

import jax
import jax.numpy as jnp
from jax import lax
from jax.experimental import pallas as pl
from jax.experimental.pallas import tpu as pltpu

D = 1024
GRID_W = 64
HD = 64
NQ = 8
NKV = 2
QPK = NQ // NKV
ATTN_W = NQ * HD
KV_W = NKV * HD
POOL_W = 256
POOL_WINDOWS = (2, 4, 8, 16)
POOL_GW = POOL_W // len(POOL_WINDOWS)
CONV_W = 256
CONV_K = 31
N_GROUPS = 4
EPG = 8
N_EXPERTS = N_GROUPS * EPG
D_EXPERT = 512
N_MOD = 6
ROPE_THETA = 10000.0
EPS = 1e-6

TM = 256
HALO = 16
EBLK = 256
LANES = 128
SUBLANES = 8
BF16_ROWS = 16
VROWS = HD + BF16_ROWS
CONV_K_PAD = 32
LOG2E = 1.4426950408889634
BOUND_MARGIN = 1.0 + 2.0 ** -5
V7X_VMEM_BYTES = 64 * 1024 * 1024
VMEM_LIMIT = V7X_VMEM_BYTES - 8 * 1024 * 1024
NEG_BIG = -1e30

_HI = lax.Precision.HIGHEST


def _cparams(sem):
    return pltpu.CompilerParams(dimension_semantics=sem, vmem_limit_bytes=VMEM_LIMIT)


def _full(shape):
    n = len(shape)
    return pl.BlockSpec(shape, lambda *a, n=n: (0,) * n)


def _layer(shape, l):
    n = len(shape)
    return pl.BlockSpec((None,) + tuple(shape), lambda *a, n=n: (l,) + (0,) * n)


def _mod_kernel(cc_ref, w_ref, b_ref, o_ref):
    cc = cc_ref[...]
    a = cc * jax.nn.sigmoid(cc)
    o_ref[0] = jnp.dot(a, w_ref[0], precision=_HI, preferred_element_type=jnp.float32) + b_ref[0]


def _modulation(cc, w_mod, b_mod):
    depth = w_mod.shape[0]
    nw = w_mod.shape[2]
    bn = D
    return pl.pallas_call(
        _mod_kernel,
        grid=(depth, nw // bn),
        in_specs=[
            pl.BlockSpec((SUBLANES, D), lambda l, j: (0, 0)),
            pl.BlockSpec((1, D, bn), lambda l, j: (l, 0, j)),
            pl.BlockSpec((1, 1, bn), lambda l, j: (l, 0, j)),
        ],
        out_specs=pl.BlockSpec((1, SUBLANES, bn), lambda l, j: (l, 0, j)),
        out_shape=jax.ShapeDtypeStruct((depth, SUBLANES, nw), jnp.float32),
        compiler_params=_cparams(("arbitrary", "arbitrary")),
        name="modulation",
    )(cc, w_mod, b_mod.reshape(depth, 1, nw))


def _ln_plain(x):
    mu = jnp.mean(x, axis=-1, keepdims=True)
    xc = x - mu
    var = jnp.mean(xc * xc, axis=-1, keepdims=True)
    return xc * lax.rsqrt(var + EPS)


def _mod_rows(mod_ref, idx):
    row = jnp.where(pl.program_id(0) == 0, 1, 0)
    return mod_ref[pl.ds(row, 1), idx * D:(idx + 1) * D]


def _rms_rope_t(t, gain, cos, sin):
    ms = jnp.mean(t * t, axis=0, keepdims=True)
    t = t * lax.rsqrt(ms + EPS) * gain
    outs = []
    for a in range(2):
        t1 = t[32 * a:32 * a + 16]
        t2 = t[32 * a + 16:32 * a + 32]
        c = cos[16 * a:16 * a + 16]
        s = sin[16 * a:16 * a + 16]
        outs.append(t1 * c - t2 * s)
        outs.append(t2 * c + t1 * s)
    return jnp.concatenate(outs, axis=0)


def _stream_tile(x_ref, ctx_ref):
    if ctx_ref is None:
        return x_ref[...]
    return jnp.where(pl.program_id(0) == 0, ctx_ref[...], x_ref[...])


def _stream_specs(xa, ctx):
    if ctx is None:
        return [pl.BlockSpec((TM, D), lambda i: (i, 0))], [xa]
    return [pl.BlockSpec((TM, D), lambda i: (jnp.maximum(i - 1, 0), 0)), _full((TM, D))], [xa, ctx]


def _inproj_kernel(*refs, split_stream):
    x_ref, ctx_ref = (refs[0], refs[1]) if split_stream else (refs[0], None)
    (mod_ref, wt_ref, wn_ref, qg_ref, kg_ref, cos_ref, sin_ref,
     q_ref, k_ref, v_ref, kmax_ref, u_ref, ag_ref) = refs[2 if split_stream else 1:]
    x = _stream_tile(x_ref, ctx_ref)
    h = _ln_plain(x) * (1.0 + _mod_rows(mod_ref, 1)) + _mod_rows(mod_ref, 0)
    hb = h.astype(jnp.bfloat16)
    qkv_t = lax.dot_general(wt_ref[...], hb, (((1,), (1,)), ((), ())),
                            preferred_element_type=jnp.float32)
    cos = cos_ref[...]
    sin = sin_ref[...]
    qg = qg_ref[...]
    kg = kg_ref[...]
    ones_row = jnp.where(lax.broadcasted_iota(jnp.int32, (HD, TM), 0) == 0, 1.0, 0.0)
    ones_pad = jnp.where(lax.broadcasted_iota(jnp.int32, (VROWS - HD, TM), 0) == 0, 1.0, 0.0)
    for hq in range(NQ):
        t = _rms_rope_t(qkv_t[hq * HD:(hq + 1) * HD], qg, cos, sin) * (LOG2E * HD ** -0.5)
        nq = jnp.sqrt(jnp.sum(t * t, axis=0, keepdims=True))
        q_ref[hq // QPK, hq % QPK] = jnp.concatenate(
            [t, ones_row * (-BOUND_MARGIN * nq)], axis=0).astype(jnp.bfloat16)
    for g in range(NKV):
        kt = _rms_rope_t(qkv_t[ATTN_W + g * HD:ATTN_W + (g + 1) * HD], kg, cos, sin)
        kn = jnp.max(jnp.sum(kt * kt, axis=0, keepdims=True), axis=1, keepdims=True)
        kmax_ref[0, g] = jnp.broadcast_to(kn, (SUBLANES, LANES))
        k_ref[g] = jnp.concatenate([kt, ones_row], axis=0).T.astype(jnp.bfloat16)
        vt = qkv_t[ATTN_W + KV_W + g * HD:ATTN_W + KV_W + (g + 1) * HD]
        v_ref[g] = jnp.concatenate([vt, ones_pad], axis=0).astype(jnp.bfloat16)
    uag = jnp.dot(hb, wn_ref[...], preferred_element_type=jnp.float32)
    u_ref[...] = uag[:, :POOL_W]
    a = uag[:, POOL_W:POOL_W + CONV_W]
    gt = uag[:, POOL_W + CONV_W:]
    ag_ref[...] = a * jax.nn.sigmoid(gt)


def _inproj(xa, ctx, mod, w_t, w_n, qg, kg, cos_t, sin_t, l):
    ttot = xa.shape[0] + (0 if ctx is None else ctx.shape[0])
    nt = ttot // TM
    stream_specs, stream_args = _stream_specs(xa, ctx)
    return pl.pallas_call(
        functools.partial(_inproj_kernel, split_stream=ctx is not None),
        grid=(nt,),
        in_specs=stream_specs + [
            _layer((SUBLANES, N_MOD * D), l),
            _layer((ATTN_W + 2 * KV_W, D), l),
            _layer((D, POOL_W + 2 * CONV_W), l),
            _layer((HD, TM), l),
            _layer((HD, TM), l),
            pl.BlockSpec((HD // 2, TM), lambda i: (0, i)),
            pl.BlockSpec((HD // 2, TM), lambda i: (0, i)),
        ],
        out_specs=[
            pl.BlockSpec((NKV, QPK, 2 * HD, TM), lambda i: (0, 0, 0, i)),
            pl.BlockSpec((NKV, TM, 2 * HD), lambda i: (0, i, 0)),
            pl.BlockSpec((NKV, VROWS, TM), lambda i: (0, 0, i)),
            pl.BlockSpec((1, NKV, 8, LANES), lambda i: (i, 0, 0, 0)),
            pl.BlockSpec((TM, POOL_W), lambda i: (i, 0)),
            pl.BlockSpec((TM, CONV_W), lambda i: (i, 0)),
        ],
        out_shape=[
            jax.ShapeDtypeStruct((NKV, QPK, 2 * HD, ttot), jnp.bfloat16),
            jax.ShapeDtypeStruct((NKV, ttot, 2 * HD), jnp.bfloat16),
            jax.ShapeDtypeStruct((NKV, VROWS, ttot), jnp.bfloat16),
            jax.ShapeDtypeStruct((nt, NKV, 8, LANES), jnp.float32),
            jax.ShapeDtypeStruct((ttot, POOL_W), jnp.float32),
            jax.ShapeDtypeStruct((ttot, CONV_W), jnp.float32),
        ],
        compiler_params=_cparams(("arbitrary",)),
        name="inproj",
    )(*stream_args, mod, w_t, w_n, qg, kg, cos_t, sin_t)


TK = 256
GCH = 5
LOOP_STEPS = 6
L_MIN = 2.0 ** -60


def _attn_kernel(kmax_ref, q_ref, k_ref, v_ref, o_ref, qs_ref, s_ref, acc_ref, m_ref, *,
                 n_chunks):
    g = pl.program_id(0)
    i = pl.program_id(1)
    nch = jnp.where(i == 0, 1, n_chunks)
    row = lax.broadcasted_iota(jnp.int32, (2 * HD, TM), 0)
    kmax = kmax_ref[g]

    for hh in range(QPK):
        qf = q_ref[0, hh].astype(jnp.float32)
        qs_ref[hh] = jnp.where(row == HD, qf * kmax, qf).astype(jnp.bfloat16)
    acc_ref[...] = jnp.zeros(acc_ref.shape, jnp.float32)

    def step(nxt, cur):
        for kk in range(max(grp[1] for grp in (nxt, cur) if grp is not None)):
            do_nxt = nxt is not None and kk < nxt[1]
            do_cur = cur is not None and kk < cur[1]
            rows = slice(kk * TK, (kk + 1) * TK)
            if do_nxt:
                kc = k_ref[0, pl.ds(pl.multiple_of((nxt[0] + kk) * TK, TK), TK), :]
            if do_cur:
                vc = v_ref[0, :, pl.ds(pl.multiple_of((cur[0] + kk) * TK, TK), TK)]
            for hh in range(QPK):
                if do_nxt:
                    s_ref[nxt[2], hh, rows, :] = jnp.dot(
                        kc, qs_ref[hh], preferred_element_type=jnp.float32)
                if do_cur:
                    p = jnp.exp2(s_ref[cur[2], hh, rows, :]).astype(jnp.bfloat16)
                    acc_ref[hh] += jnp.dot(vc, p, preferred_element_type=jnp.float32)

    @pl.when(i == 0)
    def _():
        kc = k_ref[0, 0:TM, :]
        vc = v_ref[0, :, 0:TM]
        for hh in range(QPK):
            p = jnp.exp2(jnp.dot(kc, qs_ref[hh], preferred_element_type=jnp.float32))
            acc_ref[hh] = jnp.dot(vc, p.astype(jnp.bfloat16), preferred_element_type=jnp.float32)

    @pl.when(i > 0)
    def _():
        n_mid = (n_chunks - 2) // GCH
        tail = n_chunks - 1 - GCH * n_mid
        sizes = [1] + [GCH] * n_mid + ([tail - 1, 1] if tail >= 2 else [tail])
        firsts = [sum(sizes[:n]) for n in range(len(sizes))]

        def group(n):
            return (firsts[n], sizes[n], n % 2)

        step(group(0), None)
        step(group(1), group(0))
        n_loop = max(n_mid - 1, 0) // LOOP_STEPS

        def body(j, carry):
            c0 = 1 + LOOP_STEPS * GCH * j
            for u in range(LOOP_STEPS):
                step((c0 + (u + 1) * GCH, GCH, u % 2), (c0 + u * GCH, GCH, (u + 1) % 2))
            return carry

        lax.fori_loop(0, n_loop, body, 0)
        for n in range(1 + LOOP_STEPS * n_loop, len(sizes) - 1):
            step(group(n + 1), group(n))
        step(None, group(len(sizes) - 1))

    l_min = jnp.min(acc_ref[:, HD:HD + 1, :])

    @pl.when(l_min < L_MIN)
    def _():
        for hh in range(QPK):
            qs_ref[hh] = jnp.where(row == HD, 0.0, q_ref[0, hh].astype(jnp.float32)
                                   ).astype(jnp.bfloat16)
        acc_ref[...] = jnp.zeros(acc_ref.shape, jnp.float32)
        m_ref[...] = jnp.full(m_ref.shape, NEG_BIG, jnp.float32)

        def body2(c, carry):
            start = pl.multiple_of(c * TK, TK)
            kc = k_ref[0, pl.ds(start, TK), :]
            vc = v_ref[0, :, pl.ds(start, TK)]
            for hh in range(QPK):
                s = jnp.dot(kc, qs_ref[hh], preferred_element_type=jnp.float32)
                m_old = m_ref[hh]
                m_new = jnp.maximum(m_old, jnp.max(s, axis=0, keepdims=True))
                p = jnp.exp2(s - m_new).astype(jnp.bfloat16)
                acc_ref[hh] = jnp.exp2(m_old - m_new) * acc_ref[hh] + jnp.dot(
                    vc, p, preferred_element_type=jnp.float32)
                m_ref[hh] = m_new
            return carry

        lax.fori_loop(0, nch, body2, 0)

    outs = [acc_ref[hh, 0:HD, :] / acc_ref[hh, HD:HD + 1, :] for hh in range(QPK)]
    o_ref[...] = jnp.concatenate(outs, axis=0).T


def _attention(q_t, k2, v_t, kmax):
    ttot = k2.shape[1]
    nt = ttot // TM
    assert ttot % TK == 0 and ttot // TK >= 3
    kern = functools.partial(_attn_kernel, n_chunks=ttot // TK)
    grid_spec = pltpu.PrefetchScalarGridSpec(
        num_scalar_prefetch=1,
        grid=(NKV, nt),
        in_specs=[
            pl.BlockSpec((1, QPK, 2 * HD, TM), lambda g, i, km: (g, 0, 0, i)),
            pl.BlockSpec((1, ttot, 2 * HD), lambda g, i, km: (g, 0, 0)),
            pl.BlockSpec((1, VROWS, ttot), lambda g, i, km: (g, 0, 0)),
        ],
        out_specs=pl.BlockSpec((TM, QPK * HD), lambda g, i, km: (i, g)),
        scratch_shapes=[
            pltpu.VMEM((QPK, 2 * HD, TM), jnp.bfloat16),
            pltpu.VMEM((2, QPK, GCH * TK, TM), jnp.float32),
            pltpu.VMEM((QPK, VROWS, TM), jnp.float32),
            pltpu.VMEM((QPK, 1, TM), jnp.float32),
        ],
    )
    return pl.pallas_call(
        kern,
        grid_spec=grid_spec,
        out_shape=jax.ShapeDtypeStruct((ttot, ATTN_W), jnp.float32),
        compiler_params=_cparams(("arbitrary", "arbitrary")),
        name="attention",
    )(kmax, q_t, k2, v_t)


def _fill_ext(ext_ref, cur_ref, prev_ref, next_ref, left_ok, right_ok):
    zero = jnp.zeros((HALO, cur_ref.shape[1]), jnp.float32)
    ext_ref[0:HALO, :] = jnp.where(left_ok, prev_ref[...], zero)
    ext_ref[HALO:HALO + TM, :] = cur_ref[...]
    ext_ref[HALO + TM:HALO + TM + HALO, :] = jnp.where(right_ok, next_ref[...], zero)


def _mix_kernel(*refs, alpha, n_lat, split_stream):
    x_ref, ctx_ref = (refs[0], refs[1]) if split_stream else (refs[0], None)
    (attn_ref, u_ref, up_ref, un_ref, ag_ref, agp_ref, agn_ref, mod_ref,
     wpool_ref, pscale_ref, wdw_ref, bdw_ref, cvg_ref, cvb_ref, wpw_ref, wout_ref,
     ln1g_ref, ln1b_ref, wr_ref, br_ref,
     x1_ref, h2_ref, route_ref, cnt_ref,
     uext_ref, agext_ref, agsh_ref, base_ref) = refs[2 if split_stream else 1:]
    i = pl.program_id(0)
    nt = pl.num_programs(0)
    is_ctx = i == 0
    left_ok = i >= 2
    right_ok = jnp.logical_and(i >= 1, i < nt - 1)

    @pl.when(i == 0)
    def _():
        base_ref[...] = jnp.zeros(base_ref.shape, jnp.float32)

    _fill_ext(uext_ref, u_ref, up_ref, un_ref, left_ok, right_ok)
    _fill_ext(agext_ref, ag_ref, agp_ref, agn_ref, left_ok, right_ok)

    def ush(off, rows=TM):
        return uext_ref[HALO + off:HALO + off + rows, :]

    a2 = ush(-8, TM + 15) + ush(-7, TM + 15)
    a4 = a2[0:TM + 13] + a2[2:TM + 15]
    a8 = a4[0:TM + 9] + a4[4:TM + 13]
    a16 = a8[0:TM] + a8[8:TM + 8]
    sums = (a2[7:7 + TM], a4[6:6 + TM], a8[4:4 + TM], a16)
    pos = lax.broadcasted_iota(jnp.int32, (TM, 1), 0) + jnp.where(is_ctx, 0, (i - 1) * TM)
    seq_n = jnp.where(is_ctx, TM, n_lat)
    u_cur = u_ref[...]
    lane = lax.broadcasted_iota(jnp.int32, (TM, POOL_W), 1)
    pooled = jnp.zeros((TM, POOL_W), jnp.float32)
    for gi, w in enumerate(POOL_WINDOWS):
        lo = jnp.maximum(pos - w // 2, 0)
        hi = jnp.minimum(pos + (w - w // 2), seq_n)
        inv = 1.0 / (hi - lo).astype(jnp.float32)
        in_group = jnp.logical_and(lane >= gi * POOL_GW, lane < (gi + 1) * POOL_GW)
        pooled = jnp.where(in_group, sums[gi] * inv - u_cur, pooled)
    y_pool = jnp.dot(pooled.astype(jnp.bfloat16), wpool_ref[...],
                     preferred_element_type=jnp.float32) * pscale_ref[...]

    sh_rows = TM + 2 * HALO - SUBLANES
    for b in range(1, SUBLANES):
        agsh_ref[b - 1] = agext_ref[b:b + sh_rows, :]
    conv = jnp.zeros((TM, CONV_W), jnp.float32) + bdw_ref[...]
    for k in range(CONV_K):
        off = HALO + k - CONV_K // 2
        b, a8 = off % SUBLANES, off // SUBLANES * SUBLANES
        tap = agext_ref[a8:a8 + TM, :] if b == 0 else agsh_ref[b - 1, a8:a8 + TM, :]
        conv = conv + tap * wdw_ref[k:k + 1, :]
    cn = _ln_plain(conv) * cvg_ref[...] + cvb_ref[...]
    cn = cn * jax.nn.sigmoid(cn)
    y_conv = jnp.dot(cn.astype(jnp.bfloat16), wpw_ref[...], preferred_element_type=jnp.float32)

    y = jnp.dot(attn_ref[...].astype(jnp.bfloat16), wout_ref[0:ATTN_W, :],
                preferred_element_type=jnp.float32)
    y = y + jnp.dot(y_pool.astype(jnp.bfloat16), wout_ref[ATTN_W:ATTN_W + POOL_W, :],
                    preferred_element_type=jnp.float32)
    y = y + jnp.dot(y_conv.astype(jnp.bfloat16), wout_ref[ATTN_W + POOL_W:, :],
                    preferred_element_type=jnp.float32)
    x_res = _stream_tile(x_ref, ctx_ref)
    x1 = _ln_plain(alpha * x_res + _mod_rows(mod_ref, 2) * y) * ln1g_ref[...] + ln1b_ref[...]
    x1_ref[...] = x1
    h2 = _ln_plain(x1) * (1.0 + _mod_rows(mod_ref, 4)) + _mod_rows(mod_ref, 3)
    h_hi = h2.astype(jnp.bfloat16)
    h_rounded = h_hi.astype(jnp.float32)
    h_lo = (h2 - h_rounded).astype(jnp.bfloat16)
    bits = pltpu.bitcast(h_rounded, jnp.uint32)
    h2_ref[...] = bits[:, D // 2:] | lax.shift_right_logical(bits[:, :D // 2], jnp.uint32(16))
    hw = jnp.dot(h_hi, wr_ref[...], preferred_element_type=jnp.float32)
    logits = (hw[:, :LANES] + hw[:, LANES:] + br_ref[...]
              + jnp.dot(h_lo, wr_ref[:, :LANES], preferred_element_type=jnp.float32))
    ln = lax.broadcasted_iota(jnp.int32, (TM, LANES), 1)
    lnf = ln.astype(jnp.float32)
    no_lane = jnp.float32(LANES)
    neg = jnp.float32(-jnp.inf)
    is_g = ln < N_GROUPS
    lg = jnp.where(is_g, logits, neg)
    mg = jnp.max(lg, axis=-1, keepdims=True)
    gsel = jnp.min(jnp.where(lg == mg, lnf, no_lane), axis=-1, keepdims=True)
    p_group = 1.0 / jnp.sum(jnp.exp(lg - mg), axis=-1, keepdims=True)
    e_lane = lnf - N_GROUPS
    in_sel = jnp.logical_and(e_lane >= gsel * EPG, e_lane < (gsel + 1) * EPG)
    le = jnp.where(in_sel, logits, neg)
    v1 = jnp.max(le, axis=-1, keepdims=True)
    e1 = jnp.min(jnp.where(le == v1, e_lane, no_lane), axis=-1, keepdims=True)
    le2 = jnp.where(e_lane == e1, neg, le)
    v2 = jnp.max(le2, axis=-1, keepdims=True)
    e2 = jnp.min(jnp.where(le2 == v2, e_lane, no_lane), axis=-1, keepdims=True)
    ex = jnp.exp(v2 - v1)
    gate1 = p_group / (1.0 + ex)
    gate2 = p_group * ex / (1.0 + ex)

    oh = jnp.logical_or(lnf == e1, lnf == e2)
    ohb = jnp.where(oh, 1.0, 0.0).astype(jnp.bfloat16)
    r_i = lax.broadcasted_iota(jnp.int32, (TM, TM), 0)
    c_i = lax.broadcasted_iota(jnp.int32, (TM, TM), 1)
    tri = jnp.where(c_i < r_i, 1.0, 0.0).astype(jnp.bfloat16)
    before = jnp.dot(tri, ohb, preferred_element_type=jnp.float32) + base_ref[...]
    rank1 = jnp.sum(jnp.where(lnf == e1, before, 0.0), axis=-1, keepdims=True)
    rank2 = jnp.sum(jnp.where(lnf == e2, before, 0.0), axis=-1, keepdims=True)
    base_ref[...] = base_ref[...] + jnp.sum(ohb.astype(jnp.float32), axis=0, keepdims=True)
    cnt_ref[...] = jnp.broadcast_to(base_ref[...], cnt_ref.shape)

    route = jnp.zeros((TM, LANES), jnp.float32)
    for j, val in enumerate((e1, e2, rank1, rank2, gate1, gate2)):
        route = jnp.where(ln == j, val, route)
    route_ref[...] = route


def _mixer(xa, ctx, attn, u, ag, mod, wpool, pscale, wdw, bdw, cvg, cvb, wpw, wout, ln1g, ln1b,
           wr, br, alpha, l):
    ttot = attn.shape[0]
    nt = ttot // TM
    stream_specs, stream_args = _stream_specs(xa, ctx)
    hb = TM // HALO
    nhb = ttot // HALO

    def tile(w):
        return pl.BlockSpec((TM, w), lambda i: (i, 0))

    def prev(w):
        return pl.BlockSpec((HALO, w), lambda i: (jnp.maximum(i * hb - 1, 0), 0))

    def nxt(w):
        return pl.BlockSpec((HALO, w), lambda i: (jnp.minimum((i + 1) * hb, nhb - 1), 0))

    kern = functools.partial(_mix_kernel, alpha=alpha, n_lat=ttot - TM,
                             split_stream=ctx is not None)
    return pl.pallas_call(
        kern,
        grid=(nt,),
        in_specs=stream_specs + [
            tile(ATTN_W),
            tile(POOL_W), prev(POOL_W), nxt(POOL_W),
            tile(CONV_W), prev(CONV_W), nxt(CONV_W),
            _layer((SUBLANES, N_MOD * D), l),
            _layer((POOL_W, POOL_W), l), _layer((1, POOL_W), l),
            _layer((CONV_K_PAD, CONV_W), l), _layer((1, CONV_W), l), _layer((1, CONV_W), l),
            _layer((1, CONV_W), l),
            _layer((CONV_W, CONV_W), l), _layer((D, D), l),
            _layer((1, D), l), _layer((1, D), l),
            _layer((D, 2 * LANES), l), _layer((1, LANES), l),
        ],
        out_specs=[tile(D), tile(D // 2), tile(LANES), _full((SUBLANES, LANES))],
        out_shape=[
            jax.ShapeDtypeStruct((ttot, D), jnp.float32),
            jax.ShapeDtypeStruct((ttot, D // 2), jnp.uint32),
            jax.ShapeDtypeStruct((ttot, LANES), jnp.float32),
            jax.ShapeDtypeStruct((SUBLANES, LANES), jnp.float32),
        ],
        scratch_shapes=[
            pltpu.VMEM((TM + 2 * HALO, POOL_W), jnp.float32),
            pltpu.VMEM((TM + 2 * HALO, CONV_W), jnp.float32),
            pltpu.VMEM((SUBLANES - 1, TM + 2 * HALO - SUBLANES, CONV_W), jnp.float32),
            pltpu.VMEM((1, LANES), jnp.float32),
        ],
        compiler_params=_cparams(("arbitrary",)),
        name="mixer",
    )(*stream_args, attn, u, u, u, ag, ag, ag, mod, wpool, pscale, wdw, bdw, cvg, cvb, wpw, wout,
      ln1g, ln1b, wr, br)


ROW_GROUP = SUBLANES


def _dispatch_kernel(pends_ref, d1_ref, d2_ref, h_hbm, xs_ref, zero_buf, h_buf, lsems, ssems,
                     zsem, *, n_blk):
    i = pl.program_id(0)
    nt = pl.num_programs(0)
    slot = i % 3

    def load(t, to_slot):
        return pltpu.make_async_copy(h_hbm.at[pl.ds(pl.multiple_of(t * TM, TM), TM), :],
                                     h_buf.at[to_slot], lsems.at[to_slot])

    def wait_scatter(of_slot):
        for _ in range(2):
            pltpu.make_async_copy(h_buf.at[of_slot], xs_ref.at[pl.ds(0, TM), :],
                                  ssems.at[of_slot]).wait()

    @pl.when(i == 0)
    def _():
        load(0, 0).start()

        @pl.when(nt > 1)
        def _():
            load(1, 1).start()

        zero_buf[...] = jnp.zeros(zero_buf.shape, zero_buf.dtype)

        def zero_block(first_row):
            return pltpu.make_async_copy(
                zero_buf, xs_ref.at[pl.ds(pl.multiple_of(first_row, EBLK), EBLK), :], zsem)

        def seg_nonempty(e):
            return pends_ref[e] > (pends_ref[e - 1] if e > 0 else 0)

        for e in range(N_EXPERTS):
            @pl.when(seg_nonempty(e))
            def _():
                zero_block(pends_ref[e] - EBLK).start()
        n_used = pends_ref[N_EXPERTS - 1] // EBLK

        def tail_start(b, carry):
            zero_block(b * EBLK).start()
            return carry

        lax.fori_loop(n_used, n_blk, tail_start, 0)
        for e in range(N_EXPERTS):
            @pl.when(seg_nonempty(e))
            def _():
                zero_block(pends_ref[e] - EBLK).wait()

        def tail_wait(b, carry):
            zero_block(b * EBLK).wait()
            return carry

        lax.fori_loop(n_used, n_blk, tail_wait, 0)

    load(i, slot).wait()

    def start(r8, carry):
        base = pl.multiple_of(r8 * ROW_GROUP, ROW_GROUP)
        rows = h_buf.at[slot, pl.ds(base, ROW_GROUP), :]
        for j in range(ROW_GROUP):
            for d_ref in (d1_ref, d2_ref):
                pltpu.make_async_copy(rows.at[pl.ds(j, 1), :],
                                      xs_ref.at[pl.ds(d_ref[0, 0, base + j], 1), :],
                                      ssems.at[slot]).start()
        return carry

    lax.fori_loop(0, TM // ROW_GROUP, start, 0)

    @pl.when(i >= 1)
    def _():
        wait_scatter((i + 2) % 3)

    @pl.when(i + 2 < nt)
    def _():
        load(i + 2, (i + 2) % 3).start()

    @pl.when(i == nt - 1)
    def _():
        wait_scatter(slot)


def _dispatch(h2, d1, d2, pends, n_slots):
    ttot = h2.shape[0]
    nt = ttot // TM
    smem = pl.BlockSpec((1, 1, TM), lambda i, pe: (i, 0, 0), memory_space=pltpu.SMEM)
    grid_spec = pltpu.PrefetchScalarGridSpec(
        num_scalar_prefetch=1,
        grid=(nt,),
        in_specs=[smem, smem, pl.BlockSpec(memory_space=pl.ANY)],
        out_specs=pl.BlockSpec(memory_space=pl.ANY),
        scratch_shapes=[pltpu.VMEM((EBLK, D // 2), jnp.uint32),
                        pltpu.VMEM((3, TM, D // 2), jnp.uint32),
                        pltpu.SemaphoreType.DMA((3,)), pltpu.SemaphoreType.DMA((3,)),
                        pltpu.SemaphoreType.DMA(())],
    )
    return pl.pallas_call(
        functools.partial(_dispatch_kernel, n_blk=n_slots // EBLK),
        grid_spec=grid_spec,
        out_shape=jax.ShapeDtypeStruct((n_slots, D // 2), jnp.uint32),
        compiler_params=_cparams(("arbitrary",)),
        name="dispatch",
    )(pends, d1, d2, h2)


def _expert_kernel(be_ref, ne_ref, nu_ref, x_ref, wg_hbm, wu_hbm, wd_hbm, y_ref,
                   wg_f, wu_f, wd_f, wgu_b, wd_b, seg_ref, sems, *, l):
    b = pl.program_id(0)
    e = be_ref[b]
    new_expert = jnp.logical_or(b == 0, e != be_ref[jnp.maximum(b - 1, 0)])

    def fetch(expert, slot):
        return [pltpu.make_async_copy(src.at[l, expert], dst.at[slot], sems.at[slot])
                for src, dst in ((wg_hbm, wg_f), (wu_hbm, wu_f), (wd_hbm, wd_f))]

    @pl.when(b == 0)
    def _():
        seg_ref[0] = 0
        for cp in fetch(e, 0):
            cp.start()

    @pl.when(jnp.logical_and(b < nu_ref[0], new_expert))
    def _():
        seg = seg_ref[0]
        slot = seg % 2
        for cp in fetch(e, slot):
            cp.wait()

        @pl.when(ne_ref[b] != e)
        def _():
            for cp in fetch(ne_ref[b], 1 - slot):
                cp.start()

        wgu_b[:, :D_EXPERT] = wg_f[slot].astype(jnp.bfloat16)
        wgu_b[:, D_EXPERT:] = wu_f[slot].astype(jnp.bfloat16)
        wd_b[...] = wd_f[slot].astype(jnp.bfloat16)
        seg_ref[0] = seg + 1

    @pl.when(b < nu_ref[0])
    def _():
        xp = x_ref[...]
        x_lo = pltpu.bitcast(lax.shift_left(xp, jnp.uint32(16)), jnp.float32).astype(jnp.bfloat16)
        x_hi = pltpu.bitcast(xp & jnp.uint32(0xFFFF0000), jnp.float32).astype(jnp.bfloat16)
        gu = (jnp.dot(x_lo, wgu_b[:D // 2, :], preferred_element_type=jnp.float32)
              + jnp.dot(x_hi, wgu_b[D // 2:, :], preferred_element_type=jnp.float32))
        gt = gu[:, :D_EXPERT]
        hm = (gt * jax.nn.sigmoid(gt) * gu[:, D_EXPERT:]).astype(jnp.bfloat16)
        y_ref[...] = jnp.dot(hm, wd_b[...], preferred_element_type=jnp.float32)

    @pl.when(b >= nu_ref[0])
    def _():
        y_ref[...] = jnp.zeros(y_ref.shape, jnp.float32)


def _experts(xs, blk_e, nxt_e, n_used, wg, wu, wd, l):
    n_slots = xs.shape[0]
    nblk = n_slots // EBLK

    def row_map(b, be, ne, nu):
        return (jnp.minimum(b, nu[0] - 1), 0)

    hbm = pl.BlockSpec(memory_space=pl.ANY)
    grid_spec = pltpu.PrefetchScalarGridSpec(
        num_scalar_prefetch=3,
        grid=(nblk,),
        in_specs=[pl.BlockSpec((EBLK, D // 2), row_map), hbm, hbm, hbm],
        out_specs=pl.BlockSpec((EBLK, D), lambda b, be, ne, nu: (b, 0)),
        scratch_shapes=[pltpu.VMEM((2, D, D_EXPERT), jnp.float32),
                        pltpu.VMEM((2, D, D_EXPERT), jnp.float32),
                        pltpu.VMEM((2, D_EXPERT, D), jnp.float32),
                        pltpu.VMEM((D, 2 * D_EXPERT), jnp.bfloat16),
                        pltpu.VMEM((D_EXPERT, D), jnp.bfloat16),
                        pltpu.SMEM((1,), jnp.int32),
                        pltpu.SemaphoreType.DMA((2,))],
    )
    return pl.pallas_call(
        functools.partial(_expert_kernel, l=l),
        grid_spec=grid_spec,
        out_shape=jax.ShapeDtypeStruct((n_slots, D), jnp.float32),
        compiler_params=_cparams(("arbitrary",)),
        name="experts",
    )(blk_e, nxt_e, n_used, xs, wg, wu, wd)


def _combine_kernel(d1_ref, d2_ref, d1n_ref, d2n_ref, x1_ref, route_ref, mod_ref, ln2g_ref,
                    ln2b_ref, ys_ref, o_ref, y_buf, sems, *, alpha):
    i = pl.program_id(0)
    nt = pl.num_programs(0)
    slot = i % 2

    def gather(da_ref, db_ref, to_slot):
        def start(r8, carry):
            base = pl.multiple_of(r8 * ROW_GROUP, ROW_GROUP)
            for k, d_ref in enumerate((da_ref, db_ref)):
                rows = y_buf.at[to_slot, k, pl.ds(base, ROW_GROUP), :]
                for j in range(ROW_GROUP):
                    pltpu.make_async_copy(ys_ref.at[pl.ds(d_ref[0, 0, base + j], 1), :],
                                          rows.at[pl.ds(j, 1), :], sems.at[to_slot]).start()
            return carry

        lax.fori_loop(0, TM // ROW_GROUP, start, 0)

    @pl.when(i == 0)
    def _():
        gather(d1_ref, d2_ref, 0)

    @pl.when(i < nt - 1)
    def _():
        gather(d1n_ref, d2n_ref, 1 - slot)

    for k in range(2):
        pltpu.make_async_copy(ys_ref.at[pl.ds(0, TM), :], y_buf.at[slot, k], sems.at[slot]).wait()

    route = route_ref[...]
    g1 = route[:, 4:5]
    g2 = route[:, 5:6]
    o = g1 * y_buf[slot, 0] + g2 * y_buf[slot, 1]
    z = alpha * x1_ref[...] + _mod_rows(mod_ref, 5) * o
    o_ref[...] = _ln_plain(z) * ln2g_ref[...] + ln2b_ref[...]


def _combine(x1, route, mod, ln2g, ln2b, ys, d1, d2, alpha, l, latents_only):
    ttot = x1.shape[0]
    nt = ttot // TM
    smem = pl.BlockSpec((1, 1, TM), lambda i: (i, 0, 0), memory_space=pltpu.SMEM)
    smem_next = pl.BlockSpec((1, 1, TM), lambda i: (jnp.minimum(i + 1, nt - 1), 0, 0),
                             memory_space=pltpu.SMEM)
    kern = functools.partial(_combine_kernel, alpha=alpha)
    if latents_only:
        out_spec = pl.BlockSpec((TM, D), lambda i: (jnp.maximum(i - 1, 0), 0))
        out_rows = ttot - TM
    else:
        out_spec = pl.BlockSpec((TM, D), lambda i: (i, 0))
        out_rows = ttot
    return pl.pallas_call(
        kern,
        grid=(nt,),
        in_specs=[smem, smem, smem_next, smem_next,
                  pl.BlockSpec((TM, D), lambda i: (i, 0)),
                  pl.BlockSpec((TM, LANES), lambda i: (i, 0)),
                  _layer((SUBLANES, N_MOD * D), l), _layer((1, D), l), _layer((1, D), l),
                  pl.BlockSpec(memory_space=pl.ANY)],
        out_specs=out_spec,
        out_shape=jax.ShapeDtypeStruct((out_rows, D), jnp.float32),
        scratch_shapes=[pltpu.VMEM((2, 2, TM, D), jnp.float32),
                        pltpu.SemaphoreType.DMA((2,))],
        compiler_params=_cparams(("arbitrary",)),
        name="combine",
    )(d1, d2, d1, d2, x1, route, mod, ln2g, ln2b, ys)


def _rope_tables_t(n_lat, n_ctx):
    rows = n_lat // GRID_W
    row = jnp.broadcast_to(jnp.arange(rows)[:, None], (rows, GRID_W)).reshape(-1)
    col = jnp.broadcast_to(jnp.arange(GRID_W)[None, :], (rows, GRID_W)).reshape(-1)
    n_freq = HD // 4
    inv = ROPE_THETA ** (-jnp.arange(n_freq, dtype=jnp.float32) / n_freq)
    pos = jnp.stack([row, col], axis=-1).astype(jnp.float32)
    ang = (pos[:, :, None] * inv).reshape(n_lat, 2 * n_freq)
    cos = jnp.concatenate([jnp.ones((n_ctx, 2 * n_freq), jnp.float32), jnp.cos(ang)], axis=0)
    sin = jnp.concatenate([jnp.zeros((n_ctx, 2 * n_freq), jnp.float32), jnp.sin(ang)], axis=0)
    return cos.T, sin.T


def kernel(x, c, ctx, c_ctx, w_mod, b_mod, w_in, q_gain, k_gain, w_pool, pool_scale, w_dw, b_dw,
           cv_ln_g, cv_ln_b, w_cv_pw, w_out, ln1_g, ln1_b, ln2_g, ln2_b, w_rg, b_rg, w_re, b_re,
           w_e_gate, w_e_up, w_e_down):
    depth = w_mod.shape[0]
    n_lat = x.shape[1]
    n_ctx = ctx.shape[1]
    assert x.shape[0] == 1 and n_ctx == TM and n_lat % TM == 0 and n_lat % GRID_W == 0
    ttot = n_ctx + n_lat
    alpha = float((2 * depth) ** 0.25)
    n_blk = -(-(2 * ttot) // EBLK) + N_EXPERTS
    n_slots = n_blk * EBLK

    xa, xctx = x[0], ctx[0]
    cc = jnp.concatenate([c, c_ctx[None], jnp.zeros((SUBLANES - 2, D), jnp.float32)], axis=0)
    mod = _modulation(cc, w_mod, b_mod)
    cos_t, sin_t = _rope_tables_t(n_lat, n_ctx)

    n_qkv = ATTN_W + 2 * KV_W
    bf = jnp.bfloat16
    w_t = jnp.swapaxes(w_in[:, :, :n_qkv], 1, 2).astype(bf)
    w_n = w_in[:, :, n_qkv:].astype(bf)
    qg = jnp.broadcast_to(q_gain[:, :, None], (depth, HD, TM))
    kg = jnp.broadcast_to(k_gain[:, :, None], (depth, HD, TM))
    n_pg = len(POOL_WINDOWS)
    wpool = (jnp.eye(n_pg, dtype=jnp.float32)[None, :, None, :, None] * w_pool[:, :, :, None, :]
             ).reshape(depth, POOL_W, POOL_W).astype(bf)
    wdw = jnp.pad(w_dw, ((0, 0), (0, CONV_K_PAD - CONV_K), (0, 0)))
    n_r = N_GROUPS + N_EXPERTS
    wr = jnp.concatenate([w_rg, w_re, jnp.zeros((depth, D, LANES - n_r), jnp.float32)], axis=2)
    br = jnp.concatenate([b_rg, b_re, jnp.zeros((depth, LANES - n_r), jnp.float32)], axis=1)
    wr_hi = wr.astype(bf)
    wr = jnp.concatenate([wr_hi, (wr - wr_hi.astype(jnp.float32)).astype(bf)], axis=2)
    wpw = w_cv_pw.astype(bf)
    wout = w_out.astype(bf)

    def rows(v):
        return v[:, None, :]

    e_ids = jnp.arange(N_EXPERTS, dtype=jnp.int32)
    blk_start = jnp.arange(n_blk, dtype=jnp.int32) * EBLK
    for l in range(depth):
        q_t, k2, v_t, kn2, u, ag = _inproj(xa, xctx, mod, w_t, w_n, qg, kg, cos_t, sin_t, l)
        kmax = jnp.sqrt(jnp.max(kn2[:, :, 0, 0], axis=0))
        attn = _attention(q_t, k2, v_t, kmax)
        x1, h2, route, cnt = _mixer(
            xa, xctx, attn, u, ag, mod, wpool, rows(pool_scale), wdw, rows(b_dw), rows(cv_ln_g),
            rows(cv_ln_b), wpw, wout, rows(ln1_g), rows(ln1_b), wr, rows(br), alpha, l)

        counts = cnt[0, :N_EXPERTS].astype(jnp.int32)
        padded = (counts + EBLK - 1) // EBLK * EBLK
        pends = jnp.cumsum(padded)
        pstarts = pends - padded
        ri = route[:, :4].astype(jnp.int32)

        def dest(e, r):
            start = jnp.sum(jnp.where(e[:, None] == e_ids[None, :], pstarts[None, :], 0), axis=1)
            return (start + r).reshape(ttot // TM, 1, TM)

        d1 = dest(ri[:, 0], ri[:, 2])
        d2 = dest(ri[:, 1], ri[:, 3])
        n_used = (pends[-1] // EBLK).astype(jnp.int32).reshape(1)
        first_row = jnp.minimum(blk_start, pends[-1] - 1)
        blk_e = jnp.sum((pends[None, :] <= first_row[:, None]).astype(jnp.int32), axis=1)
        blk_e = jnp.minimum(blk_e, N_EXPERTS - 1)
        seg_end = jnp.sum(jnp.where(blk_e[:, None] == e_ids[None, :], pends[None, :], 0), axis=1)
        nxt_row = jnp.minimum(seg_end, pends[-1] - 1)
        nxt_e = jnp.sum((pends[None, :] <= nxt_row[:, None]).astype(jnp.int32), axis=1)
        nxt_e = jnp.minimum(nxt_e, N_EXPERTS - 1)

        xs = _dispatch(h2, d1, d2, pends, n_slots)
        ys = _experts(xs, blk_e, nxt_e, n_used, w_e_gate, w_e_up, w_e_down, l)
        xa = _combine(x1, route, mod, rows(ln2_g), rows(ln2_b), ys, d1, d2, alpha, l,
                      latents_only=(l == depth - 1))
        xctx = None
    return xa[None]
```

```python
import functools

import jax
import jax.numpy as jnp
from jax import lax
from jax.experimental import pallas as pl
from jax.experimental.pallas import tpu as pltpu

D = 1024
GRID_W = 64
HD = 64
NQ = 8
NKV = 2
QPK = NQ // NKV
ATTN_W = NQ * HD
KV_W = NKV * HD
POOL_W = 256
POOL_WINDOWS = (2, 4, 8, 16)
POOL_GW = POOL_W // len(POOL_WINDOWS)
CONV_W = 256
CONV_K = 31
N_GROUPS = 4
EPG = 8
N_EXPERTS = N_GROUPS * EPG
D_EXPERT = 512
N_MOD = 6
ROPE_THETA = 10000.0
EPS = 1e-6

TM = 256
HALO = 16
EBLK = 256
LANES = 128
SUBLANES = 8
BF16_ROWS = 16
VROWS = HD + BF16_ROWS
CONV_K_PAD = 32
LOG2E = 1.4426950408889634
BOUND_MARGIN = 1.0 + 2.0 ** -5
V7X_VMEM_BYTES = 64 * 1024 * 1024
VMEM_LIMIT = V7X_VMEM_BYTES - 8 * 1024 * 1024
NEG_BIG = -1e30

_HI = lax.Precision.HIGHEST


def _cparams(sem):
    return pltpu.CompilerParams(dimension_semantics=sem, vmem_limit_bytes=VMEM_LIMIT)


def _full(shape):
    n = len(shape)
    return pl.BlockSpec(shape, lambda *a, n=n: (0,) * n)


def _layer(shape, l):
    n = len(shape)
    return pl.BlockSpec((None,) + tuple(shape), lambda *a, n=n: (l,) + (0,) * n)


def _mod_kernel(cc_ref, w_ref, b_ref, o_ref):
    cc = cc_ref[...]
    a = cc * jax.nn.sigmoid(cc)
    o_ref[0] = jnp.dot(a, w_ref[0], precision=_HI, preferred_element_type=jnp.float32) + b_ref[0]


def _modulation(cc, w_mod, b_mod):
    depth = w_mod.shape[0]
    nw = w_mod.shape[2]
    bn = D
    return pl.pallas_call(
        _mod_kernel,
        grid=(depth, nw // bn),
        in_specs=[
            pl.BlockSpec((SUBLANES, D), lambda l, j: (0, 0)),
            pl.BlockSpec((1, D, bn), lambda l, j: (l, 0, j)),
            pl.BlockSpec((1, 1, bn), lambda l, j: (l, 0, j)),
        ],
        out_specs=pl.BlockSpec((1, SUBLANES, bn), lambda l, j: (l, 0, j)),
        out_shape=jax.ShapeDtypeStruct((depth, SUBLANES, nw), jnp.float32),
        compiler_params=_cparams(("arbitrary", "arbitrary")),
        name="modulation",
    )(cc, w_mod, b_mod.reshape(depth, 1, nw))


def _ln_plain(x):
    mu = jnp.mean(x, axis=-1, keepdims=True)
    xc = x - mu
    var = jnp.mean(xc * xc, axis=-1, keepdims=True)
    return xc * lax.rsqrt(var + EPS)


def _mod_rows(mod_ref, idx):
    row = jnp.where(pl.program_id(0) == 0, 1, 0)
    return mod_ref[pl.ds(row, 1), idx * D:(idx + 1) * D]


def _rms_rope_t(t, gain, cos, sin):
    ms = jnp.mean(t * t, axis=0, keepdims=True)
    t = t * lax.rsqrt(ms + EPS) * gain
    outs = []
    for a in range(2):
        t1 = t[32 * a:32 * a + 16]
        t2 = t[32 * a + 16:32 * a + 32]
        c = cos[16 * a:16 * a + 16]
        s = sin[16 * a:16 * a + 16]
        outs.append(t1 * c - t2 * s)
        outs.append(t2 * c + t1 * s)
    return jnp.concatenate(outs, axis=0)


def _stream_tile(x_ref, ctx_ref):
    if ctx_ref is None:
        return x_ref[...]
    return jnp.where(pl.program_id(0) == 0, ctx_ref[...], x_ref[...])


def _stream_specs(xa, ctx):
    if ctx is None:
        return [pl.BlockSpec((TM, D), lambda i: (i, 0))], [xa]
    return [pl.BlockSpec((TM, D), lambda i: (jnp.maximum(i - 1, 0), 0)), _full((TM, D))], [xa, ctx]


def _inproj_kernel(*refs, split_stream):
    x_ref, ctx_ref = (refs[0], refs[1]) if split_stream else (refs[0], None)
    (mod_ref, wt_ref, wn_ref, qg_ref, kg_ref, cos_ref, sin_ref,
     q_ref, k_ref, v_ref, kmax_ref, u_ref, ag_ref) = refs[2 if split_stream else 1:]
    x = _stream_tile(x_ref, ctx_ref)
    h = _ln_plain(x) * (1.0 + _mod_rows(mod_ref, 1)) + _mod_rows(mod_ref, 0)
    hb = h.astype(jnp.bfloat16)
    qkv_t = lax.dot_general(wt_ref[...], hb, (((1,), (1,)), ((), ())),
                            preferred_element_type=jnp.float32)
    cos = cos_ref[...]
    sin = sin_ref[...]
    qg = qg_ref[...]
    kg = kg_ref[...]
    ones_row = jnp.where(lax.broadcasted_iota(jnp.int32, (HD, TM), 0) == 0, 1.0, 0.0)
    ones_pad = jnp.where(lax.broadcasted_iota(jnp.int32, (VROWS - HD, TM), 0) == 0, 1.0, 0.0)
    for hq in range(NQ):
        t = _rms_rope_t(qkv_t[hq * HD:(hq + 1) * HD], qg, cos, sin) * (LOG2E * HD ** -0.5)
        nq = jnp.sqrt(jnp.sum(t * t, axis=0, keepdims=True))
        q_ref[hq // QPK, hq % QPK] = jnp.concatenate(
            [t, ones_row * (-BOUND_MARGIN * nq)], axis=0).astype(jnp.bfloat16)
    for g in range(NKV):
        kt = _rms_rope_t(qkv_t[ATTN_W + g * HD:ATTN_W + (g + 1) * HD], kg, cos, sin)
        kn = jnp.max(jnp.sum(kt * kt, axis=0, keepdims=True), axis=1, keepdims=True)
        kmax_ref[0, g] = jnp.broadcast_to(kn, (SUBLANES, LANES))
        k_ref[g] = jnp.concatenate([kt, ones_row], axis=0).T.astype(jnp.bfloat16)
        vt = qkv_t[ATTN_W + KV_W + g * HD:ATTN_W + KV_W + (g + 1) * HD]
        v_ref[g] = jnp.concatenate([vt, ones_pad], axis=0).astype(jnp.bfloat16)
    uag = jnp.dot(hb, wn_ref[...], preferred_element_type=jnp.float32)
    u_ref[...] = uag[:, :POOL_W]
    a = uag[:, POOL_W:POOL_W + CONV_W]
    gt = uag[:, POOL_W + CONV_W:]
    ag_ref[...] = a * jax.nn.sigmoid(gt)


def _inproj(xa, ctx, mod, w_t, w_n, qg, kg, cos_t, sin_t, l):
    ttot = xa.shape[0] + (0 if ctx is None else ctx.shape[0])
    nt = ttot // TM
    stream_specs, stream_args = _stream_specs(xa, ctx)
    return pl.pallas_call(
        functools.partial(_inproj_kernel, split_stream=ctx is not None),
        grid=(nt,),
        in_specs=stream_specs + [
            _layer((SUBLANES, N_MOD * D), l),
            _layer((ATTN_W + 2 * KV_W, D), l),
            _layer((D, POOL_W + 2 * CONV_W), l),
            _layer((HD, TM), l),
            _layer((HD, TM), l),
            pl.BlockSpec((HD // 2, TM), lambda i: (0, i)),
            pl.BlockSpec((HD // 2, TM), lambda i: (0, i)),
        ],
        out_specs=[
            pl.BlockSpec((NKV, QPK, 2 * HD, TM), lambda i: (0, 0, 0, i)),
            pl.BlockSpec((NKV, TM, 2 * HD), lambda i: (0, i, 0)),
            pl.BlockSpec((NKV, VROWS, TM), lambda i: (0, 0, i)),
            pl.BlockSpec((1, NKV, 8, LANES), lambda i: (i, 0, 0, 0)),
            pl.BlockSpec((TM, POOL_W), lambda i: (i, 0)),
            pl.BlockSpec((TM, CONV_W), lambda i: (i, 0)),
        ],
        out_shape=[
            jax.ShapeDtypeStruct((NKV, QPK, 2 * HD, ttot), jnp.bfloat16),
            jax.ShapeDtypeStruct((NKV, ttot, 2 * HD), jnp.bfloat16),
            jax.ShapeDtypeStruct((NKV, VROWS, ttot), jnp.bfloat16),
            jax.ShapeDtypeStruct((nt, NKV, 8, LANES), jnp.float32),
            jax.ShapeDtypeStruct((ttot, POOL_W), jnp.float32),
            jax.ShapeDtypeStruct((ttot, CONV_W), jnp.float32),
        ],
        compiler_params=_cparams(("arbitrary",)),
        name="inproj",
    )(*stream_args, mod, w_t, w_n, qg, kg, cos_t, sin_t)


TK = 256
GCH = 5
LOOP_STEPS = 6
L_MIN = 2.0 ** -60


def _attn_kernel(kmax_ref, q_ref, k_ref, v_ref, o_ref, qs_ref, s_ref, acc_ref, m_ref, lmin_ref, *,
                 n_chunks):
    g = pl.program_id(0)
    i = pl.program_id(1)
    nch = jnp.where(i == 0, 1, n_chunks)
    row = lax.broadcasted_iota(jnp.int32, (2 * HD, TM), 0)
    kmax = kmax_ref[g]

    def prep():
        for hh in range(QPK):
            qf = q_ref[0, hh].astype(jnp.float32)
            qs_ref[hh] = jnp.where(row == HD, qf * kmax, qf).astype(jnp.bfloat16)

    def step(nxt, cur):
        for kk in range(max(grp[1] for grp in (nxt, cur) if grp is not None)):
            do_nxt = nxt is not None and kk < nxt[1]
            do_cur = cur is not None and kk < cur[1]
            rows = slice(kk * TK, (kk + 1) * TK)
            if do_nxt:
                kc = k_ref[0, pl.ds(pl.multiple_of((nxt[0] + kk) * TK, TK), TK), :]
            if do_cur:
                vc = v_ref[0, :, pl.ds(pl.multiple_of((cur[0] + kk) * TK, TK), TK)]
            for hh in range(QPK):
                if do_nxt:
                    s_ref[nxt[2], hh, rows, :] = jnp.dot(
                        kc, qs_ref[hh], preferred_element_type=jnp.float32)
                if do_cur:
                    p = jnp.exp2(s_ref[cur[2], hh, rows, :]).astype(jnp.bfloat16)
                    acc_ref[hh] += jnp.dot(vc, p, preferred_element_type=jnp.float32)

    def finish():
        outs = [acc_ref[hh, 0:HD, :] / acc_ref[hh, HD:HD + 1, :] for hh in range(QPK)]
        o_ref[...] = jnp.concatenate(outs, axis=0).T
        lmin_ref[0] = jnp.min(acc_ref[:, HD:HD + 1, :])

    @pl.when(i == 0)
    def _():
        prep()
        kc = k_ref[0, 0:TM, :]
        vc = v_ref[0, :, 0:TM]
        for hh in range(QPK):
            p = jnp.exp2(jnp.dot(kc, qs_ref[hh], preferred_element_type=jnp.float32))
            acc_ref[hh] = jnp.dot(vc, p.astype(jnp.bfloat16), preferred_element_type=jnp.float32)
        finish()

    @pl.when(i > 0)
    def _():
        prep()
        acc_ref[...] = jnp.zeros(acc_ref.shape, jnp.float32)
        n_mid = (n_chunks - 2) // GCH
        tail = n_chunks - 1 - GCH * n_mid
        sizes = [1] + [GCH] * n_mid + ([tail - 1, 1] if tail >= 2 else [tail])
        firsts = [sum(sizes[:n]) for n in range(len(sizes))]

        def group(n):
            return (firsts[n], sizes[n], n % 2)

        step(group(0), None)
        step(group(1), group(0))
        n_loop = max(n_mid - 1, 0) // LOOP_STEPS

        def body(j, carry):
            c0 = 1 + LOOP_STEPS * GCH * j
            for u in range(LOOP_STEPS):
                step((c0 + (u + 1) * GCH, GCH, u % 2), (c0 + u * GCH, GCH, (u + 1) % 2))
            return carry

        lax.fori_loop(0, n_loop, body, 0)
        for n in range(1 + LOOP_STEPS * n_loop, len(sizes) - 1):
            step(group(n + 1), group(n))
        step(None, group(len(sizes) - 1))
        finish()

    @pl.when(lmin_ref[0] < L_MIN)
    def _():
        for hh in range(QPK):
            qs_ref[hh] = jnp.where(row == HD, 0.0, q_ref[0, hh].astype(jnp.float32)
                                   ).astype(jnp.bfloat16)
        acc_ref[...] = jnp.zeros(acc_ref.shape, jnp.float32)
        m_ref[...] = jnp.full(m_ref.shape, NEG_BIG, jnp.float32)

        def body2(c, carry):
            start = pl.multiple_of(c * TK, TK)
            kc = k_ref[0, pl.ds(start, TK), :]
            vc = v_ref[0, :, pl.ds(start, TK)]
            for hh in range(QPK):
                s = jnp.dot(kc, qs_ref[hh], preferred_element_type=jnp.float32)
                m_old = m_ref[hh]
                m_new = jnp.maximum(m_old, jnp.max(s, axis=0, keepdims=True))
                p = jnp.exp2(s - m_new).astype(jnp.bfloat16)
                acc_ref[hh] = jnp.exp2(m_old - m_new) * acc_ref[hh] + jnp.dot(
                    vc, p, preferred_element_type=jnp.float32)
                m_ref[hh] = m_new
            return carry

        lax.fori_loop(0, nch, body2, 0)
        finish()


def _attention(q_t, k2, v_t, kmax):
    ttot = k2.shape[1]
    nt = ttot // TM
    assert ttot % TK == 0 and ttot // TK >= 3
    kern = functools.partial(_attn_kernel, n_chunks=ttot // TK)
    grid_spec = pltpu.PrefetchScalarGridSpec(
        num_scalar_prefetch=1,
        grid=(NKV, nt),
        in_specs=[
            pl.BlockSpec((1, QPK, 2 * HD, TM), lambda g, i, km: (g, 0, 0, i)),
            pl.BlockSpec((1, ttot, 2 * HD), lambda g, i, km: (g, 0, 0)),
            pl.BlockSpec((1, VROWS, ttot), lambda g, i, km: (g, 0, 0)),
        ],
        out_specs=pl.BlockSpec((TM, QPK * HD), lambda g, i, km: (i, g)),
        scratch_shapes=[
            pltpu.VMEM((QPK, 2 * HD, TM), jnp.bfloat16),
            pltpu.VMEM((2, QPK, GCH * TK, TM), jnp.float32),
            pltpu.VMEM((QPK, VROWS, TM), jnp.float32),
            pltpu.VMEM((QPK, 1, TM), jnp.float32),
            pltpu.SMEM((1,), jnp.float32),
        ],
    )
    return pl.pallas_call(
        kern,
        grid_spec=grid_spec,
        out_shape=jax.ShapeDtypeStruct((ttot, ATTN_W), jnp.float32),
        compiler_params=_cparams(("arbitrary", "arbitrary")),
        name="attention",
    )(kmax, q_t, k2, v_t)


def _fill_ext(ext_ref, cur_ref, prev_ref, next_ref, left_ok, right_ok):
    zero = jnp.zeros((HALO, cur_ref.shape[1]), jnp.float32)
    ext_ref[0:HALO, :] = jnp.where(left_ok, prev_ref[...], zero)
    ext_ref[HALO:HALO + TM, :] = cur_ref[...]
    ext_ref[HALO + TM:HALO + TM + HALO, :] = jnp.where(right_ok, next_ref[...], zero)


def _mix_kernel(*refs, alpha, n_lat, split_stream):
    x_ref, ctx_ref = (refs[0], refs[1]) if split_stream else (refs[0], None)
    (attn_ref, u_ref, up_ref, un_ref, ag_ref, agp_ref, agn_ref, mod_ref,
     wpool_ref, pscale_ref, wdw_ref, bdw_ref, cvg_ref, cvb_ref, wpw_ref, wout_ref,
     ln1g_ref, ln1b_ref, wr_ref, br_ref,
     x1_ref, h2_ref, route_ref, cnt_ref,
     uext_ref, agext_ref, agsh_ref, base_ref) = refs[2 if split_stream else 1:]
    i = pl.program_id(0)
    nt = pl.num_programs(0)
    is_ctx = i == 0
    left_ok = i >= 2
    right_ok = jnp.logical_and(i >= 1, i < nt - 1)

    @pl.when(i == 0)
    def _():
        base_ref[...] = jnp.zeros(base_ref.shape, jnp.float32)

    _fill_ext(uext_ref, u_ref, up_ref, un_ref, left_ok, right_ok)
    _fill_ext(agext_ref, ag_ref, agp_ref, agn_ref, left_ok, right_ok)

    def ush(off, rows=TM):
        return uext_ref[HALO + off:HALO + off + rows, :]

    a2 = ush(-8, TM + 15) + ush(-7, TM + 15)
    a4 = a2[0:TM + 13] + a2[2:TM + 15]
    a8 = a4[0:TM + 9] + a4[4:TM + 13]
    a16 = a8[0:TM] + a8[8:TM + 8]
    sums = (a2[7:7 + TM], a4[6:6 + TM], a8[4:4 + TM], a16)
    pos = lax.broadcasted_iota(jnp.int32, (TM, 1), 0) + jnp.where(is_ctx, 0, (i - 1) * TM)
    seq_n = jnp.where(is_ctx, TM, n_lat)
    u_cur = u_ref[...]
    lane = lax.broadcasted_iota(jnp.int32, (TM, POOL_W), 1)
    pooled = jnp.zeros((TM, POOL_W), jnp.float32)
    for gi, w in enumerate(POOL_WINDOWS):
        lo = jnp.maximum(pos - w // 2, 0)
        hi = jnp.minimum(pos + (w - w // 2), seq_n)
        inv = 1.0 / (hi - lo).astype(jnp.float32)
        in_group = jnp.logical_and(lane >= gi * POOL_GW, lane < (gi + 1) * POOL_GW)
        pooled = jnp.where(in_group, sums[gi] * inv - u_cur, pooled)
    y_pool = jnp.dot(pooled.astype(jnp.bfloat16), wpool_ref[...],
                     preferred_element_type=jnp.float32) * pscale_ref[...]

    sh_rows = TM + 2 * HALO - SUBLANES
    for b in range(1, SUBLANES):
        agsh_ref[b - 1] = agext_ref[b:b + sh_rows, :]
    conv = jnp.zeros((TM, CONV_W), jnp.float32) + bdw_ref[...]
    for k in range(CONV_K):
        off = HALO + k - CONV_K // 2
        b, a8 = off % SUBLANES, off // SUBLANES * SUBLANES
        tap = agext_ref[a8:a8 + TM, :] if b == 0 else agsh_ref[b - 1, a8:a8 + TM, :]
        conv = conv + tap * wdw_ref[k:k + 1, :]
    cn = _ln_plain(conv) * cvg_ref[...] + cvb_ref[...]
    cn = cn * jax.nn.sigmoid(cn)
    y_conv = jnp.dot(cn.astype(jnp.bfloat16), wpw_ref[...], preferred_element_type=jnp.float32)

    y = jnp.dot(attn_ref[...].astype(jnp.bfloat16), wout_ref[0:ATTN_W, :],
                preferred_element_type=jnp.float32)
    y = y + jnp.dot(y_pool.astype(jnp.bfloat16), wout_ref[ATTN_W:ATTN_W + POOL_W, :],
                    preferred_element_type=jnp.float32)
    y = y + jnp.dot(y_conv.astype(jnp.bfloat16), wout_ref[ATTN_W + POOL_W:, :],
                    preferred_element_type=jnp.float32)
    x_res = _stream_tile(x_ref, ctx_ref)
    x1 = _ln_plain(alpha * x_res + _mod_rows(mod_ref, 2) * y) * ln1g_ref[...] + ln1b_ref[...]
    x1_ref[...] = x1
    h2 = _ln_plain(x1) * (1.0 + _mod_rows(mod_ref, 4)) + _mod_rows(mod_ref, 3)
    h_hi = h2.astype(jnp.bfloat16)
    h_rounded = h_hi.astype(jnp.float32)
    h_lo = (h2 - h_rounded).astype(jnp.bfloat16)
    bits = pltpu.bitcast(h_rounded, jnp.uint32)
    h2_ref[...] = bits[:, D // 2:] | lax.shift_right_logical(bits[:, :D // 2], jnp.uint32(16))
    hw = jnp.dot(h_hi, wr_ref[...], preferred_element_type=jnp.float32)
    logits = (hw[:, :LANES] + hw[:, LANES:] + br_ref[...]
              + jnp.dot(h_lo, wr_ref[:, :LANES], preferred_element_type=jnp.float32))
    ln = lax.broadcasted_iota(jnp.int32, (TM, LANES), 1)
    lnf = ln.astype(jnp.float32)
    no_lane = jnp.float32(LANES)
    neg = jnp.float32(-jnp.inf)
    is_g = ln < N_GROUPS
    lg = jnp.where(is_g, logits, neg)
    mg = jnp.max(lg, axis=-1, keepdims=True)
    gsel = jnp.min(jnp.where(lg == mg, lnf, no_lane), axis=-1, keepdims=True)
    p_group = 1.0 / jnp.sum(jnp.exp(lg - mg), axis=-1, keepdims=True)
    e_lane = lnf - N_GROUPS
    in_sel = jnp.logical_and(e_lane >= gsel * EPG, e_lane < (gsel + 1) * EPG)
    le = jnp.where(in_sel, logits, neg)
    v1 = jnp.max(le, axis=-1, keepdims=True)
    e1 = jnp.min(jnp.where(le == v1, e_lane, no_lane), axis=-1, keepdims=True)
    le2 = jnp.where(e_lane == e1, neg, le)
    v2 = jnp.max(le2, axis=-1, keepdims=True)
    e2 = jnp.min(jnp.where(le2 == v2, e_lane, no_lane), axis=-1, keepdims=True)
    ex = jnp.exp(v2 - v1)
    gate1 = p_group / (1.0 + ex)
    gate2 = p_group * ex / (1.0 + ex)

    oh = jnp.logical_or(lnf == e1, lnf == e2)
    ohb = jnp.where(oh, 1.0, 0.0).astype(jnp.bfloat16)
    r_i = lax.broadcasted_iota(jnp.int32, (TM, TM), 0)
    c_i = lax.broadcasted_iota(jnp.int32, (TM, TM), 1)
    tri = jnp.where(c_i < r_i, 1.0, 0.0).astype(jnp.bfloat16)
    before = jnp.dot(tri, ohb, preferred_element_type=jnp.float32) + base_ref[...]
    rank1 = jnp.sum(jnp.where(lnf == e1, before, 0.0), axis=-1, keepdims=True)
    rank2 = jnp.sum(jnp.where(lnf == e2, before, 0.0), axis=-1, keepdims=True)
    base_ref[...] = base_ref[...] + jnp.sum(ohb.astype(jnp.float32), axis=0, keepdims=True)
    cnt_ref[...] = jnp.broadcast_to(base_ref[...], cnt_ref.shape)

    route = jnp.zeros((TM, LANES), jnp.float32)
    for j, val in enumerate((e1, e2, rank1, rank2, gate1, gate2)):
        route = jnp.where(ln == j, val, route)
    route_ref[...] = route


def _mixer(xa, ctx, attn, u, ag, mod, wpool, pscale, wdw, bdw, cvg, cvb, wpw, wout, ln1g, ln1b,
           wr, br, alpha, l):
    ttot = attn.shape[0]
    nt = ttot // TM
    stream_specs, stream_args = _stream_specs(xa, ctx)
    hb = TM // HALO
    nhb = ttot // HALO

    def tile(w):
        return pl.BlockSpec((TM, w), lambda i: (i, 0))

    def prev(w):
        return pl.BlockSpec((HALO, w), lambda i: (jnp.maximum(i * hb - 1, 0), 0))

    def nxt(w):
        return pl.BlockSpec((HALO, w), lambda i: (jnp.minimum((i + 1) * hb, nhb - 1), 0))

    kern = functools.partial(_mix_kernel, alpha=alpha, n_lat=ttot - TM,
                             split_stream=ctx is not None)
    return pl.pallas_call(
        kern,
        grid=(nt,),
        in_specs=stream_specs + [
            tile(ATTN_W),
            tile(POOL_W), prev(POOL_W), nxt(POOL_W),
            tile(CONV_W), prev(CONV_W), nxt(CONV_W),
            _layer((SUBLANES, N_MOD * D), l),
            _layer((POOL_W, POOL_W), l), _layer((1, POOL_W), l),
            _layer((CONV_K_PAD, CONV_W), l), _layer((1, CONV_W), l), _layer((1, CONV_W), l),
            _layer((1, CONV_W), l),
            _layer((CONV_W, CONV_W), l), _layer((D, D), l),
            _layer((1, D), l), _layer((1, D), l),
            _layer((D, 2 * LANES), l), _layer((1, LANES), l),
        ],
        out_specs=[tile(D), tile(D // 2), tile(LANES), _full((SUBLANES, LANES))],
        out_shape=[
            jax.ShapeDtypeStruct((ttot, D), jnp.float32),
            jax.ShapeDtypeStruct((ttot, D // 2), jnp.uint32),
            jax.ShapeDtypeStruct((ttot, LANES), jnp.float32),
            jax.ShapeDtypeStruct((SUBLANES, LANES), jnp.float32),
        ],
        scratch_shapes=[
            pltpu.VMEM((TM + 2 * HALO, POOL_W), jnp.float32),
            pltpu.VMEM((TM + 2 * HALO, CONV_W), jnp.float32),
            pltpu.VMEM((SUBLANES - 1, TM + 2 * HALO - SUBLANES, CONV_W), jnp.float32),
            pltpu.VMEM((1, LANES), jnp.float32),
        ],
        compiler_params=_cparams(("arbitrary",)),
        name="mixer",
    )(*stream_args, attn, u, u, u, ag, ag, ag, mod, wpool, pscale, wdw, bdw, cvg, cvb, wpw, wout,
      ln1g, ln1b, wr, br)


ROW_GROUP = SUBLANES


def _dispatch_kernel(pends_ref, d1_ref, d2_ref, h_hbm, xs_ref, zero_buf, h_buf, lsems, ssems,
                     zsem, *, n_blk):
    i = pl.program_id(0)
    nt = pl.num_programs(0)
    slot = i % 3

    def load(t, to_slot):
        return pltpu.make_async_copy(h_hbm.at[pl.ds(pl.multiple_of(t * TM, TM), TM), :],
                                     h_buf.at[to_slot], lsems.at[to_slot])

    def wait_scatter(of_slot):
        for _ in range(2):
            pltpu.make_async_copy(h_buf.at[of_slot], xs_ref.at[pl.ds(0, TM), :],
                                  ssems.at[of_slot]).wait()

    @pl.when(i == 0)
    def _():
        load(0, 0).start()

        @pl.when(nt > 1)
        def _():
            load(1, 1).start()

        zero_buf[...] = jnp.zeros(zero_buf.shape, zero_buf.dtype)

        def zero_block(first_row):
            return pltpu.make_async_copy(
                zero_buf, xs_ref.at[pl.ds(pl.multiple_of(first_row, EBLK), EBLK), :], zsem)

        def seg_nonempty(e):
            return pends_ref[e] > (pends_ref[e - 1] if e > 0 else 0)

        for e in range(N_EXPERTS):
            @pl.when(seg_nonempty(e))
            def _():
                zero_block(pends_ref[e] - EBLK).start()
        n_used = pends_ref[N_EXPERTS - 1] // EBLK

        def tail_start(b, carry):
            zero_block(b * EBLK).start()
            return carry

        lax.fori_loop(n_used, n_blk, tail_start, 0)
        for e in range(N_EXPERTS):
            @pl.when(seg_nonempty(e))
            def _():
                zero_block(pends_ref[e] - EBLK).wait()

        def tail_wait(b, carry):
            zero_block(b * EBLK).wait()
            return carry

        lax.fori_loop(n_used, n_blk, tail_wait, 0)

    load(i, slot).wait()

    def start(r8, carry):
        base = pl.multiple_of(r8 * ROW_GROUP, ROW_GROUP)
        rows = h_buf.at[slot, pl.ds(base, ROW_GROUP), :]
        for j in range(ROW_GROUP):
            for d_ref in (d1_ref, d2_ref):
                pltpu.make_async_copy(rows.at[pl.ds(j, 1), :],
                                      xs_ref.at[pl.ds(d_ref[0, 0, base + j], 1), :],
                                      ssems.at[slot]).start()
        return carry

    lax.fori_loop(0, TM // ROW_GROUP, start, 0)

    @pl.when(i >= 1)
    def _():
        wait_scatter((i + 2) % 3)

    @pl.when(i + 2 < nt)
    def _():
        load(i + 2, (i + 2) % 3).start()

    @pl.when(i == nt - 1)
    def _():
        wait_scatter(slot)


def _dispatch(h2, d1, d2, pends, n_slots):
    ttot = h2.shape[0]
    nt = ttot // TM
    smem = pl.BlockSpec((1, 1, TM), lambda i, pe: (i, 0, 0), memory_space=pltpu.SMEM)
    grid_spec = pltpu.PrefetchScalarGridSpec(
        num_scalar_prefetch=1,
        grid=(nt,),
        in_specs=[smem, smem, pl.BlockSpec(memory_space=pl.ANY)],
        out_specs=pl.BlockSpec(memory_space=pl.ANY),
        scratch_shapes=[pltpu.VMEM((EBLK, D // 2), jnp.uint32),
                        pltpu.VMEM((3, TM, D // 2), jnp.uint32),
                        pltpu.SemaphoreType.DMA((3,)), pltpu.SemaphoreType.DMA((3,)),
                        pltpu.SemaphoreType.DMA(())],
    )
    return pl.pallas_call(
        functools.partial(_dispatch_kernel, n_blk=n_slots // EBLK),
        grid_spec=grid_spec,
        out_shape=jax.ShapeDtypeStruct((n_slots, D // 2), jnp.uint32),
        compiler_params=_cparams(("arbitrary",)),
        name="dispatch",
    )(pends, d1, d2, h2)


def _expert_kernel(be_ref, ne_ref, nu_ref, x_ref, wg_hbm, wu_hbm, wd_hbm, y_ref,
                   wg_f, wu_f, wd_f, wgu_b, wd_b, seg_ref, sems, *, l):
    b = pl.program_id(0)
    e = be_ref[b]
    new_expert = jnp.logical_or(b == 0, e != be_ref[jnp.maximum(b - 1, 0)])

    def fetch(expert, slot):
        return [pltpu.make_async_copy(src.at[l, expert], dst.at[slot], sems.at[slot])
                for src, dst in ((wg_hbm, wg_f), (wu_hbm, wu_f), (wd_hbm, wd_f))]

    @pl.when(b == 0)
    def _():
        seg_ref[0] = 0
        for cp in fetch(e, 0):
            cp.start()

    @pl.when(jnp.logical_and(b < nu_ref[0], new_expert))
    def _():
        seg = seg_ref[0]
        slot = seg % 2
        for cp in fetch(e, slot):
            cp.wait()

        @pl.when(ne_ref[b] != e)
        def _():
            for cp in fetch(ne_ref[b], 1 - slot):
                cp.start()

        wgu_b[:, :D_EXPERT] = wg_f[slot].astype(jnp.bfloat16)
        wgu_b[:, D_EXPERT:] = wu_f[slot].astype(jnp.bfloat16)
        wd_b[...] = wd_f[slot].astype(jnp.bfloat16)
        seg_ref[0] = seg + 1

    @pl.when(b < nu_ref[0])
    def _():
        xp = x_ref[...]
        x_lo = pltpu.bitcast(lax.shift_left(xp, jnp.uint32(16)), jnp.float32).astype(jnp.bfloat16)
        x_hi = pltpu.bitcast(xp & jnp.uint32(0xFFFF0000), jnp.float32).astype(jnp.bfloat16)
        gu = (jnp.dot(x_lo, wgu_b[:D // 2, :], preferred_element_type=jnp.float32)
              + jnp.dot(x_hi, wgu_b[D // 2:, :], preferred_element_type=jnp.float32))
        gt = gu[:, :D_EXPERT]
        hm = (gt * jax.nn.sigmoid(gt) * gu[:, D_EXPERT:]).astype(jnp.bfloat16)
        y_ref[...] = jnp.dot(hm, wd_b[...], preferred_element_type=jnp.float32)

    @pl.when(b >= nu_ref[0])
    def _():
        y_ref[...] = jnp.zeros(y_ref.shape, jnp.float32)


def _experts(xs, blk_e, nxt_e, n_used, wg, wu, wd, l):
    n_slots = xs.shape[0]
    nblk = n_slots // EBLK

    def row_map(b, be, ne, nu):
        return (jnp.minimum(b, nu[0] - 1), 0)

    hbm = pl.BlockSpec(memory_space=pl.ANY)
    grid_spec = pltpu.PrefetchScalarGridSpec(
        num_scalar_prefetch=3,
        grid=(nblk,),
        in_specs=[pl.BlockSpec((EBLK, D // 2), row_map), hbm, hbm, hbm],
        out_specs=pl.BlockSpec((EBLK, D), lambda b, be, ne, nu: (b, 0)),
        scratch_shapes=[pltpu.VMEM((2, D, D_EXPERT), jnp.float32),
                        pltpu.VMEM((2, D, D_EXPERT), jnp.float32),
                        pltpu.VMEM((2, D_EXPERT, D), jnp.float32),
                        pltpu.VMEM((D, 2 * D_EXPERT), jnp.bfloat16),
                        pltpu.VMEM((D_EXPERT, D), jnp.bfloat16),
                        pltpu.SMEM((1,), jnp.int32),
                        pltpu.SemaphoreType.DMA((2,))],
    )
    return pl.pallas_call(
        functools.partial(_expert_kernel, l=l),
        grid_spec=grid_spec,
        out_shape=jax.ShapeDtypeStruct((n_slots, D), jnp.float32),
        compiler_params=_cparams(("arbitrary",)),
        name="experts",
    )(blk_e, nxt_e, n_used, xs, wg, wu, wd)


def _combine_kernel(d1_ref, d2_ref, d1n_ref, d2n_ref, x1_ref, route_ref, mod_ref, ln2g_ref,
                    ln2b_ref, ys_ref, o_ref, y_buf, sems, *, alpha):
    i = pl.program_id(0)
    nt = pl.num_programs(0)
    slot = i % 2

    def gather(da_ref, db_ref, to_slot):
        def start(r8, carry):
            base = pl.multiple_of(r8 * ROW_GROUP, ROW_GROUP)
            for k, d_ref in enumerate((da_ref, db_ref)):
                rows = y_buf.at[to_slot, k, pl.ds(base, ROW_GROUP), :]
                for j in range(ROW_GROUP):
                    pltpu.make_async_copy(ys_ref.at[pl.ds(d_ref[0, 0, base + j], 1), :],
                                          rows.at[pl.ds(j, 1), :], sems.at[to_slot]).start()
            return carry

        lax.fori_loop(0, TM // ROW_GROUP, start, 0)

    @pl.when(i == 0)
    def _():
        gather(d1_ref, d2_ref, 0)

    @pl.when(i < nt - 1)
    def _():
        gather(d1n_ref, d2n_ref, 1 - slot)

    for k in range(2):
        pltpu.make_async_copy(ys_ref.at[pl.ds(0, TM), :], y_buf.at[slot, k], sems.at[slot]).wait()

    route = route_ref[...]
    g1 = route[:, 4:5]
    g2 = route[:, 5:6]
    o = g1 * y_buf[slot, 0] + g2 * y_buf[slot, 1]
    z = alpha * x1_ref[...] + _mod_rows(mod_ref, 5) * o
    o_ref[...] = _ln_plain(z) * ln2g_ref[...] + ln2b_ref[...]


def _combine(x1, route, mod, ln2g, ln2b, ys, d1, d2, alpha, l, latents_only):
    ttot = x1.shape[0]
    nt = ttot // TM
    smem = pl.BlockSpec((1, 1, TM), lambda i: (i, 0, 0), memory_space=pltpu.SMEM)
    smem_next = pl.BlockSpec((1, 1, TM), lambda i: (jnp.minimum(i + 1, nt - 1), 0, 0),
                             memory_space=pltpu.SMEM)
    kern = functools.partial(_combine_kernel, alpha=alpha)
    if latents_only:
        out_spec = pl.BlockSpec((TM, D), lambda i: (jnp.maximum(i - 1, 0), 0))
        out_rows = ttot - TM
    else:
        out_spec = pl.BlockSpec((TM, D), lambda i: (i, 0))
        out_rows = ttot
    return pl.pallas_call(
        kern,
        grid=(nt,),
        in_specs=[smem, smem, smem_next, smem_next,
                  pl.BlockSpec((TM, D), lambda i: (i, 0)),
                  pl.BlockSpec((TM, LANES), lambda i: (i, 0)),
                  _layer((SUBLANES, N_MOD * D), l), _layer((1, D), l), _layer((1, D), l),
                  pl.BlockSpec(memory_space=pl.ANY)],
        out_specs=out_spec,
        out_shape=jax.ShapeDtypeStruct((out_rows, D), jnp.float32),
        scratch_shapes=[pltpu.VMEM((2, 2, TM, D), jnp.float32),
                        pltpu.SemaphoreType.DMA((2,))],
        compiler_params=_cparams(("arbitrary",)),
        name="combine",
    )(d1, d2, d1, d2, x1, route, mod, ln2g, ln2b, ys)


def _rope_tables_t(n_lat, n_ctx):
    rows = n_lat // GRID_W
    row = jnp.broadcast_to(jnp.arange(rows)[:, None], (rows, GRID_W)).reshape(-1)
    col = jnp.broadcast_to(jnp.arange(GRID_W)[None, :], (rows, GRID_W)).reshape(-1)
    n_freq = HD // 4
    inv = ROPE_THETA ** (-jnp.arange(n_freq, dtype=jnp.float32) / n_freq)
    pos = jnp.stack([row, col], axis=-1).astype(jnp.float32)
    ang = (pos[:, :, None] * inv).reshape(n_lat, 2 * n_freq)
    cos = jnp.concatenate([jnp.ones((n_ctx, 2 * n_freq), jnp.float32), jnp.cos(ang)], axis=0)
    sin = jnp.concatenate([jnp.zeros((n_ctx, 2 * n_freq), jnp.float32), jnp.sin(ang)], axis=0)
    return cos.T, sin.T


def kernel(x, c, ctx, c_ctx, w_mod, b_mod, w_in, q_gain, k_gain, w_pool, pool_scale, w_dw, b_dw,
           cv_ln_g, cv_ln_b, w_cv_pw, w_out, ln1_g, ln1_b, ln2_g, ln2_b, w_rg, b_rg, w_re, b_re,
           w_e_gate, w_e_up, w_e_down):
    depth = w_mod.shape[0]
    n_lat = x.shape[1]
    n_ctx = ctx.shape[1]
    assert x.shape[0] == 1 and n_ctx == TM and n_lat % TM == 0 and n_lat % GRID_W == 0
    ttot = n_ctx + n_lat
    alpha = float((2 * depth) ** 0.25)
    n_blk = -(-(2 * ttot) // EBLK) + N_EXPERTS
    n_slots = n_blk * EBLK

    xa, xctx = x[0], ctx[0]
    cc = jnp.concatenate([c, c_ctx[None], jnp.zeros((SUBLANES - 2, D), jnp.float32)], axis=0)
    mod = _modulation(cc, w_mod, b_mod)
    cos_t, sin_t = _rope_tables_t(n_lat, n_ctx)

    n_qkv = ATTN_W + 2 * KV_W
    bf = jnp.bfloat16
    w_t = jnp.swapaxes(w_in[:, :, :n_qkv], 1, 2).astype(bf)
    w_n = w_in[:, :, n_qkv:].astype(bf)
    qg = jnp.broadcast_to(q_gain[:, :, None], (depth, HD, TM))
    kg = jnp.broadcast_to(k_gain[:, :, None], (depth, HD, TM))
    n_pg = len(POOL_WINDOWS)
    wpool = (jnp.eye(n_pg, dtype=jnp.float32)[None, :, None, :, None] * w_pool[:, :, :, None, :]
             ).reshape(depth, POOL_W, POOL_W).astype(bf)
    wdw = jnp.pad(w_dw, ((0, 0), (0, CONV_K_PAD - CONV_K), (0, 0)))
    n_r = N_GROUPS + N_EXPERTS
    wr = jnp.concatenate([w_rg, w_re, jnp.zeros((depth, D, LANES - n_r), jnp.float32)], axis=2)
    br = jnp.concatenate([b_rg, b_re, jnp.zeros((depth, LANES - n_r), jnp.float32)], axis=1)
    wr_hi = wr.astype(bf)
    wr = jnp.concatenate([wr_hi, (wr - wr_hi.astype(jnp.float32)).astype(bf)], axis=2)
    wpw = w_cv_pw.astype(bf)
    wout = w_out.astype(bf)

    def rows(v):
        return v[:, None, :]

    e_ids = jnp.arange(N_EXPERTS, dtype=jnp.int32)
    blk_start = jnp.arange(n_blk, dtype=jnp.int32) * EBLK
    for l in range(depth):
        q_t, k2, v_t, kn2, u, ag = _inproj(xa, xctx, mod, w_t, w_n, qg, kg, cos_t, sin_t, l)
        kmax = jnp.sqrt(jnp.max(kn2[:, :, 0, 0], axis=0))
        attn = _attention(q_t, k2, v_t, kmax)
        x1, h2, route, cnt = _mixer(
            xa, xctx, attn, u, ag, mod, wpool, rows(pool_scale), wdw, rows(b_dw), rows(cv_ln_g),
            rows(cv_ln_b), wpw, wout, rows(ln1_g), rows(ln1_b), wr, rows(br), alpha, l)

        counts = cnt[0, :N_EXPERTS].astype(jnp.int32)
        padded = (counts + EBLK - 1) // EBLK * EBLK
        pends = jnp.cumsum(padded)
        pstarts = pends - padded
        ri = route[:, :4].astype(jnp.int32)

        def dest(e, r):
            start = jnp.sum(jnp.where(e[:, None] == e_ids[None, :], pstarts[None, :], 0), axis=1)
            return (start + r).reshape(ttot // TM, 1, TM)

        d1 = dest(ri[:, 0], ri[:, 2])
        d2 = dest(ri[:, 1], ri[:, 3])
        n_used = (pends[-1] // EBLK).astype(jnp.int32).reshape(1)
        first_row = jnp.minimum(blk_start, pends[-1] - 1)
        blk_e = jnp.sum((pends[None, :] <= first_row[:, None]).astype(jnp.int32), axis=1)
        blk_e = jnp.minimum(blk_e, N_EXPERTS - 1)
        seg_end = jnp.sum(jnp.where(blk_e[:, None] == e_ids[None, :], pends[None, :], 0), axis=1)
        nxt_row = jnp.minimum(seg_end, pends[-1] - 1)
        nxt_e = jnp.sum((pends[None, :] <= nxt_row[:, None]).astype(jnp.int32), axis=1)
        nxt_e = jnp.minimum(nxt_e, N_EXPERTS - 1)

        xs = _dispatch(h2, d1, d2, pends, n_slots)
        ys = _experts(xs, blk_e, nxt_e, n_used, w_e_gate, w_e_up, w_e_down, l)
        xa = _combine(x1, route, mod, rows(ln2_g), rows(ln2_b), ys, d1, d2, alpha, l,
                      latents_only=(l == depth - 1))
        xctx = None
    return xa[None]
```

```python
import functools

import jax
import jax.numpy as jnp
from jax import lax
from jax.experimental import pallas as pl
from jax.experimental.pallas import tpu as pltpu

D = 1024
GRID_W = 64
HD = 64
NQ = 8
NKV = 2
QPK = NQ // NKV
ATTN_W = NQ * HD
KV_W = NKV * HD
POOL_W = 256
POOL_WINDOWS = (2, 4, 8, 16)
POOL_GW = POOL_W // len(POOL_WINDOWS)
CONV_W = 256
CONV_K = 31
N_GROUPS = 4
EPG = 8
N_EXPERTS = N_GROUPS * EPG
D_EXPERT = 512
N_MOD = 6
ROPE_THETA = 10000.0
EPS = 1e-6

TM = 256
HALO = 16
EBLK = 256
LANES = 128
SUBLANES = 8
BF16_ROWS = 16
VROWS = HD + BF16_ROWS
CONV_K_PAD = 32
LOG2E = 1.4426950408889634
BOUND_MARGIN = 1.0 + 2.0 ** -5
V7X_VMEM_BYTES = 64 * 1024 * 1024
VMEM_LIMIT = V7X_VMEM_BYTES - 8 * 1024 * 1024
NEG_BIG = -1e30

_HI = lax.Precision.HIGHEST


def _cparams(sem):
    return pltpu.CompilerParams(dimension_semantics=sem, vmem_limit_bytes=VMEM_LIMIT)


def _full(shape):
    n = len(shape)
    return pl.BlockSpec(shape, lambda *a, n=n: (0,) * n)


def _layer(shape, l):
    n = len(shape)
    return pl.BlockSpec((None,) + tuple(shape), lambda *a, n=n: (l,) + (0,) * n)


def _mod_kernel(cc_ref, w_ref, b_ref, o_ref):
    cc = cc_ref[...]
    a = cc * jax.nn.sigmoid(cc)
    o_ref[0] = jnp.dot(a, w_ref[0], precision=_HI, preferred_element_type=jnp.float32) + b_ref[0]


def _modulation(cc, w_mod, b_mod):
    depth = w_mod.shape[0]
    nw = w_mod.shape[2]
    bn = D
    return pl.pallas_call(
        _mod_kernel,
        grid=(depth, nw // bn),
        in_specs=[
            pl.BlockSpec((SUBLANES, D), lambda l, j: (0, 0)),
            pl.BlockSpec((1, D, bn), lambda l, j: (l, 0, j)),
            pl.BlockSpec((1, 1, bn), lambda l, j: (l, 0, j)),
        ],
        out_specs=pl.BlockSpec((1, SUBLANES, bn), lambda l, j: (l, 0, j)),
        out_shape=jax.ShapeDtypeStruct((depth, SUBLANES, nw), jnp.float32),
        compiler_params=_cparams(("arbitrary", "arbitrary")),
        name="modulation",
    )(cc, w_mod, b_mod.reshape(depth, 1, nw))


def _ln_plain(x):
    mu = jnp.mean(x, axis=-1, keepdims=True)
    xc = x - mu
    var = jnp.mean(xc * xc, axis=-1, keepdims=True)
    return xc * lax.rsqrt(var + EPS)


def _mod_rows(mod_ref, idx):
    row = jnp.where(pl.program_id(0) == 0, 1, 0)
    return mod_ref[pl.ds(row, 1), idx * D:(idx + 1) * D]


def _rms_rope_t(t, gain, cos, sin):
    ms = jnp.mean(t * t, axis=0, keepdims=True)
    t = t * lax.rsqrt(ms + EPS) * gain
    outs = []
    for a in range(2):
        t1 = t[32 * a:32 * a + 16]
        t2 = t[32 * a + 16:32 * a + 32]
        c = cos[16 * a:16 * a + 16]
        s = sin[16 * a:16 * a + 16]
        outs.append(t1 * c - t2 * s)
        outs.append(t2 * c + t1 * s)
    return jnp.concatenate(outs, axis=0)


def _stream_tile(x_ref, ctx_ref):
    if ctx_ref is None:
        return x_ref[...]
    return jnp.where(pl.program_id(0) == 0, ctx_ref[...], x_ref[...])


def _stream_specs(xa, ctx):
    if ctx is None:
        return [pl.BlockSpec((TM, D), lambda i: (i, 0))], [xa]
    return [pl.BlockSpec((TM, D), lambda i: (jnp.maximum(i - 1, 0), 0)), _full((TM, D))], [xa, ctx]


def _inproj_kernel(*refs, split_stream):
    x_ref, ctx_ref = (refs[0], refs[1]) if split_stream else (refs[0], None)
    (mod_ref, wt_ref, wn_ref, qg_ref, kg_ref, cos_ref, sin_ref,
     q_ref, k_ref, v_ref, kmax_ref, u_ref, ag_ref) = refs[2 if split_stream else 1:]
    x = _stream_tile(x_ref, ctx_ref)
    h = _ln_plain(x) * (1.0 + _mod_rows(mod_ref, 1)) + _mod_rows(mod_ref, 0)
    hb = h.astype(jnp.bfloat16)
    qkv_t = lax.dot_general(wt_ref[...], hb, (((1,), (1,)), ((), ())),
                            preferred_element_type=jnp.float32)
    cos = cos_ref[...]
    sin = sin_ref[...]
    qg = qg_ref[...]
    kg = kg_ref[...]
    ones_row = jnp.where(lax.broadcasted_iota(jnp.int32, (HD, TM), 0) == 0, 1.0, 0.0)
    ones_pad = jnp.where(lax.broadcasted_iota(jnp.int32, (VROWS - HD, TM), 0) == 0, 1.0, 0.0)
    for hq in range(NQ):
        t = _rms_rope_t(qkv_t[hq * HD:(hq + 1) * HD], qg, cos, sin) * (LOG2E * HD ** -0.5)
        nq = jnp.sqrt(jnp.sum(t * t, axis=0, keepdims=True))
        q_ref[hq // QPK, hq % QPK] = jnp.concatenate(
            [t, ones_row * (-BOUND_MARGIN * nq)], axis=0).astype(jnp.bfloat16)
    for g in range(NKV):
        kt = _rms_rope_t(qkv_t[ATTN_W + g * HD:ATTN_W + (g + 1) * HD], kg, cos, sin)
        kn = jnp.max(jnp.sum(kt * kt, axis=0, keepdims=True), axis=1, keepdims=True)
        kmax_ref[0, g] = jnp.broadcast_to(kn, (SUBLANES, LANES))
        k_ref[g] = jnp.concatenate([kt, ones_row], axis=0).T.astype(jnp.bfloat16)
        vt = qkv_t[ATTN_W + KV_W + g * HD:ATTN_W + KV_W + (g + 1) * HD]
        v_ref[g] = jnp.concatenate([vt, ones_pad], axis=0).astype(jnp.bfloat16)
    uag = jnp.dot(hb, wn_ref[...], preferred_element_type=jnp.float32)
    u_ref[...] = uag[:, :POOL_W]
    a = uag[:, POOL_W:POOL_W + CONV_W]
    gt = uag[:, POOL_W + CONV_W:]
    ag_ref[...] = a * jax.nn.sigmoid(gt)


def _inproj(xa, ctx, mod, w_t, w_n, qg, kg, cos_t, sin_t, l):
    ttot = xa.shape[0] + (0 if ctx is None else ctx.shape[0])
    nt = ttot // TM
    stream_specs, stream_args = _stream_specs(xa, ctx)
    return pl.pallas_call(
        functools.partial(_inproj_kernel, split_stream=ctx is not None),
        grid=(nt,),
        in_specs=stream_specs + [
            _layer((SUBLANES, N_MOD * D), l),
            _layer((ATTN_W + 2 * KV_W, D), l),
            _layer((D, POOL_W + 2 * CONV_W), l),
            _layer((HD, TM), l),
            _layer((HD, TM), l),
            pl.BlockSpec((HD // 2, TM), lambda i: (0, i)),
            pl.BlockSpec((HD // 2, TM), lambda i: (0, i)),
        ],
        out_specs=[
            pl.BlockSpec((NKV, QPK, 2 * HD, TM), lambda i: (0, 0, 0, i)),
            pl.BlockSpec((NKV, TM, 2 * HD), lambda i: (0, i, 0)),
            pl.BlockSpec((NKV, VROWS, TM), lambda i: (0, 0, i)),
            pl.BlockSpec((1, NKV, 8, LANES), lambda i: (i, 0, 0, 0)),
            pl.BlockSpec((TM, POOL_W), lambda i: (i, 0)),
            pl.BlockSpec((TM, CONV_W), lambda i: (i, 0)),
        ],
        out_shape=[
            jax.ShapeDtypeStruct((NKV, QPK, 2 * HD, ttot), jnp.bfloat16),
            jax.ShapeDtypeStruct((NKV, ttot, 2 * HD), jnp.bfloat16),
            jax.ShapeDtypeStruct((NKV, VROWS, ttot), jnp.bfloat16),
            jax.ShapeDtypeStruct((nt, NKV, 8, LANES), jnp.float32),
            jax.ShapeDtypeStruct((ttot, POOL_W), jnp.float32),
            jax.ShapeDtypeStruct((ttot, CONV_W), jnp.float32),
        ],
        compiler_params=_cparams(("arbitrary",)),
        name="inproj",
    )(*stream_args, mod, w_t, w_n, qg, kg, cos_t, sin_t)


TK = 256
GCH = 5
LOOP_STEPS = 6
L_MIN = 2.0 ** -60


def _attn_kernel(kmax_ref, q_ref, k_ref, v_ref, o_ref, qs_ref, s_ref, acc_ref, m_ref, lmin_ref, *,
                 n_chunks):
    g = pl.program_id(0)
    i = pl.program_id(1)
    nch = jnp.where(i == 0, 1, n_chunks)
    row = lax.broadcasted_iota(jnp.int32, (2 * HD, TM), 0)
    kmax = kmax_ref[g]

    def prep():
        for hh in range(QPK):
            qf = q_ref[0, hh].astype(jnp.float32)
            qs_ref[hh] = jnp.where(row == HD, qf * kmax, qf).astype(jnp.bfloat16)

    def step(nxt, cur):
        for kk in range(max(grp[1] for grp in (nxt, cur) if grp is not None)):
            do_nxt = nxt is not None and kk < nxt[1]
            do_cur = cur is not None and kk < cur[1]
            rows = slice(kk * TK, (kk + 1) * TK)
            if do_nxt:
                kc = k_ref[0, pl.ds(pl.multiple_of((nxt[0] + kk) * TK, TK), TK), :]
            if do_cur:
                vc = v_ref[0, :, pl.ds(pl.multiple_of((cur[0] + kk) * TK, TK), TK)]
            for hh in range(QPK):
                if do_nxt:
                    s_ref[nxt[2], hh, rows, :] = jnp.dot(
                        kc, qs_ref[hh], preferred_element_type=jnp.float32)
                if do_cur:
                    p = jnp.exp2(s_ref[cur[2], hh, rows, :]).astype(jnp.bfloat16)
                    acc_ref[hh] += jnp.dot(vc, p, preferred_element_type=jnp.float32)

    def finish():
        outs = [acc_ref[hh, 0:HD, :] / acc_ref[hh, HD:HD + 1, :] for hh in range(QPK)]
        o_ref[...] = jnp.concatenate(outs, axis=0).T
        lmin_ref[0] = jnp.min(acc_ref[:, HD:HD + 1, :])

    @pl.when(i == 0)
    def _():
        prep()
        kc = k_ref[0, 0:TM, :]
        vc = v_ref[0, :, 0:TM]
        for hh in range(QPK):
            p = jnp.exp2(jnp.dot(kc, qs_ref[hh], preferred_element_type=jnp.float32))
            acc_ref[hh] = jnp.dot(vc, p.astype(jnp.bfloat16), preferred_element_type=jnp.float32)
        finish()

    @pl.when(i > 0)
    def _():
        prep()
        acc_ref[...] = jnp.zeros(acc_ref.shape, jnp.float32)
        n_mid = (n_chunks - 2) // GCH
        tail = n_chunks - 1 - GCH * n_mid
        sizes = [1] + [GCH] * n_mid + ([tail - 1, 1] if tail >= 2 else [tail])
        firsts = [sum(sizes[:n]) for n in range(len(sizes))]

        def group(n):
            return (firsts[n], sizes[n], n % 2)

        step(group(0), None)
        step(group(1), group(0))
        n_loop = max(n_mid - 1, 0) // LOOP_STEPS

        def body(j, carry):
            c0 = 1 + LOOP_STEPS * GCH * j
            for u in range(LOOP_STEPS):
                step((c0 + (u + 1) * GCH, GCH, u % 2), (c0 + u * GCH, GCH, (u + 1) % 2))
            return carry

        lax.fori_loop(0, n_loop, body, 0)
        for n in range(1 + LOOP_STEPS * n_loop, len(sizes) - 1):
            step(group(n + 1), group(n))
        step(None, group(len(sizes) - 1))
        finish()

    @pl.when(lmin_ref[0] < L_MIN)
    def _():
        for hh in range(QPK):
            qs_ref[hh] = jnp.where(row == HD, 0.0, q_ref[0, hh].astype(jnp.float32)
                                   ).astype(jnp.bfloat16)
        acc_ref[...] = jnp.zeros(acc_ref.shape, jnp.float32)
        m_ref[...] = jnp.full(m_ref.shape, NEG_BIG, jnp.float32)

        def body2(c, carry):
            start = pl.multiple_of(c * TK, TK)
            kc = k_ref[0, pl.ds(start, TK), :]
            vc = v_ref[0, :, pl.ds(start, TK)]
            for hh in range(QPK):
                s = jnp.dot(kc, qs_ref[hh], preferred_element_type=jnp.float32)
                m_old = m_ref[hh]
                m_new = jnp.maximum(m_old, jnp.max(s, axis=0, keepdims=True))
                p = jnp.exp2(s - m_new).astype(jnp.bfloat16)
                acc_ref[hh] = jnp.exp2(m_old - m_new) * acc_ref[hh] + jnp.dot(
                    vc, p, preferred_element_type=jnp.float32)
                m_ref[hh] = m_new
            return carry

        lax.fori_loop(0, nch, body2, 0)
        finish()


def _attention(q_t, k2, v_t, kmax):
    ttot = k2.shape[1]
    nt = ttot // TM
    assert ttot % TK == 0 and ttot // TK >= 3
    kern = functools.partial(_attn_kernel, n_chunks=ttot // TK)
    grid_spec = pltpu.PrefetchScalarGridSpec(
        num_scalar_prefetch=1,
        grid=(NKV, nt),
        in_specs=[
            pl.BlockSpec((1, QPK, 2 * HD, TM), lambda g, i, km: (g, 0, 0, i)),
            pl.BlockSpec((1, ttot, 2 * HD), lambda g, i, km: (g, 0, 0)),
            pl.BlockSpec((1, VROWS, ttot), lambda g, i, km: (g, 0, 0)),
        ],
        out_specs=pl.BlockSpec((TM, QPK * HD), lambda g, i, km: (i, g)),
        scratch_shapes=[
            pltpu.VMEM((QPK, 2 * HD, TM), jnp.bfloat16),
            pltpu.VMEM((2, QPK, GCH * TK, TM), jnp.float32),
            pltpu.VMEM((QPK, VROWS, TM), jnp.float32),
            pltpu.VMEM((QPK, 1, TM), jnp.float32),
            pltpu.SMEM((1,), jnp.float32),
        ],
    )
    return pl.pallas_call(
        kern,
        grid_spec=grid_spec,
        out_shape=jax.ShapeDtypeStruct((ttot, ATTN_W), jnp.float32),
        compiler_params=_cparams(("arbitrary", "arbitrary")),
        name="attention",
    )(kmax, q_t, k2, v_t)


def _fill_ext(ext_ref, cur_ref, prev_ref, next_ref, left_ok, right_ok):
    zero = jnp.zeros((HALO, cur_ref.shape[1]), jnp.float32)
    ext_ref[0:HALO, :] = jnp.where(left_ok, prev_ref[...], zero)
    ext_ref[HALO:HALO + TM, :] = cur_ref[...]
    ext_ref[HALO + TM:HALO + TM + HALO, :] = jnp.where(right_ok, next_ref[...], zero)


def _mix_kernel(*refs, alpha, n_lat, split_stream):
    x_ref, ctx_ref = (refs[0], refs[1]) if split_stream else (refs[0], None)
    (attn_ref, u_ref, up_ref, un_ref, ag_ref, agp_ref, agn_ref, mod_ref,
     wpool_ref, pscale_ref, wdw_ref, bdw_ref, cvg_ref, cvb_ref, wpw_ref, wout_ref,
     ln1g_ref, ln1b_ref, wr_ref, br_ref,
     x1_ref, h2_ref, route_ref, cnt_ref,
     uext_ref, agext_ref, agsh_ref, base_ref) = refs[2 if split_stream else 1:]
    i = pl.program_id(0)
    nt = pl.num_programs(0)
    is_ctx = i == 0
    left_ok = i >= 2
    right_ok = jnp.logical_and(i >= 1, i < nt - 1)

    @pl.when(i == 0)
    def _():
        base_ref[...] = jnp.zeros(base_ref.shape, jnp.float32)

    _fill_ext(uext_ref, u_ref, up_ref, un_ref, left_ok, right_ok)
    _fill_ext(agext_ref, ag_ref, agp_ref, agn_ref, left_ok, right_ok)

    def ush(off, rows=TM):
        return uext_ref[HALO + off:HALO + off + rows, :]

    a2 = ush(-8, TM + 15) + ush(-7, TM + 15)
    a4 = a2[0:TM + 13] + a2[2:TM + 15]
    a8 = a4[0:TM + 9] + a4[4:TM + 13]
    a16 = a8[0:TM] + a8[8:TM + 8]
    sums = (a2[7:7 + TM], a4[6:6 + TM], a8[4:4 + TM], a16)
    pos = lax.broadcasted_iota(jnp.int32, (TM, 1), 0) + jnp.where(is_ctx, 0, (i - 1) * TM)
    seq_n = jnp.where(is_ctx, TM, n_lat)
    u_cur = u_ref[...]
    lane = lax.broadcasted_iota(jnp.int32, (TM, POOL_W), 1)
    pooled = jnp.zeros((TM, POOL_W), jnp.float32)
    for gi, w in enumerate(POOL_WINDOWS):
        lo = jnp.maximum(pos - w // 2, 0)
        hi = jnp.minimum(pos + (w - w // 2), seq_n)
        inv = 1.0 / (hi - lo).astype(jnp.float32)
        in_group = jnp.logical_and(lane >= gi * POOL_GW, lane < (gi + 1) * POOL_GW)
        pooled = jnp.where(in_group, sums[gi] * inv - u_cur, pooled)
    y_pool = jnp.dot(pooled.astype(jnp.bfloat16), wpool_ref[...],
                     preferred_element_type=jnp.float32) * pscale_ref[...]

    sh_rows = TM + 2 * HALO - SUBLANES
    for b in range(1, SUBLANES):
        agsh_ref[b - 1] = agext_ref[b:b + sh_rows, :]
    conv = jnp.zeros((TM, CONV_W), jnp.float32) + bdw_ref[...]
    for k in range(CONV_K):
        off = HALO + k - CONV_K // 2
        b, a8 = off % SUBLANES, off // SUBLANES * SUBLANES
        tap = agext_ref[a8:a8 + TM, :] if b == 0 else agsh_ref[b - 1, a8:a8 + TM, :]
        conv = conv + tap * wdw_ref[k:k + 1, :]
    cn = _ln_plain(conv) * cvg_ref[...] + cvb_ref[...]
    cn = cn * jax.nn.sigmoid(cn)
    y_conv = jnp.dot(cn.astype(jnp.bfloat16), wpw_ref[...], preferred_element_type=jnp.float32)

    y = jnp.dot(attn_ref[...].astype(jnp.bfloat16), wout_ref[0:ATTN_W, :],
                preferred_element_type=jnp.float32)
    y = y + jnp.dot(y_pool.astype(jnp.bfloat16), wout_ref[ATTN_W:ATTN_W + POOL_W, :],
                    preferred_element_type=jnp.float32)
    y = y + jnp.dot(y_conv.astype(jnp.bfloat16), wout_ref[ATTN_W + POOL_W:, :],
                    preferred_element_type=jnp.float32)
    x_res = _stream_tile(x_ref, ctx_ref)
    x1 = _ln_plain(alpha * x_res + _mod_rows(mod_ref, 2) * y) * ln1g_ref[...] + ln1b_ref[...]
    x1_ref[...] = x1
    h2 = _ln_plain(x1) * (1.0 + _mod_rows(mod_ref, 4)) + _mod_rows(mod_ref, 3)
    h_hi = h2.astype(jnp.bfloat16)
    h_rounded = h_hi.astype(jnp.float32)
    h_lo = (h2 - h_rounded).astype(jnp.bfloat16)
    bits = pltpu.bitcast(h_rounded, jnp.uint32)
    h2_ref[...] = bits[:, D // 2:] | lax.shift_right_logical(bits[:, :D // 2], jnp.uint32(16))
    hw = jnp.dot(h_hi, wr_ref[...], preferred_element_type=jnp.float32)
    logits = (hw[:, :LANES] + hw[:, LANES:] + br_ref[...]
              + jnp.dot(h_lo, wr_ref[:, :LANES], preferred_element_type=jnp.float32))
    ln = lax.broadcasted_iota(jnp.int32, (TM, LANES), 1)
    lnf = ln.astype(jnp.float32)
    no_lane = jnp.float32(LANES)
    neg = jnp.float32(-jnp.inf)
    is_g = ln < N_GROUPS
    lg = jnp.where(is_g, logits, neg)
    mg = jnp.max(lg, axis=-1, keepdims=True)
    gsel = jnp.min(jnp.where(lg == mg, lnf, no_lane), axis=-1, keepdims=True)
    p_group = 1.0 / jnp.sum(jnp.exp(lg - mg), axis=-1, keepdims=True)
    e_lane = lnf - N_GROUPS
    in_sel = jnp.logical_and(e_lane >= gsel * EPG, e_lane < (gsel + 1) * EPG)
    le = jnp.where(in_sel, logits, neg)
    v1 = jnp.max(le, axis=-1, keepdims=True)
    e1 = jnp.min(jnp.where(le == v1, e_lane, no_lane), axis=-1, keepdims=True)
    le2 = jnp.where(e_lane == e1, neg, le)
    v2 = jnp.max(le2, axis=-1, keepdims=True)
    e2 = jnp.min(jnp.where(le2 == v2, e_lane, no_lane), axis=-1, keepdims=True)
    ex = jnp.exp(v2 - v1)
    gate1 = p_group / (1.0 + ex)
    gate2 = p_group * ex / (1.0 + ex)

    oh = jnp.logical_or(lnf == e1, lnf == e2)
    ohb = jnp.where(oh, 1.0, 0.0).astype(jnp.bfloat16)
    r_i = lax.broadcasted_iota(jnp.int32, (TM, TM), 0)
    c_i = lax.broadcasted_iota(jnp.int32, (TM, TM), 1)
    tri = jnp.where(c_i < r_i, 1.0, 0.0).astype(jnp.bfloat16)
    before = jnp.dot(tri, ohb, preferred_element_type=jnp.float32) + base_ref[...]
    rank1 = jnp.sum(jnp.where(lnf == e1, before, 0.0), axis=-1, keepdims=True)
    rank2 = jnp.sum(jnp.where(lnf == e2, before, 0.0), axis=-1, keepdims=True)
    base_ref[...] = base_ref[...] + jnp.sum(ohb.astype(jnp.float32), axis=0, keepdims=True)
    cnt_ref[...] = jnp.broadcast_to(base_ref[...], cnt_ref.shape)

    route = jnp.zeros((TM, LANES), jnp.float32)
    for j, val in enumerate((e1, e2, rank1, rank2, gate1, gate2)):
        route = jnp.where(ln == j, val, route)
    route_ref[...] = route


def _mixer(xa, ctx, attn, u, ag, mod, wpool, pscale, wdw, bdw, cvg, cvb, wpw, wout, ln1g, ln1b,
           wr, br, alpha, l):
    ttot = attn.shape[0]
    nt = ttot // TM
    stream_specs, stream_args = _stream_specs(xa, ctx)
    hb = TM // HALO
    nhb = ttot // HALO

    def tile(w):
        return pl.BlockSpec((TM, w), lambda i: (i, 0))

    def prev(w):
        return pl.BlockSpec((HALO, w), lambda i: (jnp.maximum(i * hb - 1, 0), 0))

    def nxt(w):
        return pl.BlockSpec((HALO, w), lambda i: (jnp.minimum((i + 1) * hb, nhb - 1), 0))

    kern = functools.partial(_mix_kernel, alpha=alpha, n_lat=ttot - TM,
                             split_stream=ctx is not None)
    return pl.pallas_call(
        kern,
        grid=(nt,),
        in_specs=stream_specs + [
            tile(ATTN_W),
            tile(POOL_W), prev(POOL_W), nxt(POOL_W),
            tile(CONV_W), prev(CONV_W), nxt(CONV_W),
            _layer((SUBLANES, N_MOD * D), l),
            _layer((POOL_W, POOL_W), l), _layer((1, POOL_W), l),
            _layer((CONV_K_PAD, CONV_W), l), _layer((1, CONV_W), l), _layer((1, CONV_W), l),
            _layer((1, CONV_W), l),
            _layer((CONV_W, CONV_W), l), _layer((D, D), l),
            _layer((1, D), l), _layer((1, D), l),
            _layer((D, 2 * LANES), l), _layer((1, LANES), l),
        ],
        out_specs=[tile(D), tile(D // 2), tile(LANES), _full((SUBLANES, LANES))],
        out_shape=[
            jax.ShapeDtypeStruct((ttot, D), jnp.float32),
            jax.ShapeDtypeStruct((ttot, D // 2), jnp.uint32),
            jax.ShapeDtypeStruct((ttot, LANES), jnp.float32),
            jax.ShapeDtypeStruct((SUBLANES, LANES), jnp.float32),
        ],
        scratch_shapes=[
            pltpu.VMEM((TM + 2 * HALO, POOL_W), jnp.float32),
            pltpu.VMEM((TM + 2 * HALO, CONV_W), jnp.float32),
            pltpu.VMEM((SUBLANES - 1, TM + 2 * HALO - SUBLANES, CONV_W), jnp.float32),
            pltpu.VMEM((1, LANES), jnp.float32),
        ],
        compiler_params=_cparams(("arbitrary",)),
        name="mixer",
    )(*stream_args, attn, u, u, u, ag, ag, ag, mod, wpool, pscale, wdw, bdw, cvg, cvb, wpw, wout,
      ln1g, ln1b, wr, br)


ROW_GROUP = SUBLANES


def _dispatch_kernel(pends_ref, d1_ref, d2_ref, h_hbm, xs_ref, zero_buf, h_buf, lsems, ssems,
                     zsem, *, n_blk):
    i = pl.program_id(0)
    nt = pl.num_programs(0)
    slot = i % 3

    def load(t, to_slot):
        return pltpu.make_async_copy(h_hbm.at[pl.ds(pl.multiple_of(t * TM, TM), TM), :],
                                     h_buf.at[to_slot], lsems.at[to_slot])

    def wait_scatter(of_slot):
        for _ in range(2):
            pltpu.make_async_copy(h_buf.at[of_slot], xs_ref.at[pl.ds(0, TM), :],
                                  ssems.at[of_slot]).wait()

    @pl.when(i == 0)
    def _():
        load(0, 0).start()

        @pl.when(nt > 1)
        def _():
            load(1, 1).start()

        zero_buf[...] = jnp.zeros(zero_buf.shape, zero_buf.dtype)

        def zero_block(first_row):
            return pltpu.make_async_copy(
                zero_buf, xs_ref.at[pl.ds(pl.multiple_of(first_row, EBLK), EBLK), :], zsem)

        def seg_nonempty(e):
            return pends_ref[e] > (pends_ref[e - 1] if e > 0 else 0)

        for e in range(N_EXPERTS):
            @pl.when(seg_nonempty(e))
            def _():
                zero_block(pends_ref[e] - EBLK).start()
        n_used = pends_ref[N_EXPERTS - 1] // EBLK

        def tail_start(b, carry):
            zero_block(b * EBLK).start()
            return carry

        lax.fori_loop(n_used, n_blk, tail_start, 0)
        for e in range(N_EXPERTS):
            @pl.when(seg_nonempty(e))
            def _():
                zero_block(pends_ref[e] - EBLK).wait()

        def tail_wait(b, carry):
            zero_block(b * EBLK).wait()
            return carry

        lax.fori_loop(n_used, n_blk, tail_wait, 0)

    load(i, slot).wait()

    def start(r8, carry):
        base = pl.multiple_of(r8 * ROW_GROUP, ROW_GROUP)
        rows = h_buf.at[slot, pl.ds(base, ROW_GROUP), :]
        for j in range(ROW_GROUP):
            for queue, d_ref in enumerate((d1_ref, d2_ref)):
                pltpu.make_async_copy(rows.at[pl.ds(j, 1), :],
                                      xs_ref.at[pl.ds(d_ref[0, 0, base + j], 1), :],
                                      ssems.at[slot]).start(priority=queue)
        return carry

    lax.fori_loop(0, TM // ROW_GROUP, start, 0)

    @pl.when(i >= 1)
    def _():
        wait_scatter((i + 2) % 3)

    @pl.when(i + 2 < nt)
    def _():
        load(i + 2, (i + 2) % 3).start()

    @pl.when(i == nt - 1)
    def _():
        wait_scatter(slot)


def _dispatch(h2, d1, d2, pends, n_slots):
    ttot = h2.shape[0]
    nt = ttot // TM
    smem = pl.BlockSpec((1, 1, TM), lambda i, pe: (i, 0, 0), memory_space=pltpu.SMEM)
    grid_spec = pltpu.PrefetchScalarGridSpec(
        num_scalar_prefetch=1,
        grid=(nt,),
        in_specs=[smem, smem, pl.BlockSpec(memory_space=pl.ANY)],
        out_specs=pl.BlockSpec(memory_space=pl.ANY),
        scratch_shapes=[pltpu.VMEM((EBLK, D // 2), jnp.uint32),
                        pltpu.VMEM((3, TM, D // 2), jnp.uint32),
                        pltpu.SemaphoreType.DMA((3,)), pltpu.SemaphoreType.DMA((3,)),
                        pltpu.SemaphoreType.DMA(())],
    )
    return pl.pallas_call(
        functools.partial(_dispatch_kernel, n_blk=n_slots // EBLK),
        grid_spec=grid_spec,
        out_shape=jax.ShapeDtypeStruct((n_slots, D // 2), jnp.uint32),
        compiler_params=_cparams(("arbitrary",)),
        name="dispatch",
    )(pends, d1, d2, h2)


def _expert_kernel(be_ref, ne_ref, nu_ref, x_ref, wg_hbm, wu_hbm, wd_hbm, y_ref,
                   wg_f, wu_f, wd_f, wgu_b, wd_b, seg_ref, sems, *, l):
    b = pl.program_id(0)
    e = be_ref[b]
    new_expert = jnp.logical_or(b == 0, e != be_ref[jnp.maximum(b - 1, 0)])

    def fetch(expert, slot):
        return [pltpu.make_async_copy(src.at[l, expert], dst.at[slot], sems.at[slot])
                for src, dst in ((wg_hbm, wg_f), (wu_hbm, wu_f), (wd_hbm, wd_f))]

    @pl.when(b == 0)
    def _():
        seg_ref[0] = 0
        for cp in fetch(e, 0):
            cp.start()

    @pl.when(jnp.logical_and(b < nu_ref[0], new_expert))
    def _():
        seg = seg_ref[0]
        slot = seg % 2
        for cp in fetch(e, slot):
            cp.wait()

        @pl.when(ne_ref[b] != e)
        def _():
            for cp in fetch(ne_ref[b], 1 - slot):
                cp.start()

        wgu_b[:, :D_EXPERT] = wg_f[slot].astype(jnp.bfloat16)
        wgu_b[:, D_EXPERT:] = wu_f[slot].astype(jnp.bfloat16)
        wd_b[...] = wd_f[slot].astype(jnp.bfloat16)
        seg_ref[0] = seg + 1

    @pl.when(b < nu_ref[0])
    def _():
        xp = x_ref[...]
        x_lo = pltpu.bitcast(lax.shift_left(xp, jnp.uint32(16)), jnp.float32).astype(jnp.bfloat16)
        x_hi = pltpu.bitcast(xp & jnp.uint32(0xFFFF0000), jnp.float32).astype(jnp.bfloat16)
        gu = (jnp.dot(x_lo, wgu_b[:D // 2, :], preferred_element_type=jnp.float32)
              + jnp.dot(x_hi, wgu_b[D // 2:, :], preferred_element_type=jnp.float32))
        gt = gu[:, :D_EXPERT]
        hm = (gt * jax.nn.sigmoid(gt) * gu[:, D_EXPERT:]).astype(jnp.bfloat16)
        y_ref[...] = jnp.dot(hm, wd_b[...], preferred_element_type=jnp.float32)

    @pl.when(b >= nu_ref[0])
    def _():
        y_ref[...] = jnp.zeros(y_ref.shape, jnp.float32)


def _experts(xs, blk_e, nxt_e, n_used, wg, wu, wd, l):
    n_slots = xs.shape[0]
    nblk = n_slots // EBLK

    def row_map(b, be, ne, nu):
        return (jnp.minimum(b, nu[0] - 1), 0)

    hbm = pl.BlockSpec(memory_space=pl.ANY)
    grid_spec = pltpu.PrefetchScalarGridSpec(
        num_scalar_prefetch=3,
        grid=(nblk,),
        in_specs=[pl.BlockSpec((EBLK, D // 2), row_map), hbm, hbm, hbm],
        out_specs=pl.BlockSpec((EBLK, D), lambda b, be, ne, nu: (b, 0)),
        scratch_shapes=[pltpu.VMEM((2, D, D_EXPERT), jnp.float32),
                        pltpu.VMEM((2, D, D_EXPERT), jnp.float32),
                        pltpu.VMEM((2, D_EXPERT, D), jnp.float32),
                        pltpu.VMEM((D, 2 * D_EXPERT), jnp.bfloat16),
                        pltpu.VMEM((D_EXPERT, D), jnp.bfloat16),
                        pltpu.SMEM((1,), jnp.int32),
                        pltpu.SemaphoreType.DMA((2,))],
    )
    return pl.pallas_call(
        functools.partial(_expert_kernel, l=l),
        grid_spec=grid_spec,
        out_shape=jax.ShapeDtypeStruct((n_slots, D), jnp.float32),
        compiler_params=_cparams(("arbitrary",)),
        name="experts",
    )(blk_e, nxt_e, n_used, xs, wg, wu, wd)


def _combine_kernel(d1_ref, d2_ref, d1n_ref, d2n_ref, x1_ref, route_ref, mod_ref, ln2g_ref,
                    ln2b_ref, ys_ref, o_ref, y_buf, sems, *, alpha):
    i = pl.program_id(0)
    nt = pl.num_programs(0)
    slot = i % 2

    def gather(da_ref, db_ref, to_slot):
        def start(r8, carry):
            base = pl.multiple_of(r8 * ROW_GROUP, ROW_GROUP)
            for k, d_ref in enumerate((da_ref, db_ref)):
                rows = y_buf.at[to_slot, k, pl.ds(base, ROW_GROUP), :]
                for j in range(ROW_GROUP):
                    pltpu.make_async_copy(ys_ref.at[pl.ds(d_ref[0, 0, base + j], 1), :],
                                          rows.at[pl.ds(j, 1), :],
                                          sems.at[to_slot]).start(priority=j % 2)
            return carry

        lax.fori_loop(0, TM // ROW_GROUP, start, 0)

    @pl.when(i == 0)
    def _():
        gather(d1_ref, d2_ref, 0)

    @pl.when(i < nt - 1)
    def _():
        gather(d1n_ref, d2n_ref, 1 - slot)

    for k in range(2):
        pltpu.make_async_copy(ys_ref.at[pl.ds(0, TM), :], y_buf.at[slot, k], sems.at[slot]).wait()

    route = route_ref[...]
    g1 = route[:, 4:5]
    g2 = route[:, 5:6]
    o = g1 * y_buf[slot, 0] + g2 * y_buf[slot, 1]
    z = alpha * x1_ref[...] + _mod_rows(mod_ref, 5) * o
    o_ref[...] = _ln_plain(z) * ln2g_ref[...] + ln2b_ref[...]


def _combine(x1, route, mod, ln2g, ln2b, ys, d1, d2, alpha, l, latents_only):
    ttot = x1.shape[0]
    nt = ttot // TM
    smem = pl.BlockSpec((1, 1, TM), lambda i: (i, 0, 0), memory_space=pltpu.SMEM)
    smem_next = pl.BlockSpec((1, 1, TM), lambda i: (jnp.minimum(i + 1, nt - 1), 0, 0),
                             memory_space=pltpu.SMEM)
    kern = functools.partial(_combine_kernel, alpha=alpha)
    if latents_only:
        out_spec = pl.BlockSpec((TM, D), lambda i: (jnp.maximum(i - 1, 0), 0))
        out_rows = ttot - TM
    else:
        out_spec = pl.BlockSpec((TM, D), lambda i: (i, 0))
        out_rows = ttot
    return pl.pallas_call(
        kern,
        grid=(nt,),
        in_specs=[smem, smem, smem_next, smem_next,
                  pl.BlockSpec((TM, D), lambda i: (i, 0)),
                  pl.BlockSpec((TM, LANES), lambda i: (i, 0)),
                  _layer((SUBLANES, N_MOD * D), l), _layer((1, D), l), _layer((1, D), l),
                  pl.BlockSpec(memory_space=pl.ANY)],
        out_specs=out_spec,
        out_shape=jax.ShapeDtypeStruct((out_rows, D), jnp.float32),
        scratch_shapes=[pltpu.VMEM((2, 2, TM, D), jnp.float32),
                        pltpu.SemaphoreType.DMA((2,))],
        compiler_params=_cparams(("arbitrary",)),
        name="combine",
    )(d1, d2, d1, d2, x1, route, mod, ln2g, ln2b, ys)


def _rope_tables_t(n_lat, n_ctx):
    rows = n_lat // GRID_W
    row = jnp.broadcast_to(jnp.arange(rows)[:, None], (rows, GRID_W)).reshape(-1)
    col = jnp.broadcast_to(jnp.arange(GRID_W)[None, :], (rows, GRID_W)).reshape(-1)
    n_freq = HD // 4
    inv = ROPE_THETA ** (-jnp.arange(n_freq, dtype=jnp.float32) / n_freq)
    pos = jnp.stack([row, col], axis=-1).astype(jnp.float32)
    ang = (pos[:, :, None] * inv).reshape(n_lat, 2 * n_freq)
    cos = jnp.concatenate([jnp.ones((n_ctx, 2 * n_freq), jnp.float32), jnp.cos(ang)], axis=0)
    sin = jnp.concatenate([jnp.zeros((n_ctx, 2 * n_freq), jnp.float32), jnp.sin(ang)], axis=0)
    return cos.T, sin.T


def kernel(x, c, ctx, c_ctx, w_mod, b_mod, w_in, q_gain, k_gain, w_pool, pool_scale, w_dw, b_dw,
           cv_ln_g, cv_ln_b, w_cv_pw, w_out, ln1_g, ln1_b, ln2_g, ln2_b, w_rg, b_rg, w_re, b_re,
           w_e_gate, w_e_up, w_e_down):
    depth = w_mod.shape[0]
    n_lat = x.shape[1]
    n_ctx = ctx.shape[1]
    assert x.shape[0] == 1 and n_ctx == TM and n_lat % TM == 0 and n_lat % GRID_W == 0
    ttot = n_ctx + n_lat
    alpha = float((2 * depth) ** 0.25)
    n_blk = -(-(2 * ttot) // EBLK) + N_EXPERTS
    n_slots = n_blk * EBLK

    xa, xctx = x[0], ctx[0]
    cc = jnp.concatenate([c, c_ctx[None], jnp.zeros((SUBLANES - 2, D), jnp.float32)], axis=0)
    mod = _modulation(cc, w_mod, b_mod)
    cos_t, sin_t = _rope_tables_t(n_lat, n_ctx)

    n_qkv = ATTN_W + 2 * KV_W
    bf = jnp.bfloat16
    w_t = jnp.swapaxes(w_in[:, :, :n_qkv], 1, 2).astype(bf)
    w_n = w_in[:, :, n_qkv:].astype(bf)
    qg = jnp.broadcast_to(q_gain[:, :, None], (depth, HD, TM))
    kg = jnp.broadcast_to(k_gain[:, :, None], (depth, HD, TM))
    n_pg = len(POOL_WINDOWS)
    wpool = (jnp.eye(n_pg, dtype=jnp.float32)[None, :, None, :, None] * w_pool[:, :, :, None, :]
             ).reshape(depth, POOL_W, POOL_W).astype(bf)
    wdw = jnp.pad(w_dw, ((0, 0), (0, CONV_K_PAD - CONV_K), (0, 0)))
    n_r = N_GROUPS + N_EXPERTS
    wr = jnp.concatenate([w_rg, w_re, jnp.zeros((depth, D, LANES - n_r), jnp.float32)], axis=2)
    br = jnp.concatenate([b_rg, b_re, jnp.zeros((depth, LANES - n_r), jnp.float32)], axis=1)
    wr_hi = wr.astype(bf)
    wr = jnp.concatenate([wr_hi, (wr - wr_hi.astype(jnp.float32)).astype(bf)], axis=2)
    wpw = w_cv_pw.astype(bf)
    wout = w_out.astype(bf)

    def rows(v):
        return v[:, None, :]

    e_ids = jnp.arange(N_EXPERTS, dtype=jnp.int32)
    blk_start = jnp.arange(n_blk, dtype=jnp.int32) * EBLK
    for l in range(depth):
        q_t, k2, v_t, kn2, u, ag = _inproj(xa, xctx, mod, w_t, w_n, qg, kg, cos_t, sin_t, l)
        kmax = jnp.sqrt(jnp.max(kn2[:, :, 0, 0], axis=0))
        attn = _attention(q_t, k2, v_t, kmax)
        x1, h2, route, cnt = _mixer(
            xa, xctx, attn, u, ag, mod, wpool, rows(pool_scale), wdw, rows(b_dw), rows(cv_ln_g),
            rows(cv_ln_b), wpw, wout, rows(ln1_g), rows(ln1_b), wr, rows(br), alpha, l)

        counts = cnt[0, :N_EXPERTS].astype(jnp.int32)
        padded = (counts + EBLK - 1) // EBLK * EBLK
        pends = jnp.cumsum(padded)
        pstarts = pends - padded
        ri = route[:, :4].astype(jnp.int32)

        def dest(e, r):
            start = jnp.sum(jnp.where(e[:, None] == e_ids[None, :], pstarts[None, :], 0), axis=1)
            return (start + r).reshape(ttot // TM, 1, TM)

        d1 = dest(ri[:, 0], ri[:, 2])
        d2 = dest(ri[:, 1], ri[:, 3])
        n_used = (pends[-1] // EBLK).astype(jnp.int32).reshape(1)
        first_row = jnp.minimum(blk_start, pends[-1] - 1)
        blk_e = jnp.sum((pends[None, :] <= first_row[:, None]).astype(jnp.int32), axis=1)
        blk_e = jnp.minimum(blk_e, N_EXPERTS - 1)
        seg_end = jnp.sum(jnp.where(blk_e[:, None] == e_ids[None, :], pends[None, :], 0), axis=1)
        nxt_row = jnp.minimum(seg_end, pends[-1] - 1)
        nxt_e = jnp.sum((pends[None, :] <= nxt_row[:, None]).astype(jnp.int32), axis=1)
        nxt_e = jnp.minimum(nxt_e, N_EXPERTS - 1)

        xs = _dispatch(h2, d1, d2, pends, n_slots)
        ys = _experts(xs, blk_e, nxt_e, n_used, w_e_gate, w_e_up, w_e_down, l)
        xa = _combine(x1, route, mod, rows(ln2_g), rows(ln2_b), ys, d1, d2, alpha, l,
                      latents_only=(l == depth - 1))
        xctx = None
    return xa[None]
```

```python
import functools

import jax
import jax.numpy as jnp
from jax import lax
from jax.experimental import pallas as pl
from jax.experimental.pallas import tpu as pltpu

D = 1024
GRID_W = 64
HD = 64
NQ = 8
NKV = 2
QPK = NQ // NKV
ATTN_W = NQ * HD
KV_W = NKV * HD
POOL_W = 256
POOL_WINDOWS = (2, 4, 8, 16)
POOL_GW = POOL_W // len(POOL_WINDOWS)
CONV_W = 256
CONV_K = 31
N_GROUPS = 4
EPG = 8
N_EXPERTS = N_GROUPS * EPG
D_EXPERT = 512
N_MOD = 6
ROPE_THETA = 10000.0
EPS = 1e-6

TM = 256
HALO = 16
EBLK = 256
LANES = 128
SUBLANES = 8
BF16_ROWS = 16
VROWS = HD + BF16_ROWS
CONV_K_PAD = 32
LOG2E = 1.4426950408889634
BOUND_MARGIN = 1.0 + 2.0 ** -5
V7X_VMEM_BYTES = 64 * 1024 * 1024
VMEM_LIMIT = V7X_VMEM_BYTES - 8 * 1024 * 1024
NEG_BIG = -1e30

_HI = lax.Precision.HIGHEST


def _cparams(sem):
    return pltpu.CompilerParams(dimension_semantics=sem, vmem_limit_bytes=VMEM_LIMIT)


def _full(shape):
    n = len(shape)
    return pl.BlockSpec(shape, lambda *a, n=n: (0,) * n)


def _layer(shape, l):
    n = len(shape)
    return pl.BlockSpec((None,) + tuple(shape), lambda *a, n=n: (l,) + (0,) * n)


def _mod_kernel(cc_ref, w_ref, b_ref, o_ref):
    cc = cc_ref[...]
    a = cc * jax.nn.sigmoid(cc)
    o_ref[0] = jnp.dot(a, w_ref[0], precision=_HI, preferred_element_type=jnp.float32) + b_ref[0]


def _modulation(cc, w_mod, b_mod):
    depth = w_mod.shape[0]
    nw = w_mod.shape[2]
    bn = D
    return pl.pallas_call(
        _mod_kernel,
        grid=(depth, nw // bn),
        in_specs=[
            pl.BlockSpec((SUBLANES, D), lambda l, j: (0, 0)),
            pl.BlockSpec((1, D, bn), lambda l, j: (l, 0, j)),
            pl.BlockSpec((1, 1, bn), lambda l, j: (l, 0, j)),
        ],
        out_specs=pl.BlockSpec((1, SUBLANES, bn), lambda l, j: (l, 0, j)),
        out_shape=jax.ShapeDtypeStruct((depth, SUBLANES, nw), jnp.float32),
        compiler_params=_cparams(("arbitrary", "arbitrary")),
        name="modulation",
    )(cc, w_mod, b_mod.reshape(depth, 1, nw))


def _ln_plain(x):
    mu = jnp.mean(x, axis=-1, keepdims=True)
    xc = x - mu
    var = jnp.mean(xc * xc, axis=-1, keepdims=True)
    return xc * lax.rsqrt(var + EPS)


def _mod_rows(mod_ref, idx):
    row = jnp.where(pl.program_id(0) == 0, 1, 0)
    return mod_ref[pl.ds(row, 1), idx * D:(idx + 1) * D]


def _rms_rope_t(t, gain, cos, sin):
    ms = jnp.mean(t * t, axis=0, keepdims=True)
    t = t * lax.rsqrt(ms + EPS) * gain
    outs = []
    for a in range(2):
        t1 = t[32 * a:32 * a + 16]
        t2 = t[32 * a + 16:32 * a + 32]
        c = cos[16 * a:16 * a + 16]
        s = sin[16 * a:16 * a + 16]
        outs.append(t1 * c - t2 * s)
        outs.append(t2 * c + t1 * s)
    return jnp.concatenate(outs, axis=0)


def _stream_tile(x_ref, ctx_ref):
    if ctx_ref is None:
        return x_ref[...]
    return jnp.where(pl.program_id(0) == 0, ctx_ref[...], x_ref[...])


def _stream_specs(xa, ctx):
    if ctx is None:
        return [pl.BlockSpec((TM, D), lambda i: (i, 0))], [xa]
    return [pl.BlockSpec((TM, D), lambda i: (jnp.maximum(i - 1, 0), 0)), _full((TM, D))], [xa, ctx]


def _inproj_kernel(*refs, split_stream):
    x_ref, ctx_ref = (refs[0], refs[1]) if split_stream else (refs[0], None)
    (mod_ref, wt_ref, wn_ref, qg_ref, kg_ref, cos_ref, sin_ref,
     q_ref, k_ref, v_ref, kmax_ref, u_ref, ag_ref) = refs[2 if split_stream else 1:]
    x = _stream_tile(x_ref, ctx_ref)
    h = _ln_plain(x) * (1.0 + _mod_rows(mod_ref, 1)) + _mod_rows(mod_ref, 0)
    hb = h.astype(jnp.bfloat16)
    qkv_t = lax.dot_general(wt_ref[...], hb, (((1,), (1,)), ((), ())),
                            preferred_element_type=jnp.float32)
    cos = cos_ref[...]
    sin = sin_ref[...]
    qg = qg_ref[...]
    kg = kg_ref[...]
    ones_row = jnp.where(lax.broadcasted_iota(jnp.int32, (HD, TM), 0) == 0, 1.0, 0.0)
    ones_pad = jnp.where(lax.broadcasted_iota(jnp.int32, (VROWS - HD, TM), 0) == 0, 1.0, 0.0)
    for hq in range(NQ):
        t = _rms_rope_t(qkv_t[hq * HD:(hq + 1) * HD], qg, cos, sin) * (LOG2E * HD ** -0.5)
        nq = jnp.sqrt(jnp.sum(t * t, axis=0, keepdims=True))
        q_ref[hq // QPK, hq % QPK] = jnp.concatenate(
            [t, ones_row * (-BOUND_MARGIN * nq)], axis=0).astype(jnp.bfloat16)
    for g in range(NKV):
        kt = _rms_rope_t(qkv_t[ATTN_W + g * HD:ATTN_W + (g + 1) * HD], kg, cos, sin)
        kn = jnp.max(jnp.sum(kt * kt, axis=0, keepdims=True), axis=1, keepdims=True)
        kmax_ref[0, g] = jnp.broadcast_to(kn, (SUBLANES, LANES))
        k_ref[g] = jnp.concatenate([kt, ones_row], axis=0).T.astype(jnp.bfloat16)
        vt = qkv_t[ATTN_W + KV_W + g * HD:ATTN_W + KV_W + (g + 1) * HD]
        v_ref[g] = jnp.concatenate([vt, ones_pad], axis=0).astype(jnp.bfloat16)
    uag = jnp.dot(hb, wn_ref[...], preferred_element_type=jnp.float32)
    u_ref[...] = uag[:, :POOL_W]
    a = uag[:, POOL_W:POOL_W + CONV_W]
    gt = uag[:, POOL_W + CONV_W:]
    ag_ref[...] = a * jax.nn.sigmoid(gt)


def _inproj(xa, ctx, mod, w_t, w_n, qg, kg, cos_t, sin_t, l):
    ttot = xa.shape[0] + (0 if ctx is None else ctx.shape[0])
    nt = ttot // TM
    stream_specs, stream_args = _stream_specs(xa, ctx)
    return pl.pallas_call(
        functools.partial(_inproj_kernel, split_stream=ctx is not None),
        grid=(nt,),
        in_specs=stream_specs + [
            _layer((SUBLANES, N_MOD * D), l),
            _layer((ATTN_W + 2 * KV_W, D), l),
            _layer((D, POOL_W + 2 * CONV_W), l),
            _layer((HD, TM), l),
            _layer((HD, TM), l),
            pl.BlockSpec((HD // 2, TM), lambda i: (0, i)),
            pl.BlockSpec((HD // 2, TM), lambda i: (0, i)),
        ],
        out_specs=[
            pl.BlockSpec((NKV, QPK, 2 * HD, TM), lambda i: (0, 0, 0, i)),
            pl.BlockSpec((NKV, TM, 2 * HD), lambda i: (0, i, 0)),
            pl.BlockSpec((NKV, VROWS, TM), lambda i: (0, 0, i)),
            pl.BlockSpec((1, NKV, 8, LANES), lambda i: (i, 0, 0, 0)),
            pl.BlockSpec((TM, POOL_W), lambda i: (i, 0)),
            pl.BlockSpec((TM, CONV_W), lambda i: (i, 0)),
        ],
        out_shape=[
            jax.ShapeDtypeStruct((NKV, QPK, 2 * HD, ttot), jnp.bfloat16),
            jax.ShapeDtypeStruct((NKV, ttot, 2 * HD), jnp.bfloat16),
            jax.ShapeDtypeStruct((NKV, VROWS, ttot), jnp.bfloat16),
            jax.ShapeDtypeStruct((nt, NKV, 8, LANES), jnp.float32),
            jax.ShapeDtypeStruct((ttot, POOL_W), jnp.float32),
            jax.ShapeDtypeStruct((ttot, CONV_W), jnp.float32),
        ],
        compiler_params=_cparams(("arbitrary",)),
        name="inproj",
    )(*stream_args, mod, w_t, w_n, qg, kg, cos_t, sin_t)


TK = 256
GCH = 5
LOOP_STEPS = 6
L_MIN = 2.0 ** -60


def _attn_kernel(kmax_ref, q_ref, k_ref, v_ref, o_ref, qs_ref, s_ref, acc_ref, m_ref, lmin_ref, *,
                 n_chunks):
    g = pl.program_id(0)
    i = pl.program_id(1)
    nch = jnp.where(i == 0, 1, n_chunks)
    row = lax.broadcasted_iota(jnp.int32, (2 * HD, TM), 0)
    kmax = kmax_ref[g]

    def prep():
        for hh in range(QPK):
            qf = q_ref[0, hh].astype(jnp.float32)
            qs_ref[hh] = jnp.where(row == HD, qf * kmax, qf).astype(jnp.bfloat16)

    def step(nxt, cur):
        for kk in range(max(grp[1] for grp in (nxt, cur) if grp is not None)):
            do_nxt = nxt is not None and kk < nxt[1]
            do_cur = cur is not None and kk < cur[1]
            rows = slice(kk * TK, (kk + 1) * TK)
            if do_nxt:
                kc = k_ref[0, pl.ds(pl.multiple_of((nxt[0] + kk) * TK, TK), TK), :]
            if do_cur:
                vc = v_ref[0, :, pl.ds(pl.multiple_of((cur[0] + kk) * TK, TK), TK)]
            for hh in range(QPK):
                if do_nxt:
                    s_ref[nxt[2], hh, rows, :] = jnp.dot(
                        kc, qs_ref[hh], preferred_element_type=jnp.float32)
                if do_cur:
                    p = jnp.exp2(s_ref[cur[2], hh, rows, :]).astype(jnp.bfloat16)
                    acc_ref[hh] += jnp.dot(vc, p, preferred_element_type=jnp.float32)

    def finish():
        outs = [acc_ref[hh, 0:HD, :] / acc_ref[hh, HD:HD + 1, :] for hh in range(QPK)]
        o_ref[...] = jnp.concatenate(outs, axis=0).T
        lmin_ref[0] = jnp.min(acc_ref[:, HD:HD + 1, :])

    @pl.when(i == 0)
    def _():
        prep()
        kc = k_ref[0, 0:TM, :]
        vc = v_ref[0, :, 0:TM]
        for hh in range(QPK):
            p = jnp.exp2(jnp.dot(kc, qs_ref[hh], preferred_element_type=jnp.float32))
            acc_ref[hh] = jnp.dot(vc, p.astype(jnp.bfloat16), preferred_element_type=jnp.float32)
        finish()

    @pl.when(i > 0)
    def _():
        prep()
        acc_ref[...] = jnp.zeros(acc_ref.shape, jnp.float32)
        n_mid = (n_chunks - 2) // GCH
        tail = n_chunks - 1 - GCH * n_mid
        taper = {1: [1], 2: [1, 1]}.get(tail, [tail - 2, 1, 1])
        sizes = [1] + [GCH] * n_mid + taper
        firsts = [sum(sizes[:n]) for n in range(len(sizes))]

        def group(n):
            return (firsts[n], sizes[n], n % 2)

        step(group(0), None)
        step(group(1), group(0))
        n_loop = max(n_mid - 1, 0) // LOOP_STEPS

        def body(j, carry):
            c0 = 1 + LOOP_STEPS * GCH * j
            for u in range(LOOP_STEPS):
                step((c0 + (u + 1) * GCH, GCH, u % 2), (c0 + u * GCH, GCH, (u + 1) % 2))
            return carry

        lax.fori_loop(0, n_loop, body, 0)
        for n in range(1 + LOOP_STEPS * n_loop, len(sizes) - 1):
            step(group(n + 1), group(n))
        step(None, group(len(sizes) - 1))
        finish()

    @pl.when(lmin_ref[0] < L_MIN)
    def _():
        for hh in range(QPK):
            qs_ref[hh] = jnp.where(row == HD, 0.0, q_ref[0, hh].astype(jnp.float32)
                                   ).astype(jnp.bfloat16)
        acc_ref[...] = jnp.zeros(acc_ref.shape, jnp.float32)
        m_ref[...] = jnp.full(m_ref.shape, NEG_BIG, jnp.float32)

        def body2(c, carry):
            start = pl.multiple_of(c * TK, TK)
            kc = k_ref[0, pl.ds(start, TK), :]
            vc = v_ref[0, :, pl.ds(start, TK)]
            for hh in range(QPK):
                s = jnp.dot(kc, qs_ref[hh], preferred_element_type=jnp.float32)
                m_old = m_ref[hh]
                m_new = jnp.maximum(m_old, jnp.max(s, axis=0, keepdims=True))
                p = jnp.exp2(s - m_new).astype(jnp.bfloat16)
                acc_ref[hh] = jnp.exp2(m_old - m_new) * acc_ref[hh] + jnp.dot(
                    vc, p, preferred_element_type=jnp.float32)
                m_ref[hh] = m_new
            return carry

        lax.fori_loop(0, nch, body2, 0)
        finish()


def _attention(q_t, k2, v_t, kmax):
    ttot = k2.shape[1]
    nt = ttot // TM
    assert ttot % TK == 0 and ttot // TK >= 3
    kern = functools.partial(_attn_kernel, n_chunks=ttot // TK)
    grid_spec = pltpu.PrefetchScalarGridSpec(
        num_scalar_prefetch=1,
        grid=(NKV, nt),
        in_specs=[
            pl.BlockSpec((1, QPK, 2 * HD, TM), lambda g, i, km: (g, 0, 0, i)),
            pl.BlockSpec((1, ttot, 2 * HD), lambda g, i, km: (g, 0, 0)),
            pl.BlockSpec((1, VROWS, ttot), lambda g, i, km: (g, 0, 0)),
        ],
        out_specs=pl.BlockSpec((TM, QPK * HD), lambda g, i, km: (i, g)),
        scratch_shapes=[
            pltpu.VMEM((QPK, 2 * HD, TM), jnp.bfloat16),
            pltpu.VMEM((2, QPK, GCH * TK, TM), jnp.float32),
            pltpu.VMEM((QPK, VROWS, TM), jnp.float32),
            pltpu.VMEM((QPK, 1, TM), jnp.float32),
            pltpu.SMEM((1,), jnp.float32),
        ],
    )
    return pl.pallas_call(
        kern,
        grid_spec=grid_spec,
        out_shape=jax.ShapeDtypeStruct((ttot, ATTN_W), jnp.float32),
        compiler_params=_cparams(("arbitrary", "arbitrary")),
        name="attention",
    )(kmax, q_t, k2, v_t)


def _fill_ext(ext_ref, cur_ref, prev_ref, next_ref, left_ok, right_ok):
    zero = jnp.zeros((HALO, cur_ref.shape[1]), jnp.float32)
    ext_ref[0:HALO, :] = jnp.where(left_ok, prev_ref[...], zero)
    ext_ref[HALO:HALO + TM, :] = cur_ref[...]
    ext_ref[HALO + TM:HALO + TM + HALO, :] = jnp.where(right_ok, next_ref[...], zero)


def _mix_kernel(*refs, alpha, n_lat, split_stream):
    x_ref, ctx_ref = (refs[0], refs[1]) if split_stream else (refs[0], None)
    (attn_ref, u_ref, up_ref, un_ref, ag_ref, agp_ref, agn_ref, mod_ref,
     wpool_ref, pscale_ref, wdw_ref, bdw_ref, cvg_ref, cvb_ref, wpw_ref, wout_ref,
     ln1g_ref, ln1b_ref, wr_ref, br_ref,
     x1_ref, h2_ref, route_ref, cnt_ref,
     uext_ref, agext_ref, agsh_ref, base_ref) = refs[2 if split_stream else 1:]
    i = pl.program_id(0)
    nt = pl.num_programs(0)
    is_ctx = i == 0
    left_ok = i >= 2
    right_ok = jnp.logical_and(i >= 1, i < nt - 1)

    @pl.when(i == 0)
    def _():
        base_ref[...] = jnp.zeros(base_ref.shape, jnp.float32)

    _fill_ext(uext_ref, u_ref, up_ref, un_ref, left_ok, right_ok)
    _fill_ext(agext_ref, ag_ref, agp_ref, agn_ref, left_ok, right_ok)

    def ush(off, rows=TM):
        return uext_ref[HALO + off:HALO + off + rows, :]

    a2 = ush(-8, TM + 15) + ush(-7, TM + 15)
    a4 = a2[0:TM + 13] + a2[2:TM + 15]
    a8 = a4[0:TM + 9] + a4[4:TM + 13]
    a16 = a8[0:TM] + a8[8:TM + 8]
    sums = (a2[7:7 + TM], a4[6:6 + TM], a8[4:4 + TM], a16)
    pos = lax.broadcasted_iota(jnp.int32, (TM, 1), 0) + jnp.where(is_ctx, 0, (i - 1) * TM)
    seq_n = jnp.where(is_ctx, TM, n_lat)
    u_cur = u_ref[...]
    lane = lax.broadcasted_iota(jnp.int32, (TM, POOL_W), 1)
    pooled = jnp.zeros((TM, POOL_W), jnp.float32)
    for gi, w in enumerate(POOL_WINDOWS):
        lo = jnp.maximum(pos - w // 2, 0)
        hi = jnp.minimum(pos + (w - w // 2), seq_n)
        inv = 1.0 / (hi - lo).astype(jnp.float32)
        in_group = jnp.logical_and(lane >= gi * POOL_GW, lane < (gi + 1) * POOL_GW)
        pooled = jnp.where(in_group, sums[gi] * inv - u_cur, pooled)
    y_pool = jnp.dot(pooled.astype(jnp.bfloat16), wpool_ref[...],
                     preferred_element_type=jnp.float32) * pscale_ref[...]

    sh_rows = TM + 2 * HALO - SUBLANES
    for b in range(1, SUBLANES):
        agsh_ref[b - 1] = agext_ref[b:b + sh_rows, :]
    conv = jnp.zeros((TM, CONV_W), jnp.float32) + bdw_ref[...]
    for k in range(CONV_K):
        off = HALO + k - CONV_K // 2
        b, a8 = off % SUBLANES, off // SUBLANES * SUBLANES
        tap = agext_ref[a8:a8 + TM, :] if b == 0 else agsh_ref[b - 1, a8:a8 + TM, :]
        conv = conv + tap * wdw_ref[k:k + 1, :]
    cn = _ln_plain(conv) * cvg_ref[...] + cvb_ref[...]
    cn = cn * jax.nn.sigmoid(cn)
    y_conv = jnp.dot(cn.astype(jnp.bfloat16), wpw_ref[...], preferred_element_type=jnp.float32)

    y = jnp.dot(attn_ref[...].astype(jnp.bfloat16), wout_ref[0:ATTN_W, :],
                preferred_element_type=jnp.float32)
    y = y + jnp.dot(y_pool.astype(jnp.bfloat16), wout_ref[ATTN_W:ATTN_W + POOL_W, :],
                    preferred_element_type=jnp.float32)
    y = y + jnp.dot(y_conv.astype(jnp.bfloat16), wout_ref[ATTN_W + POOL_W:, :],
                    preferred_element_type=jnp.float32)
    x_res = _stream_tile(x_ref, ctx_ref)
    x1 = _ln_plain(alpha * x_res + _mod_rows(mod_ref, 2) * y) * ln1g_ref[...] + ln1b_ref[...]
    x1_ref[...] = x1
    h2 = _ln_plain(x1) * (1.0 + _mod_rows(mod_ref, 4)) + _mod_rows(mod_ref, 3)
    h_hi = h2.astype(jnp.bfloat16)
    h_rounded = h_hi.astype(jnp.float32)
    h_lo = (h2 - h_rounded).astype(jnp.bfloat16)
    bits = pltpu.bitcast(h_rounded, jnp.uint32)
    h2_ref[...] = bits[:, D // 2:] | lax.shift_right_logical(bits[:, :D // 2], jnp.uint32(16))
    hw = jnp.dot(h_hi, wr_ref[...], preferred_element_type=jnp.float32)
    logits = (hw[:, :LANES] + hw[:, LANES:] + br_ref[...]
              + jnp.dot(h_lo, wr_ref[:, :LANES], preferred_element_type=jnp.float32))
    ln = lax.broadcasted_iota(jnp.int32, (TM, LANES), 1)
    lnf = ln.astype(jnp.float32)
    no_lane = jnp.float32(LANES)
    neg = jnp.float32(-jnp.inf)
    is_g = ln < N_GROUPS
    lg = jnp.where(is_g, logits, neg)
    mg = jnp.max(lg, axis=-1, keepdims=True)
    gsel = jnp.min(jnp.where(lg == mg, lnf, no_lane), axis=-1, keepdims=True)
    p_group = 1.0 / jnp.sum(jnp.exp(lg - mg), axis=-1, keepdims=True)
    e_lane = lnf - N_GROUPS
    in_sel = jnp.logical_and(e_lane >= gsel * EPG, e_lane < (gsel + 1) * EPG)
    le = jnp.where(in_sel, logits, neg)
    v1 = jnp.max(le, axis=-1, keepdims=True)
    e1 = jnp.min(jnp.where(le == v1, e_lane, no_lane), axis=-1, keepdims=True)
    le2 = jnp.where(e_lane == e1, neg, le)
    v2 = jnp.max(le2, axis=-1, keepdims=True)
    e2 = jnp.min(jnp.where(le2 == v2, e_lane, no_lane), axis=-1, keepdims=True)
    ex = jnp.exp(v2 - v1)
    gate1 = p_group / (1.0 + ex)
    gate2 = p_group * ex / (1.0 + ex)

    oh = jnp.logical_or(lnf == e1, lnf == e2)
    ohb = jnp.where(oh, 1.0, 0.0).astype(jnp.bfloat16)
    r_i = lax.broadcasted_iota(jnp.int32, (TM, TM), 0)
    c_i = lax.broadcasted_iota(jnp.int32, (TM, TM), 1)
    tri = jnp.where(c_i < r_i, 1.0, 0.0).astype(jnp.bfloat16)
    before = jnp.dot(tri, ohb, preferred_element_type=jnp.float32) + base_ref[...]
    rank1 = jnp.sum(jnp.where(lnf == e1, before, 0.0), axis=-1, keepdims=True)
    rank2 = jnp.sum(jnp.where(lnf == e2, before, 0.0), axis=-1, keepdims=True)
    base_ref[...] = base_ref[...] + jnp.sum(ohb.astype(jnp.float32), axis=0, keepdims=True)
    cnt_ref[...] = jnp.broadcast_to(base_ref[...], cnt_ref.shape)

    route = jnp.zeros((TM, LANES), jnp.float32)
    for j, val in enumerate((e1, e2, rank1, rank2, gate1, gate2)):
        route = jnp.where(ln == j, val, route)
    route_ref[...] = route


def _mixer(xa, ctx, attn, u, ag, mod, wpool, pscale, wdw, bdw, cvg, cvb, wpw, wout, ln1g, ln1b,
           wr, br, alpha, l):
    ttot = attn.shape[0]
    nt = ttot // TM
    stream_specs, stream_args = _stream_specs(xa, ctx)
    hb = TM // HALO
    nhb = ttot // HALO

    def tile(w):
        return pl.BlockSpec((TM, w), lambda i: (i, 0))

    def prev(w):
        return pl.BlockSpec((HALO, w), lambda i: (jnp.maximum(i * hb - 1, 0), 0))

    def nxt(w):
        return pl.BlockSpec((HALO, w), lambda i: (jnp.minimum((i + 1) * hb, nhb - 1), 0))

    kern = functools.partial(_mix_kernel, alpha=alpha, n_lat=ttot - TM,
                             split_stream=ctx is not None)
    return pl.pallas_call(
        kern,
        grid=(nt,),
        in_specs=stream_specs + [
            tile(ATTN_W),
            tile(POOL_W), prev(POOL_W), nxt(POOL_W),
            tile(CONV_W), prev(CONV_W), nxt(CONV_W),
            _layer((SUBLANES, N_MOD * D), l),
            _layer((POOL_W, POOL_W), l), _layer((1, POOL_W), l),
            _layer((CONV_K_PAD, CONV_W), l), _layer((1, CONV_W), l), _layer((1, CONV_W), l),
            _layer((1, CONV_W), l),
            _layer((CONV_W, CONV_W), l), _layer((D, D), l),
            _layer((1, D), l), _layer((1, D), l),
            _layer((D, 2 * LANES), l), _layer((1, LANES), l),
        ],
        out_specs=[tile(D), tile(D // 2), tile(LANES), _full((SUBLANES, LANES))],
        out_shape=[
            jax.ShapeDtypeStruct((ttot, D), jnp.float32),
            jax.ShapeDtypeStruct((ttot, D // 2), jnp.uint32),
            jax.ShapeDtypeStruct((ttot, LANES), jnp.float32),
            jax.ShapeDtypeStruct((SUBLANES, LANES), jnp.float32),
        ],
        scratch_shapes=[
            pltpu.VMEM((TM + 2 * HALO, POOL_W), jnp.float32),
            pltpu.VMEM((TM + 2 * HALO, CONV_W), jnp.float32),
            pltpu.VMEM((SUBLANES - 1, TM + 2 * HALO - SUBLANES, CONV_W), jnp.float32),
            pltpu.VMEM((1, LANES), jnp.float32),
        ],
        compiler_params=_cparams(("arbitrary",)),
        name="mixer",
    )(*stream_args, attn, u, u, u, ag, ag, ag, mod, wpool, pscale, wdw, bdw, cvg, cvb, wpw, wout,
      ln1g, ln1b, wr, br)


ROW_GROUP = SUBLANES


def _dispatch_kernel(pends_ref, d1_ref, d2_ref, h_hbm, xs_ref, zero_buf, h_buf, lsems, ssems,
                     zsem, *, n_blk):
    i = pl.program_id(0)
    nt = pl.num_programs(0)
    slot = i % 3

    def load(t, to_slot):
        return pltpu.make_async_copy(h_hbm.at[pl.ds(pl.multiple_of(t * TM, TM), TM), :],
                                     h_buf.at[to_slot], lsems.at[to_slot])

    def wait_scatter(of_slot):
        for _ in range(2):
            pltpu.make_async_copy(h_buf.at[of_slot], xs_ref.at[pl.ds(0, TM), :],
                                  ssems.at[of_slot]).wait()

    @pl.when(i == 0)
    def _():
        load(0, 0).start()

        @pl.when(nt > 1)
        def _():
            load(1, 1).start()

        zero_buf[...] = jnp.zeros(zero_buf.shape, zero_buf.dtype)

        def zero_block(first_row):
            return pltpu.make_async_copy(
                zero_buf, xs_ref.at[pl.ds(pl.multiple_of(first_row, EBLK), EBLK), :], zsem)

        def seg_nonempty(e):
            return pends_ref[e] > (pends_ref[e - 1] if e > 0 else 0)

        for e in range(N_EXPERTS):
            @pl.when(seg_nonempty(e))
            def _():
                zero_block(pends_ref[e] - EBLK).start()
        n_used = pends_ref[N_EXPERTS - 1] // EBLK

        def tail_start(b, carry):
            zero_block(b * EBLK).start()
            return carry

        lax.fori_loop(n_used, n_blk, tail_start, 0)
        for e in range(N_EXPERTS):
            @pl.when(seg_nonempty(e))
            def _():
                zero_block(pends_ref[e] - EBLK).wait()

        def tail_wait(b, carry):
            zero_block(b * EBLK).wait()
            return carry

        lax.fori_loop(n_used, n_blk, tail_wait, 0)

    load(i, slot).wait()

    def start(r8, carry):
        base = pl.multiple_of(r8 * ROW_GROUP, ROW_GROUP)
        rows = h_buf.at[slot, pl.ds(base, ROW_GROUP), :]
        for j in range(ROW_GROUP):
            for queue, d_ref in enumerate((d1_ref, d2_ref)):
                pltpu.make_async_copy(rows.at[pl.ds(j, 1), :],
                                      xs_ref.at[pl.ds(d_ref[0, 0, base + j], 1), :],
                                      ssems.at[slot]).start(priority=queue)
        return carry

    lax.fori_loop(0, TM // ROW_GROUP, start, 0)

    @pl.when(i >= 1)
    def _():
        wait_scatter((i + 2) % 3)

    @pl.when(i + 2 < nt)
    def _():
        load(i + 2, (i + 2) % 3).start()

    @pl.when(i == nt - 1)
    def _():
        wait_scatter(slot)


def _dispatch(h2, d1, d2, pends, n_slots):
    ttot = h2.shape[0]
    nt = ttot // TM
    smem = pl.BlockSpec((1, 1, TM), lambda i, pe: (i, 0, 0), memory_space=pltpu.SMEM)
    grid_spec = pltpu.PrefetchScalarGridSpec(
        num_scalar_prefetch=1,
        grid=(nt,),
        in_specs=[smem, smem, pl.BlockSpec(memory_space=pl.ANY)],
        out_specs=pl.BlockSpec(memory_space=pl.ANY),
        scratch_shapes=[pltpu.VMEM((EBLK, D // 2), jnp.uint32),
                        pltpu.VMEM((3, TM, D // 2), jnp.uint32),
                        pltpu.SemaphoreType.DMA((3,)), pltpu.SemaphoreType.DMA((3,)),
                        pltpu.SemaphoreType.DMA(())],
    )
    return pl.pallas_call(
        functools.partial(_dispatch_kernel, n_blk=n_slots // EBLK),
        grid_spec=grid_spec,
        out_shape=jax.ShapeDtypeStruct((n_slots, D // 2), jnp.uint32),
        compiler_params=_cparams(("arbitrary",)),
        name="dispatch",
    )(pends, d1, d2, h2)


def _expert_kernel(be_ref, ne_ref, nu_ref, x_ref, wg_hbm, wu_hbm, wd_hbm, y_ref,
                   wg_f, wu_f, wd_f, wgu_b, wd_b, seg_ref, sems, *, l):
    b = pl.program_id(0)
    e = be_ref[b]
    new_expert = jnp.logical_or(b == 0, e != be_ref[jnp.maximum(b - 1, 0)])

    def fetch(expert, slot):
        return [pltpu.make_async_copy(src.at[l, expert], dst.at[slot], sems.at[slot])
                for src, dst in ((wg_hbm, wg_f), (wu_hbm, wu_f), (wd_hbm, wd_f))]

    @pl.when(b == 0)
    def _():
        seg_ref[0] = 0
        for cp in fetch(e, 0):
            cp.start()

    @pl.when(jnp.logical_and(b < nu_ref[0], new_expert))
    def _():
        seg = seg_ref[0]
        slot = seg % 2
        for cp in fetch(e, slot):
            cp.wait()

        @pl.when(ne_ref[b] != e)
        def _():
            for cp in fetch(ne_ref[b], 1 - slot):
                cp.start()

        wgu_b[:, :D_EXPERT] = wg_f[slot].astype(jnp.bfloat16)
        wgu_b[:, D_EXPERT:] = wu_f[slot].astype(jnp.bfloat16)
        wd_b[...] = wd_f[slot].astype(jnp.bfloat16)
        seg_ref[0] = seg + 1

    @pl.when(b < nu_ref[0])
    def _():
        xp = x_ref[...]
        x_lo = pltpu.bitcast(lax.shift_left(xp, jnp.uint32(16)), jnp.float32).astype(jnp.bfloat16)
        x_hi = pltpu.bitcast(xp & jnp.uint32(0xFFFF0000), jnp.float32).astype(jnp.bfloat16)
        gu = (jnp.dot(x_lo, wgu_b[:D // 2, :], preferred_element_type=jnp.float32)
              + jnp.dot(x_hi, wgu_b[D // 2:, :], preferred_element_type=jnp.float32))
        gt = gu[:, :D_EXPERT]
        hm = (gt * jax.nn.sigmoid(gt) * gu[:, D_EXPERT:]).astype(jnp.bfloat16)
        y_ref[...] = jnp.dot(hm, wd_b[...], preferred_element_type=jnp.float32)

    @pl.when(b >= nu_ref[0])
    def _():
        y_ref[...] = jnp.zeros(y_ref.shape, jnp.float32)


def _experts(xs, blk_e, nxt_e, n_used, wg, wu, wd, l):
    n_slots = xs.shape[0]
    nblk = n_slots // EBLK

    def row_map(b, be, ne, nu):
        return (jnp.minimum(b, nu[0] - 1), 0)

    hbm = pl.BlockSpec(memory_space=pl.ANY)
    grid_spec = pltpu.PrefetchScalarGridSpec(
        num_scalar_prefetch=3,
        grid=(nblk,),
        in_specs=[pl.BlockSpec((EBLK, D // 2), row_map), hbm, hbm, hbm],
        out_specs=pl.BlockSpec((EBLK, D), lambda b, be, ne, nu: (b, 0)),
        scratch_shapes=[pltpu.VMEM((2, D, D_EXPERT), jnp.float32),
                        pltpu.VMEM((2, D, D_EXPERT), jnp.float32),
                        pltpu.VMEM((2, D_EXPERT, D), jnp.float32),
                        pltpu.VMEM((D, 2 * D_EXPERT), jnp.bfloat16),
                        pltpu.VMEM((D_EXPERT, D), jnp.bfloat16),
                        pltpu.SMEM((1,), jnp.int32),
                        pltpu.SemaphoreType.DMA((2,))],
    )
    return pl.pallas_call(
        functools.partial(_expert_kernel, l=l),
        grid_spec=grid_spec,
        out_shape=jax.ShapeDtypeStruct((n_slots, D), jnp.float32),
        compiler_params=_cparams(("arbitrary",)),
        name="experts",
    )(blk_e, nxt_e, n_used, xs, wg, wu, wd)


def _combine_kernel(d1_ref, d2_ref, d1n_ref, d2n_ref, x1_ref, route_ref, mod_ref, ln2g_ref,
                    ln2b_ref, ys_ref, o_ref, y_buf, sems, *, alpha):
    i = pl.program_id(0)
    nt = pl.num_programs(0)
    slot = i % 2

    def gather(da_ref, db_ref, to_slot):
        def start(r8, carry):
            base = pl.multiple_of(r8 * ROW_GROUP, ROW_GROUP)
            for k, d_ref in enumerate((da_ref, db_ref)):
                rows = y_buf.at[to_slot, k, pl.ds(base, ROW_GROUP), :]
                for j in range(ROW_GROUP):
                    pltpu.make_async_copy(ys_ref.at[pl.ds(d_ref[0, 0, base + j], 1), :],
                                          rows.at[pl.ds(j, 1), :],
                                          sems.at[to_slot]).start(priority=j % 2)
            return carry

        lax.fori_loop(0, TM // ROW_GROUP, start, 0)

    @pl.when(i == 0)
    def _():
        gather(d1_ref, d2_ref, 0)

    @pl.when(i < nt - 1)
    def _():
        gather(d1n_ref, d2n_ref, 1 - slot)

    for k in range(2):
        pltpu.make_async_copy(ys_ref.at[pl.ds(0, TM), :], y_buf.at[slot, k], sems.at[slot]).wait()

    route = route_ref[...]
    g1 = route[:, 4:5]
    g2 = route[:, 5:6]
    o = g1 * y_buf[slot, 0] + g2 * y_buf[slot, 1]
    z = alpha * x1_ref[...] + _mod_rows(mod_ref, 5) * o
    o_ref[...] = _ln_plain(z) * ln2g_ref[...] + ln2b_ref[...]


def _combine(x1, route, mod, ln2g, ln2b, ys, d1, d2, alpha, l, latents_only):
    ttot = x1.shape[0]
    nt = ttot // TM
    smem = pl.BlockSpec((1, 1, TM), lambda i: (i, 0, 0), memory_space=pltpu.SMEM)
    smem_next = pl.BlockSpec((1, 1, TM), lambda i: (jnp.minimum(i + 1, nt - 1), 0, 0),
                             memory_space=pltpu.SMEM)
    kern = functools.partial(_combine_kernel, alpha=alpha)
    if latents_only:
        out_spec = pl.BlockSpec((TM, D), lambda i: (jnp.maximum(i - 1, 0), 0))
        out_rows = ttot - TM
    else:
        out_spec = pl.BlockSpec((TM, D), lambda i: (i, 0))
        out_rows = ttot
    return pl.pallas_call(
        kern,
        grid=(nt,),
        in_specs=[smem, smem, smem_next, smem_next,
                  pl.BlockSpec((TM, D), lambda i: (i, 0)),
                  pl.BlockSpec((TM, LANES), lambda i: (i, 0)),
                  _layer((SUBLANES, N_MOD * D), l), _layer((1, D), l), _layer((1, D), l),
                  pl.BlockSpec(memory_space=pl.ANY)],
        out_specs=out_spec,
        out_shape=jax.ShapeDtypeStruct((out_rows, D), jnp.float32),
        scratch_shapes=[pltpu.VMEM((2, 2, TM, D), jnp.float32),
                        pltpu.SemaphoreType.DMA((2,))],
        compiler_params=_cparams(("arbitrary",)),
        name="combine",
    )(d1, d2, d1, d2, x1, route, mod, ln2g, ln2b, ys)


def _rope_tables_t(n_lat, n_ctx):
    rows = n_lat // GRID_W
    row = jnp.broadcast_to(jnp.arange(rows)[:, None], (rows, GRID_W)).reshape(-1)
    col = jnp.broadcast_to(jnp.arange(GRID_W)[None, :], (rows, GRID_W)).reshape(-1)
    n_freq = HD // 4
    inv = ROPE_THETA ** (-jnp.arange(n_freq, dtype=jnp.float32) / n_freq)
    pos = jnp.stack([row, col], axis=-1).astype(jnp.float32)
    ang = (pos[:, :, None] * inv).reshape(n_lat, 2 * n_freq)
    cos = jnp.concatenate([jnp.ones((n_ctx, 2 * n_freq), jnp.float32), jnp.cos(ang)], axis=0)
    sin = jnp.concatenate([jnp.zeros((n_ctx, 2 * n_freq), jnp.float32), jnp.sin(ang)], axis=0)
    return cos.T, sin.T


def kernel(x, c, ctx, c_ctx, w_mod, b_mod, w_in, q_gain, k_gain, w_pool, pool_scale, w_dw, b_dw,
           cv_ln_g, cv_ln_b, w_cv_pw, w_out, ln1_g, ln1_b, ln2_g, ln2_b, w_rg, b_rg, w_re, b_re,
           w_e_gate, w_e_up, w_e_down):
    depth = w_mod.shape[0]
    n_lat = x.shape[1]
    n_ctx = ctx.shape[1]
    assert x.shape[0] == 1 and n_ctx == TM and n_lat % TM == 0 and n_lat % GRID_W == 0
    ttot = n_ctx + n_lat
    alpha = float((2 * depth) ** 0.25)
    n_blk = -(-(2 * ttot) // EBLK) + N_EXPERTS
    n_slots = n_blk * EBLK

    xa, xctx = x[0], ctx[0]
    cc = jnp.concatenate([c, c_ctx[None], jnp.zeros((SUBLANES - 2, D), jnp.float32)], axis=0)
    mod = _modulation(cc, w_mod, b_mod)
    cos_t, sin_t = _rope_tables_t(n_lat, n_ctx)

    n_qkv = ATTN_W + 2 * KV_W
    bf = jnp.bfloat16
    w_t = jnp.swapaxes(w_in[:, :, :n_qkv], 1, 2).astype(bf)
    w_n = w_in[:, :, n_qkv:].astype(bf)
    qg = jnp.broadcast_to(q_gain[:, :, None], (depth, HD, TM))
    kg = jnp.broadcast_to(k_gain[:, :, None], (depth, HD, TM))
    n_pg = len(POOL_WINDOWS)
    wpool = (jnp.eye(n_pg, dtype=jnp.float32)[None, :, None, :, None] * w_pool[:, :, :, None, :]
             ).reshape(depth, POOL_W, POOL_W).astype(bf)
    wdw = jnp.pad(w_dw, ((0, 0), (0, CONV_K_PAD - CONV_K), (0, 0)))
    n_r = N_GROUPS + N_EXPERTS
    wr = jnp.concatenate([w_rg, w_re, jnp.zeros((depth, D, LANES - n_r), jnp.float32)], axis=2)
    br = jnp.concatenate([b_rg, b_re, jnp.zeros((depth, LANES - n_r), jnp.float32)], axis=1)
    wr_hi = wr.astype(bf)
    wr = jnp.concatenate([wr_hi, (wr - wr_hi.astype(jnp.float32)).astype(bf)], axis=2)
    wpw = w_cv_pw.astype(bf)
    wout = w_out.astype(bf)

    def rows(v):
        return v[:, None, :]

    e_ids = jnp.arange(N_EXPERTS, dtype=jnp.int32)
    blk_start = jnp.arange(n_blk, dtype=jnp.int32) * EBLK
    for l in range(depth):
        q_t, k2, v_t, kn2, u, ag = _inproj(xa, xctx, mod, w_t, w_n, qg, kg, cos_t, sin_t, l)
        kmax = jnp.sqrt(jnp.max(kn2[:, :, 0, 0], axis=0))
        attn = _attention(q_t, k2, v_t, kmax)
        x1, h2, route, cnt = _mixer(
            xa, xctx, attn, u, ag, mod, wpool, rows(pool_scale), wdw, rows(b_dw), rows(cv_ln_g),
            rows(cv_ln_b), wpw, wout, rows(ln1_g), rows(ln1_b), wr, rows(br), alpha, l)

        counts = cnt[0, :N_EXPERTS].astype(jnp.int32)
        padded = (counts + EBLK - 1) // EBLK * EBLK
        pends = jnp.cumsum(padded)
        pstarts = pends - padded
        ri = route[:, :4].astype(jnp.int32)

        def dest(e, r):
            start = jnp.sum(jnp.where(e[:, None] == e_ids[None, :], pstarts[None, :], 0), axis=1)
            return (start + r).reshape(ttot // TM, 1, TM)

        d1 = dest(ri[:, 0], ri[:, 2])
        d2 = dest(ri[:, 1], ri[:, 3])
        n_used = (pends[-1] // EBLK).astype(jnp.int32).reshape(1)
        first_row = jnp.minimum(blk_start, pends[-1] - 1)
        blk_e = jnp.sum((pends[None, :] <= first_row[:, None]).astype(jnp.int32), axis=1)
        blk_e = jnp.minimum(blk_e, N_EXPERTS - 1)
        seg_end = jnp.sum(jnp.where(blk_e[:, None] == e_ids[None, :], pends[None, :], 0), axis=1)
        nxt_row = jnp.minimum(seg_end, pends[-1] - 1)
        nxt_e = jnp.sum((pends[None, :] <= nxt_row[:, None]).astype(jnp.int32), axis=1)
        nxt_e = jnp.minimum(nxt_e, N_EXPERTS - 1)

        xs = _dispatch(h2, d1, d2, pends, n_slots)
        ys = _experts(xs, blk_e, nxt_e, n_used, w_e_gate, w_e_up, w_e_down, l)
        xa = _combine(x1, route, mod, rows(ln2_g), rows(ln2_b), ys, d1, d2, alpha, l,
                      latents_only=(l == depth - 1))
        xctx = None
    return xa[None]
```

```python
import functools

import jax
import jax.numpy as jnp
from jax import lax
from jax.experimental import pallas as pl
from jax.experimental.pallas import tpu as pltpu

D = 1024
GRID_W = 64
HD = 64
NQ = 8
NKV = 2
QPK = NQ // NKV
ATTN_W = NQ * HD
KV_W = NKV * HD
POOL_W = 256
POOL_WINDOWS = (2, 4, 8, 16)
POOL_GW = POOL_W // len(POOL_WINDOWS)
CONV_W = 256
CONV_K = 31
N_GROUPS = 4
EPG = 8
N_EXPERTS = N_GROUPS * EPG
D_EXPERT = 512
N_MOD = 6
ROPE_THETA = 10000.0
EPS = 1e-6

TM = 256
HALO = 16
EBLK = 256
LANES = 128
SUBLANES = 8
BF16_ROWS = 16
VROWS = HD + BF16_ROWS
CONV_K_PAD = 32
LOG2E = 1.4426950408889634
BOUND_MARGIN = 1.0 + 2.0 ** -5
V7X_VMEM_BYTES = 64 * 1024 * 1024
VMEM_LIMIT = V7X_VMEM_BYTES - 8 * 1024 * 1024
NEG_BIG = -1e30

_HI = lax.Precision.HIGHEST


def _cparams(sem):
    return pltpu.CompilerParams(dimension_semantics=sem, vmem_limit_bytes=VMEM_LIMIT)


def _full(shape):
    n = len(shape)
    return pl.BlockSpec(shape, lambda *a, n=n: (0,) * n)


def _layer(shape, l):
    n = len(shape)
    return pl.BlockSpec((None,) + tuple(shape), lambda *a, n=n: (l,) + (0,) * n)


def _mod_kernel(cc_ref, w_ref, b_ref, o_ref):
    cc = cc_ref[...]
    a = cc * jax.nn.sigmoid(cc)
    o_ref[0] = jnp.dot(a, w_ref[0], precision=_HI, preferred_element_type=jnp.float32) + b_ref[0]


def _modulation(cc, w_mod, b_mod):
    depth = w_mod.shape[0]
    nw = w_mod.shape[2]
    bn = D
    return pl.pallas_call(
        _mod_kernel,
        grid=(depth, nw // bn),
        in_specs=[
            pl.BlockSpec((SUBLANES, D), lambda l, j: (0, 0)),
            pl.BlockSpec((1, D, bn), lambda l, j: (l, 0, j)),
            pl.BlockSpec((1, 1, bn), lambda l, j: (l, 0, j)),
        ],
        out_specs=pl.BlockSpec((1, SUBLANES, bn), lambda l, j: (l, 0, j)),
        out_shape=jax.ShapeDtypeStruct((depth, SUBLANES, nw), jnp.float32),
        compiler_params=_cparams(("arbitrary", "arbitrary")),
        name="modulation",
    )(cc, w_mod, b_mod.reshape(depth, 1, nw))


def _ln_plain(x):
    mu = jnp.mean(x, axis=-1, keepdims=True)
    xc = x - mu
    var = jnp.mean(xc * xc, axis=-1, keepdims=True)
    return xc * lax.rsqrt(var + EPS)


def _mod_rows(mod_ref, idx):
    row = jnp.where(pl.program_id(0) == 0, 1, 0)
    return mod_ref[pl.ds(row, 1), idx * D:(idx + 1) * D]


def _rms_rope_t(t, gain, cos, sin):
    ms = jnp.mean(t * t, axis=0, keepdims=True)
    t = t * lax.rsqrt(ms + EPS) * gain
    outs = []
    for a in range(2):
        t1 = t[32 * a:32 * a + 16]
        t2 = t[32 * a + 16:32 * a + 32]
        c = cos[16 * a:16 * a + 16]
        s = sin[16 * a:16 * a + 16]
        outs.append(t1 * c - t2 * s)
        outs.append(t2 * c + t1 * s)
    return jnp.concatenate(outs, axis=0)


def _stream_tile(x_ref, ctx_ref):
    if ctx_ref is None:
        return x_ref[...]
    return jnp.where(pl.program_id(0) == 0, ctx_ref[...], x_ref[...])


def _stream_specs(xa, ctx):
    if ctx is None:
        return [pl.BlockSpec((TM, D), lambda i: (i, 0))], [xa]
    return [pl.BlockSpec((TM, D), lambda i: (jnp.maximum(i - 1, 0), 0)), _full((TM, D))], [xa, ctx]


def _inproj_kernel(*refs, split_stream):
    x_ref, ctx_ref = (refs[0], refs[1]) if split_stream else (refs[0], None)
    (mod_ref, wt_ref, wn_ref, qg_ref, kg_ref, cos_ref, sin_ref,
     q_ref, k_ref, v_ref, kmax_ref, u_ref, ag_ref) = refs[2 if split_stream else 1:]
    x = _stream_tile(x_ref, ctx_ref)
    h = _ln_plain(x) * (1.0 + _mod_rows(mod_ref, 1)) + _mod_rows(mod_ref, 0)
    hb = h.astype(jnp.bfloat16)
    qkv_t = lax.dot_general(wt_ref[...], hb, (((1,), (1,)), ((), ())),
                            preferred_element_type=jnp.float32)
    cos = cos_ref[...]
    sin = sin_ref[...]
    qg = qg_ref[...]
    kg = kg_ref[...]
    ones_row = jnp.where(lax.broadcasted_iota(jnp.int32, (HD, TM), 0) == 0, 1.0, 0.0)
    ones_pad = jnp.where(lax.broadcasted_iota(jnp.int32, (VROWS - HD, TM), 0) == 0, 1.0, 0.0)
    for hq in range(NQ):
        t = _rms_rope_t(qkv_t[hq * HD:(hq + 1) * HD], qg, cos, sin) * (LOG2E * HD ** -0.5)
        nq = jnp.sqrt(jnp.sum(t * t, axis=0, keepdims=True))
        q_ref[hq // QPK, hq % QPK] = jnp.concatenate(
            [t, ones_row * (-BOUND_MARGIN * nq)], axis=0).astype(jnp.bfloat16)
    for g in range(NKV):
        kt = _rms_rope_t(qkv_t[ATTN_W + g * HD:ATTN_W + (g + 1) * HD], kg, cos, sin)
        kn = jnp.max(jnp.sum(kt * kt, axis=0, keepdims=True), axis=1, keepdims=True)
        kmax_ref[0, g] = jnp.broadcast_to(kn, (SUBLANES, LANES))
        k_ref[g] = jnp.concatenate([kt, ones_row], axis=0).T.astype(jnp.bfloat16)
        vt = qkv_t[ATTN_W + KV_W + g * HD:ATTN_W + KV_W + (g + 1) * HD]
        v_ref[g] = jnp.concatenate([vt, ones_pad], axis=0).astype(jnp.bfloat16)
    uag = jnp.dot(hb, wn_ref[...], preferred_element_type=jnp.float32)
    u_ref[...] = uag[:, :POOL_W]
    a = uag[:, POOL_W:POOL_W + CONV_W]
    gt = uag[:, POOL_W + CONV_W:]
    ag_ref[...] = a * jax.nn.sigmoid(gt)


def _inproj(xa, ctx, mod, w_t, w_n, qg, kg, cos_t, sin_t, l):
    ttot = xa.shape[0] + (0 if ctx is None else ctx.shape[0])
    nt = ttot // TM
    stream_specs, stream_args = _stream_specs(xa, ctx)
    return pl.pallas_call(
        functools.partial(_inproj_kernel, split_stream=ctx is not None),
        grid=(nt,),
        in_specs=stream_specs + [
            _layer((SUBLANES, N_MOD * D), l),
            _layer((ATTN_W + 2 * KV_W, D), l),
            _layer((D, POOL_W + 2 * CONV_W), l),
            _layer((HD, TM), l),
            _layer((HD, TM), l),
            pl.BlockSpec((HD // 2, TM), lambda i: (0, i)),
            pl.BlockSpec((HD // 2, TM), lambda i: (0, i)),
        ],
        out_specs=[
            pl.BlockSpec((NKV, QPK, 2 * HD, TM), lambda i: (0, 0, 0, i)),
            pl.BlockSpec((NKV, TM, 2 * HD), lambda i: (0, i, 0)),
            pl.BlockSpec((NKV, VROWS, TM), lambda i: (0, 0, i)),
            pl.BlockSpec((1, NKV, 8, LANES), lambda i: (i, 0, 0, 0)),
            pl.BlockSpec((TM, POOL_W), lambda i: (i, 0)),
            pl.BlockSpec((TM, CONV_W), lambda i: (i, 0)),
        ],
        out_shape=[
            jax.ShapeDtypeStruct((NKV, QPK, 2 * HD, ttot), jnp.bfloat16),
            jax.ShapeDtypeStruct((NKV, ttot, 2 * HD), jnp.bfloat16),
            jax.ShapeDtypeStruct((NKV, VROWS, ttot), jnp.bfloat16),
            jax.ShapeDtypeStruct((nt, NKV, 8, LANES), jnp.float32),
            jax.ShapeDtypeStruct((ttot, POOL_W), jnp.float32),
            jax.ShapeDtypeStruct((ttot, CONV_W), jnp.float32),
        ],
        compiler_params=_cparams(("arbitrary",)),
        name="inproj",
    )(*stream_args, mod, w_t, w_n, qg, kg, cos_t, sin_t)


TK = 256
GCH = 5
LOOP_STEPS = 6
L_MIN = 2.0 ** -60


def _attn_kernel(kmax_ref, q_ref, k_ref, v_ref, o_ref, qs_ref, s_ref, acc_ref, m_ref, lmin_ref, *,
                 n_chunks):
    g = pl.program_id(0)
    i = pl.program_id(1)
    nch = jnp.where(i == 0, 1, n_chunks)
    row = lax.broadcasted_iota(jnp.int32, (2 * HD, TM), 0)
    kmax = kmax_ref[g]

    def prep():
        for hh in range(QPK):
            qf = q_ref[0, hh].astype(jnp.float32)
            qs_ref[hh] = jnp.where(row == HD, qf * kmax, qf).astype(jnp.bfloat16)

    def step(nxt, cur):
        for kk in range(max(grp[1] for grp in (nxt, cur) if grp is not None)):
            do_nxt = nxt is not None and kk < nxt[1]
            do_cur = cur is not None and kk < cur[1]
            rows = slice(kk * TK, (kk + 1) * TK)
            if do_nxt:
                kc = k_ref[0, pl.ds(pl.multiple_of((nxt[0] + kk) * TK, TK), TK), :]
            if do_cur:
                vc = v_ref[0, :, pl.ds(pl.multiple_of((cur[0] + kk) * TK, TK), TK)]
            for hh in range(QPK):
                if do_nxt:
                    s_ref[nxt[2], hh, rows, :] = jnp.dot(
                        kc, qs_ref[hh], preferred_element_type=jnp.float32)
                if do_cur:
                    p = jnp.exp2(s_ref[cur[2], hh, rows, :]).astype(jnp.bfloat16)
                    acc_ref[hh] += jnp.dot(vc, p, preferred_element_type=jnp.float32)

    def finish():
        outs = [acc_ref[hh, 0:HD, :] / acc_ref[hh, HD:HD + 1, :] for hh in range(QPK)]
        o_ref[...] = jnp.concatenate(outs, axis=0).T
        lmin_ref[0] = jnp.min(acc_ref[:, HD:HD + 1, :])

    @pl.when(i == 0)
    def _():
        prep()
        kc = k_ref[0, 0:TM, :]
        vc = v_ref[0, :, 0:TM]
        for hh in range(QPK):
            p = jnp.exp2(jnp.dot(kc, qs_ref[hh], preferred_element_type=jnp.float32))
            acc_ref[hh] = jnp.dot(vc, p.astype(jnp.bfloat16), preferred_element_type=jnp.float32)
        finish()

    @pl.when(i > 0)
    def _():
        prep()
        acc_ref[...] = jnp.zeros(acc_ref.shape, jnp.float32)
        n_mid = (n_chunks - 2) // GCH
        tail = n_chunks - 1 - GCH * n_mid
        sizes = [1] + [GCH] * n_mid + ([tail - 1, 1] if tail >= 2 else [tail])
        firsts = [sum(sizes[:n]) for n in range(len(sizes))]

        def group(n):
            return (firsts[n], sizes[n], n % 2)

        step(group(0), None)
        step(group(1), group(0))
        n_loop = max(n_mid - 1, 0) // LOOP_STEPS

        def body(j, carry):
            c0 = 1 + LOOP_STEPS * GCH * j
            for u in range(LOOP_STEPS):
                step((c0 + (u + 1) * GCH, GCH, u % 2), (c0 + u * GCH, GCH, (u + 1) % 2))
            return carry

        lax.fori_loop(0, n_loop, body, 0)
        for n in range(1 + LOOP_STEPS * n_loop, len(sizes) - 1):
            step(group(n + 1), group(n))
        step(None, group(len(sizes) - 1))
        finish()

    @pl.when(lmin_ref[0] < L_MIN)
    def _():
        for hh in range(QPK):
            qs_ref[hh] = jnp.where(row == HD, 0.0, q_ref[0, hh].astype(jnp.float32)
                                   ).astype(jnp.bfloat16)
        acc_ref[...] = jnp.zeros(acc_ref.shape, jnp.float32)
        m_ref[...] = jnp.full(m_ref.shape, NEG_BIG, jnp.float32)

        def body2(c, carry):
            start = pl.multiple_of(c * TK, TK)
            kc = k_ref[0, pl.ds(start, TK), :]
            vc = v_ref[0, :, pl.ds(start, TK)]
            for hh in range(QPK):
                s = jnp.dot(kc, qs_ref[hh], preferred_element_type=jnp.float32)
                m_old = m_ref[hh]
                m_new = jnp.maximum(m_old, jnp.max(s, axis=0, keepdims=True))
                p = jnp.exp2(s - m_new).astype(jnp.bfloat16)
                acc_ref[hh] = jnp.exp2(m_old - m_new) * acc_ref[hh] + jnp.dot(
                    vc, p, preferred_element_type=jnp.float32)
                m_ref[hh] = m_new
            return carry

        lax.fori_loop(0, nch, body2, 0)
        finish()


def _attention(q_t, k2, v_t, kmax):
    ttot = k2.shape[1]
    nt = ttot // TM
    assert ttot % TK == 0 and ttot // TK >= 3
    kern = functools.partial(_attn_kernel, n_chunks=ttot // TK)
    grid_spec = pltpu.PrefetchScalarGridSpec(
        num_scalar_prefetch=1,
        grid=(NKV, nt),
        in_specs=[
            pl.BlockSpec((1, QPK, 2 * HD, TM), lambda g, i, km: (g, 0, 0, i)),
            pl.BlockSpec((1, ttot, 2 * HD), lambda g, i, km: (g, 0, 0)),
            pl.BlockSpec((1, VROWS, ttot), lambda g, i, km: (g, 0, 0)),
        ],
        out_specs=pl.BlockSpec((TM, QPK * HD), lambda g, i, km: (i, g)),
        scratch_shapes=[
            pltpu.VMEM((QPK, 2 * HD, TM), jnp.bfloat16),
            pltpu.VMEM((2, QPK, GCH * TK, TM), jnp.float32),
            pltpu.VMEM((QPK, VROWS, TM), jnp.float32),
            pltpu.VMEM((QPK, 1, TM), jnp.float32),
            pltpu.SMEM((1,), jnp.float32),
        ],
    )
    return pl.pallas_call(
        kern,
        grid_spec=grid_spec,
        out_shape=jax.ShapeDtypeStruct((ttot, ATTN_W), jnp.float32),
        compiler_params=_cparams(("arbitrary", "arbitrary")),
        name="attention",
    )(kmax, q_t, k2, v_t)


def _fill_ext(ext_ref, cur_ref, prev_ref, next_ref, left_ok, right_ok):
    zero = jnp.zeros((HALO, cur_ref.shape[1]), jnp.float32)
    ext_ref[0:HALO, :] = jnp.where(left_ok, prev_ref[...], zero)
    ext_ref[HALO:HALO + TM, :] = cur_ref[...]
    ext_ref[HALO + TM:HALO + TM + HALO, :] = jnp.where(right_ok, next_ref[...], zero)


def _mix_kernel(*refs, alpha, n_lat, split_stream):
    x_ref, ctx_ref = (refs[0], refs[1]) if split_stream else (refs[0], None)
    (attn_ref, u_ref, up_ref, un_ref, ag_ref, agp_ref, agn_ref, mod_ref,
     wpool_ref, pscale_ref, wdw_ref, bdw_ref, cvg_ref, cvb_ref, wpw_ref, wout_ref,
     ln1g_ref, ln1b_ref, wr_ref, br_ref,
     x1_ref, h2_ref, route_ref, cnt_ref,
     uext_ref, agext_ref, agsh_ref, base_ref) = refs[2 if split_stream else 1:]
    i = pl.program_id(0)
    nt = pl.num_programs(0)
    is_ctx = i == 0
    left_ok = i >= 2
    right_ok = jnp.logical_and(i >= 1, i < nt - 1)

    @pl.when(i == 0)
    def _():
        base_ref[...] = jnp.zeros(base_ref.shape, jnp.float32)

    _fill_ext(uext_ref, u_ref, up_ref, un_ref, left_ok, right_ok)
    _fill_ext(agext_ref, ag_ref, agp_ref, agn_ref, left_ok, right_ok)

    def ush(off, rows=TM):
        return uext_ref[HALO + off:HALO + off + rows, :]

    a2 = ush(-8, TM + 15) + ush(-7, TM + 15)
    a4 = a2[0:TM + 13] + a2[2:TM + 15]
    a8 = a4[0:TM + 9] + a4[4:TM + 13]
    a16 = a8[0:TM] + a8[8:TM + 8]
    sums = (a2[7:7 + TM], a4[6:6 + TM], a8[4:4 + TM], a16)
    pos = lax.broadcasted_iota(jnp.int32, (TM, 1), 0) + jnp.where(is_ctx, 0, (i - 1) * TM)
    seq_n = jnp.where(is_ctx, TM, n_lat)
    u_cur = u_ref[...]
    lane = lax.broadcasted_iota(jnp.int32, (TM, POOL_W), 1)
    pooled = jnp.zeros((TM, POOL_W), jnp.float32)
    for gi, w in enumerate(POOL_WINDOWS):
        lo = jnp.maximum(pos - w // 2, 0)
        hi = jnp.minimum(pos + (w - w // 2), seq_n)
        inv = 1.0 / (hi - lo).astype(jnp.float32)
        in_group = jnp.logical_and(lane >= gi * POOL_GW, lane < (gi + 1) * POOL_GW)
        pooled = jnp.where(in_group, sums[gi] * inv - u_cur, pooled)
    y_pool = jnp.dot(pooled.astype(jnp.bfloat16), wpool_ref[...],
                     preferred_element_type=jnp.float32) * pscale_ref[...]

    sh_rows = TM + 2 * HALO - SUBLANES
    for b in range(1, SUBLANES):
        agsh_ref[b - 1] = agext_ref[b:b + sh_rows, :]
    conv = jnp.zeros((TM, CONV_W), jnp.float32) + bdw_ref[...]
    for k in range(CONV_K):
        off = HALO + k - CONV_K // 2
        b, a8 = off % SUBLANES, off // SUBLANES * SUBLANES
        tap = agext_ref[a8:a8 + TM, :] if b == 0 else agsh_ref[b - 1, a8:a8 + TM, :]
        conv = conv + tap * wdw_ref[k:k + 1, :]
    cn = _ln_plain(conv) * cvg_ref[...] + cvb_ref[...]
    cn = cn * jax.nn.sigmoid(cn)
    y_conv = jnp.dot(cn.astype(jnp.bfloat16), wpw_ref[...], preferred_element_type=jnp.float32)

    mixed = jnp.concatenate([attn_ref[...].astype(jnp.bfloat16), y_pool.astype(jnp.bfloat16),
                             y_conv.astype(jnp.bfloat16)], axis=1)
    y = jnp.dot(mixed, wout_ref[...], preferred_element_type=jnp.float32)
    x_res = _stream_tile(x_ref, ctx_ref)
    x1 = _ln_plain(alpha * x_res + _mod_rows(mod_ref, 2) * y) * ln1g_ref[...] + ln1b_ref[...]
    x1_ref[...] = x1
    h2 = _ln_plain(x1) * (1.0 + _mod_rows(mod_ref, 4)) + _mod_rows(mod_ref, 3)
    h_hi = h2.astype(jnp.bfloat16)
    h_rounded = h_hi.astype(jnp.float32)
    h_lo = (h2 - h_rounded).astype(jnp.bfloat16)
    bits = pltpu.bitcast(h_rounded, jnp.uint32)
    h2_ref[...] = bits[:, D // 2:] | lax.shift_right_logical(bits[:, :D // 2], jnp.uint32(16))
    hw = jnp.dot(h_hi, wr_ref[...], preferred_element_type=jnp.float32)
    logits = (hw[:, :LANES] + hw[:, LANES:] + br_ref[...]
              + jnp.dot(h_lo, wr_ref[:, :LANES], preferred_element_type=jnp.float32))
    ln = lax.broadcasted_iota(jnp.int32, (TM, LANES), 1)
    lnf = ln.astype(jnp.float32)
    no_lane = jnp.float32(LANES)
    neg = jnp.float32(-jnp.inf)
    is_g = ln < N_GROUPS
    lg = jnp.where(is_g, logits, neg)
    mg = jnp.max(lg, axis=-1, keepdims=True)
    gsel = jnp.min(jnp.where(lg == mg, lnf, no_lane), axis=-1, keepdims=True)
    p_group = 1.0 / jnp.sum(jnp.exp(lg - mg), axis=-1, keepdims=True)
    e_lane = lnf - N_GROUPS
    in_sel = jnp.logical_and(e_lane >= gsel * EPG, e_lane < (gsel + 1) * EPG)
    le = jnp.where(in_sel, logits, neg)
    v1 = jnp.max(le, axis=-1, keepdims=True)
    e1 = jnp.min(jnp.where(le == v1, e_lane, no_lane), axis=-1, keepdims=True)
    le2 = jnp.where(e_lane == e1, neg, le)
    v2 = jnp.max(le2, axis=-1, keepdims=True)
    e2 = jnp.min(jnp.where(le2 == v2, e_lane, no_lane), axis=-1, keepdims=True)
    ex = jnp.exp(v2 - v1)
    gate1 = p_group / (1.0 + ex)
    gate2 = p_group * ex / (1.0 + ex)

    oh = jnp.logical_or(lnf == e1, lnf == e2)
    ohb = jnp.where(oh, 1.0, 0.0).astype(jnp.bfloat16)
    r_i = lax.broadcasted_iota(jnp.int32, (TM, TM), 0)
    c_i = lax.broadcasted_iota(jnp.int32, (TM, TM), 1)
    tri = jnp.where(c_i < r_i, 1.0, 0.0).astype(jnp.bfloat16)
    before = jnp.dot(tri, ohb, preferred_element_type=jnp.float32) + base_ref[...]
    rank1 = jnp.sum(jnp.where(lnf == e1, before, 0.0), axis=-1, keepdims=True)
    rank2 = jnp.sum(jnp.where(lnf == e2, before, 0.0), axis=-1, keepdims=True)
    base_ref[...] = base_ref[...] + jnp.sum(ohb.astype(jnp.float32), axis=0, keepdims=True)
    cnt_ref[...] = jnp.broadcast_to(base_ref[...], cnt_ref.shape)

    route = jnp.zeros((TM, LANES), jnp.float32)
    for j, val in enumerate((e1, e2, rank1, rank2, gate1, gate2)):
        route = jnp.where(ln == j, val, route)
    route_ref[...] = route


def _mixer(xa, ctx, attn, u, ag, mod, wpool, pscale, wdw, bdw, cvg, cvb, wpw, wout, ln1g, ln1b,
           wr, br, alpha, l):
    ttot = attn.shape[0]
    nt = ttot // TM
    stream_specs, stream_args = _stream_specs(xa, ctx)
    hb = TM // HALO
    nhb = ttot // HALO

    def tile(w):
        return pl.BlockSpec((TM, w), lambda i: (i, 0))

    def prev(w):
        return pl.BlockSpec((HALO, w), lambda i: (jnp.maximum(i * hb - 1, 0), 0))

    def nxt(w):
        return pl.BlockSpec((HALO, w), lambda i: (jnp.minimum((i + 1) * hb, nhb - 1), 0))

    kern = functools.partial(_mix_kernel, alpha=alpha, n_lat=ttot - TM,
                             split_stream=ctx is not None)
    return pl.pallas_call(
        kern,
        grid=(nt,),
        in_specs=stream_specs + [
            tile(ATTN_W),
            tile(POOL_W), prev(POOL_W), nxt(POOL_W),
            tile(CONV_W), prev(CONV_W), nxt(CONV_W),
            _layer((SUBLANES, N_MOD * D), l),
            _layer((POOL_W, POOL_W), l), _layer((1, POOL_W), l),
            _layer((CONV_K_PAD, CONV_W), l), _layer((1, CONV_W), l), _layer((1, CONV_W), l),
            _layer((1, CONV_W), l),
            _layer((CONV_W, CONV_W), l), _layer((D, D), l),
            _layer((1, D), l), _layer((1, D), l),
            _layer((D, 2 * LANES), l), _layer((1, LANES), l),
        ],
        out_specs=[tile(D), tile(D // 2), tile(LANES), _full((SUBLANES, LANES))],
        out_shape=[
            jax.ShapeDtypeStruct((ttot, D), jnp.float32),
            jax.ShapeDtypeStruct((ttot, D // 2), jnp.uint32),
            jax.ShapeDtypeStruct((ttot, LANES), jnp.float32),
            jax.ShapeDtypeStruct((SUBLANES, LANES), jnp.float32),
        ],
        scratch_shapes=[
            pltpu.VMEM((TM + 2 * HALO, POOL_W), jnp.float32),
            pltpu.VMEM((TM + 2 * HALO, CONV_W), jnp.float32),
            pltpu.VMEM((SUBLANES - 1, TM + 2 * HALO - SUBLANES, CONV_W), jnp.float32),
            pltpu.VMEM((1, LANES), jnp.float32),
        ],
        compiler_params=_cparams(("arbitrary",)),
        name="mixer",
    )(*stream_args, attn, u, u, u, ag, ag, ag, mod, wpool, pscale, wdw, bdw, cvg, cvb, wpw, wout,
      ln1g, ln1b, wr, br)


ROW_GROUP = SUBLANES


def _dispatch_kernel(pends_ref, d1_ref, d2_ref, h_hbm, xs_ref, zero_buf, h_buf, lsems, ssems,
                     zsem, *, n_blk):
    i = pl.program_id(0)
    nt = pl.num_programs(0)
    slot = i % 3

    def load(t, to_slot):
        return pltpu.make_async_copy(h_hbm.at[pl.ds(pl.multiple_of(t * TM, TM), TM), :],
                                     h_buf.at[to_slot], lsems.at[to_slot])

    def wait_scatter(of_slot):
        for _ in range(2):
            pltpu.make_async_copy(h_buf.at[of_slot], xs_ref.at[pl.ds(0, TM), :],
                                  ssems.at[of_slot]).wait()

    @pl.when(i == 0)
    def _():
        load(0, 0).start()

        @pl.when(nt > 1)
        def _():
            load(1, 1).start()

        zero_buf[...] = jnp.zeros(zero_buf.shape, zero_buf.dtype)

        def zero_block(first_row):
            return pltpu.make_async_copy(
                zero_buf, xs_ref.at[pl.ds(pl.multiple_of(first_row, EBLK), EBLK), :], zsem)

        def seg_nonempty(e):
            return pends_ref[e] > (pends_ref[e - 1] if e > 0 else 0)

        for e in range(N_EXPERTS):
            @pl.when(seg_nonempty(e))
            def _():
                zero_block(pends_ref[e] - EBLK).start()
        n_used = pends_ref[N_EXPERTS - 1] // EBLK

        def tail_start(b, carry):
            zero_block(b * EBLK).start()
            return carry

        lax.fori_loop(n_used, n_blk, tail_start, 0)
        for e in range(N_EXPERTS):
            @pl.when(seg_nonempty(e))
            def _():
                zero_block(pends_ref[e] - EBLK).wait()

        def tail_wait(b, carry):
            zero_block(b * EBLK).wait()
            return carry

        lax.fori_loop(n_used, n_blk, tail_wait, 0)

    load(i, slot).wait()

    def start(r8, carry):
        base = pl.multiple_of(r8 * ROW_GROUP, ROW_GROUP)
        rows = h_buf.at[slot, pl.ds(base, ROW_GROUP), :]
        for j in range(ROW_GROUP):
            for queue, d_ref in enumerate((d1_ref, d2_ref)):
                pltpu.make_async_copy(rows.at[pl.ds(j, 1), :],
                                      xs_ref.at[pl.ds(d_ref[0, 0, base + j], 1), :],
                                      ssems.at[slot]).start(priority=queue)
        return carry

    lax.fori_loop(0, TM // ROW_GROUP, start, 0)

    @pl.when(i >= 1)
    def _():
        wait_scatter((i + 2) % 3)

    @pl.when(i + 2 < nt)
    def _():
        load(i + 2, (i + 2) % 3).start()

    @pl.when(i == nt - 1)
    def _():
        wait_scatter(slot)


def _dispatch(h2, d1, d2, pends, n_slots):
    ttot = h2.shape[0]
    nt = ttot // TM
    smem = pl.BlockSpec((1, 1, TM), lambda i, pe: (i, 0, 0), memory_space=pltpu.SMEM)
    grid_spec = pltpu.PrefetchScalarGridSpec(
        num_scalar_prefetch=1,
        grid=(nt,),
        in_specs=[smem, smem, pl.BlockSpec(memory_space=pl.ANY)],
        out_specs=pl.BlockSpec(memory_space=pl.ANY),
        scratch_shapes=[pltpu.VMEM((EBLK, D // 2), jnp.uint32),
                        pltpu.VMEM((3, TM, D // 2), jnp.uint32),
                        pltpu.SemaphoreType.DMA((3,)), pltpu.SemaphoreType.DMA((3,)),
                        pltpu.SemaphoreType.DMA(())],
    )
    return pl.pallas_call(
        functools.partial(_dispatch_kernel, n_blk=n_slots // EBLK),
        grid_spec=grid_spec,
        out_shape=jax.ShapeDtypeStruct((n_slots, D // 2), jnp.uint32),
        compiler_params=_cparams(("arbitrary",)),
        name="dispatch",
    )(pends, d1, d2, h2)


def _expert_kernel(be_ref, ne_ref, nu_ref, x_ref, wg_hbm, wu_hbm, wd_hbm, y_ref,
                   wg_f, wu_f, wd_f, wgu_b, wd_b, seg_ref, sems, *, l):
    b = pl.program_id(0)
    e = be_ref[b]
    new_expert = jnp.logical_or(b == 0, e != be_ref[jnp.maximum(b - 1, 0)])

    def fetch(expert, slot):
        return [pltpu.make_async_copy(src.at[l, expert], dst.at[slot], sems.at[slot])
                for src, dst in ((wg_hbm, wg_f), (wu_hbm, wu_f), (wd_hbm, wd_f))]

    @pl.when(b == 0)
    def _():
        seg_ref[0] = 0
        for cp in fetch(e, 0):
            cp.start()

    @pl.when(jnp.logical_and(b < nu_ref[0], new_expert))
    def _():
        seg = seg_ref[0]
        slot = seg % 2
        for cp in fetch(e, slot):
            cp.wait()

        @pl.when(ne_ref[b] != e)
        def _():
            for cp in fetch(ne_ref[b], 1 - slot):
                cp.start()

        wgu_b[:, :D_EXPERT] = wg_f[slot].astype(jnp.bfloat16)
        wgu_b[:, D_EXPERT:] = wu_f[slot].astype(jnp.bfloat16)
        wd_b[...] = wd_f[slot].astype(jnp.bfloat16)
        seg_ref[0] = seg + 1

    @pl.when(b < nu_ref[0])
    def _():
        xp = x_ref[...]
        x_lo = pltpu.bitcast(lax.shift_left(xp, jnp.uint32(16)), jnp.float32).astype(jnp.bfloat16)
        x_hi = pltpu.bitcast(xp & jnp.uint32(0xFFFF0000), jnp.float32).astype(jnp.bfloat16)
        gu = jnp.dot(jnp.concatenate([x_lo, x_hi], axis=1), wgu_b[...],
                     preferred_element_type=jnp.float32)
        gt = gu[:, :D_EXPERT]
        hm = (gt * jax.nn.sigmoid(gt) * gu[:, D_EXPERT:]).astype(jnp.bfloat16)
        y_ref[...] = jnp.dot(hm, wd_b[...], preferred_element_type=jnp.float32)

    @pl.when(b >= nu_ref[0])
    def _():
        y_ref[...] = jnp.zeros(y_ref.shape, jnp.float32)


def _experts(xs, blk_e, nxt_e, n_used, wg, wu, wd, l):
    n_slots = xs.shape[0]
    nblk = n_slots // EBLK

    def row_map(b, be, ne, nu):
        return (jnp.minimum(b, nu[0] - 1), 0)

    hbm = pl.BlockSpec(memory_space=pl.ANY)
    grid_spec = pltpu.PrefetchScalarGridSpec(
        num_scalar_prefetch=3,
        grid=(nblk,),
        in_specs=[pl.BlockSpec((EBLK, D // 2), row_map), hbm, hbm, hbm],
        out_specs=pl.BlockSpec((EBLK, D), lambda b, be, ne, nu: (b, 0)),
        scratch_shapes=[pltpu.VMEM((2, D, D_EXPERT), jnp.float32),
                        pltpu.VMEM((2, D, D_EXPERT), jnp.float32),
                        pltpu.VMEM((2, D_EXPERT, D), jnp.float32),
                        pltpu.VMEM((D, 2 * D_EXPERT), jnp.bfloat16),
                        pltpu.VMEM((D_EXPERT, D), jnp.bfloat16),
                        pltpu.SMEM((1,), jnp.int32),
                        pltpu.SemaphoreType.DMA((2,))],
    )
    return pl.pallas_call(
        functools.partial(_expert_kernel, l=l),
        grid_spec=grid_spec,
        out_shape=jax.ShapeDtypeStruct((n_slots, D), jnp.float32),
        compiler_params=_cparams(("arbitrary",)),
        name="experts",
    )(blk_e, nxt_e, n_used, xs, wg, wu, wd)


def _combine_kernel(d1_ref, d2_ref, d1n_ref, d2n_ref, x1_ref, route_ref, mod_ref, ln2g_ref,
                    ln2b_ref, ys_ref, o_ref, y_buf, sems, *, alpha):
    i = pl.program_id(0)
    nt = pl.num_programs(0)
    slot = i % 2

    def gather(da_ref, db_ref, to_slot):
        def start(r8, carry):
            base = pl.multiple_of(r8 * ROW_GROUP, ROW_GROUP)
            for k, d_ref in enumerate((da_ref, db_ref)):
                rows = y_buf.at[to_slot, k, pl.ds(base, ROW_GROUP), :]
                for j in range(ROW_GROUP):
                    pltpu.make_async_copy(ys_ref.at[pl.ds(d_ref[0, 0, base + j], 1), :],
                                          rows.at[pl.ds(j, 1), :],
                                          sems.at[to_slot]).start(priority=j % 2)
            return carry

        lax.fori_loop(0, TM // ROW_GROUP, start, 0)

    @pl.when(i == 0)
    def _():
        gather(d1_ref, d2_ref, 0)

    @pl.when(i < nt - 1)
    def _():
        gather(d1n_ref, d2n_ref, 1 - slot)

    for k in range(2):
        pltpu.make_async_copy(ys_ref.at[pl.ds(0, TM), :], y_buf.at[slot, k], sems.at[slot]).wait()

    route = route_ref[...]
    g1 = route[:, 4:5]
    g2 = route[:, 5:6]
    o = g1 * y_buf[slot, 0] + g2 * y_buf[slot, 1]
    z = alpha * x1_ref[...] + _mod_rows(mod_ref, 5) * o
    o_ref[...] = _ln_plain(z) * ln2g_ref[...] + ln2b_ref[...]


def _combine(x1, route, mod, ln2g, ln2b, ys, d1, d2, alpha, l, latents_only):
    ttot = x1.shape[0]
    nt = ttot // TM
    smem = pl.BlockSpec((1, 1, TM), lambda i: (i, 0, 0), memory_space=pltpu.SMEM)
    smem_next = pl.BlockSpec((1, 1, TM), lambda i: (jnp.minimum(i + 1, nt - 1), 0, 0),
                             memory_space=pltpu.SMEM)
    kern = functools.partial(_combine_kernel, alpha=alpha)
    if latents_only:
        out_spec = pl.BlockSpec((TM, D), lambda i: (jnp.maximum(i - 1, 0), 0))
        out_rows = ttot - TM
    else:
        out_spec = pl.BlockSpec((TM, D), lambda i: (i, 0))
        out_rows = ttot
    return pl.pallas_call(
        kern,
        grid=(nt,),
        in_specs=[smem, smem, smem_next, smem_next,
                  pl.BlockSpec((TM, D), lambda i: (i, 0)),
                  pl.BlockSpec((TM, LANES), lambda i: (i, 0)),
                  _layer((SUBLANES, N_MOD * D), l), _layer((1, D), l), _layer((1, D), l),
                  pl.BlockSpec(memory_space=pl.ANY)],
        out_specs=out_spec,
        out_shape=jax.ShapeDtypeStruct((out_rows, D), jnp.float32),
        scratch_shapes=[pltpu.VMEM((2, 2, TM, D), jnp.float32),
                        pltpu.SemaphoreType.DMA((2,))],
        compiler_params=_cparams(("arbitrary",)),
        name="combine",
    )(d1, d2, d1, d2, x1, route, mod, ln2g, ln2b, ys)


def _rope_tables_t(n_lat, n_ctx):
    rows = n_lat // GRID_W
    row = jnp.broadcast_to(jnp.arange(rows)[:, None], (rows, GRID_W)).reshape(-1)
    col = jnp.broadcast_to(jnp.arange(GRID_W)[None, :], (rows, GRID_W)).reshape(-1)
    n_freq = HD // 4
    inv = ROPE_THETA ** (-jnp.arange(n_freq, dtype=jnp.float32) / n_freq)
    pos = jnp.stack([row, col], axis=-1).astype(jnp.float32)
    ang = (pos[:, :, None] * inv).reshape(n_lat, 2 * n_freq)
    cos = jnp.concatenate([jnp.ones((n_ctx, 2 * n_freq), jnp.float32), jnp.cos(ang)], axis=0)
    sin = jnp.concatenate([jnp.zeros((n_ctx, 2 * n_freq), jnp.float32), jnp.sin(ang)], axis=0)
    return cos.T, sin.T


def kernel(x, c, ctx, c_ctx, w_mod, b_mod, w_in, q_gain, k_gain, w_pool, pool_scale, w_dw, b_dw,
           cv_ln_g, cv_ln_b, w_cv_pw, w_out, ln1_g, ln1_b, ln2_g, ln2_b, w_rg, b_rg, w_re, b_re,
           w_e_gate, w_e_up, w_e_down):
    depth = w_mod.shape[0]
    n_lat = x.shape[1]
    n_ctx = ctx.shape[1]
    assert x.shape[0] == 1 and n_ctx == TM and n_lat % TM == 0 and n_lat % GRID_W == 0
    ttot = n_ctx + n_lat
    alpha = float((2 * depth) ** 0.25)
    n_blk = -(-(2 * ttot) // EBLK) + N_EXPERTS
    n_slots = n_blk * EBLK

    xa, xctx = x[0], ctx[0]
    cc = jnp.concatenate([c, c_ctx[None], jnp.zeros((SUBLANES - 2, D), jnp.float32)], axis=0)
    mod = _modulation(cc, w_mod, b_mod)
    cos_t, sin_t = _rope_tables_t(n_lat, n_ctx)

    n_qkv = ATTN_W + 2 * KV_W
    bf = jnp.bfloat16
    w_t = jnp.swapaxes(w_in[:, :, :n_qkv], 1, 2).astype(bf)
    w_n = w_in[:, :, n_qkv:].astype(bf)
    qg = jnp.broadcast_to(q_gain[:, :, None], (depth, HD, TM))
    kg = jnp.broadcast_to(k_gain[:, :, None], (depth, HD, TM))
    n_pg = len(POOL_WINDOWS)
    wpool = (jnp.eye(n_pg, dtype=jnp.float32)[None, :, None, :, None] * w_pool[:, :, :, None, :]
             ).reshape(depth, POOL_W, POOL_W).astype(bf)
    wdw = jnp.pad(w_dw, ((0, 0), (0, CONV_K_PAD - CONV_K), (0, 0)))
    n_r = N_GROUPS + N_EXPERTS
    wr = jnp.concatenate([w_rg, w_re, jnp.zeros((depth, D, LANES - n_r), jnp.float32)], axis=2)
    br = jnp.concatenate([b_rg, b_re, jnp.zeros((depth, LANES - n_r), jnp.float32)], axis=1)
    wr_hi = wr.astype(bf)
    wr = jnp.concatenate([wr_hi, (wr - wr_hi.astype(jnp.float32)).astype(bf)], axis=2)
    wpw = w_cv_pw.astype(bf)
    wout = w_out.astype(bf)

    def rows(v):
        return v[:, None, :]

    e_ids = jnp.arange(N_EXPERTS, dtype=jnp.int32)
    blk_start = jnp.arange(n_blk, dtype=jnp.int32) * EBLK
    for l in range(depth):
        q_t, k2, v_t, kn2, u, ag = _inproj(xa, xctx, mod, w_t, w_n, qg, kg, cos_t, sin_t, l)
        kmax = jnp.sqrt(jnp.max(kn2[:, :, 0, 0], axis=0))
        attn = _attention(q_t, k2, v_t, kmax)
        x1, h2, route, cnt = _mixer(
            xa, xctx, attn, u, ag, mod, wpool, rows(pool_scale), wdw, rows(b_dw), rows(cv_ln_g),
            rows(cv_ln_b), wpw, wout, rows(ln1_g), rows(ln1_b), wr, rows(br), alpha, l)

        counts = cnt[0, :N_EXPERTS].astype(jnp.int32)
        padded = (counts + EBLK - 1) // EBLK * EBLK
        pends = jnp.cumsum(padded)
        pstarts = pends - padded
        ri = route[:, :4].astype(jnp.int32)

        def dest(e, r):
            start = jnp.sum(jnp.where(e[:, None] == e_ids[None, :], pstarts[None, :], 0), axis=1)
            return (start + r).reshape(ttot // TM, 1, TM)

        d1 = dest(ri[:, 0], ri[:, 2])
        d2 = dest(ri[:, 1], ri[:, 3])
        n_used = (pends[-1] // EBLK).astype(jnp.int32).reshape(1)
        first_row = jnp.minimum(blk_start, pends[-1] - 1)
        blk_e = jnp.sum((pends[None, :] <= first_row[:, None]).astype(jnp.int32), axis=1)
        blk_e = jnp.minimum(blk_e, N_EXPERTS - 1)
        seg_end = jnp.sum(jnp.where(blk_e[:, None] == e_ids[None, :], pends[None, :], 0), axis=1)
        nxt_row = jnp.minimum(seg_end, pends[-1] - 1)
        nxt_e = jnp.sum((pends[None, :] <= nxt_row[:, None]).astype(jnp.int32), axis=1)
        nxt_e = jnp.minimum(nxt_e, N_EXPERTS - 1)

        xs = _dispatch(h2, d1, d2, pends, n_slots)
        ys = _experts(xs, blk_e, nxt_e, n_used, w_e_gate, w_e_up, w_e_down, l)
        xa = _combine(x1, route, mod, rows(ln2_g), rows(ln2_b), ys, d1, d2, alpha, l,
                      latents_only=(l == depth - 1))
        xctx = None
    return xa[None]
```
